```python
import math
import jax, jax.numpy as jnp
from jax import lax
import numpy as np

D_MODEL = 1024
BATCH = 8
SEQ = 2048
DEPTH = 1

GRID_W = 64
CTX_LEN = 256
CONV_WIDTH = D_MODEL
CONV_K = 3
N_HEADS = 8
HEAD_DIM = 64
V_DIM = 2 * HEAD_DIM
ATTN_QK = N_HEADS * 2 * HEAD_DIM
ATTN_V = N_HEADS * V_DIM
N_EXPERTS = 16
EXPERT_HIDDEN = 2048
CAPACITY_FACTOR = 2
ROPE_THETA = 10000.0
Q_BLOCK = 128
EPS = 1e-6
PROJ_SIZES = (CONV_WIDTH, CONV_WIDTH, CONV_WIDTH, ATTN_QK, ATTN_QK, ATTN_V, D_MODEL, D_MODEL)
PROJ_COLS = sum(PROJ_SIZES)

kernel_name = 'hybrid_conv_diffattn_ec_moe_dit'


def rmsnorm(x, g):
    xf = x.astype(jnp.float32)
    y = xf * lax.rsqrt(jnp.mean(xf * xf, axis=-1, keepdims=True) + EPS)
    return (y * g.astype(jnp.float32)).astype(x.dtype)


def ada_mod(cond, w_ada, b_ada):
    m = jax.nn.silu(cond) @ w_ada + b_ada
    m = m.reshape(-1, 1, 6 * D_MODEL)
    return jnp.split(m, 6, axis=-1)


def modulate(h, shift, scale):
    return h * (1 + scale) + shift


def axial_rope_tables(n):
    rows = n // GRID_W
    row = jnp.repeat(jnp.arange(rows), GRID_W).astype(jnp.float32)
    col = jnp.tile(jnp.arange(GRID_W), rows).astype(jnp.float32)
    per_axis = HEAD_DIM // 2
    inv = ROPE_THETA ** (-jnp.arange(0, per_axis, 2, dtype=jnp.float32) / per_axis)
    ang = jnp.stack([row[:, None] * inv, col[:, None] * inv], axis=1)
    return jnp.cos(ang), jnp.sin(ang)


def apply_rope(x, cos, sin):
    shp = x.shape
    xr = x.astype(jnp.float32).reshape(shp[:-1] + (2, 2, HEAD_DIM // 4))
    x1, x2 = xr[..., 0, :], xr[..., 1, :]
    cs, sn = cos[:, None, None], sin[:, None, None]
    out = jnp.stack([x1 * cs - x2 * sn, x2 * cs + x1 * sn], axis=-2)
    return out.reshape(shp).astype(x.dtype)


def split_proj(p):
    idx = [int(v) for v in np.cumsum(PROJ_SIZES)[:-1]]
    return jnp.split(p, idx, axis=-1)


def attn_heads(parts):
    B, n, _ = parts[3].shape
    q = parts[3].reshape(B, n, N_HEADS, 2, HEAD_DIM)
    k = parts[4].reshape(B, n, N_HEADS, 2, HEAD_DIM)
    v = parts[5].reshape(B, n, N_HEADS, V_DIM).transpose(0, 2, 1, 3)
    return q, k, v


def to_bhm(t):
    return t.transpose(0, 2, 3, 1, 4)


def diff_attn(q, k, v, lam):
    s = jnp.einsum('bhmqd,bhmkd->bhmqk', q, k).astype(jnp.float32) * (HEAD_DIM ** -0.5)
    p = jax.nn.softmax(s, axis=-1)
    a = p[:, :, 0] - lam * p[:, :, 1]
    return jnp.einsum('bhqk,bhkv->bhqv', a.astype(v.dtype), v)


def blocked_diff_attn(q, k, v, lam):
    B, H, _, n, hd = q.shape
    nb = n // Q_BLOCK
    qb = q.reshape(B, H, 2, nb, Q_BLOCK, hd).transpose(3, 0, 1, 2, 4, 5)
    ob = lax.map(lambda qi: diff_attn(qi, k, v, lam), qb)
    return ob.transpose(1, 2, 0, 3, 4).reshape(B, H, n, V_DIM)


def short_conv_branch(u, gate_b, gate_c, conv_w, w_out_conv):
    z = gate_c * u
    zp = jnp.pad(z, ((0, 0), (1, 1), (0, 0)))
    y = zp[:, :-2] * conv_w[0] + zp[:, 1:-1] * conv_w[1] + zp[:, 2:] * conv_w[2]
    return (gate_b * y) @ w_out_conv


def merge_branches(parts, o, conv_w, w_out_conv, subln_g, lam_init, w_o_attn, w_out):
    y_conv = short_conv_branch(parts[0], parts[1], parts[2], conv_w, w_out_conv)
    B, H, n, _ = o.shape
    o = rmsnorm(o, subln_g) * (1.0 - lam_init)
    y_attn = o.transpose(0, 2, 1, 3).reshape(B, n, ATTN_V) @ w_o_attn
    merged = jax.nn.sigmoid(parts[6]) * y_conv + jax.nn.sigmoid(parts[7]) * y_attn
    return merged @ w_out


def expert_choice_ffn(h, w_router, w_gate, w_up, w_down):
    B, n, _ = h.shape
    cap = CAPACITY_FACTOR * n // N_EXPERTS
    aff = jax.nn.softmax((h @ w_router).astype(jnp.float32), axis=-1)
    g, idx = lax.top_k(aff.transpose(0, 2, 1), cap)
    bidx = jnp.arange(B)[:, None, None]
    xs = h[bidx, idx]
    a = jnp.einsum('becd,edf->becf', xs, w_gate)
    u = jnp.einsum('becd,edf->becf', xs, w_up)
    y = jnp.einsum('becf,efd->becd', jax.nn.silu(a) * u, w_down)
    y = y * g[..., None].astype(y.dtype)
    return jnp.zeros_like(h).at[bidx, idx].add(y)


def setup_inputs(seed: int = 0) -> dict:
    key = jax.random.key(seed)
    ks = jax.random.split(key, 32)
    L, D, F, E = DEPTH, D_MODEL, EXPERT_HIDDEN, N_EXPERTS

    def nrm(k, shape, scale):
        return jax.random.normal(k, shape, jnp.float32) * scale

    return {
        'x': nrm(ks[0], (BATCH, SEQ, D), 1.0),
        'c': nrm(ks[1], (BATCH, D), 1.0),
        'ctx': nrm(ks[2], (BATCH, CTX_LEN, D), 1.0),
        'c_ctx': nrm(ks[3], (D,), 1.0),
        'norm1_g': 1.0 + nrm(ks[4], (L, D), 0.02),
        'norm2_g': 1.0 + nrm(ks[5], (L, D), 0.02),
        'w_ada': nrm(ks[6], (L, D, 6 * D), 0.5 * D ** -0.5),
        'b_ada': nrm(ks[7], (L, 6 * D), 0.02),
        'w_in': nrm(ks[8], (L, D, PROJ_COLS), D ** -0.5),
        'conv_w': nrm(ks[9], (L, CONV_K, CONV_WIDTH), CONV_K ** -0.5),
        'w_out_conv': nrm(ks[10], (L, CONV_WIDTH, D), CONV_WIDTH ** -0.5),
        'lambda_q1': nrm(ks[11], (L, HEAD_DIM), 0.1),
        'lambda_k1': nrm(ks[12], (L, HEAD_DIM), 0.1),
        'lambda_q2': nrm(ks[13], (L, HEAD_DIM), 0.1),
        'lambda_k2': nrm(ks[14], (L, HEAD_DIM), 0.1),
        'subln_g': 1.0 + nrm(ks[15], (L, V_DIM), 0.02),
        'w_o_attn': nrm(ks[16], (L, ATTN_V, D), ATTN_V ** -0.5),
        'w_out': nrm(ks[17], (L, D, D), D ** -0.5),
        'w_router': nrm(ks[18], (L, D, E), D ** -0.5),
        'w_gate_e': nrm(ks[19], (L, E, D, F), D ** -0.5),
        'w_up_e': nrm(ks[20], (L, E, D, F), D ** -0.5),
        'w_down_e': nrm(ks[21], (L, E, F, D), F ** -0.5),
        'final_g': 1.0 + nrm(ks[22], (D,), 0.02),
    }


def reference(x, c, ctx, c_ctx, norm1_g, norm2_g, w_ada, b_ada, w_in, conv_w, w_out_conv,
              lambda_q1, lambda_k1, lambda_q2, lambda_k2, subln_g, w_o_attn, w_out,
              w_router, w_gate_e, w_up_e, w_down_e, final_g):
    n = x.shape[1]
    cos, sin = axial_rope_tables(n)
    f32 = jnp.float32
    for l in range(DEPTH):
        last = l == DEPTH - 1
        sh1, sc1, g1, sh2, sc2, g2 = ada_mod(c, w_ada[l], b_ada[l])
        sh1c, sc1c, g1c, sh2c, sc2c, g2c = ada_mod(c_ctx, w_ada[l], b_ada[l])
        lam_init = 0.8 - 0.6 * math.exp(-0.3 * l)
        lam = (jnp.exp(jnp.sum(lambda_q1[l].astype(f32) * lambda_k1[l].astype(f32)))
               - jnp.exp(jnp.sum(lambda_q2[l].astype(f32) * lambda_k2[l].astype(f32)))
               + lam_init)

        hx = modulate(rmsnorm(x, norm1_g[l]), sh1, sc1)
        hc = modulate(rmsnorm(ctx, norm1_g[l]), sh1c, sc1c)
        px = split_proj(hx @ w_in[l])
        pc = split_proj(hc @ w_in[l])
        qx, kx, vx = attn_heads(px)
        qc, kc, vc = attn_heads(pc)
        qx = apply_rope(qx, cos, sin)
        kx = apply_rope(kx, cos, sin)
        k_all = jnp.concatenate([to_bhm(kc), to_bhm(kx)], axis=3)
        v_all = jnp.concatenate([vc, vx], axis=2)
        ox = blocked_diff_attn(to_bhm(qx), k_all, v_all, lam)
        mix_x = merge_branches(px, ox, conv_w[l], w_out_conv[l], subln_g[l], lam_init,
                               w_o_attn[l], w_out[l])
        x = x + g1 * mix_x
        if not last:
            oc = diff_attn(to_bhm(qc), to_bhm(kc), vc, lam)
            ctx = ctx + g1c * merge_branches(pc, oc, conv_w[l], w_out_conv[l], subln_g[l],
                                             lam_init, w_o_attn[l], w_out[l])

        hx2 = modulate(rmsnorm(x, norm2_g[l]), sh2, sc2)
        x = x + g2 * expert_choice_ffn(hx2, w_router[l], w_gate_e[l], w_up_e[l], w_down_e[l])
        if not last:
            hc2 = modulate(rmsnorm(ctx, norm2_g[l]), sh2c, sc2c)
            ctx = ctx + g2c * expert_choice_ffn(hc2, w_router[l], w_gate_e[l], w_up_e[l], w_down_e[l])
    return rmsnorm(x, final_g)
```

```python
import functools
import math

import jax
import jax.numpy as jnp
from jax import lax
from jax.experimental import pallas as pl
from jax.experimental.pallas import tpu as pltpu

D_MODEL = 1024
BATCH = 8
SEQ = 2048
GRID_W = 64
CTX_LEN = 256
N_HEADS = 8
HEAD_DIM = 64
V_DIM = 2 * HEAD_DIM
N_EXPERTS = 16
EXPERT_HIDDEN = 2048
CAPACITY = 2 * SEQ // N_EXPERTS
ROPE_THETA = 10000.0
EPS = 1e-6
LAM_INIT = 0.8 - 0.6 * math.exp(-0.3 * 0)
N_PARTS = 8
N_KEYS = CTX_LEN + SEQ

LANES = 128
F32 = jnp.float32
BF16 = jnp.bfloat16

ROW_CHUNK = 512
COND_ROWS = 16
ADA_TN = 512
ATTN_TQ = 512
ATTN_SQ = 256
MERGE_TM = 512
FFN_TF = 512
COMBINE_TQ = 512
VMEM_LIMIT = 56 * 1024 * 1024


def _dot(a, b):
    return jnp.dot(a, b, preferred_element_type=F32)


def _dot_nt(a, b):
    return lax.dot_general(a, b, (((1,), (1,)), ((), ())), preferred_element_type=F32)


def _params(sem, vmem=VMEM_LIMIT):
    return pltpu.CompilerParams(dimension_semantics=sem, vmem_limit_bytes=vmem)


def _ada_kernel(cond_ref, w_ref, b_ref, o_ref):
    c = cond_ref[...]
    s = c / (1.0 + jnp.exp(-c))
    o_ref[...] = _dot(s.astype(BF16), w_ref[...].astype(BF16)) + b_ref[...]


def _ada(cond, w_ada, b_ada):
    n = w_ada.shape[1]
    return pl.pallas_call(
        _ada_kernel,
        out_shape=jax.ShapeDtypeStruct((COND_ROWS, n), F32),
        grid=(n // ADA_TN,),
        in_specs=[
            pl.BlockSpec((COND_ROWS, D_MODEL), lambda i: (0, 0)),
            pl.BlockSpec((D_MODEL, ADA_TN), lambda i: (0, i)),
            pl.BlockSpec((1, ADA_TN), lambda i: (0, i)),
        ],
        out_specs=pl.BlockSpec((COND_ROWS, ADA_TN), lambda i: (0, i)),
        compiler_params=_params(("arbitrary",)),
        name="ada",
    )(cond, w_ada, b_ada)


def _norm_modulate(x, g, shift, scale):
    ms = jnp.mean(x * x, axis=-1, keepdims=True)
    return (x * lax.rsqrt(ms + EPS) * g) * (1.0 + scale) + shift


def _rope(a, cos, sin_hi, sin_lo):
    return a * cos + pltpu.roll(a, LANES - 16, 1) * sin_hi + pltpu.roll(a, 16, 1) * sin_lo


def _inproj_kernel(x_ref, mod_ref, g_ref, w_ref, cw_ref, cos_ref, shi_ref, slo_ref, o_ref,
                   hx_ref, z_ref, *, q_scale):
    j = pl.program_id(1)
    chunks = [slice(r * ROW_CHUNK, (r + 1) * ROW_CHUNK) for r in range(SEQ // ROW_CHUNK)]

    @pl.when(j == 0)
    def _():
        shift = mod_ref[0, 0:1, :]
        scale = mod_ref[0, 1:2, :]
        for rows in chunks:
            hx_ref[rows, :] = _norm_modulate(x_ref[0, rows, :], g_ref[...], shift, scale).astype(BF16)
        z_ref[0:8, :] = jnp.zeros((8, D_MODEL), F32)
        z_ref[SEQ + 8:SEQ + 16, :] = jnp.zeros((8, D_MODEL), F32)
        for rows in chunks:
            z_ref[pl.ds(rows.start + 8, ROW_CHUNK), :] = _dot(hx_ref[rows, :], w_ref[...])

    @pl.when(j == 1)
    def _():
        for rows in chunks:
            zr = pl.ds(rows.start + 8, ROW_CHUNK)
            z_ref[zr, :] = z_ref[zr, :] * _dot(hx_ref[rows, :], w_ref[...])

    @pl.when(j == 2)
    def _():
        for rows in chunks:
            zp = z_ref[pl.ds(rows.start + 7, ROW_CHUNK), :]
            zc = z_ref[pl.ds(rows.start + 8, ROW_CHUNK), :]
            zn = z_ref[pl.ds(rows.start + 9, ROW_CHUNK), :]
            y = zp * cw_ref[0:1, :] + zc * cw_ref[1:2, :] + zn * cw_ref[2:3, :]
            o_ref[0, rows, :] = (_dot(hx_ref[rows, :], w_ref[...]) * y).astype(BF16)

    def rope_part(scale):
        for rows in chunks:
            acc = _dot(hx_ref[rows, :], w_ref[...])
            cos, shi, slo = cos_ref[rows, :], shi_ref[rows, :], slo_ref[rows, :]
            for h in range(D_MODEL // LANES):
                cols = slice(h * LANES, (h + 1) * LANES)
                r = _rope(acc[:, cols], cos, shi, slo)
                if scale is not None:
                    r = r * scale
                o_ref[0, rows, cols] = r.astype(BF16)

    @pl.when(j == 3)
    def _():
        rope_part(q_scale)

    @pl.when(j == 4)
    def _():
        rope_part(None)

    @pl.when(j == 5)
    def _():
        for rows in chunks:
            o_ref[0, rows, :] = _dot(hx_ref[rows, :], w_ref[...]).astype(BF16)

    @pl.when(j >= 6)
    def _():
        for rows in chunks:
            acc = _dot(hx_ref[rows, :], w_ref[...])
            o_ref[0, rows, :] = (1.0 / (1.0 + jnp.exp(-acc))).astype(BF16)


def _inproj(x, mod3, norm_g, w_in_bf, conv_w, cos, sin_hi, sin_lo, q_scale):
    def w_map(b, j):
        return (0, jnp.where(j == 1, 2, jnp.where(j == 2, 1, j)))

    tab = pl.BlockSpec((SEQ, LANES), lambda b, j: (0, 0))
    return pl.pallas_call(
        functools.partial(_inproj_kernel, q_scale=q_scale),
        out_shape=jax.ShapeDtypeStruct((N_PARTS - 2, BATCH * SEQ, D_MODEL), BF16),
        grid=(BATCH, N_PARTS),
        in_specs=[
            pl.BlockSpec((1, SEQ, D_MODEL), lambda b, j: (b, 0, 0)),
            pl.BlockSpec((1, 6, D_MODEL), lambda b, j: (b, 0, 0)),
            pl.BlockSpec((1, D_MODEL), lambda b, j: (0, 0)),
            pl.BlockSpec((D_MODEL, D_MODEL), w_map),
            pl.BlockSpec((3, D_MODEL), lambda b, j: (0, 0)),
            tab, tab, tab,
        ],
        out_specs=pl.BlockSpec((1, SEQ, D_MODEL), lambda b, j: (jnp.maximum(j - 2, 0), b, 0)),
        scratch_shapes=[pltpu.VMEM((SEQ, D_MODEL), BF16), pltpu.VMEM((SEQ + 16, D_MODEL), F32)],
        compiler_params=_params(("arbitrary", "arbitrary")),
        name="inproj",
    )(x, mod3, norm_g, w_in_bf, conv_w, cos, sin_hi, sin_lo)


def _ctxproj_kernel(c_ref, mod_ref, g_ref, w_ref, o_ref, hc_ref):
    j = pl.program_id(0)
    chunks = [slice(r * ROW_CHUNK, (r + 1) * ROW_CHUNK) for r in range(BATCH * CTX_LEN // ROW_CHUNK)]

    @pl.when(j == 0)
    def _():
        shift = mod_ref[0, 0:1, :]
        scale = mod_ref[0, 1:2, :]
        for rows in chunks:
            hc_ref[rows, :] = _norm_modulate(c_ref[rows, :], g_ref[...], shift, scale).astype(BF16)

    for rows in chunks:
        o_ref[0, rows, :] = _dot(hc_ref[rows, :], w_ref[...]).astype(BF16)


def _ctxproj(ctx2, mod3, norm_g, w_in_bf):
    rows = BATCH * CTX_LEN
    return pl.pallas_call(
        _ctxproj_kernel,
        out_shape=jax.ShapeDtypeStruct((2, rows, D_MODEL), BF16),
        grid=(2,),
        in_specs=[
            pl.BlockSpec((rows, D_MODEL), lambda j: (0, 0)),
            pl.BlockSpec((1, 6, D_MODEL), lambda j: (BATCH, 0, 0)),
            pl.BlockSpec((1, D_MODEL), lambda j: (0, 0)),
            pl.BlockSpec((D_MODEL, D_MODEL), lambda j: (0, 4 + j)),
        ],
        out_specs=pl.BlockSpec((1, rows, D_MODEL), lambda j: (j, 0, 0)),
        scratch_shapes=[pltpu.VMEM((rows, D_MODEL), BF16)],
        compiler_params=_params(("arbitrary",)),
        name="ctxproj",
    )(ctx2, mod3, norm_g, w_in_bf)


def _attn_kernel(lp_ref, q_ref, k_ref, v_ref, kc_ref, vc_ref, g_ref, o_ref):
    lp = lp_ref[...]
    lam = (jnp.exp(jnp.sum(lp[0:1] * lp[1:2], axis=-1, keepdims=True))
           - jnp.exp(jnp.sum(lp[2:3] * lp[3:4], axis=-1, keepdims=True)) + LAM_INIT)
    k_all = jnp.concatenate([kc_ref[0], k_ref[0]], axis=0)
    v_all = jnp.concatenate([vc_ref[0], v_ref[0]], axis=0)
    first_map = lax.broadcasted_iota(jnp.int32, (1, LANES), 1) < HEAD_DIM
    for r in range(ATTN_TQ // ATTN_SQ):
        rows = slice(r * ATTN_SQ, (r + 1) * ATTN_SQ)
        q = q_ref[0, rows, :]
        zero = jnp.zeros_like(q)
        s0 = _dot_nt(jnp.where(first_map, q, zero), k_all)
        s1 = _dot_nt(jnp.where(first_map, zero, q), k_all)
        p0 = jnp.exp2(s0 - jnp.max(s0, axis=-1, keepdims=True))
        p1 = jnp.exp2(s1 - jnp.max(s1, axis=-1, keepdims=True))
        r0 = 1.0 / jnp.sum(p0, axis=-1, keepdims=True)
        r1 = lam / jnp.sum(p1, axis=-1, keepdims=True)
        a = p0 * r0 - p1 * r1
        o = _dot(a.astype(BF16), v_all)
        ms = jnp.mean(o * o, axis=-1, keepdims=True)
        o_ref[rows, :] = ((o * lax.rsqrt(ms + EPS) * g_ref[...]) * (1.0 - LAM_INIT)).astype(BF16)


def _attn(lam_params, parts, ctx_kv, subln_g):
    nq = SEQ // ATTN_TQ
    return pl.pallas_call(
        _attn_kernel,
        out_shape=jax.ShapeDtypeStruct((BATCH * SEQ, D_MODEL), BF16),
        grid=(BATCH, N_HEADS, nq),
        in_specs=[
            pl.BlockSpec((4, HEAD_DIM), lambda b, h, i: (0, 0)),
            pl.BlockSpec((1, ATTN_TQ, LANES), lambda b, h, i: (1, b * nq + i, h)),
            pl.BlockSpec((1, SEQ, LANES), lambda b, h, i: (2, b, h)),
            pl.BlockSpec((1, SEQ, LANES), lambda b, h, i: (3, b, h)),
            pl.BlockSpec((1, CTX_LEN, LANES), lambda b, h, i: (0, b, h)),
            pl.BlockSpec((1, CTX_LEN, LANES), lambda b, h, i: (1, b, h)),
            pl.BlockSpec((1, V_DIM), lambda b, h, i: (0, 0)),
        ],
        out_specs=pl.BlockSpec((ATTN_TQ, LANES), lambda b, h, i: (b * nq + i, h)),
        compiler_params=_params(("arbitrary", "arbitrary", "arbitrary")),
        name="attn",
    )(lam_params, parts, parts, parts, ctx_kv, ctx_kv, subln_g)


def _merge_kernel(yb_ref, s6_ref, s7_ref, o_ref, x_ref, mod_ref, g2_ref, wc_ref, wa_ref, wo_ref,
                  wrh_ref, wrl_ref, x1_ref, h2_ref, lg_ref):
    y_conv = _dot(yb_ref[0], wc_ref[...])
    y_attn = _dot(o_ref[...], wa_ref[...])
    merged = s6_ref[0].astype(F32) * y_conv + s7_ref[0].astype(F32) * y_attn
    mix = _dot(merged.astype(BF16), wo_ref[...])
    x1 = x_ref[...] + mod_ref[0, 2:3, :] * mix
    x1_ref[...] = x1
    h2 = _norm_modulate(x1, g2_ref[...], mod_ref[0, 3:4, :], mod_ref[0, 4:5, :])
    h_hi = h2.astype(BF16)
    h2_ref[...] = h_hi
    h_lo = (h2 - h_hi.astype(F32)).astype(BF16)
    lg_ref[0] = _dot_nt(wrh_ref[...], h_hi) + (_dot_nt(wrh_ref[...], h_lo) + _dot_nt(wrl_ref[...], h_hi))


def _merge(parts, attn_o, x2, mod3, norm2_g, w_conv_bf, w_attn_bf, w_out_bf, wr_hi, wr_lo):
    per_b = SEQ // MERGE_TM
    sq = pl.BlockSpec((D_MODEL, D_MODEL), lambda i: (0, 0))
    wr = pl.BlockSpec((N_EXPERTS, D_MODEL), lambda i: (0, 0))
    tile = pl.BlockSpec((MERGE_TM, D_MODEL), lambda i: (i, 0))
    return pl.pallas_call(
        _merge_kernel,
        out_shape=(
            jax.ShapeDtypeStruct((BATCH * SEQ, D_MODEL), F32),
            jax.ShapeDtypeStruct((BATCH * SEQ, D_MODEL), BF16),
            jax.ShapeDtypeStruct((BATCH, N_EXPERTS, SEQ), F32),
        ),
        grid=(BATCH * per_b,),
        in_specs=[
            pl.BlockSpec((1, MERGE_TM, D_MODEL), lambda i: (0, i, 0)),
            pl.BlockSpec((1, MERGE_TM, D_MODEL), lambda i: (4, i, 0)),
            pl.BlockSpec((1, MERGE_TM, D_MODEL), lambda i: (5, i, 0)),
            tile, tile,
            pl.BlockSpec((1, 6, D_MODEL), lambda i: (i // per_b, 0, 0)),
            pl.BlockSpec((1, D_MODEL), lambda i: (0, 0)),
            sq, sq, sq, wr, wr,
        ],
        out_specs=(
            tile, tile,
            pl.BlockSpec((1, N_EXPERTS, MERGE_TM), lambda i: (i // per_b, 0, i % per_b)),
        ),
        compiler_params=_params(("arbitrary",)),
        name="merge",
    )(parts, parts, parts, attn_o, x2, mod3, norm2_g, w_conv_bf, w_attn_bf, w_out_bf, wr_hi, wr_lo)


def _route_kernel(lg_ref, rank_e_ref, rank_t_ref, gate_t_ref):
    lg = lg_ref[...]
    ex = jnp.exp(lg - jnp.max(lg, axis=1, keepdims=True))
    aff = (ex / jnp.sum(ex, axis=1, keepdims=True)).reshape(BATCH * N_EXPERTS, SEQ)
    bits = lax.bitcast_convert_type(aff, jnp.int32)

    def count(mask):
        return jnp.sum(jnp.where(mask, 1.0, 0.0), axis=1, keepdims=True)

    def search(i, t):
        cand = t + lax.shift_left(jnp.int32(1), 30 - i)
        return jnp.where(count(bits >= cand) >= CAPACITY, cand, t)

    thr = lax.fori_loop(0, 31, search, jnp.zeros((BATCH * N_EXPERTS, 1), jnp.int32))
    need = CAPACITY - count(bits > thr)
    before = (lax.broadcasted_iota(jnp.int32, (LANES, LANES), 0)
              < lax.broadcasted_iota(jnp.int32, (LANES, LANES), 1)).astype(BF16)
    n_tied = jnp.zeros((BATCH * N_EXPERTS, 1), F32)
    n_sel = jnp.zeros((BATCH * N_EXPERTS, 1), F32)
    for c in range(SEQ // LANES):
        cols = slice(c * LANES, (c + 1) * LANES)
        bits_c = bits[:, cols]
        tied_c = jnp.where(bits_c == thr, 1.0, 0.0)
        tied_before = _dot(tied_c.astype(BF16), before) + n_tied
        sel_c = jnp.where(bits_c > thr, 1.0, jnp.where(tied_before < need, tied_c, 0.0))
        rank = _dot(sel_c.astype(BF16), before) + n_sel
        rank_c = jnp.where(sel_c > 0.0, rank, -1.0)
        gate_c = jnp.where(sel_c > 0.0, aff[:, cols], 0.0)
        n_tied = n_tied + jnp.sum(tied_c, axis=1, keepdims=True)
        n_sel = n_sel + jnp.sum(sel_c, axis=1, keepdims=True)
        rank_e_ref[:, cols] = rank_c
        rank_ct = rank_c.T
        gate_ct = gate_c.T
        for b in range(BATCH):
            ecols = slice(b * N_EXPERTS, (b + 1) * N_EXPERTS)
            rank_t_ref[b, cols, :] = rank_ct[:, ecols]
            gate_t_ref[b, cols, :] = gate_ct[:, ecols]


def _route(logits_t):
    return pl.pallas_call(
        _route_kernel,
        out_shape=(
            jax.ShapeDtypeStruct((BATCH * N_EXPERTS, SEQ), F32),
            jax.ShapeDtypeStruct((BATCH, SEQ, N_EXPERTS), F32),
            jax.ShapeDtypeStruct((BATCH, SEQ, N_EXPERTS), F32),
        ),
        compiler_params=pltpu.CompilerParams(vmem_limit_bytes=VMEM_LIMIT),
        name="route",
    )(logits_t)


def _dispatch_kernel(rank_ref, h_ref, o_ref):
    slot = lax.broadcasted_iota(jnp.int32, (CAPACITY, SEQ), 0).astype(F32)
    onehot = jnp.where(rank_ref[0] == slot, 1.0, 0.0).astype(BF16)
    o_ref[0] = _dot(onehot, h_ref[...]).astype(BF16)


def _dispatch(rank_e3, h2):
    return pl.pallas_call(
        _dispatch_kernel,
        out_shape=jax.ShapeDtypeStruct((N_EXPERTS, BATCH * CAPACITY, D_MODEL), BF16),
        grid=(BATCH, N_EXPERTS),
        in_specs=[
            pl.BlockSpec((1, 1, SEQ), lambda b, e: (b * N_EXPERTS + e, 0, 0)),
            pl.BlockSpec((SEQ, D_MODEL), lambda b, e: (b, 0)),
        ],
        out_specs=pl.BlockSpec((1, CAPACITY, D_MODEL), lambda b, e: (e, b, 0)),
        compiler_params=_params(("arbitrary", "arbitrary")),
        name="dispatch",
    )(rank_e3, h2)


def _expert_kernel(xs_ref, wg_ref, wu_ref, wd_ref, y_ref, acc_ref, wg_bf, wu_bf, wd_bf):
    f = pl.program_id(1)
    wg_bf[...] = wg_ref[0].astype(BF16)
    wu_bf[...] = wu_ref[0].astype(BF16)
    wd_bf[...] = wd_ref[0].astype(BF16)

    @pl.when(f == 0)
    def _():
        acc_ref[...] = jnp.zeros_like(acc_ref)

    for r in range(BATCH * CAPACITY // ROW_CHUNK):
        rows = slice(r * ROW_CHUNK, (r + 1) * ROW_CHUNK)
        xs = xs_ref[0, rows, :]
        a = _dot(xs, wg_bf[...])
        u = _dot(xs, wu_bf[...])
        hidden = (a / (1.0 + jnp.exp(-a))) * u
        acc_ref[rows, :] += _dot(hidden.astype(BF16), wd_bf[...])

    @pl.when(f == pl.num_programs(1) - 1)
    def _():
        y_ref[0] = acc_ref[...].astype(BF16)


def _experts(xs, w_gate, w_up, w_down):
    rows = BATCH * CAPACITY
    return pl.pallas_call(
        _expert_kernel,
        out_shape=jax.ShapeDtypeStruct((N_EXPERTS, rows, D_MODEL), BF16),
        grid=(N_EXPERTS, EXPERT_HIDDEN // FFN_TF),
        in_specs=[
            pl.BlockSpec((1, rows, D_MODEL), lambda e, f: (e, 0, 0)),
            pl.BlockSpec((1, D_MODEL, FFN_TF), lambda e, f: (e, 0, f)),
            pl.BlockSpec((1, D_MODEL, FFN_TF), lambda e, f: (e, 0, f)),
            pl.BlockSpec((1, FFN_TF, D_MODEL), lambda e, f: (e, f, 0)),
        ],
        out_specs=pl.BlockSpec((1, rows, D_MODEL), lambda e, f: (e, 0, 0)),
        scratch_shapes=[
            pltpu.VMEM((rows, D_MODEL), F32),
            pltpu.VMEM((D_MODEL, FFN_TF), BF16),
            pltpu.VMEM((D_MODEL, FFN_TF), BF16),
            pltpu.VMEM((FFN_TF, D_MODEL), BF16),
        ],
        compiler_params=_params(("arbitrary", "arbitrary")),
        name="experts",
    )(xs, w_gate, w_up, w_down)


def _combine_kernel(rank_ref, gate_ref, y_ref, x1_ref, mod_ref, fg_ref, o_ref):
    slot = lax.broadcasted_iota(jnp.int32, (1, CAPACITY), 1).astype(F32)
    acc = jnp.zeros((COMBINE_TQ, D_MODEL), F32)
    for e in range(N_EXPERTS):
        rank = rank_ref[0, :, e:e + 1]
        gate = gate_ref[0, :, e:e + 1]
        scatter = jnp.where(rank == slot, gate, 0.0).astype(BF16)
        acc = acc + _dot(scatter, y_ref[e])
    x2 = x1_ref[...] + mod_ref[0, 5:6, :] * acc
    ms = jnp.mean(x2 * x2, axis=-1, keepdims=True)
    o_ref[...] = x2 * lax.rsqrt(ms + EPS) * fg_ref[...]


def _combine(rank_t, gate_t, y, x1, mod3, final_g):
    per_b = SEQ // COMBINE_TQ
    return pl.pallas_call(
        _combine_kernel,
        out_shape=jax.ShapeDtypeStruct((BATCH * SEQ, D_MODEL), F32),
        grid=(BATCH, per_b),
        in_specs=[
            pl.BlockSpec((1, COMBINE_TQ, N_EXPERTS), lambda b, t: (b, t, 0)),
            pl.BlockSpec((1, COMBINE_TQ, N_EXPERTS), lambda b, t: (b, t, 0)),
            pl.BlockSpec((N_EXPERTS, CAPACITY, D_MODEL), lambda b, t: (0, b, 0)),
            pl.BlockSpec((COMBINE_TQ, D_MODEL), lambda b, t: (b * per_b + t, 0)),
            pl.BlockSpec((1, 6, D_MODEL), lambda b, t: (b, 0, 0)),
            pl.BlockSpec((1, D_MODEL), lambda b, t: (0, 0)),
        ],
        out_specs=pl.BlockSpec((COMBINE_TQ, D_MODEL), lambda b, t: (b * per_b + t, 0)),
        compiler_params=_params(("arbitrary", "arbitrary")),
        name="combine",
    )(rank_t, gate_t, y, x1, mod3, final_g)


def _rope_tables():
    t = jnp.arange(SEQ)
    pos = jnp.stack([t // GRID_W, t % GRID_W], axis=1).astype(F32)
    per_axis = HEAD_DIM // 2
    inv = ROPE_THETA ** (-jnp.arange(0, per_axis, 2, dtype=F32) / per_axis)
    ang = pos[:, :, None] * inv
    cos, sin = jnp.cos(ang), jnp.sin(ang)
    zero = jnp.zeros_like(sin)

    def lanes(first_half, second_half):
        one_map = jnp.stack([first_half, second_half], axis=2).reshape(SEQ, HEAD_DIM)
        return jnp.concatenate([one_map, one_map], axis=1)

    return lanes(cos, cos), lanes(-sin, zero), lanes(zero, sin)


def kernel(x, c, ctx, c_ctx, norm1_g, norm2_g, w_ada, b_ada, w_in, conv_w, w_out_conv, lambda_q1,
           lambda_k1, lambda_q2, lambda_k2, subln_g, w_o_attn, w_out, w_router, w_gate_e, w_up_e,
           w_down_e, final_g):
    cond = jnp.concatenate([c, c_ctx[None, :], jnp.zeros((COND_ROWS - BATCH - 1, D_MODEL), F32)], axis=0)
    mod3 = _ada(cond, w_ada[0], b_ada).reshape(COND_ROWS, 6, D_MODEL)

    w_in_bf = w_in[0].astype(BF16)
    cos, sin_hi, sin_lo = _rope_tables()
    q_scale = math.log2(math.e) * HEAD_DIM ** -0.5
    parts = _inproj(x, mod3, norm1_g, w_in_bf, conv_w[0], cos, sin_hi, sin_lo, q_scale)
    ctx_kv = _ctxproj(ctx.reshape(BATCH * CTX_LEN, D_MODEL), mod3, norm1_g, w_in_bf)

    lam_params = jnp.concatenate([lambda_q1, lambda_k1, lambda_q2, lambda_k2], axis=0)
    attn_o = _attn(lam_params, parts, ctx_kv, subln_g)

    wr_t = w_router[0].T
    wr_hi = wr_t.astype(BF16)
    wr_lo = (wr_t - wr_hi.astype(F32)).astype(BF16)
    x1, h2, logits_t = _merge(parts, attn_o, x.reshape(BATCH * SEQ, D_MODEL), mod3, norm2_g,
                              w_out_conv[0].astype(BF16), w_o_attn[0].astype(BF16), w_out[0].astype(BF16),
                              wr_hi, wr_lo)

    rank_e, rank_t, gate_t = _route(logits_t)
    xs = _dispatch(rank_e.reshape(BATCH * N_EXPERTS, 1, SEQ), h2)
    y = _experts(xs, w_gate_e[0], w_up_e[0], w_down_e[0])
    out = _combine(rank_t, gate_t, y, x1, mod3, final_g[None, :])
    return out.reshape(BATCH, SEQ, D_MODEL)
```

```python
import functools
import math

import jax
import jax.numpy as jnp
from jax import lax
from jax.experimental import pallas as pl
from jax.experimental.pallas import tpu as pltpu

D_MODEL = 1024
BATCH = 8
SEQ = 2048
GRID_W = 64
CTX_LEN = 256
N_HEADS = 8
HEAD_DIM = 64
V_DIM = 2 * HEAD_DIM
N_EXPERTS = 16
EXPERT_HIDDEN = 2048
CAPACITY = 2 * SEQ // N_EXPERTS
ROPE_THETA = 10000.0
EPS = 1e-6
LAM_INIT = 0.8 - 0.6 * math.exp(-0.3 * 0)
N_PARTS = 8
N_KEYS = CTX_LEN + SEQ

LANES = 128
F32 = jnp.float32
BF16 = jnp.bfloat16

ROW_CHUNK = 512
COND_ROWS = 16
ADA_TN = 512
ATTN_TQ = 2048
ATTN_SQ = 256
MERGE_TM = 512
FFN_TF = 512
COMBINE_TQ = 512
VMEM_LIMIT = 56 * 1024 * 1024


def _dot(a, b):
    return jnp.dot(a, b, preferred_element_type=F32)


def _dot_nt(a, b):
    return lax.dot_general(a, b, (((1,), (1,)), ((), ())), preferred_element_type=F32)


def _params(sem, vmem=VMEM_LIMIT):
    return pltpu.CompilerParams(dimension_semantics=sem, vmem_limit_bytes=vmem)


def _ada_kernel(cond_ref, w_ref, b_ref, o_ref):
    c = cond_ref[...]
    s = c / (1.0 + jnp.exp(-c))
    o_ref[...] = _dot(s.astype(BF16), w_ref[...].astype(BF16)) + b_ref[...]


def _ada(cond, w_ada, b_ada):
    n = w_ada.shape[1]
    return pl.pallas_call(
        _ada_kernel,
        out_shape=jax.ShapeDtypeStruct((COND_ROWS, n), F32),
        grid=(n // ADA_TN,),
        in_specs=[
            pl.BlockSpec((COND_ROWS, D_MODEL), lambda i: (0, 0)),
            pl.BlockSpec((D_MODEL, ADA_TN), lambda i: (0, i)),
            pl.BlockSpec((1, ADA_TN), lambda i: (0, i)),
        ],
        out_specs=pl.BlockSpec((COND_ROWS, ADA_TN), lambda i: (0, i)),
        compiler_params=_params(("arbitrary",)),
        name="ada",
    )(cond, w_ada, b_ada)


def _norm_modulate(x, g, shift, scale):
    ms = jnp.mean(x * x, axis=-1, keepdims=True)
    return (x * lax.rsqrt(ms + EPS) * g) * (1.0 + scale) + shift


def _rope(a, cos, sin_hi, sin_lo):
    return a * cos + pltpu.roll(a, LANES - 16, 1) * sin_hi + pltpu.roll(a, 16, 1) * sin_lo


def _inproj_kernel(x_ref, mod_ref, g_ref, w_ref, cw_ref, cos_ref, shi_ref, slo_ref, o_ref,
                   hx_ref, z_ref, *, q_scale):
    j = pl.program_id(1)
    chunks = [slice(r * ROW_CHUNK, (r + 1) * ROW_CHUNK) for r in range(SEQ // ROW_CHUNK)]

    @pl.when(j == 0)
    def _():
        shift = mod_ref[0, 0:1, :]
        scale = mod_ref[0, 1:2, :]
        for rows in chunks:
            hx_ref[rows, :] = _norm_modulate(x_ref[0, rows, :], g_ref[...], shift, scale).astype(BF16)
        z_ref[0:8, :] = jnp.zeros((8, D_MODEL), F32)
        z_ref[SEQ + 8:SEQ + 16, :] = jnp.zeros((8, D_MODEL), F32)
        for rows in chunks:
            z_ref[pl.ds(rows.start + 8, ROW_CHUNK), :] = _dot(hx_ref[rows, :], w_ref[...])

    @pl.when(j == 1)
    def _():
        for rows in chunks:
            zr = pl.ds(rows.start + 8, ROW_CHUNK)
            z_ref[zr, :] = z_ref[zr, :] * _dot(hx_ref[rows, :], w_ref[...])

    @pl.when(j == 2)
    def _():
        for rows in chunks:
            zp = z_ref[pl.ds(rows.start + 7, ROW_CHUNK), :]
            zc = z_ref[pl.ds(rows.start + 8, ROW_CHUNK), :]
            zn = z_ref[pl.ds(rows.start + 9, ROW_CHUNK), :]
            y = zp * cw_ref[0:1, :] + zc * cw_ref[1:2, :] + zn * cw_ref[2:3, :]
            o_ref[0, rows, :] = (_dot(hx_ref[rows, :], w_ref[...]) * y).astype(BF16)

    def rope_part(scale):
        for rows in chunks:
            acc = _dot(hx_ref[rows, :], w_ref[...])
            cos, shi, slo = cos_ref[rows, :], shi_ref[rows, :], slo_ref[rows, :]
            for h in range(D_MODEL // LANES):
                cols = slice(h * LANES, (h + 1) * LANES)
                r = _rope(acc[:, cols], cos, shi, slo)
                if scale is not None:
                    r = r * scale
                o_ref[0, rows, cols] = r.astype(BF16)

    @pl.when(j == 3)
    def _():
        rope_part(q_scale)

    @pl.when(j == 4)
    def _():
        rope_part(None)

    @pl.when(j == 5)
    def _():
        for rows in chunks:
            o_ref[0, rows, :] = _dot(hx_ref[rows, :], w_ref[...]).astype(BF16)

    @pl.when(j >= 6)
    def _():
        for rows in chunks:
            acc = _dot(hx_ref[rows, :], w_ref[...])
            o_ref[0, rows, :] = (1.0 / (1.0 + jnp.exp(-acc))).astype(BF16)


def _inproj(x, mod3, norm_g, w_in_bf, conv_w, cos, sin_hi, sin_lo, q_scale):
    def w_map(b, j):
        return (0, jnp.where(j == 1, 2, jnp.where(j == 2, 1, j)))

    tab = pl.BlockSpec((SEQ, LANES), lambda b, j: (0, 0))
    return pl.pallas_call(
        functools.partial(_inproj_kernel, q_scale=q_scale),
        out_shape=jax.ShapeDtypeStruct((N_PARTS - 2, BATCH * SEQ, D_MODEL), BF16),
        grid=(BATCH, N_PARTS),
        in_specs=[
            pl.BlockSpec((1, SEQ, D_MODEL), lambda b, j: (b, 0, 0)),
            pl.BlockSpec((1, 6, D_MODEL), lambda b, j: (b, 0, 0)),
            pl.BlockSpec((1, D_MODEL), lambda b, j: (0, 0)),
            pl.BlockSpec((D_MODEL, D_MODEL), w_map),
            pl.BlockSpec((3, D_MODEL), lambda b, j: (0, 0)),
            tab, tab, tab,
        ],
        out_specs=pl.BlockSpec((1, SEQ, D_MODEL), lambda b, j: (jnp.maximum(j - 2, 0), b, 0)),
        scratch_shapes=[pltpu.VMEM((SEQ, D_MODEL), BF16), pltpu.VMEM((SEQ + 16, D_MODEL), F32)],
        compiler_params=_params(("arbitrary", "arbitrary")),
        name="inproj",
    )(x, mod3, norm_g, w_in_bf, conv_w, cos, sin_hi, sin_lo)


def _ctxproj_kernel(c_ref, mod_ref, g_ref, w_ref, o_ref, hc_ref):
    j = pl.program_id(0)
    chunks = [slice(r * ROW_CHUNK, (r + 1) * ROW_CHUNK) for r in range(BATCH * CTX_LEN // ROW_CHUNK)]

    @pl.when(j == 0)
    def _():
        shift = mod_ref[0, 0:1, :]
        scale = mod_ref[0, 1:2, :]
        for rows in chunks:
            hc_ref[rows, :] = _norm_modulate(c_ref[rows, :], g_ref[...], shift, scale).astype(BF16)

    for rows in chunks:
        o_ref[0, rows, :] = _dot(hc_ref[rows, :], w_ref[...]).astype(BF16)


def _ctxproj(ctx2, mod3, norm_g, w_in_bf):
    rows = BATCH * CTX_LEN
    return pl.pallas_call(
        _ctxproj_kernel,
        out_shape=jax.ShapeDtypeStruct((2, rows, D_MODEL), BF16),
        grid=(2,),
        in_specs=[
            pl.BlockSpec((rows, D_MODEL), lambda j: (0, 0)),
            pl.BlockSpec((1, 6, D_MODEL), lambda j: (BATCH, 0, 0)),
            pl.BlockSpec((1, D_MODEL), lambda j: (0, 0)),
            pl.BlockSpec((D_MODEL, D_MODEL), lambda j: (0, 4 + j)),
        ],
        out_specs=pl.BlockSpec((1, rows, D_MODEL), lambda j: (j, 0, 0)),
        scratch_shapes=[pltpu.VMEM((rows, D_MODEL), BF16)],
        compiler_params=_params(("arbitrary",)),
        name="ctxproj",
    )(ctx2, mod3, norm_g, w_in_bf)


def _attn_kernel(lp_ref, q_ref, k_ref, v_ref, kc_ref, vc_ref, g_ref, o_ref, k_all, v_ext, s_ref, m_ref):
    lp = lp_ref[...]
    lam = (jnp.exp(jnp.sum(lp[0:1] * lp[1:2], axis=-1, keepdims=True))
           - jnp.exp(jnp.sum(lp[2:3] * lp[3:4], axis=-1, keepdims=True)) + LAM_INIT)
    k_all[0:CTX_LEN, :] = kc_ref[0]
    k_all[CTX_LEN:N_KEYS, :] = k_ref[0]
    v_ext[0:CTX_LEN, 0:LANES] = vc_ref[0]
    v_ext[CTX_LEN:N_KEYS, 0:LANES] = v_ref[0]
    v_ext[:, LANES:2 * LANES] = jnp.ones((N_KEYS, LANES), BF16)
    first_map = lax.broadcasted_iota(jnp.int32, (1, LANES), 1) < HEAD_DIM

    def scores(u):
        q = q_ref[0, u * ATTN_SQ:(u + 1) * ATTN_SQ, :]
        zero = jnp.zeros_like(q)
        for mp, qm in enumerate((jnp.where(first_map, q, zero), jnp.where(first_map, zero, q))):
            s = _dot_nt(qm, k_all[...])
            s_ref[u % 2, mp] = s
            m_ref[u % 2, mp] = jnp.max(s, axis=-1, keepdims=True)

    def values(u):
        o = []
        for mp in range(2):
            p = jnp.exp2(s_ref[u % 2, mp] - m_ref[u % 2, mp])
            o.append(_dot(p.astype(BF16), v_ext[...]))
        a = o[0][:, :LANES] / o[0][:, LANES:] - o[1][:, :LANES] * (lam / o[1][:, LANES:])
        ms = jnp.mean(a * a, axis=-1, keepdims=True)
        o_ref[u * ATTN_SQ:(u + 1) * ATTN_SQ, :] = (
            (a * lax.rsqrt(ms + EPS) * g_ref[...]) * (1.0 - LAM_INIT)).astype(BF16)

    n_units = ATTN_TQ // ATTN_SQ
    scores(0)
    for u in range(n_units):
        if u + 1 < n_units:
            scores(u + 1)
        values(u)


def _attn(lam_params, parts, ctx_kv, subln_g):
    nq = SEQ // ATTN_TQ
    return pl.pallas_call(
        _attn_kernel,
        out_shape=jax.ShapeDtypeStruct((BATCH * SEQ, D_MODEL), BF16),
        grid=(BATCH, N_HEADS, nq),
        in_specs=[
            pl.BlockSpec((4, HEAD_DIM), lambda b, h, i: (0, 0)),
            pl.BlockSpec((1, ATTN_TQ, LANES), lambda b, h, i: (1, b * nq + i, h)),
            pl.BlockSpec((1, SEQ, LANES), lambda b, h, i: (2, b, h)),
            pl.BlockSpec((1, SEQ, LANES), lambda b, h, i: (3, b, h)),
            pl.BlockSpec((1, CTX_LEN, LANES), lambda b, h, i: (0, b, h)),
            pl.BlockSpec((1, CTX_LEN, LANES), lambda b, h, i: (1, b, h)),
            pl.BlockSpec((1, V_DIM), lambda b, h, i: (0, 0)),
        ],
        out_specs=pl.BlockSpec((ATTN_TQ, LANES), lambda b, h, i: (b * nq + i, h)),
        scratch_shapes=[
            pltpu.VMEM((N_KEYS, LANES), BF16),
            pltpu.VMEM((N_KEYS, 2 * LANES), BF16),
            pltpu.VMEM((2, 2, ATTN_SQ, N_KEYS), F32),
            pltpu.VMEM((2, 2, ATTN_SQ, 1), F32),
        ],
        compiler_params=_params(("arbitrary", "arbitrary", "arbitrary")),
        name="attn",
    )(lam_params, parts, parts, parts, ctx_kv, ctx_kv, subln_g)


def _merge_kernel(yb_ref, s6_ref, s7_ref, o_ref, x_ref, mod_ref, g2_ref, wc_ref, wa_ref, wo_ref,
                  wrh_ref, wrl_ref, x1_ref, h2_ref, lg_ref):
    y_conv = _dot(yb_ref[0], wc_ref[...])
    y_attn = _dot(o_ref[...], wa_ref[...])
    merged = s6_ref[0].astype(F32) * y_conv + s7_ref[0].astype(F32) * y_attn
    mix = _dot(merged.astype(BF16), wo_ref[...])
    x1 = x_ref[...] + mod_ref[0, 2:3, :] * mix
    x1_ref[...] = x1
    h2 = _norm_modulate(x1, g2_ref[...], mod_ref[0, 3:4, :], mod_ref[0, 4:5, :])
    h_hi = h2.astype(BF16)
    h2_ref[...] = h_hi
    h_lo = (h2 - h_hi.astype(F32)).astype(BF16)
    lg_ref[0] = _dot_nt(wrh_ref[...], h_hi) + (_dot_nt(wrh_ref[...], h_lo) + _dot_nt(wrl_ref[...], h_hi))


def _merge(parts, attn_o, x2, mod3, norm2_g, w_conv_bf, w_attn_bf, w_out_bf, wr_hi, wr_lo):
    per_b = SEQ // MERGE_TM
    sq = pl.BlockSpec((D_MODEL, D_MODEL), lambda i: (0, 0))
    wr = pl.BlockSpec((N_EXPERTS, D_MODEL), lambda i: (0, 0))
    tile = pl.BlockSpec((MERGE_TM, D_MODEL), lambda i: (i, 0))
    return pl.pallas_call(
        _merge_kernel,
        out_shape=(
            jax.ShapeDtypeStruct((BATCH * SEQ, D_MODEL), F32),
            jax.ShapeDtypeStruct((BATCH * SEQ, D_MODEL), BF16),
            jax.ShapeDtypeStruct((BATCH, N_EXPERTS, SEQ), F32),
        ),
        grid=(BATCH * per_b,),
        in_specs=[
            pl.BlockSpec((1, MERGE_TM, D_MODEL), lambda i: (0, i, 0)),
            pl.BlockSpec((1, MERGE_TM, D_MODEL), lambda i: (4, i, 0)),
            pl.BlockSpec((1, MERGE_TM, D_MODEL), lambda i: (5, i, 0)),
            tile, tile,
            pl.BlockSpec((1, 6, D_MODEL), lambda i: (i // per_b, 0, 0)),
            pl.BlockSpec((1, D_MODEL), lambda i: (0, 0)),
            sq, sq, sq, wr, wr,
        ],
        out_specs=(
            tile, tile,
            pl.BlockSpec((1, N_EXPERTS, MERGE_TM), lambda i: (i // per_b, 0, i % per_b)),
        ),
        compiler_params=_params(("arbitrary",)),
        name="merge",
    )(parts, parts, parts, attn_o, x2, mod3, norm2_g, w_conv_bf, w_attn_bf, w_out_bf, wr_hi, wr_lo)


def _route_kernel(lg_ref, rank_e_ref, rank_t_ref, gate_t_ref):
    lg = lg_ref[...]
    ex = jnp.exp(lg - jnp.max(lg, axis=1, keepdims=True))
    aff = (ex / jnp.sum(ex, axis=1, keepdims=True)).reshape(BATCH * N_EXPERTS, SEQ)
    bits = lax.bitcast_convert_type(aff, jnp.int32)

    def count(mask):
        return jnp.sum(jnp.where(mask, 1.0, 0.0), axis=1, keepdims=True)

    def search(i, t):
        cand = t + lax.shift_left(jnp.int32(1), 30 - i)
        return jnp.where(count(bits >= cand) >= CAPACITY, cand, t)

    thr = lax.fori_loop(0, 31, search, jnp.zeros((BATCH * N_EXPERTS, 1), jnp.int32))
    need = CAPACITY - count(bits > thr)
    before = (lax.broadcasted_iota(jnp.int32, (LANES, LANES), 0)
              < lax.broadcasted_iota(jnp.int32, (LANES, LANES), 1)).astype(BF16)
    n_tied = jnp.zeros((BATCH * N_EXPERTS, 1), F32)
    n_sel = jnp.zeros((BATCH * N_EXPERTS, 1), F32)
    for c in range(SEQ // LANES):
        cols = slice(c * LANES, (c + 1) * LANES)
        bits_c = bits[:, cols]
        tied_c = jnp.where(bits_c == thr, 1.0, 0.0)
        tied_before = _dot(tied_c.astype(BF16), before) + n_tied
        sel_c = jnp.where(bits_c > thr, 1.0, jnp.where(tied_before < need, tied_c, 0.0))
        rank = _dot(sel_c.astype(BF16), before) + n_sel
        rank_c = jnp.where(sel_c > 0.0, rank, -1.0)
        gate_c = jnp.where(sel_c > 0.0, aff[:, cols], 0.0)
        n_tied = n_tied + jnp.sum(tied_c, axis=1, keepdims=True)
        n_sel = n_sel + jnp.sum(sel_c, axis=1, keepdims=True)
        rank_e_ref[:, cols] = rank_c
        rank_ct = rank_c.T
        gate_ct = gate_c.T
        for b in range(BATCH):
            ecols = slice(b * N_EXPERTS, (b + 1) * N_EXPERTS)
            rank_t_ref[b, cols, :] = rank_ct[:, ecols]
            gate_t_ref[b, cols, :] = gate_ct[:, ecols]


def _route(logits_t):
    return pl.pallas_call(
        _route_kernel,
        out_shape=(
            jax.ShapeDtypeStruct((BATCH * N_EXPERTS, SEQ), F32),
            jax.ShapeDtypeStruct((BATCH, SEQ, N_EXPERTS), F32),
            jax.ShapeDtypeStruct((BATCH, SEQ, N_EXPERTS), F32),
        ),
        compiler_params=pltpu.CompilerParams(vmem_limit_bytes=VMEM_LIMIT),
        name="route",
    )(logits_t)


def _dispatch_kernel(rank_ref, h_ref, o_ref):
    slot = lax.broadcasted_iota(jnp.int32, (CAPACITY, SEQ), 0).astype(F32)
    onehot = jnp.where(rank_ref[0] == slot, 1.0, 0.0).astype(BF16)
    o_ref[0] = _dot(onehot, h_ref[...]).astype(BF16)


def _dispatch(rank_e3, h2):
    return pl.pallas_call(
        _dispatch_kernel,
        out_shape=jax.ShapeDtypeStruct((N_EXPERTS, BATCH * CAPACITY, D_MODEL), BF16),
        grid=(BATCH, N_EXPERTS),
        in_specs=[
            pl.BlockSpec((1, 1, SEQ), lambda b, e: (b * N_EXPERTS + e, 0, 0)),
            pl.BlockSpec((SEQ, D_MODEL), lambda b, e: (b, 0)),
        ],
        out_specs=pl.BlockSpec((1, CAPACITY, D_MODEL), lambda b, e: (e, b, 0)),
        compiler_params=_params(("arbitrary", "arbitrary")),
        name="dispatch",
    )(rank_e3, h2)


def _expert_kernel(xs_ref, wg_ref, wu_ref, wd_ref, y_ref, acc_ref, wg_bf, wu_bf, wd_bf):
    f = pl.program_id(1)
    wg_bf[...] = wg_ref[0].astype(BF16)
    wu_bf[...] = wu_ref[0].astype(BF16)
    wd_bf[...] = wd_ref[0].astype(BF16)

    @pl.when(f == 0)
    def _():
        acc_ref[...] = jnp.zeros_like(acc_ref)

    for r in range(BATCH * CAPACITY // ROW_CHUNK):
        rows = slice(r * ROW_CHUNK, (r + 1) * ROW_CHUNK)
        xs = xs_ref[0, rows, :]
        a = _dot(xs, wg_bf[...])
        u = _dot(xs, wu_bf[...])
        hidden = (a / (1.0 + jnp.exp(-a))) * u
        acc_ref[rows, :] += _dot(hidden.astype(BF16), wd_bf[...])

    @pl.when(f == pl.num_programs(1) - 1)
    def _():
        y_ref[0] = acc_ref[...].astype(BF16)


def _experts(xs, w_gate, w_up, w_down):
    rows = BATCH * CAPACITY
    return pl.pallas_call(
        _expert_kernel,
        out_shape=jax.ShapeDtypeStruct((N_EXPERTS, rows, D_MODEL), BF16),
        grid=(N_EXPERTS, EXPERT_HIDDEN // FFN_TF),
        in_specs=[
            pl.BlockSpec((1, rows, D_MODEL), lambda e, f: (e, 0, 0)),
            pl.BlockSpec((1, D_MODEL, FFN_TF), lambda e, f: (e, 0, f)),
            pl.BlockSpec((1, D_MODEL, FFN_TF), lambda e, f: (e, 0, f)),
            pl.BlockSpec((1, FFN_TF, D_MODEL), lambda e, f: (e, f, 0)),
        ],
        out_specs=pl.BlockSpec((1, rows, D_MODEL), lambda e, f: (e, 0, 0)),
        scratch_shapes=[
            pltpu.VMEM((rows, D_MODEL), F32),
            pltpu.VMEM((D_MODEL, FFN_TF), BF16),
            pltpu.VMEM((D_MODEL, FFN_TF), BF16),
            pltpu.VMEM((FFN_TF, D_MODEL), BF16),
        ],
        compiler_params=_params(("arbitrary", "arbitrary")),
        name="experts",
    )(xs, w_gate, w_up, w_down)


def _combine_kernel(rank_ref, gate_ref, y_ref, x1_ref, mod_ref, fg_ref, o_ref):
    slot = lax.broadcasted_iota(jnp.int32, (1, CAPACITY), 1).astype(F32)
    acc = jnp.zeros((COMBINE_TQ, D_MODEL), F32)
    for e in range(N_EXPERTS):
        rank = rank_ref[0, :, e:e + 1]
        gate = gate_ref[0, :, e:e + 1]
        scatter = jnp.where(rank == slot, gate, 0.0).astype(BF16)
        acc = acc + _dot(scatter, y_ref[e])
    x2 = x1_ref[...] + mod_ref[0, 5:6, :] * acc
    ms = jnp.mean(x2 * x2, axis=-1, keepdims=True)
    o_ref[...] = x2 * lax.rsqrt(ms + EPS) * fg_ref[...]


def _combine(rank_t, gate_t, y, x1, mod3, final_g):
    per_b = SEQ // COMBINE_TQ
    return pl.pallas_call(
        _combine_kernel,
        out_shape=jax.ShapeDtypeStruct((BATCH * SEQ, D_MODEL), F32),
        grid=(BATCH, per_b),
        in_specs=[
            pl.BlockSpec((1, COMBINE_TQ, N_EXPERTS), lambda b, t: (b, t, 0)),
            pl.BlockSpec((1, COMBINE_TQ, N_EXPERTS), lambda b, t: (b, t, 0)),
            pl.BlockSpec((N_EXPERTS, CAPACITY, D_MODEL), lambda b, t: (0, b, 0)),
            pl.BlockSpec((COMBINE_TQ, D_MODEL), lambda b, t: (b * per_b + t, 0)),
            pl.BlockSpec((1, 6, D_MODEL), lambda b, t: (b, 0, 0)),
            pl.BlockSpec((1, D_MODEL), lambda b, t: (0, 0)),
        ],
        out_specs=pl.BlockSpec((COMBINE_TQ, D_MODEL), lambda b, t: (b * per_b + t, 0)),
        compiler_params=_params(("arbitrary", "arbitrary")),
        name="combine",
    )(rank_t, gate_t, y, x1, mod3, final_g)


def _rope_tables():
    t = jnp.arange(SEQ)
    pos = jnp.stack([t // GRID_W, t % GRID_W], axis=1).astype(F32)
    per_axis = HEAD_DIM // 2
    inv = ROPE_THETA ** (-jnp.arange(0, per_axis, 2, dtype=F32) / per_axis)
    ang = pos[:, :, None] * inv
    cos, sin = jnp.cos(ang), jnp.sin(ang)
    zero = jnp.zeros_like(sin)

    def lanes(first_half, second_half):
        one_map = jnp.stack([first_half, second_half], axis=2).reshape(SEQ, HEAD_DIM)
        return jnp.concatenate([one_map, one_map], axis=1)

    return lanes(cos, cos), lanes(-sin, zero), lanes(zero, sin)


def kernel(x, c, ctx, c_ctx, norm1_g, norm2_g, w_ada, b_ada, w_in, conv_w, w_out_conv, lambda_q1,
           lambda_k1, lambda_q2, lambda_k2, subln_g, w_o_attn, w_out, w_router, w_gate_e, w_up_e,
           w_down_e, final_g):
    cond = jnp.concatenate([c, c_ctx[None, :], jnp.zeros((COND_ROWS - BATCH - 1, D_MODEL), F32)], axis=0)
    mod3 = _ada(cond, w_ada[0], b_ada).reshape(COND_ROWS, 6, D_MODEL)

    w_in_bf = w_in[0].astype(BF16)
    cos, sin_hi, sin_lo = _rope_tables()
    q_scale = math.log2(math.e) * HEAD_DIM ** -0.5
    parts = _inproj(x, mod3, norm1_g, w_in_bf, conv_w[0], cos, sin_hi, sin_lo, q_scale)
    ctx_kv = _ctxproj(ctx.reshape(BATCH * CTX_LEN, D_MODEL), mod3, norm1_g, w_in_bf)

    lam_params = jnp.concatenate([lambda_q1, lambda_k1, lambda_q2, lambda_k2], axis=0)
    attn_o = _attn(lam_params, parts, ctx_kv, subln_g)

    wr_t = w_router[0].T
    wr_hi = wr_t.astype(BF16)
    wr_lo = (wr_t - wr_hi.astype(F32)).astype(BF16)
    x1, h2, logits_t = _merge(parts, attn_o, x.reshape(BATCH * SEQ, D_MODEL), mod3, norm2_g,
                              w_out_conv[0].astype(BF16), w_o_attn[0].astype(BF16), w_out[0].astype(BF16),
                              wr_hi, wr_lo)

    rank_e, rank_t, gate_t = _route(logits_t)
    xs = _dispatch(rank_e.reshape(BATCH * N_EXPERTS, 1, SEQ), h2)
    y = _experts(xs, w_gate_e[0], w_up_e[0], w_down_e[0])
    out = _combine(rank_t, gate_t, y, x1, mod3, final_g[None, :])
    return out.reshape(BATCH, SEQ, D_MODEL)
```

```python
import functools
import math

import jax
import jax.numpy as jnp
from jax import lax
from jax.experimental import pallas as pl
from jax.experimental.pallas import tpu as pltpu

D_MODEL = 1024
BATCH = 8
SEQ = 2048
GRID_W = 64
CTX_LEN = 256
N_HEADS = 8
HEAD_DIM = 64
V_DIM = 2 * HEAD_DIM
N_EXPERTS = 16
EXPERT_HIDDEN = 2048
CAPACITY = 2 * SEQ // N_EXPERTS
ROPE_THETA = 10000.0
EPS = 1e-6
LAM_INIT = 0.8 - 0.6 * math.exp(-0.3 * 0)
N_PARTS = 8
N_KEYS = CTX_LEN + SEQ

LANES = 128
F32 = jnp.float32
BF16 = jnp.bfloat16

ROW_CHUNK = 512
COND_ROWS = 16
ADA_TN = 512
ATTN_TQ = 2048
ATTN_SQ = 256
MERGE_TM = 512
FFN_TF = 512
COMBINE_TQ = 512
VMEM_LIMIT = 56 * 1024 * 1024


def _dot(a, b):
    return jnp.dot(a, b, preferred_element_type=F32)


def _dot_nt(a, b):
    return lax.dot_general(a, b, (((1,), (1,)), ((), ())), preferred_element_type=F32)


def _params(sem, vmem=VMEM_LIMIT):
    return pltpu.CompilerParams(dimension_semantics=sem, vmem_limit_bytes=vmem)


def _ada_kernel(cond_ref, w_ref, b_ref, o_ref):
    c = cond_ref[...]
    s = c / (1.0 + jnp.exp(-c))
    o_ref[...] = _dot(s.astype(BF16), w_ref[...].astype(BF16)) + b_ref[...]


def _ada(cond, w_ada, b_ada):
    n = w_ada.shape[1]
    return pl.pallas_call(
        _ada_kernel,
        out_shape=jax.ShapeDtypeStruct((COND_ROWS, n), F32),
        grid=(n // ADA_TN,),
        in_specs=[
            pl.BlockSpec((COND_ROWS, D_MODEL), lambda i: (0, 0)),
            pl.BlockSpec((D_MODEL, ADA_TN), lambda i: (0, i)),
            pl.BlockSpec((1, ADA_TN), lambda i: (0, i)),
        ],
        out_specs=pl.BlockSpec((COND_ROWS, ADA_TN), lambda i: (0, i)),
        compiler_params=_params(("arbitrary",)),
        name="ada",
    )(cond, w_ada, b_ada)


def _norm_modulate(x, g, shift, scale):
    ms = jnp.mean(x * x, axis=-1, keepdims=True)
    return (x * lax.rsqrt(ms + EPS) * g) * (1.0 + scale) + shift


def _rope(a, cos, sin_hi, sin_lo):
    return a * cos + pltpu.roll(a, LANES - 16, 1) * sin_hi + pltpu.roll(a, 16, 1) * sin_lo


def _inproj_kernel(x_ref, mod_ref, g_ref, w_ref, cw_ref, cos_ref, shi_ref, slo_ref, o_ref,
                   hx_ref, z_ref, *, q_scale):
    j = pl.program_id(1)
    chunks = [slice(r * ROW_CHUNK, (r + 1) * ROW_CHUNK) for r in range(SEQ // ROW_CHUNK)]

    @pl.when(j == 0)
    def _():
        shift = mod_ref[0, 0:1, :]
        scale = mod_ref[0, 1:2, :]
        for rows in chunks:
            hx_ref[rows, :] = _norm_modulate(x_ref[0, rows, :], g_ref[...], shift, scale).astype(BF16)
        z_ref[0:8, :] = jnp.zeros((8, D_MODEL), F32)
        z_ref[SEQ + 8:SEQ + 16, :] = jnp.zeros((8, D_MODEL), F32)
        for rows in chunks:
            z_ref[pl.ds(rows.start + 8, ROW_CHUNK), :] = _dot(hx_ref[rows, :], w_ref[...])

    @pl.when(j == 1)
    def _():
        for rows in chunks:
            zr = pl.ds(rows.start + 8, ROW_CHUNK)
            z_ref[zr, :] = z_ref[zr, :] * _dot(hx_ref[rows, :], w_ref[...])

    @pl.when(j == 2)
    def _():
        first_row = lax.broadcasted_iota(jnp.int32, (8, 1), 0) == 0
        last_row = lax.broadcasted_iota(jnp.int32, (8, 1), 0) == 7
        for rows in chunks:
            base = rows.start + 8
            zc = z_ref[pl.ds(base, ROW_CHUNK), :]
            before = pltpu.roll(zc * cw_ref[0:1, :], 1, 0)
            after = pltpu.roll(zc * cw_ref[2:3, :], ROW_CHUNK - 1, 0)
            edge_b = jnp.where(first_row, z_ref[pl.ds(base - 8, 8), :][7:8, :] * cw_ref[0:1, :], before[0:8, :])
            edge_a = jnp.where(last_row, z_ref[pl.ds(base + ROW_CHUNK, 8), :][0:1, :] * cw_ref[2:3, :],
                               after[ROW_CHUNK - 8:, :])
            before = jnp.concatenate([edge_b, before[8:, :]], axis=0)
            after = jnp.concatenate([after[:ROW_CHUNK - 8, :], edge_a], axis=0)
            y = before + zc * cw_ref[1:2, :] + after
            o_ref[0, rows, :] = (_dot(hx_ref[rows, :], w_ref[...]) * y).astype(BF16)

    def rope_part(scale):
        for rows in chunks:
            acc = _dot(hx_ref[rows, :], w_ref[...])
            cos, shi, slo = cos_ref[rows, :], shi_ref[rows, :], slo_ref[rows, :]
            for h in range(D_MODEL // LANES):
                cols = slice(h * LANES, (h + 1) * LANES)
                r = _rope(acc[:, cols], cos, shi, slo)
                if scale is not None:
                    r = r * scale
                o_ref[0, rows, cols] = r.astype(BF16)

    @pl.when(j == 3)
    def _():
        rope_part(q_scale)

    @pl.when(j == 4)
    def _():
        rope_part(None)

    @pl.when(j == 5)
    def _():
        for rows in chunks:
            o_ref[0, rows, :] = _dot(hx_ref[rows, :], w_ref[...]).astype(BF16)

    @pl.when(j >= 6)
    def _():
        for rows in chunks:
            acc = _dot(hx_ref[rows, :], w_ref[...])
            o_ref[0, rows, :] = (1.0 / (1.0 + jnp.exp(-acc))).astype(BF16)


def _inproj(x, mod3, norm_g, w_in_bf, conv_w, cos, sin_hi, sin_lo, q_scale):
    def w_map(b, j):
        return (0, jnp.where(j == 1, 2, jnp.where(j == 2, 1, j)))

    tab = pl.BlockSpec((SEQ, LANES), lambda b, j: (0, 0))
    return pl.pallas_call(
        functools.partial(_inproj_kernel, q_scale=q_scale),
        out_shape=jax.ShapeDtypeStruct((N_PARTS - 2, BATCH * SEQ, D_MODEL), BF16),
        grid=(BATCH, N_PARTS),
        in_specs=[
            pl.BlockSpec((1, SEQ, D_MODEL), lambda b, j: (b, 0, 0)),
            pl.BlockSpec((1, 6, D_MODEL), lambda b, j: (b, 0, 0)),
            pl.BlockSpec((1, D_MODEL), lambda b, j: (0, 0)),
            pl.BlockSpec((D_MODEL, D_MODEL), w_map),
            pl.BlockSpec((3, D_MODEL), lambda b, j: (0, 0)),
            tab, tab, tab,
        ],
        out_specs=pl.BlockSpec((1, SEQ, D_MODEL), lambda b, j: (jnp.maximum(j - 2, 0), b, 0)),
        scratch_shapes=[pltpu.VMEM((SEQ, D_MODEL), BF16), pltpu.VMEM((SEQ + 16, D_MODEL), F32)],
        compiler_params=_params(("arbitrary", "arbitrary")),
        name="inproj",
    )(x, mod3, norm_g, w_in_bf, conv_w, cos, sin_hi, sin_lo)


def _ctxproj_kernel(c_ref, mod_ref, g_ref, w_ref, o_ref, hc_ref):
    j = pl.program_id(0)
    chunks = [slice(r * ROW_CHUNK, (r + 1) * ROW_CHUNK) for r in range(BATCH * CTX_LEN // ROW_CHUNK)]

    @pl.when(j == 0)
    def _():
        shift = mod_ref[0, 0:1, :]
        scale = mod_ref[0, 1:2, :]
        for rows in chunks:
            hc_ref[rows, :] = _norm_modulate(c_ref[rows, :], g_ref[...], shift, scale).astype(BF16)

    for rows in chunks:
        o_ref[0, rows, :] = _dot(hc_ref[rows, :], w_ref[...]).astype(BF16)


def _ctxproj(ctx2, mod3, norm_g, w_in_bf):
    rows = BATCH * CTX_LEN
    return pl.pallas_call(
        _ctxproj_kernel,
        out_shape=jax.ShapeDtypeStruct((2, rows, D_MODEL), BF16),
        grid=(2,),
        in_specs=[
            pl.BlockSpec((rows, D_MODEL), lambda j: (0, 0)),
            pl.BlockSpec((1, 6, D_MODEL), lambda j: (BATCH, 0, 0)),
            pl.BlockSpec((1, D_MODEL), lambda j: (0, 0)),
            pl.BlockSpec((D_MODEL, D_MODEL), lambda j: (0, 4 + j)),
        ],
        out_specs=pl.BlockSpec((1, rows, D_MODEL), lambda j: (j, 0, 0)),
        scratch_shapes=[pltpu.VMEM((rows, D_MODEL), BF16)],
        compiler_params=_params(("arbitrary",)),
        name="ctxproj",
    )(ctx2, mod3, norm_g, w_in_bf)


def _attn_kernel(lp_ref, q_ref, k_ref, v_ref, kc_ref, vc_ref, g_ref, o_ref, k_all, v_ext, s_ref, m_ref):
    lp = lp_ref[...]
    lam = (jnp.exp(jnp.sum(lp[0:1] * lp[1:2], axis=-1, keepdims=True))
           - jnp.exp(jnp.sum(lp[2:3] * lp[3:4], axis=-1, keepdims=True)) + LAM_INIT)
    k_all[0:CTX_LEN, :] = kc_ref[0]
    k_all[CTX_LEN:N_KEYS, :] = k_ref[0]
    v_ext[0:CTX_LEN, 0:LANES] = vc_ref[0]
    v_ext[CTX_LEN:N_KEYS, 0:LANES] = v_ref[0]
    v_ext[:, LANES:2 * LANES] = jnp.ones((N_KEYS, LANES), BF16)
    first_map = lax.broadcasted_iota(jnp.int32, (1, LANES), 1) < HEAD_DIM

    def scores(u):
        q = q_ref[0, u * ATTN_SQ:(u + 1) * ATTN_SQ, :]
        zero = jnp.zeros_like(q)
        for mp, qm in enumerate((jnp.where(first_map, q, zero), jnp.where(first_map, zero, q))):
            s = _dot_nt(qm, k_all[...])
            s_ref[u % 2, mp] = s
            m_ref[u % 2, mp] = jnp.max(s, axis=-1, keepdims=True)

    def values(u):
        o = []
        for mp in range(2):
            p = jnp.exp2(s_ref[u % 2, mp] - m_ref[u % 2, mp])
            o.append(_dot(p.astype(BF16), v_ext[...]))
        a = o[0][:, :LANES] / o[0][:, LANES:] - o[1][:, :LANES] * (lam / o[1][:, LANES:])
        ms = jnp.mean(a * a, axis=-1, keepdims=True)
        o_ref[u * ATTN_SQ:(u + 1) * ATTN_SQ, :] = (
            (a * lax.rsqrt(ms + EPS) * g_ref[...]) * (1.0 - LAM_INIT)).astype(BF16)

    n_units = ATTN_TQ // ATTN_SQ
    scores(0)
    for u in range(n_units):
        if u + 1 < n_units:
            scores(u + 1)
        values(u)


def _attn(lam_params, parts, ctx_kv, subln_g):
    nq = SEQ // ATTN_TQ
    return pl.pallas_call(
        _attn_kernel,
        out_shape=jax.ShapeDtypeStruct((BATCH * SEQ, D_MODEL), BF16),
        grid=(BATCH, N_HEADS, nq),
        in_specs=[
            pl.BlockSpec((4, HEAD_DIM), lambda b, h, i: (0, 0)),
            pl.BlockSpec((1, ATTN_TQ, LANES), lambda b, h, i: (1, b * nq + i, h)),
            pl.BlockSpec((1, SEQ, LANES), lambda b, h, i: (2, b, h)),
            pl.BlockSpec((1, SEQ, LANES), lambda b, h, i: (3, b, h)),
            pl.BlockSpec((1, CTX_LEN, LANES), lambda b, h, i: (0, b, h)),
            pl.BlockSpec((1, CTX_LEN, LANES), lambda b, h, i: (1, b, h)),
            pl.BlockSpec((1, V_DIM), lambda b, h, i: (0, 0)),
        ],
        out_specs=pl.BlockSpec((ATTN_TQ, LANES), lambda b, h, i: (b * nq + i, h)),
        scratch_shapes=[
            pltpu.VMEM((N_KEYS, LANES), BF16),
            pltpu.VMEM((N_KEYS, 2 * LANES), BF16),
            pltpu.VMEM((2, 2, ATTN_SQ, N_KEYS), F32),
            pltpu.VMEM((2, 2, ATTN_SQ, 1), F32),
        ],
        compiler_params=_params(("arbitrary", "arbitrary", "arbitrary")),
        name="attn",
    )(lam_params, parts, parts, parts, ctx_kv, ctx_kv, subln_g)


def _merge_kernel(yb_ref, s6_ref, s7_ref, o_ref, x_ref, mod_ref, g2_ref, wc_ref, wa_ref, wo_ref,
                  wrh_ref, wrl_ref, x1_ref, h2_ref, lg_ref):
    y_conv = _dot(yb_ref[0], wc_ref[...])
    y_attn = _dot(o_ref[...], wa_ref[...])
    merged = s6_ref[0].astype(F32) * y_conv + s7_ref[0].astype(F32) * y_attn
    mix = _dot(merged.astype(BF16), wo_ref[...])
    x1 = x_ref[...] + mod_ref[0, 2:3, :] * mix
    x1_ref[...] = x1
    h2 = _norm_modulate(x1, g2_ref[...], mod_ref[0, 3:4, :], mod_ref[0, 4:5, :])
    h_hi = h2.astype(BF16)
    h2_ref[...] = h_hi
    h_lo = (h2 - h_hi.astype(F32)).astype(BF16)
    lg_ref[0] = _dot_nt(wrh_ref[...], h_hi) + (_dot_nt(wrh_ref[...], h_lo) + _dot_nt(wrl_ref[...], h_hi))


def _merge(parts, attn_o, x2, mod3, norm2_g, w_conv_bf, w_attn_bf, w_out_bf, wr_hi, wr_lo):
    per_b = SEQ // MERGE_TM
    sq = pl.BlockSpec((D_MODEL, D_MODEL), lambda i: (0, 0), pipeline_mode=pl.Buffered(1))
    wr = pl.BlockSpec((N_EXPERTS, D_MODEL), lambda i: (0, 0), pipeline_mode=pl.Buffered(1))
    tile = pl.BlockSpec((MERGE_TM, D_MODEL), lambda i: (i, 0))
    return pl.pallas_call(
        _merge_kernel,
        out_shape=(
            jax.ShapeDtypeStruct((BATCH * SEQ, D_MODEL), F32),
            jax.ShapeDtypeStruct((BATCH * SEQ, D_MODEL), BF16),
            jax.ShapeDtypeStruct((BATCH, N_EXPERTS, SEQ), F32),
        ),
        grid=(BATCH * per_b,),
        in_specs=[
            pl.BlockSpec((1, MERGE_TM, D_MODEL), lambda i: (0, i, 0)),
            pl.BlockSpec((1, MERGE_TM, D_MODEL), lambda i: (4, i, 0)),
            pl.BlockSpec((1, MERGE_TM, D_MODEL), lambda i: (5, i, 0)),
            tile, tile,
            pl.BlockSpec((1, 6, D_MODEL), lambda i: (i // per_b, 0, 0)),
            pl.BlockSpec((1, D_MODEL), lambda i: (0, 0)),
            sq, sq, sq, wr, wr,
        ],
        out_specs=(
            tile, tile,
            pl.BlockSpec((1, N_EXPERTS, MERGE_TM), lambda i: (i // per_b, 0, i % per_b)),
        ),
        compiler_params=_params(("arbitrary",)),
        name="merge",
    )(parts, parts, parts, attn_o, x2, mod3, norm2_g, w_conv_bf, w_attn_bf, w_out_bf, wr_hi, wr_lo)


def _route_kernel(lg_ref, rank_e_ref, rank_t_ref, gate_t_ref):
    lg = lg_ref[...]
    ex = jnp.exp(lg - jnp.max(lg, axis=1, keepdims=True))
    aff = (ex / jnp.sum(ex, axis=1, keepdims=True)).reshape(BATCH * N_EXPERTS, SEQ)
    bits = lax.bitcast_convert_type(aff, jnp.int32)

    def count(mask):
        return jnp.sum(jnp.where(mask, 1.0, 0.0), axis=1, keepdims=True)

    def search(i, t):
        cand = t + lax.shift_left(jnp.int32(1), 30 - i)
        return jnp.where(count(bits >= cand) >= CAPACITY, cand, t)

    thr = lax.fori_loop(0, 31, search, jnp.zeros((BATCH * N_EXPERTS, 1), jnp.int32))
    need = CAPACITY - count(bits > thr)
    before = (lax.broadcasted_iota(jnp.int32, (LANES, LANES), 0)
              < lax.broadcasted_iota(jnp.int32, (LANES, LANES), 1)).astype(BF16)
    n_tied = jnp.zeros((BATCH * N_EXPERTS, 1), F32)
    n_sel = jnp.zeros((BATCH * N_EXPERTS, 1), F32)
    for c in range(SEQ // LANES):
        cols = slice(c * LANES, (c + 1) * LANES)
        bits_c = bits[:, cols]
        tied_c = jnp.where(bits_c == thr, 1.0, 0.0)
        tied_before = _dot(tied_c.astype(BF16), before) + n_tied
        sel_c = jnp.where(bits_c > thr, 1.0, jnp.where(tied_before < need, tied_c, 0.0))
        rank = _dot(sel_c.astype(BF16), before) + n_sel
        rank_c = jnp.where(sel_c > 0.0, rank, -1.0)
        gate_c = jnp.where(sel_c > 0.0, aff[:, cols], 0.0)
        n_tied = n_tied + jnp.sum(tied_c, axis=1, keepdims=True)
        n_sel = n_sel + jnp.sum(sel_c, axis=1, keepdims=True)
        rank_e_ref[:, cols] = rank_c
        rank_ct = rank_c.T
        gate_ct = gate_c.T
        for b in range(BATCH):
            ecols = slice(b * N_EXPERTS, (b + 1) * N_EXPERTS)
            rank_t_ref[b, cols, :] = rank_ct[:, ecols]
            gate_t_ref[b, cols, :] = gate_ct[:, ecols]


def _route(logits_t):
    return pl.pallas_call(
        _route_kernel,
        out_shape=(
            jax.ShapeDtypeStruct((BATCH * N_EXPERTS, SEQ), F32),
            jax.ShapeDtypeStruct((BATCH, SEQ, N_EXPERTS), F32),
            jax.ShapeDtypeStruct((BATCH, SEQ, N_EXPERTS), F32),
        ),
        compiler_params=pltpu.CompilerParams(vmem_limit_bytes=VMEM_LIMIT),
        name="route",
    )(logits_t)


def _dispatch_kernel(rank_ref, h_ref, o_ref):
    slot = lax.broadcasted_iota(jnp.int32, (CAPACITY, SEQ), 0).astype(F32)
    onehot = jnp.where(rank_ref[0] == slot, 1.0, 0.0).astype(BF16)
    o_ref[0] = _dot(onehot, h_ref[...]).astype(BF16)


def _dispatch(rank_e3, h2):
    return pl.pallas_call(
        _dispatch_kernel,
        out_shape=jax.ShapeDtypeStruct((N_EXPERTS, BATCH * CAPACITY, D_MODEL), BF16),
        grid=(BATCH, N_EXPERTS),
        in_specs=[
            pl.BlockSpec((1, 1, SEQ), lambda b, e: (b * N_EXPERTS + e, 0, 0)),
            pl.BlockSpec((SEQ, D_MODEL), lambda b, e: (b, 0)),
        ],
        out_specs=pl.BlockSpec((1, CAPACITY, D_MODEL), lambda b, e: (e, b, 0)),
        compiler_params=_params(("arbitrary", "arbitrary")),
        name="dispatch",
    )(rank_e3, h2)


def _expert_kernel(xs_ref, wg_ref, wu_ref, wd_ref, y_ref, acc_ref, wg_bf, wu_bf, wd_bf):
    t = pl.program_id(0)
    n_f = EXPERT_HIDDEN // FFN_TF
    f = jnp.where(t == 0, 0, (t - 1) % n_f)

    n_chunks = BATCH * CAPACITY // ROW_CHUNK

    def stage(slot, part):
        up = slice(part * D_MODEL // n_chunks, (part + 1) * D_MODEL // n_chunks)
        down = slice(part * FFN_TF // n_chunks, (part + 1) * FFN_TF // n_chunks)
        wg_bf[slot, up, :] = wg_ref[0, up, :].astype(BF16)
        wu_bf[slot, up, :] = wu_ref[0, up, :].astype(BF16)
        wd_bf[slot, down, :] = wd_ref[0, down, :].astype(BF16)

    def step(cur):
        for r in range(n_chunks):
            stage(1 - cur, r)
            rows = slice(r * ROW_CHUNK, (r + 1) * ROW_CHUNK)
            xs = xs_ref[0, rows, :]
            a = _dot(xs, wg_bf[cur])
            u = _dot(xs, wu_bf[cur])
            hidden = (a / (1.0 + jnp.exp(-a))) * u
            total = jnp.where(f == 0, 0.0, acc_ref[rows, :]) + _dot(hidden.astype(BF16), wd_bf[cur])
            acc_ref[rows, :] = total
            y_ref[0, rows, :] = total.astype(BF16)

    @pl.when(t == 0)
    def _():
        for part in range(n_chunks):
            stage(1, part)

    @pl.when(t % 2 == 0)
    def _():
        step(1)

    @pl.when(t % 2 == 1)
    def _():
        step(0)


def _experts(xs, w_gate, w_up, w_down):
    rows = BATCH * CAPACITY
    n_f = EXPERT_HIDDEN // FFN_TF
    last = N_EXPERTS * n_f - 1

    def staged(t):
        return jnp.minimum(t, last)

    def computed(t):
        return jnp.maximum(t - 1, 0) // n_f

    return pl.pallas_call(
        _expert_kernel,
        out_shape=jax.ShapeDtypeStruct((N_EXPERTS, rows, D_MODEL), BF16),
        grid=(N_EXPERTS * n_f + 1,),
        in_specs=[
            pl.BlockSpec((1, rows, D_MODEL), lambda t: (computed(t), 0, 0)),
            pl.BlockSpec((1, D_MODEL, FFN_TF), lambda t: (staged(t) // n_f, 0, staged(t) % n_f)),
            pl.BlockSpec((1, D_MODEL, FFN_TF), lambda t: (staged(t) // n_f, 0, staged(t) % n_f)),
            pl.BlockSpec((1, FFN_TF, D_MODEL), lambda t: (staged(t) // n_f, staged(t) % n_f, 0)),
        ],
        out_specs=pl.BlockSpec((1, rows, D_MODEL), lambda t: (computed(t), 0, 0)),
        scratch_shapes=[
            pltpu.VMEM((rows, D_MODEL), F32),
            pltpu.VMEM((2, D_MODEL, FFN_TF), BF16),
            pltpu.VMEM((2, D_MODEL, FFN_TF), BF16),
            pltpu.VMEM((2, FFN_TF, D_MODEL), BF16),
        ],
        compiler_params=_params(("arbitrary",)),
        name="experts",
    )(xs, w_gate, w_up, w_down)


def _combine_kernel(rank_ref, gate_ref, y_ref, x1_ref, mod_ref, fg_ref, o_ref):
    slot = lax.broadcasted_iota(jnp.int32, (1, CAPACITY), 1).astype(F32)
    acc = jnp.zeros((COMBINE_TQ, D_MODEL), F32)
    for e in range(N_EXPERTS):
        rank = rank_ref[0, :, e:e + 1]
        gate = gate_ref[0, :, e:e + 1]
        scatter = jnp.where(rank == slot, gate, 0.0).astype(BF16)
        acc = acc + _dot(scatter, y_ref[e])
    x2 = x1_ref[...] + mod_ref[0, 5:6, :] * acc
    ms = jnp.mean(x2 * x2, axis=-1, keepdims=True)
    o_ref[...] = x2 * lax.rsqrt(ms + EPS) * fg_ref[...]


def _combine(rank_t, gate_t, y, x1, mod3, final_g):
    per_b = SEQ // COMBINE_TQ
    return pl.pallas_call(
        _combine_kernel,
        out_shape=jax.ShapeDtypeStruct((BATCH * SEQ, D_MODEL), F32),
        grid=(BATCH, per_b),
        in_specs=[
            pl.BlockSpec((1, COMBINE_TQ, N_EXPERTS), lambda b, t: (b, t, 0)),
            pl.BlockSpec((1, COMBINE_TQ, N_EXPERTS), lambda b, t: (b, t, 0)),
            pl.BlockSpec((N_EXPERTS, CAPACITY, D_MODEL), lambda b, t: (0, b, 0)),
            pl.BlockSpec((COMBINE_TQ, D_MODEL), lambda b, t: (b * per_b + t, 0)),
            pl.BlockSpec((1, 6, D_MODEL), lambda b, t: (b, 0, 0)),
            pl.BlockSpec((1, D_MODEL), lambda b, t: (0, 0)),
        ],
        out_specs=pl.BlockSpec((COMBINE_TQ, D_MODEL), lambda b, t: (b * per_b + t, 0)),
        compiler_params=_params(("arbitrary", "arbitrary")),
        name="combine",
    )(rank_t, gate_t, y, x1, mod3, final_g)


def _rope_tables():
    t = jnp.arange(SEQ)
    pos = jnp.stack([t // GRID_W, t % GRID_W], axis=1).astype(F32)
    per_axis = HEAD_DIM // 2
    inv = ROPE_THETA ** (-jnp.arange(0, per_axis, 2, dtype=F32) / per_axis)
    ang = pos[:, :, None] * inv
    cos, sin = jnp.cos(ang), jnp.sin(ang)
    zero = jnp.zeros_like(sin)

    def lanes(first_half, second_half):
        one_map = jnp.stack([first_half, second_half], axis=2).reshape(SEQ, HEAD_DIM)
        return jnp.concatenate([one_map, one_map], axis=1)

    return lanes(cos, cos), lanes(-sin, zero), lanes(zero, sin)


def kernel(x, c, ctx, c_ctx, norm1_g, norm2_g, w_ada, b_ada, w_in, conv_w, w_out_conv, lambda_q1,
           lambda_k1, lambda_q2, lambda_k2, subln_g, w_o_attn, w_out, w_router, w_gate_e, w_up_e,
           w_down_e, final_g):
    cond = jnp.concatenate([c, c_ctx[None, :], jnp.zeros((COND_ROWS - BATCH - 1, D_MODEL), F32)], axis=0)
    mod3 = _ada(cond, w_ada[0], b_ada).reshape(COND_ROWS, 6, D_MODEL)

    w_in_bf = w_in[0].astype(BF16)
    cos, sin_hi, sin_lo = _rope_tables()
    q_scale = math.log2(math.e) * HEAD_DIM ** -0.5
    parts = _inproj(x, mod3, norm1_g, w_in_bf, conv_w[0], cos, sin_hi, sin_lo, q_scale)
    ctx_kv = _ctxproj(ctx.reshape(BATCH * CTX_LEN, D_MODEL), mod3, norm1_g, w_in_bf)

    lam_params = jnp.concatenate([lambda_q1, lambda_k1, lambda_q2, lambda_k2], axis=0)
    attn_o = _attn(lam_params, parts, ctx_kv, subln_g)

    wr_t = w_router[0].T
    wr_hi = wr_t.astype(BF16)
    wr_lo = (wr_t - wr_hi.astype(F32)).astype(BF16)
    x1, h2, logits_t = _merge(parts, attn_o, x.reshape(BATCH * SEQ, D_MODEL), mod3, norm2_g,
                              w_out_conv[0].astype(BF16), w_o_attn[0].astype(BF16), w_out[0].astype(BF16),
                              wr_hi, wr_lo)

    rank_e, rank_t, gate_t = _route(logits_t)
    xs = _dispatch(rank_e.reshape(BATCH * N_EXPERTS, 1, SEQ), h2)
    y = _experts(xs, w_gate_e[0], w_up_e[0], w_down_e[0])
    out = _combine(rank_t, gate_t, y, x1, mod3, final_g[None, :])
    return out.reshape(BATCH, SEQ, D_MODEL)
```

```python
import functools
import math

import jax
import jax.numpy as jnp
from jax import lax
from jax.experimental import pallas as pl
from jax.experimental.pallas import tpu as pltpu
from jax.experimental.pallas import tpu_sc as plsc

D_MODEL = 1024
BATCH = 8
SEQ = 2048
GRID_W = 64
CTX_LEN = 256
N_HEADS = 8
HEAD_DIM = 64
V_DIM = 2 * HEAD_DIM
N_EXPERTS = 16
EXPERT_HIDDEN = 2048
CAPACITY = 2 * SEQ // N_EXPERTS
ROPE_THETA = 10000.0
EPS = 1e-6
LAM_INIT = 0.8 - 0.6 * math.exp(-0.3 * 0)
N_PARTS = 8
N_KEYS = CTX_LEN + SEQ

LANES = 128
SC_CORES = 2
SC_SUBCORES = 16
SC_LANES = 16
F32 = jnp.float32
BF16 = jnp.bfloat16

ROW_CHUNK = 512
COND_ROWS = 16
ADA_TN = 512
ATTN_TQ = 2048
ATTN_SQ = 256
MERGE_TM = 512
FFN_TF = 512
COMBINE_TQ = 512
VMEM_LIMIT = 56 * 1024 * 1024


def _dot(a, b):
    return jnp.dot(a, b, preferred_element_type=F32)


def _dot_nt(a, b):
    return lax.dot_general(a, b, (((1,), (1,)), ((), ())), preferred_element_type=F32)


def _params(sem, vmem=VMEM_LIMIT):
    return pltpu.CompilerParams(dimension_semantics=sem, vmem_limit_bytes=vmem)


def _ada_kernel(cond_ref, w_ref, b_ref, o_ref):
    c = cond_ref[...]
    s = c / (1.0 + jnp.exp(-c))
    o_ref[...] = _dot(s.astype(BF16), w_ref[...].astype(BF16)) + b_ref[...]


def _ada(cond, w_ada, b_ada):
    n = w_ada.shape[1]
    return pl.pallas_call(
        _ada_kernel,
        out_shape=jax.ShapeDtypeStruct((COND_ROWS, n), F32),
        grid=(n // ADA_TN,),
        in_specs=[
            pl.BlockSpec((COND_ROWS, D_MODEL), lambda i: (0, 0)),
            pl.BlockSpec((D_MODEL, ADA_TN), lambda i: (0, i)),
            pl.BlockSpec((1, ADA_TN), lambda i: (0, i)),
        ],
        out_specs=pl.BlockSpec((COND_ROWS, ADA_TN), lambda i: (0, i)),
        compiler_params=_params(("arbitrary",)),
        name="ada",
    )(cond, w_ada, b_ada)


def _norm_modulate(x, g, shift, scale):
    ms = jnp.mean(x * x, axis=-1, keepdims=True)
    return (x * lax.rsqrt(ms + EPS) * g) * (1.0 + scale) + shift


def _rope(a, cos, sin_hi, sin_lo):
    return a * cos + pltpu.roll(a, LANES - 16, 1) * sin_hi + pltpu.roll(a, 16, 1) * sin_lo


def _inproj_kernel(x_ref, mod_ref, g_ref, w_ref, cw_ref, cos_ref, shi_ref, slo_ref, o_ref,
                   hx_ref, z_ref, *, q_scale):
    j = pl.program_id(1)
    chunks = [slice(r * ROW_CHUNK, (r + 1) * ROW_CHUNK) for r in range(SEQ // ROW_CHUNK)]

    @pl.when(j == 0)
    def _():
        shift = mod_ref[0, 0:1, :]
        scale = mod_ref[0, 1:2, :]
        for rows in chunks:
            hx_ref[rows, :] = _norm_modulate(x_ref[0, rows, :], g_ref[...], shift, scale).astype(BF16)
        z_ref[0:8, :] = jnp.zeros((8, D_MODEL), F32)
        z_ref[SEQ + 8:SEQ + 16, :] = jnp.zeros((8, D_MODEL), F32)
        for rows in chunks:
            z_ref[pl.ds(rows.start + 8, ROW_CHUNK), :] = _dot(hx_ref[rows, :], w_ref[...])

    @pl.when(j == 1)
    def _():
        for rows in chunks:
            zr = pl.ds(rows.start + 8, ROW_CHUNK)
            z_ref[zr, :] = z_ref[zr, :] * _dot(hx_ref[rows, :], w_ref[...])

    @pl.when(j == 2)
    def _():
        first_row = lax.broadcasted_iota(jnp.int32, (8, 1), 0) == 0
        last_row = lax.broadcasted_iota(jnp.int32, (8, 1), 0) == 7
        for rows in chunks:
            base = rows.start + 8
            zc = z_ref[pl.ds(base, ROW_CHUNK), :]
            before = pltpu.roll(zc * cw_ref[0:1, :], 1, 0)
            after = pltpu.roll(zc * cw_ref[2:3, :], ROW_CHUNK - 1, 0)
            edge_b = jnp.where(first_row, z_ref[pl.ds(base - 8, 8), :][7:8, :] * cw_ref[0:1, :], before[0:8, :])
            edge_a = jnp.where(last_row, z_ref[pl.ds(base + ROW_CHUNK, 8), :][0:1, :] * cw_ref[2:3, :],
                               after[ROW_CHUNK - 8:, :])
            before = jnp.concatenate([edge_b, before[8:, :]], axis=0)
            after = jnp.concatenate([after[:ROW_CHUNK - 8, :], edge_a], axis=0)
            y = before + zc * cw_ref[1:2, :] + after
            o_ref[0, rows, :] = (_dot(hx_ref[rows, :], w_ref[...]) * y).astype(BF16)

    def rope_part(scale):
        for rows in chunks:
            acc = _dot(hx_ref[rows, :], w_ref[...])
            cos, shi, slo = cos_ref[rows, :], shi_ref[rows, :], slo_ref[rows, :]
            for h in range(D_MODEL // LANES):
                cols = slice(h * LANES, (h + 1) * LANES)
                r = _rope(acc[:, cols], cos, shi, slo)
                if scale is not None:
                    r = r * scale
                o_ref[0, rows, cols] = r.astype(BF16)

    @pl.when(j == 3)
    def _():
        rope_part(q_scale)

    @pl.when(j == 4)
    def _():
        rope_part(None)

    @pl.when(j == 5)
    def _():
        for rows in chunks:
            o_ref[0, rows, :] = _dot(hx_ref[rows, :], w_ref[...]).astype(BF16)

    @pl.when(j >= 6)
    def _():
        for rows in chunks:
            acc = _dot(hx_ref[rows, :], w_ref[...])
            o_ref[0, rows, :] = (1.0 / (1.0 + jnp.exp(-acc))).astype(BF16)


def _inproj(x, mod3, norm_g, w_in_bf, conv_w, cos, sin_hi, sin_lo, q_scale):
    def w_map(b, j):
        return (0, jnp.where(j == 1, 2, jnp.where(j == 2, 1, j)))

    tab = pl.BlockSpec((SEQ, LANES), lambda b, j: (0, 0))
    return pl.pallas_call(
        functools.partial(_inproj_kernel, q_scale=q_scale),
        out_shape=jax.ShapeDtypeStruct((N_PARTS - 2, BATCH * SEQ, D_MODEL), BF16),
        grid=(BATCH, N_PARTS),
        in_specs=[
            pl.BlockSpec((1, SEQ, D_MODEL), lambda b, j: (b, 0, 0)),
            pl.BlockSpec((1, 6, D_MODEL), lambda b, j: (b, 0, 0)),
            pl.BlockSpec((1, D_MODEL), lambda b, j: (0, 0)),
            pl.BlockSpec((D_MODEL, D_MODEL), w_map),
            pl.BlockSpec((3, D_MODEL), lambda b, j: (0, 0)),
            tab, tab, tab,
        ],
        out_specs=pl.BlockSpec((1, SEQ, D_MODEL), lambda b, j: (jnp.maximum(j - 2, 0), b, 0)),
        scratch_shapes=[pltpu.VMEM((SEQ, D_MODEL), BF16), pltpu.VMEM((SEQ + 16, D_MODEL), F32)],
        compiler_params=_params(("arbitrary", "arbitrary")),
        name="inproj",
    )(x, mod3, norm_g, w_in_bf, conv_w, cos, sin_hi, sin_lo)


def _ctxproj_kernel(c_ref, mod_ref, g_ref, w_ref, o_ref, hc_ref):
    j = pl.program_id(0)
    chunks = [slice(r * ROW_CHUNK, (r + 1) * ROW_CHUNK) for r in range(BATCH * CTX_LEN // ROW_CHUNK)]

    @pl.when(j == 0)
    def _():
        shift = mod_ref[0, 0:1, :]
        scale = mod_ref[0, 1:2, :]
        for rows in chunks:
            hc_ref[rows, :] = _norm_modulate(c_ref[rows, :], g_ref[...], shift, scale).astype(BF16)

    for rows in chunks:
        o_ref[0, rows, :] = _dot(hc_ref[rows, :], w_ref[...]).astype(BF16)


def _ctxproj(ctx2, mod3, norm_g, w_in_bf):
    rows = BATCH * CTX_LEN
    return pl.pallas_call(
        _ctxproj_kernel,
        out_shape=jax.ShapeDtypeStruct((2, rows, D_MODEL), BF16),
        grid=(2,),
        in_specs=[
            pl.BlockSpec((rows, D_MODEL), lambda j: (0, 0)),
            pl.BlockSpec((1, 6, D_MODEL), lambda j: (BATCH, 0, 0)),
            pl.BlockSpec((1, D_MODEL), lambda j: (0, 0)),
            pl.BlockSpec((D_MODEL, D_MODEL), lambda j: (0, 4 + j)),
        ],
        out_specs=pl.BlockSpec((1, rows, D_MODEL), lambda j: (j, 0, 0)),
        scratch_shapes=[pltpu.VMEM((rows, D_MODEL), BF16)],
        compiler_params=_params(("arbitrary",)),
        name="ctxproj",
    )(ctx2, mod3, norm_g, w_in_bf)


def _attn_kernel(lp_ref, q_ref, k_ref, v_ref, kc_ref, vc_ref, g_ref, o_ref, k_all, v_ext, s_ref, m_ref):
    lp = lp_ref[...]
    lam = (jnp.exp(jnp.sum(lp[0:1] * lp[1:2], axis=-1, keepdims=True))
           - jnp.exp(jnp.sum(lp[2:3] * lp[3:4], axis=-1, keepdims=True)) + LAM_INIT)
    k_all[0:CTX_LEN, :] = kc_ref[0]
    k_all[CTX_LEN:N_KEYS, :] = k_ref[0]
    v_ext[0:CTX_LEN, 0:LANES] = vc_ref[0]
    v_ext[CTX_LEN:N_KEYS, 0:LANES] = v_ref[0]
    v_ext[:, LANES:2 * LANES] = jnp.ones((N_KEYS, LANES), BF16)
    first_map = lax.broadcasted_iota(jnp.int32, (1, LANES), 1) < HEAD_DIM

    def scores(u):
        q = q_ref[0, u * ATTN_SQ:(u + 1) * ATTN_SQ, :]
        zero = jnp.zeros_like(q)
        for mp, qm in enumerate((jnp.where(first_map, q, zero), jnp.where(first_map, zero, q))):
            s = _dot_nt(qm, k_all[...])
            s_ref[u % 2, mp] = s
            m_ref[u % 2, mp] = jnp.max(s, axis=-1, keepdims=True)

    def values(u):
        o = []
        for mp in range(2):
            p = jnp.exp2(s_ref[u % 2, mp] - m_ref[u % 2, mp])
            o.append(_dot(p.astype(BF16), v_ext[...]))
        a = o[0][:, :LANES] / o[0][:, LANES:] - o[1][:, :LANES] * (lam / o[1][:, LANES:])
        ms = jnp.mean(a * a, axis=-1, keepdims=True)
        o_ref[u * ATTN_SQ:(u + 1) * ATTN_SQ, :] = (
            (a * lax.rsqrt(ms + EPS) * g_ref[...]) * (1.0 - LAM_INIT)).astype(BF16)

    n_units = ATTN_TQ // ATTN_SQ
    scores(0)
    for u in range(n_units):
        if u + 1 < n_units:
            scores(u + 1)
        values(u)


def _attn(lam_params, parts, ctx_kv, subln_g):
    nq = SEQ // ATTN_TQ
    return pl.pallas_call(
        _attn_kernel,
        out_shape=jax.ShapeDtypeStruct((BATCH * SEQ, D_MODEL), BF16),
        grid=(BATCH, N_HEADS, nq),
        in_specs=[
            pl.BlockSpec((4, HEAD_DIM), lambda b, h, i: (0, 0)),
            pl.BlockSpec((1, ATTN_TQ, LANES), lambda b, h, i: (1, b * nq + i, h)),
            pl.BlockSpec((1, SEQ, LANES), lambda b, h, i: (2, b, h)),
            pl.BlockSpec((1, SEQ, LANES), lambda b, h, i: (3, b, h)),
            pl.BlockSpec((1, CTX_LEN, LANES), lambda b, h, i: (0, b, h)),
            pl.BlockSpec((1, CTX_LEN, LANES), lambda b, h, i: (1, b, h)),
            pl.BlockSpec((1, V_DIM), lambda b, h, i: (0, 0)),
        ],
        out_specs=pl.BlockSpec((ATTN_TQ, LANES), lambda b, h, i: (b * nq + i, h)),
        scratch_shapes=[
            pltpu.VMEM((N_KEYS, LANES), BF16),
            pltpu.VMEM((N_KEYS, 2 * LANES), BF16),
            pltpu.VMEM((2, 2, ATTN_SQ, N_KEYS), F32),
            pltpu.VMEM((2, 2, ATTN_SQ, 1), F32),
        ],
        compiler_params=_params(("arbitrary", "arbitrary", "arbitrary")),
        name="attn",
    )(lam_params, parts, parts, parts, ctx_kv, ctx_kv, subln_g)


def _merge_kernel(yb_ref, s6_ref, s7_ref, o_ref, x_ref, mod_ref, g2_ref, wc_ref, wa_ref, wo_ref,
                  wrh_ref, wrl_ref, x1_ref, h2_ref, lg_ref):
    y_conv = _dot(yb_ref[0], wc_ref[...])
    y_attn = _dot(o_ref[...], wa_ref[...])
    merged = s6_ref[0].astype(F32) * y_conv + s7_ref[0].astype(F32) * y_attn
    mix = _dot(merged.astype(BF16), wo_ref[...])
    x1 = x_ref[...] + mod_ref[0, 2:3, :] * mix
    x1_ref[...] = x1
    h2 = _norm_modulate(x1, g2_ref[...], mod_ref[0, 3:4, :], mod_ref[0, 4:5, :])
    h_hi = h2.astype(BF16)
    h2_ref[...] = pltpu.pack_elementwise([h2[:, :D_MODEL // 2], h2[:, D_MODEL // 2:]], packed_dtype=BF16)
    h_lo = (h2 - h_hi.astype(F32)).astype(BF16)
    lg_ref[0] = _dot_nt(wrh_ref[...], h_hi) + (_dot_nt(wrh_ref[...], h_lo) + _dot_nt(wrl_ref[...], h_hi))


def _merge(parts, attn_o, x2, mod3, norm2_g, w_conv_bf, w_attn_bf, w_out_bf, wr_hi, wr_lo):
    per_b = SEQ // MERGE_TM
    sq = pl.BlockSpec((D_MODEL, D_MODEL), lambda i: (0, 0), pipeline_mode=pl.Buffered(1))
    wr = pl.BlockSpec((N_EXPERTS, D_MODEL), lambda i: (0, 0), pipeline_mode=pl.Buffered(1))
    tile = pl.BlockSpec((MERGE_TM, D_MODEL), lambda i: (i, 0))
    return pl.pallas_call(
        _merge_kernel,
        out_shape=(
            jax.ShapeDtypeStruct((BATCH * SEQ, D_MODEL), F32),
            jax.ShapeDtypeStruct((BATCH * SEQ, D_MODEL // 2), jnp.uint32),
            jax.ShapeDtypeStruct((BATCH, N_EXPERTS, SEQ), F32),
        ),
        grid=(BATCH * per_b,),
        in_specs=[
            pl.BlockSpec((1, MERGE_TM, D_MODEL), lambda i: (0, i, 0)),
            pl.BlockSpec((1, MERGE_TM, D_MODEL), lambda i: (4, i, 0)),
            pl.BlockSpec((1, MERGE_TM, D_MODEL), lambda i: (5, i, 0)),
            tile, tile,
            pl.BlockSpec((1, 6, D_MODEL), lambda i: (i // per_b, 0, 0)),
            pl.BlockSpec((1, D_MODEL), lambda i: (0, 0)),
            sq, sq, sq, wr, wr,
        ],
        out_specs=(
            tile,
            pl.BlockSpec((MERGE_TM, D_MODEL // 2), lambda i: (i, 0)),
            pl.BlockSpec((1, N_EXPERTS, MERGE_TM), lambda i: (i // per_b, 0, i % per_b)),
        ),
        compiler_params=_params(("arbitrary",)),
        name="merge",
    )(parts, parts, parts, attn_o, x2, mod3, norm2_g, w_conv_bf, w_attn_bf, w_out_bf, wr_hi, wr_lo)


def _route_kernel(lg_ref, rank_e_ref, rank_t_ref, gate_t_ref):
    lg = lg_ref[...]
    ex = jnp.exp(lg - jnp.max(lg, axis=1, keepdims=True))
    aff = (ex / jnp.sum(ex, axis=1, keepdims=True)).reshape(BATCH * N_EXPERTS, SEQ)
    bits = lax.bitcast_convert_type(aff, jnp.int32)

    def count(mask):
        return jnp.sum(jnp.where(mask, 1.0, 0.0), axis=1, keepdims=True)

    def search(i, t):
        cand = t + lax.shift_left(jnp.int32(1), 30 - i)
        return jnp.where(count(bits >= cand) >= CAPACITY, cand, t)

    thr = lax.fori_loop(0, 31, search, jnp.zeros((BATCH * N_EXPERTS, 1), jnp.int32))
    need = CAPACITY - count(bits > thr)
    before = (lax.broadcasted_iota(jnp.int32, (LANES, LANES), 0)
              < lax.broadcasted_iota(jnp.int32, (LANES, LANES), 1)).astype(BF16)
    n_tied = jnp.zeros((BATCH * N_EXPERTS, 1), F32)
    n_sel = jnp.zeros((BATCH * N_EXPERTS, 1), F32)
    for c in range(SEQ // LANES):
        cols = slice(c * LANES, (c + 1) * LANES)
        bits_c = bits[:, cols]
        tied_c = jnp.where(bits_c == thr, 1.0, 0.0)
        tied_before = _dot(tied_c.astype(BF16), before) + n_tied
        sel_c = jnp.where(bits_c > thr, 1.0, jnp.where(tied_before < need, tied_c, 0.0))
        rank = _dot(sel_c.astype(BF16), before) + n_sel
        rank_c = jnp.where(sel_c > 0.0, rank, -1.0)
        gate_c = jnp.where(sel_c > 0.0, aff[:, cols], 0.0)
        n_tied = n_tied + jnp.sum(tied_c, axis=1, keepdims=True)
        n_sel = n_sel + jnp.sum(sel_c, axis=1, keepdims=True)
        rank_e_ref[:, cols] = rank_c.astype(jnp.int32)
        rank_ct = rank_c.T
        gate_ct = gate_c.T
        for b in range(BATCH):
            ecols = slice(b * N_EXPERTS, (b + 1) * N_EXPERTS)
            rank_t_ref[b, cols, :] = rank_ct[:, ecols]
            gate_t_ref[b, cols, :] = gate_ct[:, ecols]


def _route(logits_t):
    return pl.pallas_call(
        _route_kernel,
        out_shape=(
            jax.ShapeDtypeStruct((BATCH * N_EXPERTS, SEQ), jnp.int32),
            jax.ShapeDtypeStruct((BATCH, SEQ, N_EXPERTS), F32),
            jax.ShapeDtypeStruct((BATCH, SEQ, N_EXPERTS), F32),
        ),
        compiler_params=pltpu.CompilerParams(vmem_limit_bytes=VMEM_LIMIT),
        name="route",
    )(logits_t)


def _dispatch(rank_e, h_packed):
    per_worker = BATCH * N_EXPERTS // (SC_CORES * SC_SUBCORES)
    half = CAPACITY // 2
    mesh = plsc.VectorSubcoreMesh(core_axis_name="c", subcore_axis_name="s")

    @functools.partial(
        pl.kernel, mesh=mesh,
        out_type=jax.ShapeDtypeStruct((N_EXPERTS * BATCH * CAPACITY, D_MODEL // 2), h_packed.dtype),
        scratch_types=[
            pltpu.VMEM((SEQ,), jnp.int32),
            pltpu.VMEM((2, half), jnp.int32),
            pltpu.VMEM((half, D_MODEL // 2), h_packed.dtype),
            pltpu.SemaphoreType.DMA,
        ],
        compiler_params=pltpu.CompilerParams(needs_layout_passes=False),
        name="dispatch",
    )
    def k(table_hbm, rank_hbm, out_hbm, rank_v, idx_v, rows_v, sem):
        worker = lax.axis_index("s") * SC_CORES + lax.axis_index("c")
        expert = worker // (BATCH // per_worker)

        @pl.loop(0, per_worker)
        def _(p):
            sample = (worker % (BATCH // per_worker)) * per_worker + p
            pltpu.sync_copy(rank_hbm.at[sample * N_EXPERTS + expert], rank_v)

            @pl.loop(0, SEQ // SC_LANES)
            def _(i):
                rank = rank_v[pl.ds(i * SC_LANES, SC_LANES)]
                row = lax.iota(jnp.int32, SC_LANES) + (i * SC_LANES + sample * SEQ)
                plsc.store_scatter(idx_v, [lax.shift_right_arithmetic(rank, 7), rank & (half - 1)], row,
                                   mask=rank >= 0)

            for c in range(2):
                pltpu.async_copy(table_hbm.at[idx_v.at[c]], rows_v, sem).wait()
                first = (expert * BATCH + sample) * CAPACITY + c * half
                pltpu.sync_copy(rows_v, out_hbm.at[pl.ds(first, half)])

    return k(h_packed, rank_e)


def _expert_kernel(xs_ref, wg_ref, wu_ref, wd_ref, y_ref, acc_ref, wg_bf, wu_bf, wd_bf):
    t = pl.program_id(0)
    n_f = EXPERT_HIDDEN // FFN_TF
    f = jnp.where(t == 0, 0, (t - 1) % n_f)

    n_chunks = BATCH * CAPACITY // ROW_CHUNK

    def stage(slot, part):
        up = slice(part * D_MODEL // n_chunks, (part + 1) * D_MODEL // n_chunks)
        down = slice(part * FFN_TF // n_chunks, (part + 1) * FFN_TF // n_chunks)
        wg_bf[slot, up, :] = wg_ref[0, up, :].astype(BF16)
        wu_bf[slot, up, :] = wu_ref[0, up, :].astype(BF16)
        wd_bf[slot, down, :] = wd_ref[0, down, :].astype(BF16)

    def step(cur):
        for r in range(n_chunks):
            stage(1 - cur, r)
            rows = slice(r * ROW_CHUNK, (r + 1) * ROW_CHUNK)
            packed = xs_ref[0, rows, :]
            half = D_MODEL // 2
            x_lo = pltpu.unpack_elementwise(packed, index=0, packed_dtype=BF16, unpacked_dtype=F32).astype(BF16)
            x_hi = pltpu.unpack_elementwise(packed, index=1, packed_dtype=BF16, unpacked_dtype=F32).astype(BF16)
            a = _dot(x_lo, wg_bf[cur, :half, :]) + _dot(x_hi, wg_bf[cur, half:, :])
            u = _dot(x_lo, wu_bf[cur, :half, :]) + _dot(x_hi, wu_bf[cur, half:, :])
            hidden = (a / (1.0 + jnp.exp(-a))) * u
            total = jnp.where(f == 0, 0.0, acc_ref[rows, :]) + _dot(hidden.astype(BF16), wd_bf[cur])
            acc_ref[rows, :] = total
            y_ref[0, rows, :] = total.astype(BF16)

    @pl.when(t == 0)
    def _():
        for part in range(n_chunks):
            stage(1, part)

    @pl.when(t % 2 == 0)
    def _():
        step(1)

    @pl.when(t % 2 == 1)
    def _():
        step(0)


def _experts(xs, w_gate, w_up, w_down):
    rows = BATCH * CAPACITY
    n_f = EXPERT_HIDDEN // FFN_TF
    last = N_EXPERTS * n_f - 1

    def staged(t):
        return jnp.minimum(t, last)

    def computed(t):
        return jnp.maximum(t - 1, 0) // n_f

    return pl.pallas_call(
        _expert_kernel,
        out_shape=jax.ShapeDtypeStruct((N_EXPERTS, rows, D_MODEL), BF16),
        grid=(N_EXPERTS * n_f + 1,),
        in_specs=[
            pl.BlockSpec((1, rows, D_MODEL // 2), lambda t: (computed(t), 0, 0)),
            pl.BlockSpec((1, D_MODEL, FFN_TF), lambda t: (staged(t) // n_f, 0, staged(t) % n_f)),
            pl.BlockSpec((1, D_MODEL, FFN_TF), lambda t: (staged(t) // n_f, 0, staged(t) % n_f)),
            pl.BlockSpec((1, FFN_TF, D_MODEL), lambda t: (staged(t) // n_f, staged(t) % n_f, 0)),
        ],
        out_specs=pl.BlockSpec((1, rows, D_MODEL), lambda t: (computed(t), 0, 0)),
        scratch_shapes=[
            pltpu.VMEM((rows, D_MODEL), F32),
            pltpu.VMEM((2, D_MODEL, FFN_TF), BF16),
            pltpu.VMEM((2, D_MODEL, FFN_TF), BF16),
            pltpu.VMEM((2, FFN_TF, D_MODEL), BF16),
        ],
        compiler_params=_params(("arbitrary",)),
        name="experts",
    )(xs, w_gate, w_up, w_down)


def _combine_kernel(rank_ref, gate_ref, y_ref, x1_ref, mod_ref, fg_ref, o_ref):
    slot = lax.broadcasted_iota(jnp.int32, (1, CAPACITY), 1).astype(F32)
    acc = jnp.zeros((COMBINE_TQ, D_MODEL), F32)
    for e in range(N_EXPERTS):
        rank = rank_ref[0, :, e:e + 1]
        gate = gate_ref[0, :, e:e + 1]
        scatter = jnp.where(rank == slot, gate, 0.0).astype(BF16)
        acc = acc + _dot(scatter, y_ref[e])
    x2 = x1_ref[...] + mod_ref[0, 5:6, :] * acc
    ms = jnp.mean(x2 * x2, axis=-1, keepdims=True)
    o_ref[...] = x2 * lax.rsqrt(ms + EPS) * fg_ref[...]


def _combine(rank_t, gate_t, y, x1, mod3, final_g):
    per_b = SEQ // COMBINE_TQ
    return pl.pallas_call(
        _combine_kernel,
        out_shape=jax.ShapeDtypeStruct((BATCH * SEQ, D_MODEL), F32),
        grid=(BATCH, per_b),
        in_specs=[
            pl.BlockSpec((1, COMBINE_TQ, N_EXPERTS), lambda b, t: (b, t, 0)),
            pl.BlockSpec((1, COMBINE_TQ, N_EXPERTS), lambda b, t: (b, t, 0)),
            pl.BlockSpec((N_EXPERTS, CAPACITY, D_MODEL), lambda b, t: (0, b, 0)),
            pl.BlockSpec((COMBINE_TQ, D_MODEL), lambda b, t: (b * per_b + t, 0)),
            pl.BlockSpec((1, 6, D_MODEL), lambda b, t: (b, 0, 0)),
            pl.BlockSpec((1, D_MODEL), lambda b, t: (0, 0)),
        ],
        out_specs=pl.BlockSpec((COMBINE_TQ, D_MODEL), lambda b, t: (b * per_b + t, 0)),
        compiler_params=_params(("arbitrary", "arbitrary")),
        name="combine",
    )(rank_t, gate_t, y, x1, mod3, final_g)


def _rope_tables():
    t = jnp.arange(SEQ)
    pos = jnp.stack([t // GRID_W, t % GRID_W], axis=1).astype(F32)
    per_axis = HEAD_DIM // 2
    inv = ROPE_THETA ** (-jnp.arange(0, per_axis, 2, dtype=F32) / per_axis)
    ang = pos[:, :, None] * inv
    cos, sin = jnp.cos(ang), jnp.sin(ang)
    zero = jnp.zeros_like(sin)

    def lanes(first_half, second_half):
        one_map = jnp.stack([first_half, second_half], axis=2).reshape(SEQ, HEAD_DIM)
        return jnp.concatenate([one_map, one_map], axis=1)

    return lanes(cos, cos), lanes(-sin, zero), lanes(zero, sin)


def kernel(x, c, ctx, c_ctx, norm1_g, norm2_g, w_ada, b_ada, w_in, conv_w, w_out_conv, lambda_q1,
           lambda_k1, lambda_q2, lambda_k2, subln_g, w_o_attn, w_out, w_router, w_gate_e, w_up_e,
           w_down_e, final_g):
    cond = jnp.concatenate([c, c_ctx[None, :], jnp.zeros((COND_ROWS - BATCH - 1, D_MODEL), F32)], axis=0)
    mod3 = _ada(cond, w_ada[0], b_ada).reshape(COND_ROWS, 6, D_MODEL)

    w_in_bf = w_in[0].astype(BF16)
    cos, sin_hi, sin_lo = _rope_tables()
    q_scale = math.log2(math.e) * HEAD_DIM ** -0.5
    parts = _inproj(x, mod3, norm1_g, w_in_bf, conv_w[0], cos, sin_hi, sin_lo, q_scale)
    ctx_kv = _ctxproj(ctx.reshape(BATCH * CTX_LEN, D_MODEL), mod3, norm1_g, w_in_bf)

    lam_params = jnp.concatenate([lambda_q1, lambda_k1, lambda_q2, lambda_k2], axis=0)
    attn_o = _attn(lam_params, parts, ctx_kv, subln_g)

    wr_t = w_router[0].T
    wr_hi = wr_t.astype(BF16)
    wr_lo = (wr_t - wr_hi.astype(F32)).astype(BF16)
    x1, h2, logits_t = _merge(parts, attn_o, x.reshape(BATCH * SEQ, D_MODEL), mod3, norm2_g,
                              w_out_conv[0].astype(BF16), w_o_attn[0].astype(BF16), w_out[0].astype(BF16),
                              wr_hi, wr_lo)

    rank_e, rank_t, gate_t = _route(logits_t)
    xs = _dispatch(rank_e, h2).reshape(N_EXPERTS, BATCH * CAPACITY, D_MODEL // 2)
    y = _experts(xs, w_gate_e[0], w_up_e[0], w_down_e[0])
    out = _combine(rank_t, gate_t, y, x1, mod3, final_g[None, :])
    return out.reshape(BATCH, SEQ, D_MODEL)
```

```python
import functools
import math

import jax
import jax.numpy as jnp
from jax import lax
from jax.experimental import pallas as pl
from jax.experimental.pallas import tpu as pltpu
from jax.experimental.pallas import tpu_sc as plsc

D_MODEL = 1024
BATCH = 8
SEQ = 2048
GRID_W = 64
CTX_LEN = 256
N_HEADS = 8
HEAD_DIM = 64
V_DIM = 2 * HEAD_DIM
N_EXPERTS = 16
EXPERT_HIDDEN = 2048
CAPACITY = 2 * SEQ // N_EXPERTS
ROPE_THETA = 10000.0
EPS = 1e-6
LAM_INIT = 0.8 - 0.6 * math.exp(-0.3 * 0)
N_PARTS = 8
N_KEYS = CTX_LEN + SEQ
N_GROUPS = 2
GROUP_EXPERTS = N_EXPERTS // N_GROUPS

LANES = 128
SC_CORES = 2
SC_SUBCORES = 16
SC_LANES = 16
F32 = jnp.float32
BF16 = jnp.bfloat16

ROW_CHUNK = 512
COND_ROWS = 16
ADA_TN = 512
ATTN_TQ = 2048
ATTN_SQ = 256
MERGE_TM = 512
FFN_TF = 512
COMBINE_TQ = 512
VMEM_LIMIT = 56 * 1024 * 1024


def _dot(a, b):
    return jnp.dot(a, b, preferred_element_type=F32)


def _dot_nt(a, b):
    return lax.dot_general(a, b, (((1,), (1,)), ((), ())), preferred_element_type=F32)


def _params(sem, vmem=VMEM_LIMIT):
    return pltpu.CompilerParams(dimension_semantics=sem, vmem_limit_bytes=vmem)


def _ada_kernel(cond_ref, w_ref, b_ref, o_ref):
    c = cond_ref[...]
    s = c / (1.0 + jnp.exp(-c))
    o_ref[...] = _dot(s.astype(BF16), w_ref[...].astype(BF16)) + b_ref[...]


def _ada(cond, w_ada, b_ada):
    n = w_ada.shape[1]
    return pl.pallas_call(
        _ada_kernel,
        out_shape=jax.ShapeDtypeStruct((COND_ROWS, n), F32),
        grid=(n // ADA_TN,),
        in_specs=[
            pl.BlockSpec((COND_ROWS, D_MODEL), lambda i: (0, 0)),
            pl.BlockSpec((D_MODEL, ADA_TN), lambda i: (0, i)),
            pl.BlockSpec((1, ADA_TN), lambda i: (0, i)),
        ],
        out_specs=pl.BlockSpec((COND_ROWS, ADA_TN), lambda i: (0, i)),
        compiler_params=_params(("arbitrary",)),
        name="ada",
    )(cond, w_ada, b_ada)


def _norm_modulate(x, g, shift, scale):
    ms = jnp.mean(x * x, axis=-1, keepdims=True)
    return (x * lax.rsqrt(ms + EPS) * g) * (1.0 + scale) + shift


def _rope(a, cos, sin_hi, sin_lo):
    return a * cos + pltpu.roll(a, LANES - 16, 1) * sin_hi + pltpu.roll(a, 16, 1) * sin_lo


def _inproj_kernel(x_ref, mod_ref, g_ref, w_ref, cw_ref, cos_ref, shi_ref, slo_ref, o_ref,
                   hx_ref, z_ref, *, q_scale):
    j = pl.program_id(1)
    chunks = [slice(r * ROW_CHUNK, (r + 1) * ROW_CHUNK) for r in range(SEQ // ROW_CHUNK)]

    @pl.when(j == 0)
    def _():
        shift = mod_ref[0, 0:1, :]
        scale = mod_ref[0, 1:2, :]
        for rows in chunks:
            hx_ref[rows, :] = _norm_modulate(x_ref[0, rows, :], g_ref[...], shift, scale).astype(BF16)
        z_ref[0:8, :] = jnp.zeros((8, D_MODEL), F32)
        z_ref[SEQ + 8:SEQ + 16, :] = jnp.zeros((8, D_MODEL), F32)
        for rows in chunks:
            z_ref[pl.ds(rows.start + 8, ROW_CHUNK), :] = _dot(hx_ref[rows, :], w_ref[...])

    @pl.when(j == 1)
    def _():
        for rows in chunks:
            zr = pl.ds(rows.start + 8, ROW_CHUNK)
            z_ref[zr, :] = z_ref[zr, :] * _dot(hx_ref[rows, :], w_ref[...])

    @pl.when(j == 2)
    def _():
        first_row = lax.broadcasted_iota(jnp.int32, (8, 1), 0) == 0
        last_row = lax.broadcasted_iota(jnp.int32, (8, 1), 0) == 7
        for rows in chunks:
            base = rows.start + 8
            zc = z_ref[pl.ds(base, ROW_CHUNK), :]
            before = pltpu.roll(zc * cw_ref[0:1, :], 1, 0)
            after = pltpu.roll(zc * cw_ref[2:3, :], ROW_CHUNK - 1, 0)
            edge_b = jnp.where(first_row, z_ref[pl.ds(base - 8, 8), :][7:8, :] * cw_ref[0:1, :], before[0:8, :])
            edge_a = jnp.where(last_row, z_ref[pl.ds(base + ROW_CHUNK, 8), :][0:1, :] * cw_ref[2:3, :],
                               after[ROW_CHUNK - 8:, :])
            before = jnp.concatenate([edge_b, before[8:, :]], axis=0)
            after = jnp.concatenate([after[:ROW_CHUNK - 8, :], edge_a], axis=0)
            y = before + zc * cw_ref[1:2, :] + after
            o_ref[0, rows, :] = (_dot(hx_ref[rows, :], w_ref[...]) * y).astype(BF16)

    def rope_part(scale):
        for rows in chunks:
            acc = _dot(hx_ref[rows, :], w_ref[...])
            cos, shi, slo = cos_ref[rows, :], shi_ref[rows, :], slo_ref[rows, :]
            for h in range(D_MODEL // LANES):
                cols = slice(h * LANES, (h + 1) * LANES)
                r = _rope(acc[:, cols], cos, shi, slo)
                if scale is not None:
                    r = r * scale
                o_ref[0, rows, cols] = r.astype(BF16)

    @pl.when(j == 3)
    def _():
        rope_part(q_scale)

    @pl.when(j == 4)
    def _():
        rope_part(None)

    @pl.when(j == 5)
    def _():
        for rows in chunks:
            o_ref[0, rows, :] = _dot(hx_ref[rows, :], w_ref[...]).astype(BF16)

    @pl.when(j >= 6)
    def _():
        for rows in chunks:
            acc = _dot(hx_ref[rows, :], w_ref[...])
            o_ref[0, rows, :] = (1.0 / (1.0 + jnp.exp(-acc))).astype(BF16)


def _inproj(x, mod3, norm_g, w_in_bf, conv_w, cos, sin_hi, sin_lo, q_scale):
    def w_map(b, j):
        return (0, jnp.where(j == 1, 2, jnp.where(j == 2, 1, j)))

    tab = pl.BlockSpec((SEQ, LANES), lambda b, j: (0, 0))
    return pl.pallas_call(
        functools.partial(_inproj_kernel, q_scale=q_scale),
        out_shape=jax.ShapeDtypeStruct((N_PARTS - 2, BATCH * SEQ, D_MODEL), BF16),
        grid=(BATCH, N_PARTS),
        in_specs=[
            pl.BlockSpec((1, SEQ, D_MODEL), lambda b, j: (b, 0, 0)),
            pl.BlockSpec((1, 6, D_MODEL), lambda b, j: (b, 0, 0)),
            pl.BlockSpec((1, D_MODEL), lambda b, j: (0, 0)),
            pl.BlockSpec((D_MODEL, D_MODEL), w_map),
            pl.BlockSpec((3, D_MODEL), lambda b, j: (0, 0)),
            tab, tab, tab,
        ],
        out_specs=pl.BlockSpec((1, SEQ, D_MODEL), lambda b, j: (jnp.maximum(j - 2, 0), b, 0)),
        scratch_shapes=[pltpu.VMEM((SEQ, D_MODEL), BF16), pltpu.VMEM((SEQ + 16, D_MODEL), F32)],
        compiler_params=_params(("arbitrary", "arbitrary")),
        name="inproj",
    )(x, mod3, norm_g, w_in_bf, conv_w, cos, sin_hi, sin_lo)


def _ctxproj_kernel(c_ref, mod_ref, g_ref, w_ref, o_ref, hc_ref):
    j = pl.program_id(0)
    chunks = [slice(r * ROW_CHUNK, (r + 1) * ROW_CHUNK) for r in range(BATCH * CTX_LEN // ROW_CHUNK)]

    @pl.when(j == 0)
    def _():
        shift = mod_ref[0, 0:1, :]
        scale = mod_ref[0, 1:2, :]
        for rows in chunks:
            hc_ref[rows, :] = _norm_modulate(c_ref[rows, :], g_ref[...], shift, scale).astype(BF16)

    for rows in chunks:
        o_ref[0, rows, :] = _dot(hc_ref[rows, :], w_ref[...]).astype(BF16)


def _ctxproj(ctx2, mod3, norm_g, w_in_bf):
    rows = BATCH * CTX_LEN
    return pl.pallas_call(
        _ctxproj_kernel,
        out_shape=jax.ShapeDtypeStruct((2, rows, D_MODEL), BF16),
        grid=(2,),
        in_specs=[
            pl.BlockSpec((rows, D_MODEL), lambda j: (0, 0)),
            pl.BlockSpec((1, 6, D_MODEL), lambda j: (BATCH, 0, 0)),
            pl.BlockSpec((1, D_MODEL), lambda j: (0, 0)),
            pl.BlockSpec((D_MODEL, D_MODEL), lambda j: (0, 4 + j)),
        ],
        out_specs=pl.BlockSpec((1, rows, D_MODEL), lambda j: (j, 0, 0)),
        scratch_shapes=[pltpu.VMEM((rows, D_MODEL), BF16)],
        compiler_params=_params(("arbitrary",)),
        name="ctxproj",
    )(ctx2, mod3, norm_g, w_in_bf)


def _attn_kernel(lp_ref, q_ref, k_ref, v_ref, kc_ref, vc_ref, g_ref, o_ref, k_all, v_ext, s_ref, m_ref):
    lp = lp_ref[...]
    lam = (jnp.exp(jnp.sum(lp[0:1] * lp[1:2], axis=-1, keepdims=True))
           - jnp.exp(jnp.sum(lp[2:3] * lp[3:4], axis=-1, keepdims=True)) + LAM_INIT)
    k_all[0:CTX_LEN, :] = kc_ref[0]
    k_all[CTX_LEN:N_KEYS, :] = k_ref[0]
    v_ext[0:CTX_LEN, 0:LANES] = vc_ref[0]
    v_ext[CTX_LEN:N_KEYS, 0:LANES] = v_ref[0]
    v_ext[:, LANES:2 * LANES] = jnp.ones((N_KEYS, LANES), BF16)
    first_map = lax.broadcasted_iota(jnp.int32, (1, LANES), 1) < HEAD_DIM

    def scores(u):
        q = q_ref[0, u * ATTN_SQ:(u + 1) * ATTN_SQ, :]
        zero = jnp.zeros_like(q)
        for mp, qm in enumerate((jnp.where(first_map, q, zero), jnp.where(first_map, zero, q))):
            s = _dot_nt(qm, k_all[...])
            s_ref[u % 2, mp] = s
            m_ref[u % 2, mp] = jnp.max(s, axis=-1, keepdims=True)

    def values(u):
        o = []
        for mp in range(2):
            p = jnp.exp2(s_ref[u % 2, mp] - m_ref[u % 2, mp])
            o.append(_dot(p.astype(BF16), v_ext[...]))
        a = o[0][:, :LANES] / o[0][:, LANES:] - o[1][:, :LANES] * (lam / o[1][:, LANES:])
        ms = jnp.mean(a * a, axis=-1, keepdims=True)
        o_ref[u * ATTN_SQ:(u + 1) * ATTN_SQ, :] = (
            (a * lax.rsqrt(ms + EPS) * g_ref[...]) * (1.0 - LAM_INIT)).astype(BF16)

    n_units = ATTN_TQ // ATTN_SQ
    scores(0)
    for u in range(n_units):
        if u + 1 < n_units:
            scores(u + 1)
        values(u)


def _attn(lam_params, parts, ctx_kv, subln_g):
    nq = SEQ // ATTN_TQ
    return pl.pallas_call(
        _attn_kernel,
        out_shape=jax.ShapeDtypeStruct((BATCH * SEQ, D_MODEL), BF16),
        grid=(BATCH, N_HEADS, nq),
        in_specs=[
            pl.BlockSpec((4, HEAD_DIM), lambda b, h, i: (0, 0)),
            pl.BlockSpec((1, ATTN_TQ, LANES), lambda b, h, i: (1, b * nq + i, h)),
            pl.BlockSpec((1, SEQ, LANES), lambda b, h, i: (2, b, h)),
            pl.BlockSpec((1, SEQ, LANES), lambda b, h, i: (3, b, h)),
            pl.BlockSpec((1, CTX_LEN, LANES), lambda b, h, i: (0, b, h)),
            pl.BlockSpec((1, CTX_LEN, LANES), lambda b, h, i: (1, b, h)),
            pl.BlockSpec((1, V_DIM), lambda b, h, i: (0, 0)),
        ],
        out_specs=pl.BlockSpec((ATTN_TQ, LANES), lambda b, h, i: (b * nq + i, h)),
        scratch_shapes=[
            pltpu.VMEM((N_KEYS, LANES), BF16),
            pltpu.VMEM((N_KEYS, 2 * LANES), BF16),
            pltpu.VMEM((2, 2, ATTN_SQ, N_KEYS), F32),
            pltpu.VMEM((2, 2, ATTN_SQ, 1), F32),
        ],
        compiler_params=_params(("arbitrary", "arbitrary", "arbitrary")),
        name="attn",
    )(lam_params, parts, parts, parts, ctx_kv, ctx_kv, subln_g)


def _merge_kernel(yb_ref, s6_ref, s7_ref, o_ref, x_ref, mod_ref, g2_ref, wc_ref, wa_ref, wo_ref,
                  wrh_ref, wrl_ref, x1_ref, h2_ref, lg_ref):
    y_conv = _dot(yb_ref[0], wc_ref[...])
    y_attn = _dot(o_ref[...], wa_ref[...])
    merged = s6_ref[0].astype(F32) * y_conv + s7_ref[0].astype(F32) * y_attn
    mix = _dot(merged.astype(BF16), wo_ref[...])
    x1 = x_ref[...] + mod_ref[0, 2:3, :] * mix
    x1_ref[...] = x1
    h2 = _norm_modulate(x1, g2_ref[...], mod_ref[0, 3:4, :], mod_ref[0, 4:5, :])
    h_hi = h2.astype(BF16)
    h2_ref[...] = pltpu.pack_elementwise([h2[:, :D_MODEL // 2], h2[:, D_MODEL // 2:]], packed_dtype=BF16)
    h_lo = (h2 - h_hi.astype(F32)).astype(BF16)
    lg_ref[0] = _dot_nt(wrh_ref[...], h_hi) + (_dot_nt(wrh_ref[...], h_lo) + _dot_nt(wrl_ref[...], h_hi))


def _merge(parts, attn_o, x2, mod3, norm2_g, w_conv_bf, w_attn_bf, w_out_bf, wr_hi, wr_lo):
    per_b = SEQ // MERGE_TM
    sq = pl.BlockSpec((D_MODEL, D_MODEL), lambda i: (0, 0), pipeline_mode=pl.Buffered(1))
    wr = pl.BlockSpec((N_EXPERTS, D_MODEL), lambda i: (0, 0), pipeline_mode=pl.Buffered(1))
    tile = pl.BlockSpec((MERGE_TM, D_MODEL), lambda i: (i, 0))
    return pl.pallas_call(
        _merge_kernel,
        out_shape=(
            jax.ShapeDtypeStruct((BATCH * SEQ, D_MODEL), F32),
            jax.ShapeDtypeStruct((BATCH * SEQ, D_MODEL // 2), jnp.uint32),
            jax.ShapeDtypeStruct((BATCH, N_EXPERTS, SEQ), F32),
        ),
        grid=(BATCH * per_b,),
        in_specs=[
            pl.BlockSpec((1, MERGE_TM, D_MODEL), lambda i: (0, i, 0)),
            pl.BlockSpec((1, MERGE_TM, D_MODEL), lambda i: (4, i, 0)),
            pl.BlockSpec((1, MERGE_TM, D_MODEL), lambda i: (5, i, 0)),
            tile, tile,
            pl.BlockSpec((1, 6, D_MODEL), lambda i: (i // per_b, 0, 0)),
            pl.BlockSpec((1, D_MODEL), lambda i: (0, 0)),
            sq, sq, sq, wr, wr,
        ],
        out_specs=(
            tile,
            pl.BlockSpec((MERGE_TM, D_MODEL // 2), lambda i: (i, 0)),
            pl.BlockSpec((1, N_EXPERTS, MERGE_TM), lambda i: (i // per_b, 0, i % per_b)),
        ),
        compiler_params=_params(("arbitrary",)),
        name="merge",
    )(parts, parts, parts, attn_o, x2, mod3, norm2_g, w_conv_bf, w_attn_bf, w_out_bf, wr_hi, wr_lo)


def _route_kernel(lg_ref, rank_e_ref, rank_t_ref, gate_t_ref):
    lg = lg_ref[...]
    ex = jnp.exp(lg - jnp.max(lg, axis=1, keepdims=True))
    aff = (ex / jnp.sum(ex, axis=1, keepdims=True)).reshape(BATCH * N_EXPERTS, SEQ)
    bits = lax.bitcast_convert_type(aff, jnp.int32)

    def count(mask):
        return jnp.sum(jnp.where(mask, 1.0, 0.0), axis=1, keepdims=True)

    def search(i, t):
        cand = t + lax.shift_left(jnp.int32(1), 30 - i)
        return jnp.where(count(bits >= cand) >= CAPACITY, cand, t)

    thr = lax.fori_loop(0, 31, search, jnp.zeros((BATCH * N_EXPERTS, 1), jnp.int32))
    need = CAPACITY - count(bits > thr)
    before = (lax.broadcasted_iota(jnp.int32, (LANES, LANES), 0)
              < lax.broadcasted_iota(jnp.int32, (LANES, LANES), 1)).astype(BF16)
    n_tied = jnp.zeros((BATCH * N_EXPERTS, 1), F32)
    n_sel = jnp.zeros((BATCH * N_EXPERTS, 1), F32)
    for c in range(SEQ // LANES):
        cols = slice(c * LANES, (c + 1) * LANES)
        bits_c = bits[:, cols]
        tied_c = jnp.where(bits_c == thr, 1.0, 0.0)
        tied_before = _dot(tied_c.astype(BF16), before) + n_tied
        sel_c = jnp.where(bits_c > thr, 1.0, jnp.where(tied_before < need, tied_c, 0.0))
        rank = _dot(sel_c.astype(BF16), before) + n_sel
        rank_c = jnp.where(sel_c > 0.0, rank, -1.0)
        gate_c = jnp.where(sel_c > 0.0, aff[:, cols], 0.0)
        n_tied = n_tied + jnp.sum(tied_c, axis=1, keepdims=True)
        n_sel = n_sel + jnp.sum(sel_c, axis=1, keepdims=True)
        rank_e_ref[:, cols] = rank_c.astype(jnp.int32)
        rank_ct = rank_c.T
        gate_ct = gate_c.T
        for b in range(BATCH):
            ecols = slice(b * N_EXPERTS, (b + 1) * N_EXPERTS)
            rank_t_ref[b, cols, :] = rank_ct[:, ecols]
            gate_t_ref[b, cols, :] = gate_ct[:, ecols]


def _route(logits_t):
    return pl.pallas_call(
        _route_kernel,
        out_shape=(
            jax.ShapeDtypeStruct((BATCH * N_EXPERTS, SEQ), jnp.int32),
            jax.ShapeDtypeStruct((BATCH, SEQ, N_EXPERTS), F32),
            jax.ShapeDtypeStruct((BATCH, SEQ, N_EXPERTS), F32),
        ),
        compiler_params=pltpu.CompilerParams(vmem_limit_bytes=VMEM_LIMIT),
        name="route",
    )(logits_t)


def _dispatch(rank_e, h_packed, first_expert):
    per_worker = BATCH * GROUP_EXPERTS // (SC_CORES * SC_SUBCORES)
    half = CAPACITY // 2
    mesh = plsc.VectorSubcoreMesh(core_axis_name="c", subcore_axis_name="s")

    @functools.partial(
        pl.kernel, mesh=mesh,
        out_type=jax.ShapeDtypeStruct((GROUP_EXPERTS * BATCH * CAPACITY, D_MODEL // 2), h_packed.dtype),
        scratch_types=[
            pltpu.VMEM((SEQ,), jnp.int32),
            pltpu.VMEM((2, half), jnp.int32),
            pltpu.VMEM((half, D_MODEL // 2), h_packed.dtype),
            pltpu.SemaphoreType.DMA,
        ],
        compiler_params=pltpu.CompilerParams(needs_layout_passes=False),
        name="dispatch",
    )
    def k(table_hbm, rank_hbm, out_hbm, rank_v, idx_v, rows_v, sem):
        worker = lax.axis_index("s") * SC_CORES + lax.axis_index("c")
        expert = worker // (BATCH // per_worker)

        @pl.loop(0, per_worker)
        def _(p):
            sample = (worker % (BATCH // per_worker)) * per_worker + p
            pltpu.sync_copy(rank_hbm.at[sample * N_EXPERTS + first_expert + expert], rank_v)

            @pl.loop(0, SEQ // SC_LANES)
            def _(i):
                rank = rank_v[pl.ds(i * SC_LANES, SC_LANES)]
                row = lax.iota(jnp.int32, SC_LANES) + (i * SC_LANES + sample * SEQ)
                plsc.store_scatter(idx_v, [lax.shift_right_arithmetic(rank, 7), rank & (half - 1)], row,
                                   mask=rank >= 0)

            for c in range(2):
                pltpu.async_copy(table_hbm.at[idx_v.at[c]], rows_v, sem).wait()
                first = (expert * BATCH + sample) * CAPACITY + c * half
                pltpu.sync_copy(rows_v, out_hbm.at[pl.ds(first, half)])

    return k(h_packed, rank_e)


def _expert_kernel(xs_ref, wg_ref, wu_ref, wd_ref, y_ref, acc_ref, wg_bf, wu_bf, wd_bf):
    t = pl.program_id(0)
    n_f = EXPERT_HIDDEN // FFN_TF
    f = (t + n_f - 1) % n_f

    n_chunks = BATCH * CAPACITY // ROW_CHUNK

    def stage(slot, part):
        up = slice(part * D_MODEL // n_chunks, (part + 1) * D_MODEL // n_chunks)
        down = slice(part * FFN_TF // n_chunks, (part + 1) * FFN_TF // n_chunks)
        wg_bf[slot, up, :] = wg_ref[0, up, :].astype(BF16)
        wu_bf[slot, up, :] = wu_ref[0, up, :].astype(BF16)
        wd_bf[slot, down, :] = wd_ref[0, down, :].astype(BF16)

    def step(cur):
        for r in range(n_chunks):
            stage(1 - cur, r)
            rows = slice(r * ROW_CHUNK, (r + 1) * ROW_CHUNK)
            packed = xs_ref[0, rows, :]
            half = D_MODEL // 2
            x_lo = pltpu.unpack_elementwise(packed, index=0, packed_dtype=BF16, unpacked_dtype=F32).astype(BF16)
            x_hi = pltpu.unpack_elementwise(packed, index=1, packed_dtype=BF16, unpacked_dtype=F32).astype(BF16)
            a = _dot(x_lo, wg_bf[cur, :half, :]) + _dot(x_hi, wg_bf[cur, half:, :])
            u = _dot(x_lo, wu_bf[cur, :half, :]) + _dot(x_hi, wu_bf[cur, half:, :])
            hidden = (a / (1.0 + jnp.exp(-a))) * u
            total = jnp.where(f == 0, 0.0, acc_ref[rows, :]) + _dot(hidden.astype(BF16), wd_bf[cur])
            acc_ref[rows, :] = total
            y_ref[0, rows, :] = total.astype(BF16)

    @pl.when(t == 0)
    def _():
        for part in range(n_chunks):
            stage(0, part)

    @pl.when(t % 2 == 1)
    def _():
        step(0)

    @pl.when((t % 2 == 0) & (t > 0))
    def _():
        step(1)


def _experts(xs, w_gate, w_up, w_down, first_expert):
    rows = BATCH * CAPACITY
    n_f = EXPERT_HIDDEN // FFN_TF
    last = GROUP_EXPERTS * n_f - 1

    def staged(t):
        return jnp.minimum(t, last)

    def computed(t):
        return jnp.maximum(t - 1, 0) // n_f

    return pl.pallas_call(
        _expert_kernel,
        out_shape=jax.ShapeDtypeStruct((GROUP_EXPERTS, rows, D_MODEL), BF16),
        grid=(GROUP_EXPERTS * n_f + 1,),
        in_specs=[
            pl.BlockSpec((1, rows, D_MODEL // 2), lambda t: (computed(t), 0, 0)),
            pl.BlockSpec((1, D_MODEL, FFN_TF), lambda t: (first_expert + staged(t) // n_f, 0, staged(t) % n_f)),
            pl.BlockSpec((1, D_MODEL, FFN_TF), lambda t: (first_expert + staged(t) // n_f, 0, staged(t) % n_f)),
            pl.BlockSpec((1, FFN_TF, D_MODEL), lambda t: (first_expert + staged(t) // n_f, staged(t) % n_f, 0)),
        ],
        out_specs=pl.BlockSpec((1, rows, D_MODEL), lambda t: (computed(t), 0, 0)),
        scratch_shapes=[
            pltpu.VMEM((rows, D_MODEL), F32),
            pltpu.VMEM((2, D_MODEL, FFN_TF), BF16),
            pltpu.VMEM((2, D_MODEL, FFN_TF), BF16),
            pltpu.VMEM((2, FFN_TF, D_MODEL), BF16),
        ],
        compiler_params=_params(("arbitrary",)),
        name="experts",
    )(xs, w_gate, w_up, w_down)


def _combine_kernel(rank_ref, gate_ref, *refs):
    y_refs, (x1_ref, mod_ref, fg_ref, o_ref) = refs[:N_GROUPS], refs[N_GROUPS:]
    slot = lax.broadcasted_iota(jnp.int32, (1, CAPACITY), 1).astype(F32)
    acc = jnp.zeros((COMBINE_TQ, D_MODEL), F32)
    for e in range(N_EXPERTS):
        rank = rank_ref[0, :, e:e + 1]
        gate = gate_ref[0, :, e:e + 1]
        scatter = jnp.where(rank == slot, gate, 0.0).astype(BF16)
        acc = acc + _dot(scatter, y_refs[e // GROUP_EXPERTS][e % GROUP_EXPERTS])
    x2 = x1_ref[...] + mod_ref[0, 5:6, :] * acc
    ms = jnp.mean(x2 * x2, axis=-1, keepdims=True)
    o_ref[...] = x2 * lax.rsqrt(ms + EPS) * fg_ref[...]


def _combine(rank_t, gate_t, ys, x1, mod3, final_g):
    per_b = SEQ // COMBINE_TQ
    return pl.pallas_call(
        _combine_kernel,
        out_shape=jax.ShapeDtypeStruct((BATCH * SEQ, D_MODEL), F32),
        grid=(BATCH, per_b),
        in_specs=[
            pl.BlockSpec((1, COMBINE_TQ, N_EXPERTS), lambda b, t: (b, t, 0)),
            pl.BlockSpec((1, COMBINE_TQ, N_EXPERTS), lambda b, t: (b, t, 0)),
            *[pl.BlockSpec((GROUP_EXPERTS, CAPACITY, D_MODEL), lambda b, t: (0, b, 0)) for _ in ys],
            pl.BlockSpec((COMBINE_TQ, D_MODEL), lambda b, t: (b * per_b + t, 0)),
            pl.BlockSpec((1, 6, D_MODEL), lambda b, t: (b, 0, 0)),
            pl.BlockSpec((1, D_MODEL), lambda b, t: (0, 0)),
        ],
        out_specs=pl.BlockSpec((COMBINE_TQ, D_MODEL), lambda b, t: (b * per_b + t, 0)),
        compiler_params=_params(("arbitrary", "arbitrary")),
        name="combine",
    )(rank_t, gate_t, *ys, x1, mod3, final_g)


def _rope_tables():
    t = jnp.arange(SEQ)
    pos = jnp.stack([t // GRID_W, t % GRID_W], axis=1).astype(F32)
    per_axis = HEAD_DIM // 2
    inv = ROPE_THETA ** (-jnp.arange(0, per_axis, 2, dtype=F32) / per_axis)
    ang = pos[:, :, None] * inv
    cos, sin = jnp.cos(ang), jnp.sin(ang)
    zero = jnp.zeros_like(sin)

    def lanes(first_half, second_half):
        one_map = jnp.stack([first_half, second_half], axis=2).reshape(SEQ, HEAD_DIM)
        return jnp.concatenate([one_map, one_map], axis=1)

    return lanes(cos, cos), lanes(-sin, zero), lanes(zero, sin)


def kernel(x, c, ctx, c_ctx, norm1_g, norm2_g, w_ada, b_ada, w_in, conv_w, w_out_conv, lambda_q1,
           lambda_k1, lambda_q2, lambda_k2, subln_g, w_o_attn, w_out, w_router, w_gate_e, w_up_e,
           w_down_e, final_g):
    cond = jnp.concatenate([c, c_ctx[None, :], jnp.zeros((COND_ROWS - BATCH - 1, D_MODEL), F32)], axis=0)
    mod3 = _ada(cond, w_ada[0], b_ada).reshape(COND_ROWS, 6, D_MODEL)

    w_in_bf = w_in[0].astype(BF16)
    cos, sin_hi, sin_lo = _rope_tables()
    q_scale = math.log2(math.e) * HEAD_DIM ** -0.5
    parts = _inproj(x, mod3, norm1_g, w_in_bf, conv_w[0], cos, sin_hi, sin_lo, q_scale)
    ctx_kv = _ctxproj(ctx.reshape(BATCH * CTX_LEN, D_MODEL), mod3, norm1_g, w_in_bf)

    lam_params = jnp.concatenate([lambda_q1, lambda_k1, lambda_q2, lambda_k2], axis=0)
    attn_o = _attn(lam_params, parts, ctx_kv, subln_g)

    wr_t = w_router[0].T
    wr_hi = wr_t.astype(BF16)
    wr_lo = (wr_t - wr_hi.astype(F32)).astype(BF16)
    x1, h2, logits_t = _merge(parts, attn_o, x.reshape(BATCH * SEQ, D_MODEL), mod3, norm2_g,
                              w_out_conv[0].astype(BF16), w_o_attn[0].astype(BF16), w_out[0].astype(BF16),
                              wr_hi, wr_lo)

    rank_e, rank_t, gate_t = _route(logits_t)
    ys = []
    for first in range(0, N_EXPERTS, GROUP_EXPERTS):
        xs = _dispatch(rank_e, h2, first).reshape(GROUP_EXPERTS, BATCH * CAPACITY, D_MODEL // 2)
        ys.append(_experts(xs, w_gate_e[0], w_up_e[0], w_down_e[0], first))
    out = _combine(rank_t, gate_t, ys, x1, mod3, final_g[None, :])
    return out.reshape(BATCH, SEQ, D_MODEL)
```

```python
import functools
import math

import jax
import jax.numpy as jnp
from jax import lax
from jax.experimental import pallas as pl
from jax.experimental.pallas import tpu as pltpu
from jax.experimental.pallas import tpu_sc as plsc

D_MODEL = 1024
BATCH = 8
SEQ = 2048
GRID_W = 64
CTX_LEN = 256
N_HEADS = 8
HEAD_DIM = 64
V_DIM = 2 * HEAD_DIM
N_EXPERTS = 16
EXPERT_HIDDEN = 2048
CAPACITY = 2 * SEQ // N_EXPERTS
ROPE_THETA = 10000.0
EPS = 1e-6
LAM_INIT = 0.8 - 0.6 * math.exp(-0.3 * 0)
N_PARTS = 8
N_KEYS = CTX_LEN + SEQ
N_GROUPS = 4
GROUP_EXPERTS = N_EXPERTS // N_GROUPS

LANES = 128
SC_CORES = 2
SC_SUBCORES = 16
SC_LANES = 16
F32 = jnp.float32
BF16 = jnp.bfloat16

ROW_CHUNK = 512
COND_ROWS = 16
ADA_TN = 512
ATTN_TQ = 2048
ATTN_SQ = 256
MERGE_TM = 512
FFN_TF = 512
COMBINE_TQ = 512
VMEM_LIMIT = 56 * 1024 * 1024


def _dot(a, b):
    return jnp.dot(a, b, preferred_element_type=F32)


def _dot_nt(a, b):
    return lax.dot_general(a, b, (((1,), (1,)), ((), ())), preferred_element_type=F32)


def _params(sem, vmem=VMEM_LIMIT):
    return pltpu.CompilerParams(dimension_semantics=sem, vmem_limit_bytes=vmem)


def _ada_kernel(cond_ref, w_ref, b_ref, o_ref):
    c = cond_ref[...]
    s = c / (1.0 + jnp.exp(-c))
    o_ref[...] = _dot(s.astype(BF16), w_ref[...].astype(BF16)) + b_ref[...]


def _ada(cond, w_ada, b_ada):
    n = w_ada.shape[1]
    return pl.pallas_call(
        _ada_kernel,
        out_shape=jax.ShapeDtypeStruct((COND_ROWS, n), F32),
        grid=(n // ADA_TN,),
        in_specs=[
            pl.BlockSpec((COND_ROWS, D_MODEL), lambda i: (0, 0)),
            pl.BlockSpec((D_MODEL, ADA_TN), lambda i: (0, i)),
            pl.BlockSpec((1, ADA_TN), lambda i: (0, i)),
        ],
        out_specs=pl.BlockSpec((COND_ROWS, ADA_TN), lambda i: (0, i)),
        compiler_params=_params(("arbitrary",)),
        name="ada",
    )(cond, w_ada, b_ada)


def _norm_modulate(x, g, shift, scale):
    ms = jnp.mean(x * x, axis=-1, keepdims=True)
    return (x * lax.rsqrt(ms + EPS) * g) * (1.0 + scale) + shift


def _rope(a, cos, sin_hi, sin_lo):
    return a * cos + pltpu.roll(a, LANES - 16, 1) * sin_hi + pltpu.roll(a, 16, 1) * sin_lo


def _inproj_kernel(x_ref, mod_ref, g_ref, w_ref, cw_ref, cos_ref, shi_ref, slo_ref, o_ref,
                   hx_ref, z_ref, *, q_scale):
    j = pl.program_id(1)
    chunks = [slice(r * ROW_CHUNK, (r + 1) * ROW_CHUNK) for r in range(SEQ // ROW_CHUNK)]

    @pl.when(j == 0)
    def _():
        shift = mod_ref[0, 0:1, :]
        scale = mod_ref[0, 1:2, :]
        for rows in chunks:
            hx_ref[rows, :] = _norm_modulate(x_ref[0, rows, :], g_ref[...], shift, scale).astype(BF16)
        z_ref[0:8, :] = jnp.zeros((8, D_MODEL), F32)
        z_ref[SEQ + 8:SEQ + 16, :] = jnp.zeros((8, D_MODEL), F32)
        for rows in chunks:
            z_ref[pl.ds(rows.start + 8, ROW_CHUNK), :] = _dot(hx_ref[rows, :], w_ref[...])

    @pl.when(j == 1)
    def _():
        for rows in chunks:
            zr = pl.ds(rows.start + 8, ROW_CHUNK)
            z_ref[zr, :] = z_ref[zr, :] * _dot(hx_ref[rows, :], w_ref[...])

    @pl.when(j == 2)
    def _():
        first_row = lax.broadcasted_iota(jnp.int32, (8, 1), 0) == 0
        last_row = lax.broadcasted_iota(jnp.int32, (8, 1), 0) == 7
        for rows in chunks:
            base = rows.start + 8
            zc = z_ref[pl.ds(base, ROW_CHUNK), :]
            before = pltpu.roll(zc * cw_ref[0:1, :], 1, 0)
            after = pltpu.roll(zc * cw_ref[2:3, :], ROW_CHUNK - 1, 0)
            edge_b = jnp.where(first_row, z_ref[pl.ds(base - 8, 8), :][7:8, :] * cw_ref[0:1, :], before[0:8, :])
            edge_a = jnp.where(last_row, z_ref[pl.ds(base + ROW_CHUNK, 8), :][0:1, :] * cw_ref[2:3, :],
                               after[ROW_CHUNK - 8:, :])
            before = jnp.concatenate([edge_b, before[8:, :]], axis=0)
            after = jnp.concatenate([after[:ROW_CHUNK - 8, :], edge_a], axis=0)
            y = before + zc * cw_ref[1:2, :] + after
            o_ref[0, rows, :] = (_dot(hx_ref[rows, :], w_ref[...]) * y).astype(BF16)

    def rope_part(scale):
        for rows in chunks:
            acc = _dot(hx_ref[rows, :], w_ref[...])
            cos, shi, slo = cos_ref[rows, :], shi_ref[rows, :], slo_ref[rows, :]
            for h in range(D_MODEL // LANES):
                cols = slice(h * LANES, (h + 1) * LANES)
                r = _rope(acc[:, cols], cos, shi, slo)
                if scale is not None:
                    r = r * scale
                o_ref[0, rows, cols] = r.astype(BF16)

    @pl.when(j == 3)
    def _():
        rope_part(q_scale)

    @pl.when(j == 4)
    def _():
        rope_part(None)

    @pl.when(j == 5)
    def _():
        for rows in chunks:
            o_ref[0, rows, :] = _dot(hx_ref[rows, :], w_ref[...]).astype(BF16)

    @pl.when(j >= 6)
    def _():
        for rows in chunks:
            acc = _dot(hx_ref[rows, :], w_ref[...])
            o_ref[0, rows, :] = (1.0 / (1.0 + jnp.exp(-acc))).astype(BF16)


def _inproj(x, mod3, norm_g, w_in_bf, conv_w, cos, sin_hi, sin_lo, q_scale):
    def w_map(b, j):
        return (0, jnp.where(j == 1, 2, jnp.where(j == 2, 1, j)))

    tab = pl.BlockSpec((SEQ, LANES), lambda b, j: (0, 0))
    return pl.pallas_call(
        functools.partial(_inproj_kernel, q_scale=q_scale),
        out_shape=jax.ShapeDtypeStruct((N_PARTS - 2, BATCH * SEQ, D_MODEL), BF16),
        grid=(BATCH, N_PARTS),
        in_specs=[
            pl.BlockSpec((1, SEQ, D_MODEL), lambda b, j: (b, 0, 0)),
            pl.BlockSpec((1, 6, D_MODEL), lambda b, j: (b, 0, 0)),
            pl.BlockSpec((1, D_MODEL), lambda b, j: (0, 0)),
            pl.BlockSpec((D_MODEL, D_MODEL), w_map),
            pl.BlockSpec((3, D_MODEL), lambda b, j: (0, 0)),
            tab, tab, tab,
        ],
        out_specs=pl.BlockSpec((1, SEQ, D_MODEL), lambda b, j: (jnp.maximum(j - 2, 0), b, 0)),
        scratch_shapes=[pltpu.VMEM((SEQ, D_MODEL), BF16), pltpu.VMEM((SEQ + 16, D_MODEL), F32)],
        compiler_params=_params(("arbitrary", "arbitrary")),
        name="inproj",
    )(x, mod3, norm_g, w_in_bf, conv_w, cos, sin_hi, sin_lo)


def _ctxproj_kernel(c_ref, mod_ref, g_ref, w_ref, o_ref, hc_ref):
    j = pl.program_id(0)
    chunks = [slice(r * ROW_CHUNK, (r + 1) * ROW_CHUNK) for r in range(BATCH * CTX_LEN // ROW_CHUNK)]

    @pl.when(j == 0)
    def _():
        shift = mod_ref[0, 0:1, :]
        scale = mod_ref[0, 1:2, :]
        for rows in chunks:
            hc_ref[rows, :] = _norm_modulate(c_ref[rows, :], g_ref[...], shift, scale).astype(BF16)

    for rows in chunks:
        o_ref[0, rows, :] = _dot(hc_ref[rows, :], w_ref[...]).astype(BF16)


def _ctxproj(ctx2, mod3, norm_g, w_in_bf):
    rows = BATCH * CTX_LEN
    return pl.pallas_call(
        _ctxproj_kernel,
        out_shape=jax.ShapeDtypeStruct((2, rows, D_MODEL), BF16),
        grid=(2,),
        in_specs=[
            pl.BlockSpec((rows, D_MODEL), lambda j: (0, 0)),
            pl.BlockSpec((1, 6, D_MODEL), lambda j: (BATCH, 0, 0)),
            pl.BlockSpec((1, D_MODEL), lambda j: (0, 0)),
            pl.BlockSpec((D_MODEL, D_MODEL), lambda j: (0, 4 + j)),
        ],
        out_specs=pl.BlockSpec((1, rows, D_MODEL), lambda j: (j, 0, 0)),
        scratch_shapes=[pltpu.VMEM((rows, D_MODEL), BF16)],
        compiler_params=_params(("arbitrary",)),
        name="ctxproj",
    )(ctx2, mod3, norm_g, w_in_bf)


def _attn_kernel(lp_ref, q_ref, k_ref, v_ref, kc_ref, vc_ref, g_ref, o_ref, k_all, v_ext, s_ref, m_ref):
    lp = lp_ref[...]
    lam = (jnp.exp(jnp.sum(lp[0:1] * lp[1:2], axis=-1, keepdims=True))
           - jnp.exp(jnp.sum(lp[2:3] * lp[3:4], axis=-1, keepdims=True)) + LAM_INIT)
    k_all[0:CTX_LEN, :] = kc_ref[0]
    k_all[CTX_LEN:N_KEYS, :] = k_ref[0]
    v_ext[0:CTX_LEN, 0:LANES] = vc_ref[0]
    v_ext[CTX_LEN:N_KEYS, 0:LANES] = v_ref[0]
    v_ext[:, LANES:2 * LANES] = jnp.ones((N_KEYS, LANES), BF16)
    first_map = lax.broadcasted_iota(jnp.int32, (1, LANES), 1) < HEAD_DIM

    def scores(u):
        q = q_ref[0, u * ATTN_SQ:(u + 1) * ATTN_SQ, :]
        zero = jnp.zeros_like(q)
        for mp, qm in enumerate((jnp.where(first_map, q, zero), jnp.where(first_map, zero, q))):
            s = _dot_nt(qm, k_all[...])
            s_ref[u % 2, mp] = s
            m_ref[u % 2, mp] = jnp.max(s, axis=-1, keepdims=True)

    def values(u):
        o = []
        for mp in range(2):
            p = jnp.exp2(s_ref[u % 2, mp] - m_ref[u % 2, mp])
            o.append(_dot(p.astype(BF16), v_ext[...]))
        a = o[0][:, :LANES] / o[0][:, LANES:] - o[1][:, :LANES] * (lam / o[1][:, LANES:])
        ms = jnp.mean(a * a, axis=-1, keepdims=True)
        o_ref[u * ATTN_SQ:(u + 1) * ATTN_SQ, :] = (
            (a * lax.rsqrt(ms + EPS) * g_ref[...]) * (1.0 - LAM_INIT)).astype(BF16)

    n_units = ATTN_TQ // ATTN_SQ
    scores(0)
    for u in range(n_units):
        if u + 1 < n_units:
            scores(u + 1)
        values(u)


def _attn(lam_params, parts, ctx_kv, subln_g):
    nq = SEQ // ATTN_TQ
    return pl.pallas_call(
        _attn_kernel,
        out_shape=jax.ShapeDtypeStruct((BATCH * SEQ, D_MODEL), BF16),
        grid=(BATCH, N_HEADS, nq),
        in_specs=[
            pl.BlockSpec((4, HEAD_DIM), lambda b, h, i: (0, 0)),
            pl.BlockSpec((1, ATTN_TQ, LANES), lambda b, h, i: (1, b * nq + i, h)),
            pl.BlockSpec((1, SEQ, LANES), lambda b, h, i: (2, b, h)),
            pl.BlockSpec((1, SEQ, LANES), lambda b, h, i: (3, b, h)),
            pl.BlockSpec((1, CTX_LEN, LANES), lambda b, h, i: (0, b, h)),
            pl.BlockSpec((1, CTX_LEN, LANES), lambda b, h, i: (1, b, h)),
            pl.BlockSpec((1, V_DIM), lambda b, h, i: (0, 0)),
        ],
        out_specs=pl.BlockSpec((ATTN_TQ, LANES), lambda b, h, i: (b * nq + i, h)),
        scratch_shapes=[
            pltpu.VMEM((N_KEYS, LANES), BF16),
            pltpu.VMEM((N_KEYS, 2 * LANES), BF16),
            pltpu.VMEM((2, 2, ATTN_SQ, N_KEYS), F32),
            pltpu.VMEM((2, 2, ATTN_SQ, 1), F32),
        ],
        compiler_params=_params(("arbitrary", "arbitrary", "arbitrary")),
        name="attn",
    )(lam_params, parts, parts, parts, ctx_kv, ctx_kv, subln_g)


def _merge_kernel(yb_ref, s6_ref, s7_ref, o_ref, x_ref, mod_ref, g2_ref, wc_ref, wa_ref, wo_ref,
                  wrh_ref, wrl_ref, x1_ref, h2_ref, lg_ref):
    y_conv = _dot(yb_ref[0], wc_ref[...])
    y_attn = _dot(o_ref[...], wa_ref[...])
    merged = s6_ref[0].astype(F32) * y_conv + s7_ref[0].astype(F32) * y_attn
    mix = _dot(merged.astype(BF16), wo_ref[...])
    x1 = x_ref[...] + mod_ref[0, 2:3, :] * mix
    x1_ref[...] = x1
    h2 = _norm_modulate(x1, g2_ref[...], mod_ref[0, 3:4, :], mod_ref[0, 4:5, :])
    h_hi = h2.astype(BF16)
    h2_ref[...] = pltpu.pack_elementwise([h2[:, :D_MODEL // 2], h2[:, D_MODEL // 2:]], packed_dtype=BF16)
    h_lo = (h2 - h_hi.astype(F32)).astype(BF16)
    lg_ref[0] = _dot_nt(wrh_ref[...], h_hi) + (_dot_nt(wrh_ref[...], h_lo) + _dot_nt(wrl_ref[...], h_hi))


def _merge(parts, attn_o, x2, mod3, norm2_g, w_conv_bf, w_attn_bf, w_out_bf, wr_hi, wr_lo):
    per_b = SEQ // MERGE_TM
    sq = pl.BlockSpec((D_MODEL, D_MODEL), lambda i: (0, 0), pipeline_mode=pl.Buffered(1))
    wr = pl.BlockSpec((N_EXPERTS, D_MODEL), lambda i: (0, 0), pipeline_mode=pl.Buffered(1))
    tile = pl.BlockSpec((MERGE_TM, D_MODEL), lambda i: (i, 0))
    return pl.pallas_call(
        _merge_kernel,
        out_shape=(
            jax.ShapeDtypeStruct((BATCH * SEQ, D_MODEL), F32),
            jax.ShapeDtypeStruct((BATCH * SEQ, D_MODEL // 2), jnp.uint32),
            jax.ShapeDtypeStruct((BATCH, N_EXPERTS, SEQ), F32),
        ),
        grid=(BATCH * per_b,),
        in_specs=[
            pl.BlockSpec((1, MERGE_TM, D_MODEL), lambda i: (0, i, 0)),
            pl.BlockSpec((1, MERGE_TM, D_MODEL), lambda i: (4, i, 0)),
            pl.BlockSpec((1, MERGE_TM, D_MODEL), lambda i: (5, i, 0)),
            tile, tile,
            pl.BlockSpec((1, 6, D_MODEL), lambda i: (i // per_b, 0, 0)),
            pl.BlockSpec((1, D_MODEL), lambda i: (0, 0)),
            sq, sq, sq, wr, wr,
        ],
        out_specs=(
            tile,
            pl.BlockSpec((MERGE_TM, D_MODEL // 2), lambda i: (i, 0)),
            pl.BlockSpec((1, N_EXPERTS, MERGE_TM), lambda i: (i // per_b, 0, i % per_b)),
        ),
        compiler_params=_params(("arbitrary",)),
        name="merge",
    )(parts, parts, parts, attn_o, x2, mod3, norm2_g, w_conv_bf, w_attn_bf, w_out_bf, wr_hi, wr_lo)


def _route_kernel(lg_ref, rank_e_ref, rank_t_ref, gate_t_ref):
    lg = lg_ref[...]
    ex = jnp.exp(lg - jnp.max(lg, axis=1, keepdims=True))
    aff = (ex / jnp.sum(ex, axis=1, keepdims=True)).reshape(BATCH * N_EXPERTS, SEQ)
    bits = lax.bitcast_convert_type(aff, jnp.int32)

    def count(mask):
        return jnp.sum(jnp.where(mask, 1.0, 0.0), axis=1, keepdims=True)

    def search(i, t):
        cand = t + lax.shift_left(jnp.int32(1), 30 - i)
        return jnp.where(count(bits >= cand) >= CAPACITY, cand, t)

    thr = lax.fori_loop(0, 31, search, jnp.zeros((BATCH * N_EXPERTS, 1), jnp.int32))
    need = CAPACITY - count(bits > thr)
    before = (lax.broadcasted_iota(jnp.int32, (LANES, LANES), 0)
              < lax.broadcasted_iota(jnp.int32, (LANES, LANES), 1)).astype(BF16)
    n_tied = jnp.zeros((BATCH * N_EXPERTS, 1), F32)
    n_sel = jnp.zeros((BATCH * N_EXPERTS, 1), F32)
    for c in range(SEQ // LANES):
        cols = slice(c * LANES, (c + 1) * LANES)
        bits_c = bits[:, cols]
        tied_c = jnp.where(bits_c == thr, 1.0, 0.0)
        tied_before = _dot(tied_c.astype(BF16), before) + n_tied
        sel_c = jnp.where(bits_c > thr, 1.0, jnp.where(tied_before < need, tied_c, 0.0))
        rank = _dot(sel_c.astype(BF16), before) + n_sel
        rank_c = jnp.where(sel_c > 0.0, rank, -1.0)
        gate_c = jnp.where(sel_c > 0.0, aff[:, cols], 0.0)
        n_tied = n_tied + jnp.sum(tied_c, axis=1, keepdims=True)
        n_sel = n_sel + jnp.sum(sel_c, axis=1, keepdims=True)
        rank_e_ref[:, cols] = rank_c.astype(jnp.int32)
        rank_ct = rank_c.T
        gate_ct = gate_c.T
        for b in range(BATCH):
            ecols = slice(b * N_EXPERTS, (b + 1) * N_EXPERTS)
            rank_t_ref[b, cols, :] = rank_ct[:, ecols]
            gate_t_ref[b, cols, :] = gate_ct[:, ecols]


def _route(logits_t):
    return pl.pallas_call(
        _route_kernel,
        out_shape=(
            jax.ShapeDtypeStruct((BATCH * N_EXPERTS, SEQ), jnp.int32),
            jax.ShapeDtypeStruct((BATCH, SEQ, N_EXPERTS), F32),
            jax.ShapeDtypeStruct((BATCH, SEQ, N_EXPERTS), F32),
        ),
        compiler_params=pltpu.CompilerParams(vmem_limit_bytes=VMEM_LIMIT),
        name="route",
    )(logits_t)


def _dispatch(rank_e, h_packed, first_expert):
    per_worker = BATCH * GROUP_EXPERTS // (SC_CORES * SC_SUBCORES)
    half = CAPACITY // 2
    mesh = plsc.VectorSubcoreMesh(core_axis_name="c", subcore_axis_name="s")

    @functools.partial(
        pl.kernel, mesh=mesh,
        out_type=jax.ShapeDtypeStruct((GROUP_EXPERTS * BATCH * CAPACITY, D_MODEL // 2), h_packed.dtype),
        scratch_types=[
            pltpu.VMEM((SEQ,), jnp.int32),
            pltpu.VMEM((2, half), jnp.int32),
            pltpu.VMEM((half, D_MODEL // 2), h_packed.dtype),
            pltpu.SemaphoreType.DMA,
        ],
        compiler_params=pltpu.CompilerParams(needs_layout_passes=False),
        name="dispatch",
    )
    def k(table_hbm, rank_hbm, out_hbm, rank_v, idx_v, rows_v, sem):
        worker = lax.axis_index("s") * SC_CORES + lax.axis_index("c")
        expert = worker // (BATCH // per_worker)

        @pl.loop(0, per_worker)
        def _(p):
            sample = (worker % (BATCH // per_worker)) * per_worker + p
            pltpu.sync_copy(rank_hbm.at[sample * N_EXPERTS + first_expert + expert], rank_v)

            @pl.loop(0, SEQ // SC_LANES)
            def _(i):
                rank = rank_v[pl.ds(i * SC_LANES, SC_LANES)]
                row = lax.iota(jnp.int32, SC_LANES) + (i * SC_LANES + sample * SEQ)
                plsc.store_scatter(idx_v, [lax.shift_right_arithmetic(rank, 7), rank & (half - 1)], row,
                                   mask=rank >= 0)

            for c in range(2):
                pltpu.async_copy(table_hbm.at[idx_v.at[c]], rows_v, sem).wait()
                first = (expert * BATCH + sample) * CAPACITY + c * half
                pltpu.sync_copy(rows_v, out_hbm.at[pl.ds(first, half)])

    return k(h_packed, rank_e)


def _expert_kernel(xs_ref, wg_ref, wu_ref, wd_ref, y_ref, acc_ref, wg_bf, wu_bf, wd_bf):
    t = pl.program_id(0)
    n_f = EXPERT_HIDDEN // FFN_TF
    f = (t + n_f - 1) % n_f

    n_chunks = BATCH * CAPACITY // ROW_CHUNK

    def stage(slot, part):
        up = slice(part * D_MODEL // n_chunks, (part + 1) * D_MODEL // n_chunks)
        down = slice(part * FFN_TF // n_chunks, (part + 1) * FFN_TF // n_chunks)
        wg_bf[slot, up, :] = wg_ref[0, up, :].astype(BF16)
        wu_bf[slot, up, :] = wu_ref[0, up, :].astype(BF16)
        wd_bf[slot, down, :] = wd_ref[0, down, :].astype(BF16)

    def step(cur):
        for r in range(n_chunks):
            stage(1 - cur, r)
            rows = slice(r * ROW_CHUNK, (r + 1) * ROW_CHUNK)
            packed = xs_ref[0, rows, :]
            half = D_MODEL // 2
            x_lo = pltpu.unpack_elementwise(packed, index=0, packed_dtype=BF16, unpacked_dtype=F32).astype(BF16)
            x_hi = pltpu.unpack_elementwise(packed, index=1, packed_dtype=BF16, unpacked_dtype=F32).astype(BF16)
            a = _dot(x_lo, wg_bf[cur, :half, :]) + _dot(x_hi, wg_bf[cur, half:, :])
            u = _dot(x_lo, wu_bf[cur, :half, :]) + _dot(x_hi, wu_bf[cur, half:, :])
            hidden = (a / (1.0 + jnp.exp(-a))) * u
            total = jnp.where(f == 0, 0.0, acc_ref[rows, :]) + _dot(hidden.astype(BF16), wd_bf[cur])
            acc_ref[rows, :] = total
            y_ref[0, rows, :] = total.astype(BF16)

    @pl.when(t == 0)
    def _():
        for part in range(n_chunks):
            stage(0, part)

    @pl.when(t % 2 == 1)
    def _():
        step(0)

    @pl.when((t % 2 == 0) & (t > 0))
    def _():
        step(1)


def _experts(xs, w_gate, w_up, w_down, first_expert):
    rows = BATCH * CAPACITY
    n_f = EXPERT_HIDDEN // FFN_TF
    last = GROUP_EXPERTS * n_f - 1

    def staged(t):
        return jnp.minimum(t, last)

    def computed(t):
        return jnp.maximum(t - 1, 0) // n_f

    return pl.pallas_call(
        _expert_kernel,
        out_shape=jax.ShapeDtypeStruct((GROUP_EXPERTS, rows, D_MODEL), BF16),
        grid=(GROUP_EXPERTS * n_f + 1,),
        in_specs=[
            pl.BlockSpec((1, rows, D_MODEL // 2), lambda t: (computed(t), 0, 0)),
            pl.BlockSpec((1, D_MODEL, FFN_TF), lambda t: (first_expert + staged(t) // n_f, 0, staged(t) % n_f)),
            pl.BlockSpec((1, D_MODEL, FFN_TF), lambda t: (first_expert + staged(t) // n_f, 0, staged(t) % n_f)),
            pl.BlockSpec((1, FFN_TF, D_MODEL), lambda t: (first_expert + staged(t) // n_f, staged(t) % n_f, 0)),
        ],
        out_specs=pl.BlockSpec((1, rows, D_MODEL), lambda t: (computed(t), 0, 0)),
        scratch_shapes=[
            pltpu.VMEM((rows, D_MODEL), F32),
            pltpu.VMEM((2, D_MODEL, FFN_TF), BF16),
            pltpu.VMEM((2, D_MODEL, FFN_TF), BF16),
            pltpu.VMEM((2, FFN_TF, D_MODEL), BF16),
        ],
        compiler_params=_params(("arbitrary",)),
        name="experts",
    )(xs, w_gate, w_up, w_down)


def _combine_kernel(rank_ref, gate_ref, *refs):
    y_refs, (x1_ref, mod_ref, fg_ref, o_ref) = refs[:N_GROUPS], refs[N_GROUPS:]
    slot = lax.broadcasted_iota(jnp.int32, (1, CAPACITY), 1).astype(F32)
    acc = jnp.zeros((COMBINE_TQ, D_MODEL), F32)
    for e in range(N_EXPERTS):
        rank = rank_ref[0, :, e:e + 1]
        gate = gate_ref[0, :, e:e + 1]
        scatter = jnp.where(rank == slot, gate, 0.0).astype(BF16)
        acc = acc + _dot(scatter, y_refs[e // GROUP_EXPERTS][e % GROUP_EXPERTS])
    x2 = x1_ref[...] + mod_ref[0, 5:6, :] * acc
    ms = jnp.mean(x2 * x2, axis=-1, keepdims=True)
    o_ref[...] = x2 * lax.rsqrt(ms + EPS) * fg_ref[...]


def _combine(rank_t, gate_t, ys, x1, mod3, final_g):
    per_b = SEQ // COMBINE_TQ
    return pl.pallas_call(
        _combine_kernel,
        out_shape=jax.ShapeDtypeStruct((BATCH * SEQ, D_MODEL), F32),
        grid=(BATCH, per_b),
        in_specs=[
            pl.BlockSpec((1, COMBINE_TQ, N_EXPERTS), lambda b, t: (b, t, 0)),
            pl.BlockSpec((1, COMBINE_TQ, N_EXPERTS), lambda b, t: (b, t, 0)),
            *[pl.BlockSpec((GROUP_EXPERTS, CAPACITY, D_MODEL), lambda b, t: (0, b, 0)) for _ in ys],
            pl.BlockSpec((COMBINE_TQ, D_MODEL), lambda b, t: (b * per_b + t, 0)),
            pl.BlockSpec((1, 6, D_MODEL), lambda b, t: (b, 0, 0)),
            pl.BlockSpec((1, D_MODEL), lambda b, t: (0, 0)),
        ],
        out_specs=pl.BlockSpec((COMBINE_TQ, D_MODEL), lambda b, t: (b * per_b + t, 0)),
        compiler_params=_params(("arbitrary", "arbitrary")),
        name="combine",
    )(rank_t, gate_t, *ys, x1, mod3, final_g)


def _rope_tables():
    t = jnp.arange(SEQ)
    pos = jnp.stack([t // GRID_W, t % GRID_W], axis=1).astype(F32)
    per_axis = HEAD_DIM // 2
    inv = ROPE_THETA ** (-jnp.arange(0, per_axis, 2, dtype=F32) / per_axis)
    ang = pos[:, :, None] * inv
    cos, sin = jnp.cos(ang), jnp.sin(ang)
    zero = jnp.zeros_like(sin)

    def lanes(first_half, second_half):
        one_map = jnp.stack([first_half, second_half], axis=2).reshape(SEQ, HEAD_DIM)
        return jnp.concatenate([one_map, one_map], axis=1)

    return lanes(cos, cos), lanes(-sin, zero), lanes(zero, sin)


def kernel(x, c, ctx, c_ctx, norm1_g, norm2_g, w_ada, b_ada, w_in, conv_w, w_out_conv, lambda_q1,
           lambda_k1, lambda_q2, lambda_k2, subln_g, w_o_attn, w_out, w_router, w_gate_e, w_up_e,
           w_down_e, final_g):
    cond = jnp.concatenate([c, c_ctx[None, :], jnp.zeros((COND_ROWS - BATCH - 1, D_MODEL), F32)], axis=0)
    mod3 = _ada(cond, w_ada[0], b_ada).reshape(COND_ROWS, 6, D_MODEL)

    w_in_bf = w_in[0].astype(BF16)
    cos, sin_hi, sin_lo = _rope_tables()
    q_scale = math.log2(math.e) * HEAD_DIM ** -0.5
    parts = _inproj(x, mod3, norm1_g, w_in_bf, conv_w[0], cos, sin_hi, sin_lo, q_scale)
    ctx_kv = _ctxproj(ctx.reshape(BATCH * CTX_LEN, D_MODEL), mod3, norm1_g, w_in_bf)

    lam_params = jnp.concatenate([lambda_q1, lambda_k1, lambda_q2, lambda_k2], axis=0)
    attn_o = _attn(lam_params, parts, ctx_kv, subln_g)

    wr_t = w_router[0].T
    wr_hi = wr_t.astype(BF16)
    wr_lo = (wr_t - wr_hi.astype(F32)).astype(BF16)
    x1, h2, logits_t = _merge(parts, attn_o, x.reshape(BATCH * SEQ, D_MODEL), mod3, norm2_g,
                              w_out_conv[0].astype(BF16), w_o_attn[0].astype(BF16), w_out[0].astype(BF16),
                              wr_hi, wr_lo)

    rank_e, rank_t, gate_t = _route(logits_t)
    ys = []
    for first in range(0, N_EXPERTS, GROUP_EXPERTS):
        xs = _dispatch(rank_e, h2, first).reshape(GROUP_EXPERTS, BATCH * CAPACITY, D_MODEL // 2)
        ys.append(_experts(xs, w_gate_e[0], w_up_e[0], w_down_e[0], first))
    out = _combine(rank_t, gate_t, ys, x1, mod3, final_g[None, :])
    return out.reshape(BATCH, SEQ, D_MODEL)
```

```python
import functools
import math

import jax
import jax.numpy as jnp
from jax import lax
from jax.experimental import pallas as pl
from jax.experimental.pallas import tpu as pltpu
from jax.experimental.pallas import tpu_sc as plsc

D_MODEL = 1024
BATCH = 8
SEQ = 2048
GRID_W = 64
CTX_LEN = 256
N_HEADS = 8
HEAD_DIM = 64
V_DIM = 2 * HEAD_DIM
N_EXPERTS = 16
EXPERT_HIDDEN = 2048
CAPACITY = 2 * SEQ // N_EXPERTS
ROPE_THETA = 10000.0
EPS = 1e-6
LAM_INIT = 0.8 - 0.6 * math.exp(-0.3 * 0)
N_PARTS = 8
N_KEYS = CTX_LEN + SEQ
N_GROUPS = 2
GROUP_EXPERTS = N_EXPERTS // N_GROUPS

LANES = 128
SC_CORES = 2
SC_SUBCORES = 16
SC_LANES = 16
F32 = jnp.float32
BF16 = jnp.bfloat16

ROW_CHUNK = 512
COND_ROWS = 16
ADA_TN = 512
ATTN_TQ = 2048
ATTN_SQ = 256
MERGE_TM = 1024
FFN_TF = 512
COMBINE_TQ = 512
VMEM_LIMIT = 56 * 1024 * 1024


def _dot(a, b):
    return jnp.dot(a, b, preferred_element_type=F32)


def _dot_nt(a, b):
    return lax.dot_general(a, b, (((1,), (1,)), ((), ())), preferred_element_type=F32)


def _params(sem, vmem=VMEM_LIMIT):
    return pltpu.CompilerParams(dimension_semantics=sem, vmem_limit_bytes=vmem)


def _ada_kernel(cond_ref, w_ref, b_ref, o_ref):
    c = cond_ref[...]
    s = c / (1.0 + jnp.exp(-c))
    o_ref[...] = _dot(s.astype(BF16), w_ref[...].astype(BF16)) + b_ref[...]


def _ada(cond, w_ada, b_ada):
    n = w_ada.shape[1]
    return pl.pallas_call(
        _ada_kernel,
        out_shape=jax.ShapeDtypeStruct((COND_ROWS, n), F32),
        grid=(n // ADA_TN,),
        in_specs=[
            pl.BlockSpec((COND_ROWS, D_MODEL), lambda i: (0, 0)),
            pl.BlockSpec((D_MODEL, ADA_TN), lambda i: (0, i)),
            pl.BlockSpec((1, ADA_TN), lambda i: (0, i)),
        ],
        out_specs=pl.BlockSpec((COND_ROWS, ADA_TN), lambda i: (0, i)),
        compiler_params=_params(("arbitrary",)),
        name="ada",
    )(cond, w_ada, b_ada)


def _norm_modulate(x, g, shift, scale):
    ms = jnp.mean(x * x, axis=-1, keepdims=True)
    return (x * lax.rsqrt(ms + EPS) * g) * (1.0 + scale) + shift


def _rope(a, cos, sin_hi, sin_lo):
    return a * cos + pltpu.roll(a, LANES - 16, 1) * sin_hi + pltpu.roll(a, 16, 1) * sin_lo


def _inproj_kernel(x_ref, mod_ref, g_ref, w_ref, cw_ref, cos_ref, shi_ref, slo_ref, o_ref,
                   hx_ref, z_ref, w_bf, *, q_scale):
    t = pl.program_id(0)
    j = (t + N_PARTS - 1) % N_PARTS
    chunks = [slice(r * ROW_CHUNK, (r + 1) * ROW_CHUNK) for r in range(SEQ // ROW_CHUNK)]

    def stage(slot, part):
        rows = slice(part * D_MODEL // len(chunks), (part + 1) * D_MODEL // len(chunks))
        w_bf[slot, rows, :] = w_ref[rows, :].astype(BF16)

    def proj(part, r):
        stage(1 - part % 2, r)
        return _dot(hx_ref[chunks[r], :], w_bf[part % 2])

    def branch(part):
        return pl.when((t > 0) & (j == part))

    @pl.when(t == 0)
    def _():
        for r in range(len(chunks)):
            stage(0, r)

    @branch(0)
    def _():
        shift = mod_ref[0, 0:1, :]
        scale = mod_ref[0, 1:2, :]
        for rows in chunks:
            hx_ref[rows, :] = _norm_modulate(x_ref[0, rows, :], g_ref[...], shift, scale).astype(BF16)
        z_ref[0:8, :] = jnp.zeros((8, D_MODEL), F32)
        z_ref[SEQ + 8:SEQ + 16, :] = jnp.zeros((8, D_MODEL), F32)
        for r, rows in enumerate(chunks):
            z_ref[pl.ds(rows.start + 8, ROW_CHUNK), :] = proj(0, r)

    @branch(1)
    def _():
        for r, rows in enumerate(chunks):
            zr = pl.ds(rows.start + 8, ROW_CHUNK)
            z_ref[zr, :] = z_ref[zr, :] * proj(1, r)

    @branch(2)
    def _():
        first_row = lax.broadcasted_iota(jnp.int32, (8, 1), 0) == 0
        last_row = lax.broadcasted_iota(jnp.int32, (8, 1), 0) == 7
        for r, rows in enumerate(chunks):
            base = rows.start + 8
            zc = z_ref[pl.ds(base, ROW_CHUNK), :]
            before = pltpu.roll(zc * cw_ref[0:1, :], 1, 0)
            after = pltpu.roll(zc * cw_ref[2:3, :], ROW_CHUNK - 1, 0)
            edge_b = jnp.where(first_row, z_ref[pl.ds(base - 8, 8), :][7:8, :] * cw_ref[0:1, :], before[0:8, :])
            edge_a = jnp.where(last_row, z_ref[pl.ds(base + ROW_CHUNK, 8), :][0:1, :] * cw_ref[2:3, :],
                               after[ROW_CHUNK - 8:, :])
            before = jnp.concatenate([edge_b, before[8:, :]], axis=0)
            after = jnp.concatenate([after[:ROW_CHUNK - 8, :], edge_a], axis=0)
            y = before + zc * cw_ref[1:2, :] + after
            o_ref[0, rows, :] = (proj(2, r) * y).astype(BF16)

    def rope_part(part, scale):
        for r, rows in enumerate(chunks):
            acc = proj(part, r)
            cos, shi, slo = cos_ref[rows, :], shi_ref[rows, :], slo_ref[rows, :]
            for h in range(D_MODEL // LANES):
                cols = slice(h * LANES, (h + 1) * LANES)
                roped = _rope(acc[:, cols], cos, shi, slo)
                if scale is not None:
                    roped = roped * scale
                o_ref[0, rows, cols] = roped.astype(BF16)

    @branch(3)
    def _():
        rope_part(3, q_scale)

    @branch(4)
    def _():
        rope_part(4, None)

    @branch(5)
    def _():
        for r, rows in enumerate(chunks):
            o_ref[0, rows, :] = proj(5, r).astype(BF16)

    for gate_part in (6, 7):
        @branch(gate_part)
        def _(gate_part=gate_part):
            for r, rows in enumerate(chunks):
                o_ref[0, rows, :] = (1.0 / (1.0 + jnp.exp(-proj(gate_part, r)))).astype(BF16)


def _inproj(x, mod3, norm_g, w_in, conv_w, cos, sin_hi, sin_lo, q_scale):
    last = BATCH * N_PARTS - 1

    def part(t):
        return jnp.maximum(t - 1, 0) % N_PARTS

    def sample(t):
        return jnp.maximum(t - 1, 0) // N_PARTS

    def w_map(t):
        nxt = jnp.minimum(t, last) % N_PARTS
        return (0, jnp.where(nxt == 1, 2, jnp.where(nxt == 2, 1, nxt)))

    tab = pl.BlockSpec((SEQ, LANES), lambda t: (0, 0))
    return pl.pallas_call(
        functools.partial(_inproj_kernel, q_scale=q_scale),
        out_shape=jax.ShapeDtypeStruct((N_PARTS - 2, BATCH * SEQ, D_MODEL), BF16),
        grid=(BATCH * N_PARTS + 1,),
        in_specs=[
            pl.BlockSpec((1, SEQ, D_MODEL), lambda t: (sample(t), 0, 0)),
            pl.BlockSpec((1, 6, D_MODEL), lambda t: (sample(t), 0, 0)),
            pl.BlockSpec((1, D_MODEL), lambda t: (0, 0)),
            pl.BlockSpec((D_MODEL, D_MODEL), w_map),
            pl.BlockSpec((3, D_MODEL), lambda t: (0, 0)),
            tab, tab, tab,
        ],
        out_specs=pl.BlockSpec((1, SEQ, D_MODEL), lambda t: (jnp.maximum(part(t) - 2, 0), sample(t), 0)),
        scratch_shapes=[
            pltpu.VMEM((SEQ, D_MODEL), BF16),
            pltpu.VMEM((SEQ + 16, D_MODEL), F32),
            pltpu.VMEM((2, D_MODEL, D_MODEL), BF16),
        ],
        compiler_params=_params(("arbitrary",)),
        name="inproj",
    )(x, mod3, norm_g, w_in, conv_w, cos, sin_hi, sin_lo)


def _ctxproj_kernel(c_ref, mod_ref, g_ref, w_ref, o_ref, hc_ref):
    j = pl.program_id(0)
    chunks = [slice(r * ROW_CHUNK, (r + 1) * ROW_CHUNK) for r in range(BATCH * CTX_LEN // ROW_CHUNK)]

    @pl.when(j == 0)
    def _():
        shift = mod_ref[0, 0:1, :]
        scale = mod_ref[0, 1:2, :]
        for rows in chunks:
            hc_ref[rows, :] = _norm_modulate(c_ref[rows, :], g_ref[...], shift, scale).astype(BF16)

    w = w_ref[...].astype(BF16)
    for rows in chunks:
        o_ref[0, rows, :] = _dot(hc_ref[rows, :], w).astype(BF16)


def _ctxproj(ctx2, mod3, norm_g, w_in):
    rows = BATCH * CTX_LEN
    return pl.pallas_call(
        _ctxproj_kernel,
        out_shape=jax.ShapeDtypeStruct((2, rows, D_MODEL), BF16),
        grid=(2,),
        in_specs=[
            pl.BlockSpec((rows, D_MODEL), lambda j: (0, 0)),
            pl.BlockSpec((1, 6, D_MODEL), lambda j: (BATCH, 0, 0)),
            pl.BlockSpec((1, D_MODEL), lambda j: (0, 0)),
            pl.BlockSpec((D_MODEL, D_MODEL), lambda j: (0, 4 + j)),
        ],
        out_specs=pl.BlockSpec((1, rows, D_MODEL), lambda j: (j, 0, 0)),
        scratch_shapes=[pltpu.VMEM((rows, D_MODEL), BF16)],
        compiler_params=_params(("arbitrary",)),
        name="ctxproj",
    )(ctx2, mod3, norm_g, w_in)


def _attn_kernel(lp_ref, q_ref, k_ref, v_ref, kc_ref, vc_ref, g_ref, o_ref, k_all, v_ext, s_ref, m_ref):
    lp = lp_ref[...]
    lam = (jnp.exp(jnp.sum(lp[0:1] * lp[1:2], axis=-1, keepdims=True))
           - jnp.exp(jnp.sum(lp[2:3] * lp[3:4], axis=-1, keepdims=True)) + LAM_INIT)
    k_all[0:CTX_LEN, :] = kc_ref[0]
    k_all[CTX_LEN:N_KEYS, :] = k_ref[0]
    v_ext[0:CTX_LEN, 0:LANES] = vc_ref[0]
    v_ext[CTX_LEN:N_KEYS, 0:LANES] = v_ref[0]
    v_ext[:, LANES:2 * LANES] = jnp.ones((N_KEYS, LANES), BF16)
    first_map = lax.broadcasted_iota(jnp.int32, (1, LANES), 1) < HEAD_DIM

    def scores(u):
        q = q_ref[0, u * ATTN_SQ:(u + 1) * ATTN_SQ, :]
        zero = jnp.zeros_like(q)
        for mp, qm in enumerate((jnp.where(first_map, q, zero), jnp.where(first_map, zero, q))):
            s = _dot_nt(qm, k_all[...])
            s_ref[u % 2, mp] = s
            m_ref[u % 2, mp] = jnp.max(s, axis=-1, keepdims=True)

    def values(u):
        o = []
        for mp in range(2):
            p = jnp.exp2(s_ref[u % 2, mp] - m_ref[u % 2, mp])
            o.append(_dot(p.astype(BF16), v_ext[...]))
        a = o[0][:, :LANES] / o[0][:, LANES:] - o[1][:, :LANES] * (lam / o[1][:, LANES:])
        ms = jnp.mean(a * a, axis=-1, keepdims=True)
        o_ref[u * ATTN_SQ:(u + 1) * ATTN_SQ, :] = (
            (a * lax.rsqrt(ms + EPS) * g_ref[...]) * (1.0 - LAM_INIT)).astype(BF16)

    n_units = ATTN_TQ // ATTN_SQ
    scores(0)
    for u in range(n_units):
        if u + 1 < n_units:
            scores(u + 1)
        values(u)


def _attn(lam_params, parts, ctx_kv, subln_g):
    nq = SEQ // ATTN_TQ
    return pl.pallas_call(
        _attn_kernel,
        out_shape=jax.ShapeDtypeStruct((BATCH * SEQ, D_MODEL), BF16),
        grid=(BATCH, N_HEADS, nq),
        in_specs=[
            pl.BlockSpec((4, HEAD_DIM), lambda b, h, i: (0, 0)),
            pl.BlockSpec((1, ATTN_TQ, LANES), lambda b, h, i: (1, b * nq + i, h)),
            pl.BlockSpec((1, SEQ, LANES), lambda b, h, i: (2, b, h)),
            pl.BlockSpec((1, SEQ, LANES), lambda b, h, i: (3, b, h)),
            pl.BlockSpec((1, CTX_LEN, LANES), lambda b, h, i: (0, b, h)),
            pl.BlockSpec((1, CTX_LEN, LANES), lambda b, h, i: (1, b, h)),
            pl.BlockSpec((1, V_DIM), lambda b, h, i: (0, 0)),
        ],
        out_specs=pl.BlockSpec((ATTN_TQ, LANES), lambda b, h, i: (b * nq + i, h)),
        scratch_shapes=[
            pltpu.VMEM((N_KEYS, LANES), BF16),
            pltpu.VMEM((N_KEYS, 2 * LANES), BF16),
            pltpu.VMEM((2, 2, ATTN_SQ, N_KEYS), F32),
            pltpu.VMEM((2, 2, ATTN_SQ, 1), F32),
        ],
        compiler_params=_params(("arbitrary", "arbitrary", "arbitrary")),
        name="attn",
    )(lam_params, parts, parts, parts, ctx_kv, ctx_kv, subln_g)


def _merge_kernel(yb_ref, s6_ref, s7_ref, o_ref, x_ref, mod_ref, g2_ref, wc_ref, wa_ref, wo_ref,
                  wrh_ref, wrl_ref, x1_ref, h2_ref, lg_ref):
    chunks = [slice(r * ROW_CHUNK, (r + 1) * ROW_CHUNK) for r in range(MERGE_TM // ROW_CHUNK)]
    for rows in chunks:
        y_conv = _dot(yb_ref[0, rows, :], wc_ref[...])
        y_attn = _dot(o_ref[rows, :], wa_ref[...])
        merged = s6_ref[0, rows, :].astype(F32) * y_conv + s7_ref[0, rows, :].astype(F32) * y_attn
        mix = _dot(merged.astype(BF16), wo_ref[...])
        x1_ref[rows, :] = x_ref[rows, :] + mod_ref[0, 2:3, :] * mix
    for rows in chunks:
        h2 = _norm_modulate(x1_ref[rows, :], g2_ref[...], mod_ref[0, 3:4, :], mod_ref[0, 4:5, :])
        h_hi = h2.astype(BF16)
        h2_ref[rows, :] = pltpu.pack_elementwise([h2[:, :D_MODEL // 2], h2[:, D_MODEL // 2:]], packed_dtype=BF16)
        h_lo = (h2 - h_hi.astype(F32)).astype(BF16)
        lg_ref[0, :, rows] = (_dot_nt(wrh_ref[...], h_hi)
                              + (_dot_nt(wrh_ref[...], h_lo) + _dot_nt(wrl_ref[...], h_hi)))


def _merge(parts, attn_o, x2, mod3, norm2_g, w_conv_bf, w_attn_bf, w_out_bf, wr_hi, wr_lo):
    per_b = SEQ // MERGE_TM
    sq = pl.BlockSpec((D_MODEL, D_MODEL), lambda i: (0, 0), pipeline_mode=pl.Buffered(1))
    wr = pl.BlockSpec((N_EXPERTS, D_MODEL), lambda i: (0, 0), pipeline_mode=pl.Buffered(1))
    tile = pl.BlockSpec((MERGE_TM, D_MODEL), lambda i: (i, 0))
    return pl.pallas_call(
        _merge_kernel,
        out_shape=(
            jax.ShapeDtypeStruct((BATCH * SEQ, D_MODEL), F32),
            jax.ShapeDtypeStruct((BATCH * SEQ, D_MODEL // 2), jnp.uint32),
            jax.ShapeDtypeStruct((BATCH, N_EXPERTS, SEQ), F32),
        ),
        grid=(BATCH * per_b,),
        in_specs=[
            pl.BlockSpec((1, MERGE_TM, D_MODEL), lambda i: (0, i, 0)),
            pl.BlockSpec((1, MERGE_TM, D_MODEL), lambda i: (4, i, 0)),
            pl.BlockSpec((1, MERGE_TM, D_MODEL), lambda i: (5, i, 0)),
            tile, tile,
            pl.BlockSpec((1, 6, D_MODEL), lambda i: (i // per_b, 0, 0)),
            pl.BlockSpec((1, D_MODEL), lambda i: (0, 0)),
            sq, sq, sq, wr, wr,
        ],
        out_specs=(
            tile,
            pl.BlockSpec((MERGE_TM, D_MODEL // 2), lambda i: (i, 0)),
            pl.BlockSpec((1, N_EXPERTS, MERGE_TM), lambda i: (i // per_b, 0, i % per_b)),
        ),
        compiler_params=_params(("arbitrary",)),
        name="merge",
    )(parts, parts, parts, attn_o, x2, mod3, norm2_g, w_conv_bf, w_attn_bf, w_out_bf, wr_hi, wr_lo)


def _route_kernel(lg_ref, rank_e_ref, rank_t_ref, gate_t_ref):
    lg = lg_ref[...]
    ex = jnp.exp(lg - jnp.max(lg, axis=1, keepdims=True))
    aff = (ex / jnp.sum(ex, axis=1, keepdims=True)).reshape(BATCH * N_EXPERTS, SEQ)
    bits = lax.bitcast_convert_type(aff, jnp.int32)

    def count(mask):
        return jnp.sum(jnp.where(mask, 1.0, 0.0), axis=1, keepdims=True)

    def search(i, t):
        cand = t + lax.shift_left(jnp.int32(1), 30 - i)
        return jnp.where(count(bits >= cand) >= CAPACITY, cand, t)

    thr = lax.fori_loop(0, 31, search, jnp.zeros((BATCH * N_EXPERTS, 1), jnp.int32))
    need = CAPACITY - count(bits > thr)
    before = (lax.broadcasted_iota(jnp.int32, (LANES, LANES), 0)
              < lax.broadcasted_iota(jnp.int32, (LANES, LANES), 1)).astype(BF16)
    n_tied = jnp.zeros((BATCH * N_EXPERTS, 1), F32)
    n_sel = jnp.zeros((BATCH * N_EXPERTS, 1), F32)
    for c in range(SEQ // LANES):
        cols = slice(c * LANES, (c + 1) * LANES)
        bits_c = bits[:, cols]
        tied_c = jnp.where(bits_c == thr, 1.0, 0.0)
        tied_before = _dot(tied_c.astype(BF16), before) + n_tied
        sel_c = jnp.where(bits_c > thr, 1.0, jnp.where(tied_before < need, tied_c, 0.0))
        rank = _dot(sel_c.astype(BF16), before) + n_sel
        rank_c = jnp.where(sel_c > 0.0, rank, -1.0)
        gate_c = jnp.where(sel_c > 0.0, aff[:, cols], 0.0)
        n_tied = n_tied + jnp.sum(tied_c, axis=1, keepdims=True)
        n_sel = n_sel + jnp.sum(sel_c, axis=1, keepdims=True)
        rank_e_ref[:, cols] = rank_c.astype(jnp.int32)
        rank_ct = rank_c.T
        gate_ct = gate_c.T
        for b in range(BATCH):
            ecols = slice(b * N_EXPERTS, (b + 1) * N_EXPERTS)
            rank_t_ref[b, cols, :] = rank_ct[:, ecols]
            gate_t_ref[b, cols, :] = gate_ct[:, ecols]


def _route(logits_t):
    return pl.pallas_call(
        _route_kernel,
        out_shape=(
            jax.ShapeDtypeStruct((BATCH * N_EXPERTS, SEQ), jnp.int32),
            jax.ShapeDtypeStruct((BATCH, SEQ, N_EXPERTS), F32),
            jax.ShapeDtypeStruct((BATCH, SEQ, N_EXPERTS), F32),
        ),
        compiler_params=pltpu.CompilerParams(vmem_limit_bytes=VMEM_LIMIT),
        name="route",
    )(logits_t)


def _dispatch(rank_e, h_packed, first_expert):
    per_worker = BATCH * GROUP_EXPERTS // (SC_CORES * SC_SUBCORES)
    half = CAPACITY // 2
    mesh = plsc.VectorSubcoreMesh(core_axis_name="c", subcore_axis_name="s")

    @functools.partial(
        pl.kernel, mesh=mesh,
        out_type=jax.ShapeDtypeStruct((GROUP_EXPERTS * BATCH * CAPACITY, D_MODEL // 2), h_packed.dtype),
        scratch_types=[
            pltpu.VMEM((SEQ,), jnp.int32),
            pltpu.VMEM((2, half), jnp.int32),
            pltpu.VMEM((half, D_MODEL // 2), h_packed.dtype),
            pltpu.SemaphoreType.DMA,
        ],
        compiler_params=pltpu.CompilerParams(needs_layout_passes=False),
        name="dispatch",
    )
    def k(table_hbm, rank_hbm, out_hbm, rank_v, idx_v, rows_v, sem):
        worker = lax.axis_index("s") * SC_CORES + lax.axis_index("c")
        expert = worker // (BATCH // per_worker)

        @pl.loop(0, per_worker)
        def _(p):
            sample = (worker % (BATCH // per_worker)) * per_worker + p
            pltpu.sync_copy(rank_hbm.at[sample * N_EXPERTS + first_expert + expert], rank_v)

            @pl.loop(0, SEQ // SC_LANES)
            def _(i):
                rank = rank_v[pl.ds(i * SC_LANES, SC_LANES)]
                row = lax.iota(jnp.int32, SC_LANES) + (i * SC_LANES + sample * SEQ)
                plsc.store_scatter(idx_v, [lax.shift_right_arithmetic(rank, 7), rank & (half - 1)], row,
                                   mask=rank >= 0)

            for c in range(2):
                pltpu.async_copy(table_hbm.at[idx_v.at[c]], rows_v, sem).wait()
                first = (expert * BATCH + sample) * CAPACITY + c * half
                pltpu.sync_copy(rows_v, out_hbm.at[pl.ds(first, half)])

    return k(h_packed, rank_e)


def _expert_kernel(xs_ref, wg_ref, wu_ref, wd_ref, y_ref, acc_ref, wg_bf, wu_bf, wd_bf):
    t = pl.program_id(0)
    n_f = EXPERT_HIDDEN // FFN_TF
    f = (t + n_f - 1) % n_f

    n_chunks = BATCH * CAPACITY // ROW_CHUNK

    def stage(slot, part):
        up = slice(part * D_MODEL // n_chunks, (part + 1) * D_MODEL // n_chunks)
        down = slice(part * FFN_TF // n_chunks, (part + 1) * FFN_TF // n_chunks)
        wg_bf[slot, up, :] = wg_ref[0, up, :].astype(BF16)
        wu_bf[slot, up, :] = wu_ref[0, up, :].astype(BF16)
        wd_bf[slot, down, :] = wd_ref[0, down, :].astype(BF16)

    def step(cur):
        for r in range(n_chunks):
            stage(1 - cur, r)
            rows = slice(r * ROW_CHUNK, (r + 1) * ROW_CHUNK)
            packed = xs_ref[0, rows, :]
            half = D_MODEL // 2
            x_lo = pltpu.unpack_elementwise(packed, index=0, packed_dtype=BF16, unpacked_dtype=F32).astype(BF16)
            x_hi = pltpu.unpack_elementwise(packed, index=1, packed_dtype=BF16, unpacked_dtype=F32).astype(BF16)
            a = _dot(x_lo, wg_bf[cur, :half, :]) + _dot(x_hi, wg_bf[cur, half:, :])
            u = _dot(x_lo, wu_bf[cur, :half, :]) + _dot(x_hi, wu_bf[cur, half:, :])
            hidden = (a / (1.0 + jnp.exp(-a))) * u
            total = jnp.where(f == 0, 0.0, acc_ref[rows, :]) + _dot(hidden.astype(BF16), wd_bf[cur])
            acc_ref[rows, :] = total
            y_ref[0, rows, :] = total.astype(BF16)

    @pl.when(t == 0)
    def _():
        for part in range(n_chunks):
            stage(0, part)

    @pl.when(t % 2 == 1)
    def _():
        step(0)

    @pl.when((t % 2 == 0) & (t > 0))
    def _():
        step(1)


def _experts(xs, w_gate, w_up, w_down, first_expert):
    rows = BATCH * CAPACITY
    n_f = EXPERT_HIDDEN // FFN_TF
    last = GROUP_EXPERTS * n_f - 1

    def staged(t):
        return jnp.minimum(t, last)

    def computed(t):
        return jnp.maximum(t - 1, 0) // n_f

    return pl.pallas_call(
        _expert_kernel,
        out_shape=jax.ShapeDtypeStruct((GROUP_EXPERTS, rows, D_MODEL), BF16),
        grid=(GROUP_EXPERTS * n_f + 1,),
        in_specs=[
            pl.BlockSpec((1, rows, D_MODEL // 2), lambda t: (computed(t), 0, 0)),
            pl.BlockSpec((1, D_MODEL, FFN_TF), lambda t: (first_expert + staged(t) // n_f, 0, staged(t) % n_f)),
            pl.BlockSpec((1, D_MODEL, FFN_TF), lambda t: (first_expert + staged(t) // n_f, 0, staged(t) % n_f)),
            pl.BlockSpec((1, FFN_TF, D_MODEL), lambda t: (first_expert + staged(t) // n_f, staged(t) % n_f, 0)),
        ],
        out_specs=pl.BlockSpec((1, rows, D_MODEL), lambda t: (computed(t), 0, 0)),
        scratch_shapes=[
            pltpu.VMEM((rows, D_MODEL), F32),
            pltpu.VMEM((2, D_MODEL, FFN_TF), BF16),
            pltpu.VMEM((2, D_MODEL, FFN_TF), BF16),
            pltpu.VMEM((2, FFN_TF, D_MODEL), BF16),
        ],
        compiler_params=_params(("arbitrary",)),
        name="experts",
    )(xs, w_gate, w_up, w_down)


def _combine_kernel(rank_ref, gate_ref, *refs):
    y_refs, (x1_ref, mod_ref, fg_ref, o_ref) = refs[:N_GROUPS], refs[N_GROUPS:]
    slot = lax.broadcasted_iota(jnp.int32, (1, CAPACITY), 1).astype(F32)
    acc = jnp.zeros((COMBINE_TQ, D_MODEL), F32)
    for e in range(N_EXPERTS):
        rank = rank_ref[0, :, e:e + 1]
        gate = gate_ref[0, :, e:e + 1]
        scatter = jnp.where(rank == slot, gate, 0.0).astype(BF16)
        acc = acc + _dot(scatter, y_refs[e // GROUP_EXPERTS][e % GROUP_EXPERTS])
    x2 = x1_ref[...] + mod_ref[0, 5:6, :] * acc
    ms = jnp.mean(x2 * x2, axis=-1, keepdims=True)
    o_ref[...] = x2 * lax.rsqrt(ms + EPS) * fg_ref[...]


def _combine(rank_t, gate_t, ys, x1, mod3, final_g):
    per_b = SEQ // COMBINE_TQ
    return pl.pallas_call(
        _combine_kernel,
        out_shape=jax.ShapeDtypeStruct((BATCH * SEQ, D_MODEL), F32),
        grid=(BATCH, per_b),
        in_specs=[
            pl.BlockSpec((1, COMBINE_TQ, N_EXPERTS), lambda b, t: (b, t, 0)),
            pl.BlockSpec((1, COMBINE_TQ, N_EXPERTS), lambda b, t: (b, t, 0)),
            *[pl.BlockSpec((GROUP_EXPERTS, CAPACITY, D_MODEL), lambda b, t: (0, b, 0)) for _ in ys],
            pl.BlockSpec((COMBINE_TQ, D_MODEL), lambda b, t: (b * per_b + t, 0)),
            pl.BlockSpec((1, 6, D_MODEL), lambda b, t: (b, 0, 0)),
            pl.BlockSpec((1, D_MODEL), lambda b, t: (0, 0)),
        ],
        out_specs=pl.BlockSpec((COMBINE_TQ, D_MODEL), lambda b, t: (b * per_b + t, 0)),
        compiler_params=_params(("arbitrary", "arbitrary")),
        name="combine",
    )(rank_t, gate_t, *ys, x1, mod3, final_g)


def _rope_tables():
    t = jnp.arange(SEQ)
    pos = jnp.stack([t // GRID_W, t % GRID_W], axis=1).astype(F32)
    per_axis = HEAD_DIM // 2
    inv = ROPE_THETA ** (-jnp.arange(0, per_axis, 2, dtype=F32) / per_axis)
    ang = pos[:, :, None] * inv
    cos, sin = jnp.cos(ang), jnp.sin(ang)
    zero = jnp.zeros_like(sin)

    def lanes(first_half, second_half):
        one_map = jnp.stack([first_half, second_half], axis=2).reshape(SEQ, HEAD_DIM)
        return jnp.concatenate([one_map, one_map], axis=1)

    return lanes(cos, cos), lanes(-sin, zero), lanes(zero, sin)


def kernel(x, c, ctx, c_ctx, norm1_g, norm2_g, w_ada, b_ada, w_in, conv_w, w_out_conv, lambda_q1,
           lambda_k1, lambda_q2, lambda_k2, subln_g, w_o_attn, w_out, w_router, w_gate_e, w_up_e,
           w_down_e, final_g):
    cond = jnp.concatenate([c, c_ctx[None, :], jnp.zeros((COND_ROWS - BATCH - 1, D_MODEL), F32)], axis=0)
    mod3 = _ada(cond, w_ada[0], b_ada).reshape(COND_ROWS, 6, D_MODEL)

    cos, sin_hi, sin_lo = _rope_tables()
    q_scale = math.log2(math.e) * HEAD_DIM ** -0.5
    parts = _inproj(x, mod3, norm1_g, w_in[0], conv_w[0], cos, sin_hi, sin_lo, q_scale)
    ctx_kv = _ctxproj(ctx.reshape(BATCH * CTX_LEN, D_MODEL), mod3, norm1_g, w_in[0])

    lam_params = jnp.concatenate([lambda_q1, lambda_k1, lambda_q2, lambda_k2], axis=0)
    attn_o = _attn(lam_params, parts, ctx_kv, subln_g)

    wr_t = w_router[0].T
    wr_hi = wr_t.astype(BF16)
    wr_lo = (wr_t - wr_hi.astype(F32)).astype(BF16)
    x1, h2, logits_t = _merge(parts, attn_o, x.reshape(BATCH * SEQ, D_MODEL), mod3, norm2_g,
                              w_out_conv[0].astype(BF16), w_o_attn[0].astype(BF16), w_out[0].astype(BF16),
                              wr_hi, wr_lo)

    rank_e, rank_t, gate_t = _route(logits_t)
    ys = []
    for first in range(0, N_EXPERTS, GROUP_EXPERTS):
        xs = _dispatch(rank_e, h2, first).reshape(GROUP_EXPERTS, BATCH * CAPACITY, D_MODEL // 2)
        ys.append(_experts(xs, w_gate_e[0], w_up_e[0], w_down_e[0], first))
    out = _combine(rank_t, gate_t, ys, x1, mod3, final_g[None, :])
    return out.reshape(BATCH, SEQ, D_MODEL)
```

```python
import functools
import math

import jax
import jax.numpy as jnp
from jax import lax
from jax.experimental import pallas as pl
from jax.experimental.pallas import tpu as pltpu
from jax.experimental.pallas import tpu_sc as plsc

D_MODEL = 1024
BATCH = 8
SEQ = 2048
GRID_W = 64
CTX_LEN = 256
N_HEADS = 8
HEAD_DIM = 64
V_DIM = 2 * HEAD_DIM
N_EXPERTS = 16
EXPERT_HIDDEN = 2048
CAPACITY = 2 * SEQ // N_EXPERTS
ROPE_THETA = 10000.0
EPS = 1e-6
LAM_INIT = 0.8 - 0.6 * math.exp(-0.3 * 0)
N_PARTS = 8
N_KEYS = CTX_LEN + SEQ
N_GROUPS = 2
GROUP_EXPERTS = N_EXPERTS // N_GROUPS

LANES = 128
SC_CORES = 2
SC_SUBCORES = 16
SC_LANES = 16
F32 = jnp.float32
BF16 = jnp.bfloat16

ROW_CHUNK = 512
COND_ROWS = 16
ADA_TN = 512
ATTN_TQ = 2048
ATTN_SQ = 256
MERGE_TM = 1024
FFN_TF = 512
COMBINE_TQ = 512
VMEM_LIMIT = 56 * 1024 * 1024


def _dot(a, b):
    return jnp.dot(a, b, preferred_element_type=F32)


def _dot_nt(a, b):
    return lax.dot_general(a, b, (((1,), (1,)), ((), ())), preferred_element_type=F32)


def _params(sem, vmem=VMEM_LIMIT):
    return pltpu.CompilerParams(dimension_semantics=sem, vmem_limit_bytes=vmem)


def _ada_kernel(cond_ref, w_ref, b_ref, o_ref):
    c = cond_ref[...]
    s = c / (1.0 + jnp.exp(-c))
    o_ref[...] = _dot(s.astype(BF16), w_ref[...].astype(BF16)) + b_ref[...]


def _ada(cond, w_ada, b_ada):
    n = w_ada.shape[1]
    return pl.pallas_call(
        _ada_kernel,
        out_shape=jax.ShapeDtypeStruct((COND_ROWS, n), F32),
        grid=(n // ADA_TN,),
        in_specs=[
            pl.BlockSpec((COND_ROWS, D_MODEL), lambda i: (0, 0)),
            pl.BlockSpec((D_MODEL, ADA_TN), lambda i: (0, i)),
            pl.BlockSpec((1, ADA_TN), lambda i: (0, i)),
        ],
        out_specs=pl.BlockSpec((COND_ROWS, ADA_TN), lambda i: (0, i)),
        compiler_params=_params(("arbitrary",)),
        name="ada",
    )(cond, w_ada, b_ada)


def _norm_modulate(x, g, shift, scale):
    ms = jnp.mean(x * x, axis=-1, keepdims=True)
    return (x * lax.rsqrt(ms + EPS) * g) * (1.0 + scale) + shift


def _rope(a, cos, sin_hi, sin_lo):
    return a * cos + pltpu.roll(a, LANES - 16, 1) * sin_hi + pltpu.roll(a, 16, 1) * sin_lo


def _inproj_kernel(x_lo_ref, x_hi_ref, mod_ref, g_ref, w_ref, cw_ref, cos_ref, shi_ref, slo_ref, o_ref,
                   hx_ref, z_ref, w_bf, *, q_scale):
    t = pl.program_id(0)
    j = (t + N_PARTS - 1) % N_PARTS
    chunks = [slice(r * ROW_CHUNK, (r + 1) * ROW_CHUNK) for r in range(SEQ // ROW_CHUNK)]

    def stage(slot, part):
        rows = slice(part * D_MODEL // len(chunks), (part + 1) * D_MODEL // len(chunks))
        w_bf[slot, rows, :] = w_ref[rows, :].astype(BF16)

    def proj(part, r):
        stage(1 - part % 2, r)
        return _dot(hx_ref[chunks[r], :], w_bf[part % 2])

    def branch(part):
        return pl.when((t > 0) & (j == part))

    @pl.when(t == 0)
    def _():
        for r in range(len(chunks)):
            stage(0, r)

    @branch(0)
    def _():
        shift = mod_ref[0, 0:1, :]
        scale = mod_ref[0, 1:2, :]
        for rows in chunks:
            x_ref, local = (x_lo_ref, rows) if rows.start < SEQ // 2 else (
                x_hi_ref, slice(rows.start - SEQ // 2, rows.stop - SEQ // 2))
            hx_ref[rows, :] = _norm_modulate(x_ref[0, local, :], g_ref[...], shift, scale).astype(BF16)
        z_ref[0:8, :] = jnp.zeros((8, D_MODEL), F32)
        z_ref[SEQ + 8:SEQ + 16, :] = jnp.zeros((8, D_MODEL), F32)
        for r, rows in enumerate(chunks):
            z_ref[pl.ds(rows.start + 8, ROW_CHUNK), :] = proj(0, r)

    @branch(1)
    def _():
        for r, rows in enumerate(chunks):
            zr = pl.ds(rows.start + 8, ROW_CHUNK)
            z_ref[zr, :] = z_ref[zr, :] * proj(1, r)

    @branch(2)
    def _():
        first_row = lax.broadcasted_iota(jnp.int32, (8, 1), 0) == 0
        last_row = lax.broadcasted_iota(jnp.int32, (8, 1), 0) == 7
        for r, rows in enumerate(chunks):
            base = rows.start + 8
            zc = z_ref[pl.ds(base, ROW_CHUNK), :]
            before = pltpu.roll(zc * cw_ref[0:1, :], 1, 0)
            after = pltpu.roll(zc * cw_ref[2:3, :], ROW_CHUNK - 1, 0)
            edge_b = jnp.where(first_row, z_ref[pl.ds(base - 8, 8), :][7:8, :] * cw_ref[0:1, :], before[0:8, :])
            edge_a = jnp.where(last_row, z_ref[pl.ds(base + ROW_CHUNK, 8), :][0:1, :] * cw_ref[2:3, :],
                               after[ROW_CHUNK - 8:, :])
            before = jnp.concatenate([edge_b, before[8:, :]], axis=0)
            after = jnp.concatenate([after[:ROW_CHUNK - 8, :], edge_a], axis=0)
            y = before + zc * cw_ref[1:2, :] + after
            o_ref[0, rows, :] = (proj(2, r) * y).astype(BF16)

    def rope_part(part, scale):
        for r, rows in enumerate(chunks):
            acc = proj(part, r)
            cos, shi, slo = cos_ref[rows, :], shi_ref[rows, :], slo_ref[rows, :]
            for h in range(D_MODEL // LANES):
                cols = slice(h * LANES, (h + 1) * LANES)
                roped = _rope(acc[:, cols], cos, shi, slo)
                if scale is not None:
                    roped = roped * scale
                o_ref[0, rows, cols] = roped.astype(BF16)

    @branch(3)
    def _():
        rope_part(3, q_scale)

    @branch(4)
    def _():
        rope_part(4, None)

    @branch(5)
    def _():
        for r, rows in enumerate(chunks):
            o_ref[0, rows, :] = proj(5, r).astype(BF16)

    for gate_part in (6, 7):
        @branch(gate_part)
        def _(gate_part=gate_part):
            for r, rows in enumerate(chunks):
                o_ref[0, rows, :] = (1.0 / (1.0 + jnp.exp(-proj(gate_part, r)))).astype(BF16)


def _inproj(x, mod3, norm_g, w_in, conv_w, cos, sin_hi, sin_lo, q_scale):
    last = BATCH * N_PARTS - 1

    def part(t):
        return jnp.maximum(t - 1, 0) % N_PARTS

    def sample(t):
        return jnp.maximum(t - 1, 0) // N_PARTS

    def w_map(t):
        nxt = jnp.minimum(t, last) % N_PARTS
        return (0, jnp.where(nxt == 1, 2, jnp.where(nxt == 2, 1, nxt)))

    tab = pl.BlockSpec((SEQ, LANES), lambda t: (0, 0))
    return pl.pallas_call(
        functools.partial(_inproj_kernel, q_scale=q_scale),
        out_shape=jax.ShapeDtypeStruct((N_PARTS - 2, BATCH * SEQ, D_MODEL), BF16),
        grid=(BATCH * N_PARTS + 1,),
        in_specs=[
            pl.BlockSpec((1, SEQ // 2, D_MODEL), lambda t: (sample(t), 0, 0)),
            pl.BlockSpec((1, SEQ // 2, D_MODEL), lambda t: (sample(jnp.minimum(t + 1, last + 1)), 1, 0)),
            pl.BlockSpec((1, 6, D_MODEL), lambda t: (sample(t), 0, 0)),
            pl.BlockSpec((1, D_MODEL), lambda t: (0, 0)),
            pl.BlockSpec((D_MODEL, D_MODEL), w_map),
            pl.BlockSpec((3, D_MODEL), lambda t: (0, 0)),
            tab, tab, tab,
        ],
        out_specs=pl.BlockSpec((1, SEQ, D_MODEL), lambda t: (jnp.maximum(part(t) - 2, 0), sample(t), 0)),
        scratch_shapes=[
            pltpu.VMEM((SEQ, D_MODEL), BF16),
            pltpu.VMEM((SEQ + 16, D_MODEL), F32),
            pltpu.VMEM((2, D_MODEL, D_MODEL), BF16),
        ],
        compiler_params=_params(("arbitrary",)),
        name="inproj",
    )(x, x, mod3, norm_g, w_in, conv_w, cos, sin_hi, sin_lo)


def _ctxproj_kernel(c_ref, mod_ref, g_ref, w_ref, o_ref, hc_ref):
    j = pl.program_id(0)
    chunks = [slice(r * ROW_CHUNK, (r + 1) * ROW_CHUNK) for r in range(BATCH * CTX_LEN // ROW_CHUNK)]

    @pl.when(j == 0)
    def _():
        shift = mod_ref[0, 0:1, :]
        scale = mod_ref[0, 1:2, :]
        for rows in chunks:
            hc_ref[rows, :] = _norm_modulate(c_ref[rows, :], g_ref[...], shift, scale).astype(BF16)

    w = w_ref[...].astype(BF16)
    for rows in chunks:
        o_ref[0, rows, :] = _dot(hc_ref[rows, :], w).astype(BF16)


def _ctxproj(ctx2, mod3, norm_g, w_in):
    rows = BATCH * CTX_LEN
    return pl.pallas_call(
        _ctxproj_kernel,
        out_shape=jax.ShapeDtypeStruct((2, rows, D_MODEL), BF16),
        grid=(2,),
        in_specs=[
            pl.BlockSpec((rows, D_MODEL), lambda j: (0, 0)),
            pl.BlockSpec((1, 6, D_MODEL), lambda j: (BATCH, 0, 0)),
            pl.BlockSpec((1, D_MODEL), lambda j: (0, 0)),
            pl.BlockSpec((D_MODEL, D_MODEL), lambda j: (0, 4 + j)),
        ],
        out_specs=pl.BlockSpec((1, rows, D_MODEL), lambda j: (j, 0, 0)),
        scratch_shapes=[pltpu.VMEM((rows, D_MODEL), BF16)],
        compiler_params=_params(("arbitrary",)),
        name="ctxproj",
    )(ctx2, mod3, norm_g, w_in)


def _attn_kernel(lp_ref, q_ref, k_ref, v_ref, kc_ref, vc_ref, g_ref, o_ref, k_all, v_ext, s_ref, m_ref):
    lp = lp_ref[...]
    lam = (jnp.exp(jnp.sum(lp[0:1] * lp[1:2], axis=-1, keepdims=True))
           - jnp.exp(jnp.sum(lp[2:3] * lp[3:4], axis=-1, keepdims=True)) + LAM_INIT)
    k_all[0:CTX_LEN, :] = kc_ref[0]
    k_all[CTX_LEN:N_KEYS, :] = k_ref[0]
    v_ext[0:CTX_LEN, 0:LANES] = vc_ref[0]
    v_ext[CTX_LEN:N_KEYS, 0:LANES] = v_ref[0]
    v_ext[:, LANES:2 * LANES] = jnp.ones((N_KEYS, LANES), BF16)
    first_map = lax.broadcasted_iota(jnp.int32, (1, LANES), 1) < HEAD_DIM

    def scores(u):
        q = q_ref[0, u * ATTN_SQ:(u + 1) * ATTN_SQ, :]
        zero = jnp.zeros_like(q)
        for mp, qm in enumerate((jnp.where(first_map, q, zero), jnp.where(first_map, zero, q))):
            s = _dot_nt(qm, k_all[...])
            s_ref[u % 2, mp] = s
            m_ref[u % 2, mp] = jnp.max(s, axis=-1, keepdims=True)

    def values(u):
        o = []
        for mp in range(2):
            p = jnp.exp2(s_ref[u % 2, mp] - m_ref[u % 2, mp])
            o.append(_dot(p.astype(BF16), v_ext[...]))
        a = o[0][:, :LANES] / o[0][:, LANES:] - o[1][:, :LANES] * (lam / o[1][:, LANES:])
        ms = jnp.mean(a * a, axis=-1, keepdims=True)
        o_ref[u * ATTN_SQ:(u + 1) * ATTN_SQ, :] = (
            (a * lax.rsqrt(ms + EPS) * g_ref[...]) * (1.0 - LAM_INIT)).astype(BF16)

    n_units = ATTN_TQ // ATTN_SQ
    scores(0)
    for u in range(n_units):
        if u + 1 < n_units:
            scores(u + 1)
        values(u)


def _attn(lam_params, parts, ctx_kv, subln_g):
    nq = SEQ // ATTN_TQ
    return pl.pallas_call(
        _attn_kernel,
        out_shape=jax.ShapeDtypeStruct((BATCH * SEQ, D_MODEL), BF16),
        grid=(BATCH, N_HEADS, nq),
        in_specs=[
            pl.BlockSpec((4, HEAD_DIM), lambda b, h, i: (0, 0)),
            pl.BlockSpec((1, ATTN_TQ, LANES), lambda b, h, i: (1, b * nq + i, h)),
            pl.BlockSpec((1, SEQ, LANES), lambda b, h, i: (2, b, h)),
            pl.BlockSpec((1, SEQ, LANES), lambda b, h, i: (3, b, h)),
            pl.BlockSpec((1, CTX_LEN, LANES), lambda b, h, i: (0, b, h)),
            pl.BlockSpec((1, CTX_LEN, LANES), lambda b, h, i: (1, b, h)),
            pl.BlockSpec((1, V_DIM), lambda b, h, i: (0, 0)),
        ],
        out_specs=pl.BlockSpec((ATTN_TQ, LANES), lambda b, h, i: (b * nq + i, h)),
        scratch_shapes=[
            pltpu.VMEM((N_KEYS, LANES), BF16),
            pltpu.VMEM((N_KEYS, 2 * LANES), BF16),
            pltpu.VMEM((2, 2, ATTN_SQ, N_KEYS), F32),
            pltpu.VMEM((2, 2, ATTN_SQ, 1), F32),
        ],
        compiler_params=_params(("arbitrary", "arbitrary", "arbitrary")),
        name="attn",
    )(lam_params, parts, parts, parts, ctx_kv, ctx_kv, subln_g)


def _merge_kernel(yb_ref, s6_ref, s7_ref, o_ref, x_ref, mod_ref, g2_ref, wc_ref, wa_ref, wo_ref,
                  wrh_ref, wrl_ref, x1_ref, h2_ref, lg_ref):
    chunks = [slice(r * ROW_CHUNK, (r + 1) * ROW_CHUNK) for r in range(MERGE_TM // ROW_CHUNK)]
    for rows in chunks:
        y_conv = _dot(yb_ref[0, rows, :], wc_ref[...])
        y_attn = _dot(o_ref[rows, :], wa_ref[...])
        merged = s6_ref[0, rows, :].astype(F32) * y_conv + s7_ref[0, rows, :].astype(F32) * y_attn
        mix = _dot(merged.astype(BF16), wo_ref[...])
        x1_ref[rows, :] = x_ref[rows, :] + mod_ref[0, 2:3, :] * mix
    for rows in chunks:
        h2 = _norm_modulate(x1_ref[rows, :], g2_ref[...], mod_ref[0, 3:4, :], mod_ref[0, 4:5, :])
        h_hi = h2.astype(BF16)
        h2_ref[rows, :] = pltpu.pack_elementwise([h2[:, :D_MODEL // 2], h2[:, D_MODEL // 2:]], packed_dtype=BF16)
        h_lo = (h2 - h_hi.astype(F32)).astype(BF16)
        lg_ref[0, :, rows] = (_dot_nt(wrh_ref[...], h_hi)
                              + (_dot_nt(wrh_ref[...], h_lo) + _dot_nt(wrl_ref[...], h_hi)))


def _merge(parts, attn_o, x2, mod3, norm2_g, w_conv_bf, w_attn_bf, w_out_bf, wr_hi, wr_lo):
    per_b = SEQ // MERGE_TM
    sq = pl.BlockSpec((D_MODEL, D_MODEL), lambda i: (0, 0), pipeline_mode=pl.Buffered(1))
    wr = pl.BlockSpec((N_EXPERTS, D_MODEL), lambda i: (0, 0), pipeline_mode=pl.Buffered(1))
    tile = pl.BlockSpec((MERGE_TM, D_MODEL), lambda i: (i, 0))
    return pl.pallas_call(
        _merge_kernel,
        out_shape=(
            jax.ShapeDtypeStruct((BATCH * SEQ, D_MODEL), F32),
            jax.ShapeDtypeStruct((BATCH * SEQ, D_MODEL // 2), jnp.uint32),
            jax.ShapeDtypeStruct((BATCH, N_EXPERTS, SEQ), F32),
        ),
        grid=(BATCH * per_b,),
        in_specs=[
            pl.BlockSpec((1, MERGE_TM, D_MODEL), lambda i: (0, i, 0)),
            pl.BlockSpec((1, MERGE_TM, D_MODEL), lambda i: (4, i, 0)),
            pl.BlockSpec((1, MERGE_TM, D_MODEL), lambda i: (5, i, 0)),
            tile, tile,
            pl.BlockSpec((1, 6, D_MODEL), lambda i: (i // per_b, 0, 0)),
            pl.BlockSpec((1, D_MODEL), lambda i: (0, 0)),
            sq, sq, sq, wr, wr,
        ],
        out_specs=(
            tile,
            pl.BlockSpec((MERGE_TM, D_MODEL // 2), lambda i: (i, 0)),
            pl.BlockSpec((1, N_EXPERTS, MERGE_TM), lambda i: (i // per_b, 0, i % per_b)),
        ),
        compiler_params=_params(("arbitrary",)),
        name="merge",
    )(parts, parts, parts, attn_o, x2, mod3, norm2_g, w_conv_bf, w_attn_bf, w_out_bf, wr_hi, wr_lo)


def _route_kernel(lg_ref, rank_e_ref, rank_t_ref, gate_t_ref):
    lg = lg_ref[...]
    ex = jnp.exp(lg - jnp.max(lg, axis=1, keepdims=True))
    aff = (ex / jnp.sum(ex, axis=1, keepdims=True)).reshape(BATCH * N_EXPERTS, SEQ)
    bits = lax.bitcast_convert_type(aff, jnp.int32)

    def count(mask):
        return jnp.sum(jnp.where(mask, 1.0, 0.0), axis=1, keepdims=True)

    def search(i, t):
        cand = t + lax.shift_left(jnp.int32(1), 30 - i)
        return jnp.where(count(bits >= cand) >= CAPACITY, cand, t)

    thr = lax.fori_loop(0, 31, search, jnp.zeros((BATCH * N_EXPERTS, 1), jnp.int32))
    need = CAPACITY - count(bits > thr)
    before = (lax.broadcasted_iota(jnp.int32, (LANES, LANES), 0)
              < lax.broadcasted_iota(jnp.int32, (LANES, LANES), 1)).astype(BF16)
    n_tied = jnp.zeros((BATCH * N_EXPERTS, 1), F32)
    n_sel = jnp.zeros((BATCH * N_EXPERTS, 1), F32)
    for c in range(SEQ // LANES):
        cols = slice(c * LANES, (c + 1) * LANES)
        bits_c = bits[:, cols]
        tied_c = jnp.where(bits_c == thr, 1.0, 0.0)
        tied_before = _dot(tied_c.astype(BF16), before) + n_tied
        sel_c = jnp.where(bits_c > thr, 1.0, jnp.where(tied_before < need, tied_c, 0.0))
        rank = _dot(sel_c.astype(BF16), before) + n_sel
        rank_c = jnp.where(sel_c > 0.0, rank, -1.0)
        gate_c = jnp.where(sel_c > 0.0, aff[:, cols], 0.0)
        n_tied = n_tied + jnp.sum(tied_c, axis=1, keepdims=True)
        n_sel = n_sel + jnp.sum(sel_c, axis=1, keepdims=True)
        rank_e_ref[:, cols] = rank_c.astype(jnp.int32)
        rank_ct = rank_c.T
        gate_ct = gate_c.T
        for b in range(BATCH):
            ecols = slice(b * N_EXPERTS, (b + 1) * N_EXPERTS)
            rank_t_ref[b, cols, :] = rank_ct[:, ecols]
            gate_t_ref[b, cols, :] = gate_ct[:, ecols]


def _route(logits_t):
    return pl.pallas_call(
        _route_kernel,
        out_shape=(
            jax.ShapeDtypeStruct((BATCH * N_EXPERTS, SEQ), jnp.int32),
            jax.ShapeDtypeStruct((BATCH, SEQ, N_EXPERTS), F32),
            jax.ShapeDtypeStruct((BATCH, SEQ, N_EXPERTS), F32),
        ),
        compiler_params=pltpu.CompilerParams(vmem_limit_bytes=VMEM_LIMIT),
        name="route",
    )(logits_t)


def _dispatch(rank_e, h_packed, first_expert):
    per_worker = BATCH * GROUP_EXPERTS // (SC_CORES * SC_SUBCORES)
    half = CAPACITY // 2
    mesh = plsc.VectorSubcoreMesh(core_axis_name="c", subcore_axis_name="s")

    @functools.partial(
        pl.kernel, mesh=mesh,
        out_type=jax.ShapeDtypeStruct((GROUP_EXPERTS * BATCH * CAPACITY, D_MODEL // 2), h_packed.dtype),
        scratch_types=[
            pltpu.VMEM((SEQ,), jnp.int32),
            pltpu.VMEM((2, half), jnp.int32),
            pltpu.VMEM((half, D_MODEL // 2), h_packed.dtype),
            pltpu.SemaphoreType.DMA,
        ],
        compiler_params=pltpu.CompilerParams(needs_layout_passes=False),
        name="dispatch",
    )
    def k(table_hbm, rank_hbm, out_hbm, rank_v, idx_v, rows_v, sem):
        worker = lax.axis_index("s") * SC_CORES + lax.axis_index("c")
        expert = worker // (BATCH // per_worker)

        @pl.loop(0, per_worker)
        def _(p):
            sample = (worker % (BATCH // per_worker)) * per_worker + p
            pltpu.sync_copy(rank_hbm.at[sample * N_EXPERTS + first_expert + expert], rank_v)

            @pl.loop(0, SEQ // SC_LANES)
            def _(i):
                rank = rank_v[pl.ds(i * SC_LANES, SC_LANES)]
                row = lax.iota(jnp.int32, SC_LANES) + (i * SC_LANES + sample * SEQ)
                plsc.store_scatter(idx_v, [lax.shift_right_arithmetic(rank, 7), rank & (half - 1)], row,
                                   mask=rank >= 0)

            for c in range(2):
                pltpu.async_copy(table_hbm.at[idx_v.at[c]], rows_v, sem).wait()
                first = (expert * BATCH + sample) * CAPACITY + c * half
                pltpu.sync_copy(rows_v, out_hbm.at[pl.ds(first, half)])

    return k(h_packed, rank_e)


def _expert_kernel(xs_ref, wg_ref, wu_ref, wd_ref, y_ref, acc_ref, wg_bf, wu_bf, wd_bf):
    t = pl.program_id(0)
    n_f = EXPERT_HIDDEN // FFN_TF
    f = (t + n_f - 1) % n_f

    n_chunks = BATCH * CAPACITY // ROW_CHUNK

    def stage(slot, part):
        up = slice(part * D_MODEL // n_chunks, (part + 1) * D_MODEL // n_chunks)
        down = slice(part * FFN_TF // n_chunks, (part + 1) * FFN_TF // n_chunks)
        wg_bf[slot, up, :] = wg_ref[0, up, :].astype(BF16)
        wu_bf[slot, up, :] = wu_ref[0, up, :].astype(BF16)
        wd_bf[slot, down, :] = wd_ref[0, down, :].astype(BF16)

    def step(cur):
        for r in range(n_chunks):
            stage(1 - cur, r)
            rows = slice(r * ROW_CHUNK, (r + 1) * ROW_CHUNK)
            packed = xs_ref[0, rows, :]
            half = D_MODEL // 2
            x_lo = pltpu.unpack_elementwise(packed, index=0, packed_dtype=BF16, unpacked_dtype=F32).astype(BF16)
            x_hi = pltpu.unpack_elementwise(packed, index=1, packed_dtype=BF16, unpacked_dtype=F32).astype(BF16)
            a = _dot(x_lo, wg_bf[cur, :half, :]) + _dot(x_hi, wg_bf[cur, half:, :])
            u = _dot(x_lo, wu_bf[cur, :half, :]) + _dot(x_hi, wu_bf[cur, half:, :])
            hidden = (a / (1.0 + jnp.exp(-a))) * u
            total = jnp.where(f == 0, 0.0, acc_ref[rows, :]) + _dot(hidden.astype(BF16), wd_bf[cur])
            acc_ref[rows, :] = total
            y_ref[0, rows, :] = total.astype(BF16)

    @pl.when(t == 0)
    def _():
        for part in range(n_chunks):
            stage(0, part)

    @pl.when(t % 2 == 1)
    def _():
        step(0)

    @pl.when((t % 2 == 0) & (t > 0))
    def _():
        step(1)


def _experts(xs, w_gate, w_up, w_down, first_expert):
    rows = BATCH * CAPACITY
    n_f = EXPERT_HIDDEN // FFN_TF
    last = GROUP_EXPERTS * n_f - 1

    def staged(t):
        return jnp.minimum(t, last)

    def computed(t):
        return jnp.maximum(t - 1, 0) // n_f

    return pl.pallas_call(
        _expert_kernel,
        out_shape=jax.ShapeDtypeStruct((GROUP_EXPERTS, rows, D_MODEL), BF16),
        grid=(GROUP_EXPERTS * n_f + 1,),
        in_specs=[
            pl.BlockSpec((1, rows, D_MODEL // 2), lambda t: (computed(t), 0, 0)),
            pl.BlockSpec((1, D_MODEL, FFN_TF), lambda t: (first_expert + staged(t) // n_f, 0, staged(t) % n_f)),
            pl.BlockSpec((1, D_MODEL, FFN_TF), lambda t: (first_expert + staged(t) // n_f, 0, staged(t) % n_f)),
            pl.BlockSpec((1, FFN_TF, D_MODEL), lambda t: (first_expert + staged(t) // n_f, staged(t) % n_f, 0)),
        ],
        out_specs=pl.BlockSpec((1, rows, D_MODEL), lambda t: (computed(t), 0, 0)),
        scratch_shapes=[
            pltpu.VMEM((rows, D_MODEL), F32),
            pltpu.VMEM((2, D_MODEL, FFN_TF), BF16),
            pltpu.VMEM((2, D_MODEL, FFN_TF), BF16),
            pltpu.VMEM((2, FFN_TF, D_MODEL), BF16),
        ],
        compiler_params=_params(("arbitrary",)),
        name="experts",
    )(xs, w_gate, w_up, w_down)


def _combine_kernel(rank_ref, gate_ref, *refs):
    y_refs, (x1_ref, mod_ref, fg_ref, o_ref) = refs[:N_GROUPS], refs[N_GROUPS:]
    slot = lax.broadcasted_iota(jnp.int32, (1, CAPACITY), 1).astype(F32)
    acc = jnp.zeros((COMBINE_TQ, D_MODEL), F32)
    for e in range(N_EXPERTS):
        rank = rank_ref[0, :, e:e + 1]
        gate = gate_ref[0, :, e:e + 1]
        scatter = jnp.where(rank == slot, gate, 0.0).astype(BF16)
        acc = acc + _dot(scatter, y_refs[e // GROUP_EXPERTS][e % GROUP_EXPERTS])
    x2 = x1_ref[...] + mod_ref[0, 5:6, :] * acc
    ms = jnp.mean(x2 * x2, axis=-1, keepdims=True)
    o_ref[...] = x2 * lax.rsqrt(ms + EPS) * fg_ref[...]


def _combine(rank_t, gate_t, ys, x1, mod3, final_g):
    per_b = SEQ // COMBINE_TQ
    return pl.pallas_call(
        _combine_kernel,
        out_shape=jax.ShapeDtypeStruct((BATCH * SEQ, D_MODEL), F32),
        grid=(BATCH, per_b),
        in_specs=[
            pl.BlockSpec((1, COMBINE_TQ, N_EXPERTS), lambda b, t: (b, t, 0)),
            pl.BlockSpec((1, COMBINE_TQ, N_EXPERTS), lambda b, t: (b, t, 0)),
            *[pl.BlockSpec((GROUP_EXPERTS, CAPACITY, D_MODEL), lambda b, t: (0, b, 0)) for _ in ys],
            pl.BlockSpec((COMBINE_TQ, D_MODEL), lambda b, t: (b * per_b + t, 0)),
            pl.BlockSpec((1, 6, D_MODEL), lambda b, t: (b, 0, 0)),
            pl.BlockSpec((1, D_MODEL), lambda b, t: (0, 0)),
        ],
        out_specs=pl.BlockSpec((COMBINE_TQ, D_MODEL), lambda b, t: (b * per_b + t, 0)),
        compiler_params=_params(("arbitrary", "arbitrary")),
        name="combine",
    )(rank_t, gate_t, *ys, x1, mod3, final_g)


def _rope_tables():
    t = jnp.arange(SEQ)
    pos = jnp.stack([t // GRID_W, t % GRID_W], axis=1).astype(F32)
    per_axis = HEAD_DIM // 2
    inv = ROPE_THETA ** (-jnp.arange(0, per_axis, 2, dtype=F32) / per_axis)
    ang = pos[:, :, None] * inv
    cos, sin = jnp.cos(ang), jnp.sin(ang)
    zero = jnp.zeros_like(sin)

    def lanes(first_half, second_half):
        one_map = jnp.stack([first_half, second_half], axis=2).reshape(SEQ, HEAD_DIM)
        return jnp.concatenate([one_map, one_map], axis=1)

    return lanes(cos, cos), lanes(-sin, zero), lanes(zero, sin)


def kernel(x, c, ctx, c_ctx, norm1_g, norm2_g, w_ada, b_ada, w_in, conv_w, w_out_conv, lambda_q1,
           lambda_k1, lambda_q2, lambda_k2, subln_g, w_o_attn, w_out, w_router, w_gate_e, w_up_e,
           w_down_e, final_g):
    cond = jnp.concatenate([c, c_ctx[None, :], jnp.zeros((COND_ROWS - BATCH - 1, D_MODEL), F32)], axis=0)
    mod3 = _ada(cond, w_ada[0], b_ada).reshape(COND_ROWS, 6, D_MODEL)

    cos, sin_hi, sin_lo = _rope_tables()
    q_scale = math.log2(math.e) * HEAD_DIM ** -0.5
    parts = _inproj(x, mod3, norm1_g, w_in[0], conv_w[0], cos, sin_hi, sin_lo, q_scale)
    ctx_kv = _ctxproj(ctx.reshape(BATCH * CTX_LEN, D_MODEL), mod3, norm1_g, w_in[0])

    lam_params = jnp.concatenate([lambda_q1, lambda_k1, lambda_q2, lambda_k2], axis=0)
    attn_o = _attn(lam_params, parts, ctx_kv, subln_g)

    wr_t = w_router[0].T
    wr_hi = wr_t.astype(BF16)
    wr_lo = (wr_t - wr_hi.astype(F32)).astype(BF16)
    x1, h2, logits_t = _merge(parts, attn_o, x.reshape(BATCH * SEQ, D_MODEL), mod3, norm2_g,
                              w_out_conv[0].astype(BF16), w_o_attn[0].astype(BF16), w_out[0].astype(BF16),
                              wr_hi, wr_lo)

    rank_e, rank_t, gate_t = _route(logits_t)
    ys = []
    for first in range(0, N_EXPERTS, GROUP_EXPERTS):
        xs = _dispatch(rank_e, h2, first).reshape(GROUP_EXPERTS, BATCH * CAPACITY, D_MODEL // 2)
        ys.append(_experts(xs, w_gate_e[0], w_up_e[0], w_down_e[0], first))
    out = _combine(rank_t, gate_t, ys, x1, mod3, final_g[None, :])
    return out.reshape(BATCH, SEQ, D_MODEL)
```

```python
import functools
import math

import jax
import jax.numpy as jnp
from jax import lax
from jax.experimental import pallas as pl
from jax.experimental.pallas import tpu as pltpu
from jax.experimental.pallas import tpu_sc as plsc

D_MODEL = 1024
BATCH = 8
SEQ = 2048
GRID_W = 64
CTX_LEN = 256
N_HEADS = 8
HEAD_DIM = 64
V_DIM = 2 * HEAD_DIM
N_EXPERTS = 16
EXPERT_HIDDEN = 2048
CAPACITY = 2 * SEQ // N_EXPERTS
ROPE_THETA = 10000.0
EPS = 1e-6
LAM_INIT = 0.8 - 0.6 * math.exp(-0.3 * 0)
N_PARTS = 8
N_KEYS = CTX_LEN + SEQ
N_GROUPS = 2
GROUP_EXPERTS = N_EXPERTS // N_GROUPS

LANES = 128
SC_CORES = 2
SC_SUBCORES = 16
SC_LANES = 16
F32 = jnp.float32
BF16 = jnp.bfloat16

ROW_CHUNK = 512
COND_ROWS = 16
ADA_TN = 512
ATTN_TQ = 2048
ATTN_SQ = 256
MERGE_TM = 1024
FFN_TF = 512
COMBINE_TQ = 512
VMEM_LIMIT = 56 * 1024 * 1024


def _dot(a, b):
    return jnp.dot(a, b, preferred_element_type=F32)


def _dot_nt(a, b):
    return lax.dot_general(a, b, (((1,), (1,)), ((), ())), preferred_element_type=F32)


def _params(sem, vmem=VMEM_LIMIT):
    return pltpu.CompilerParams(dimension_semantics=sem, vmem_limit_bytes=vmem)


def _ada_kernel(cond_ref, w_ref, b_ref, o_ref):
    c = cond_ref[...]
    s = c / (1.0 + jnp.exp(-c))
    o_ref[...] = _dot(s.astype(BF16), w_ref[...].astype(BF16)) + b_ref[...]


def _ada(cond, w_ada, b_ada):
    n = w_ada.shape[1]
    return pl.pallas_call(
        _ada_kernel,
        out_shape=jax.ShapeDtypeStruct((COND_ROWS, n), F32),
        grid=(n // ADA_TN,),
        in_specs=[
            pl.BlockSpec((COND_ROWS, D_MODEL), lambda i: (0, 0)),
            pl.BlockSpec((D_MODEL, ADA_TN), lambda i: (0, i)),
            pl.BlockSpec((1, ADA_TN), lambda i: (0, i)),
        ],
        out_specs=pl.BlockSpec((COND_ROWS, ADA_TN), lambda i: (0, i)),
        compiler_params=_params(("arbitrary",)),
        name="ada",
    )(cond, w_ada, b_ada)


def _norm_modulate(x, g, shift, scale):
    ms = jnp.mean(x * x, axis=-1, keepdims=True)
    return (x * lax.rsqrt(ms + EPS) * g) * (1.0 + scale) + shift


def _rope(a, cos, sin_hi, sin_lo):
    return a * cos + pltpu.roll(a, LANES - 16, 1) * sin_hi + pltpu.roll(a, 16, 1) * sin_lo


def _inproj_kernel(x_ref, mod_ref, g_ref, w_ref, cw_ref, cos_ref, shi_ref, slo_ref, o_ref,
                   hx_ref, z_ref, w_bf, *, q_scale):
    t = pl.program_id(0)
    j = (t + N_PARTS - 1) % N_PARTS
    chunks = [slice(r * ROW_CHUNK, (r + 1) * ROW_CHUNK) for r in range(SEQ // ROW_CHUNK)]

    def stage(slot, part):
        rows = slice(part * D_MODEL // len(chunks), (part + 1) * D_MODEL // len(chunks))
        w_bf[slot, rows, :] = w_ref[rows, :].astype(BF16)

    def proj(part, r):
        stage(1 - part % 2, r)
        return _dot(hx_ref[chunks[r], :], w_bf[part % 2])

    def branch(part):
        return pl.when((t > 0) & (j == part))

    @pl.when(t == 0)
    def _():
        for r in range(len(chunks)):
            stage(0, r)

    @branch(0)
    def _():
        shift = mod_ref[0, 0:1, :]
        scale = mod_ref[0, 1:2, :]
        for rows in chunks:
            hx_ref[rows, :] = _norm_modulate(x_ref[0, rows, :], g_ref[...], shift, scale).astype(BF16)
        z_ref[0:8, :] = jnp.zeros((8, D_MODEL), F32)
        z_ref[SEQ + 8:SEQ + 16, :] = jnp.zeros((8, D_MODEL), F32)
        for r, rows in enumerate(chunks):
            z_ref[pl.ds(rows.start + 8, ROW_CHUNK), :] = proj(0, r)

    @branch(1)
    def _():
        for r, rows in enumerate(chunks):
            zr = pl.ds(rows.start + 8, ROW_CHUNK)
            z_ref[zr, :] = z_ref[zr, :] * proj(1, r)

    @branch(2)
    def _():
        first_row = lax.broadcasted_iota(jnp.int32, (8, 1), 0) == 0
        last_row = lax.broadcasted_iota(jnp.int32, (8, 1), 0) == 7
        for r, rows in enumerate(chunks):
            base = rows.start + 8
            zc = z_ref[pl.ds(base, ROW_CHUNK), :]
            before = pltpu.roll(zc * cw_ref[0:1, :], 1, 0)
            after = pltpu.roll(zc * cw_ref[2:3, :], ROW_CHUNK - 1, 0)
            edge_b = jnp.where(first_row, z_ref[pl.ds(base - 8, 8), :][7:8, :] * cw_ref[0:1, :], before[0:8, :])
            edge_a = jnp.where(last_row, z_ref[pl.ds(base + ROW_CHUNK, 8), :][0:1, :] * cw_ref[2:3, :],
                               after[ROW_CHUNK - 8:, :])
            before = jnp.concatenate([edge_b, before[8:, :]], axis=0)
            after = jnp.concatenate([after[:ROW_CHUNK - 8, :], edge_a], axis=0)
            y = before + zc * cw_ref[1:2, :] + after
            o_ref[0, rows, :] = (proj(2, r) * y).astype(BF16)

    def rope_part(part, scale):
        for r, rows in enumerate(chunks):
            acc = proj(part, r)
            cos, shi, slo = cos_ref[rows, :], shi_ref[rows, :], slo_ref[rows, :]
            for h in range(D_MODEL // LANES):
                cols = slice(h * LANES, (h + 1) * LANES)
                roped = _rope(acc[:, cols], cos, shi, slo)
                if scale is not None:
                    roped = roped * scale
                o_ref[0, rows, cols] = roped.astype(BF16)

    @branch(3)
    def _():
        rope_part(3, q_scale)

    @branch(4)
    def _():
        rope_part(4, None)

    @branch(5)
    def _():
        for r, rows in enumerate(chunks):
            o_ref[0, rows, :] = proj(5, r).astype(BF16)

    for gate_part in (6, 7):
        @branch(gate_part)
        def _(gate_part=gate_part):
            for r, rows in enumerate(chunks):
                o_ref[0, rows, :] = (1.0 / (1.0 + jnp.exp(-proj(gate_part, r)))).astype(BF16)


def _inproj(x, mod3, norm_g, w_in, conv_w, cos, sin_hi, sin_lo, q_scale):
    last = BATCH * N_PARTS - 1

    def part(t):
        return jnp.maximum(t - 1, 0) % N_PARTS

    def sample(t):
        return jnp.maximum(t - 1, 0) // N_PARTS

    def w_map(t):
        nxt = jnp.minimum(t, last) % N_PARTS
        return (0, jnp.where(nxt == 1, 2, jnp.where(nxt == 2, 1, nxt)))

    tab = pl.BlockSpec((SEQ, LANES), lambda t: (0, 0))
    return pl.pallas_call(
        functools.partial(_inproj_kernel, q_scale=q_scale),
        out_shape=jax.ShapeDtypeStruct((N_PARTS - 2, BATCH * SEQ, D_MODEL), BF16),
        grid=(BATCH * N_PARTS + 1,),
        in_specs=[
            pl.BlockSpec((1, SEQ, D_MODEL), lambda t: (sample(t), 0, 0)),
            pl.BlockSpec((1, 6, D_MODEL), lambda t: (sample(t), 0, 0)),
            pl.BlockSpec((1, D_MODEL), lambda t: (0, 0)),
            pl.BlockSpec((D_MODEL, D_MODEL), w_map),
            pl.BlockSpec((3, D_MODEL), lambda t: (0, 0)),
            tab, tab, tab,
        ],
        out_specs=pl.BlockSpec((1, SEQ, D_MODEL), lambda t: (jnp.maximum(part(t) - 2, 0), sample(t), 0)),
        scratch_shapes=[
            pltpu.VMEM((SEQ, D_MODEL), BF16),
            pltpu.VMEM((SEQ + 16, D_MODEL), F32),
            pltpu.VMEM((2, D_MODEL, D_MODEL), BF16),
        ],
        compiler_params=_params(("arbitrary",)),
        name="inproj",
    )(x, mod3, norm_g, w_in, conv_w, cos, sin_hi, sin_lo)


def _ctxproj_kernel(c_ref, mod_ref, g_ref, w_ref, o_ref, hc_ref):
    j = pl.program_id(0)
    chunks = [slice(r * ROW_CHUNK, (r + 1) * ROW_CHUNK) for r in range(BATCH * CTX_LEN // ROW_CHUNK)]

    @pl.when(j == 0)
    def _():
        shift = mod_ref[0, 0:1, :]
        scale = mod_ref[0, 1:2, :]
        for rows in chunks:
            hc_ref[rows, :] = _norm_modulate(c_ref[rows, :], g_ref[...], shift, scale).astype(BF16)

    w = w_ref[...].astype(BF16)
    for rows in chunks:
        o_ref[0, rows, :] = _dot(hc_ref[rows, :], w).astype(BF16)


def _ctxproj(ctx2, mod3, norm_g, w_in):
    rows = BATCH * CTX_LEN
    return pl.pallas_call(
        _ctxproj_kernel,
        out_shape=jax.ShapeDtypeStruct((2, rows, D_MODEL), BF16),
        grid=(2,),
        in_specs=[
            pl.BlockSpec((rows, D_MODEL), lambda j: (0, 0)),
            pl.BlockSpec((1, 6, D_MODEL), lambda j: (BATCH, 0, 0)),
            pl.BlockSpec((1, D_MODEL), lambda j: (0, 0)),
            pl.BlockSpec((D_MODEL, D_MODEL), lambda j: (0, 4 + j)),
        ],
        out_specs=pl.BlockSpec((1, rows, D_MODEL), lambda j: (j, 0, 0)),
        scratch_shapes=[pltpu.VMEM((rows, D_MODEL), BF16)],
        compiler_params=_params(("arbitrary",)),
        name="ctxproj",
    )(ctx2, mod3, norm_g, w_in)


def _attn_kernel(lp_ref, q_ref, k_ref, v_ref, kc_ref, vc_ref, g_ref, o_ref, k_all, v_ext, s_ref, m_ref):
    lp = lp_ref[...]
    lam = (jnp.exp(jnp.sum(lp[0:1] * lp[1:2], axis=-1, keepdims=True))
           - jnp.exp(jnp.sum(lp[2:3] * lp[3:4], axis=-1, keepdims=True)) + LAM_INIT)
    k_all[0:CTX_LEN, :] = kc_ref[0]
    k_all[CTX_LEN:N_KEYS, :] = k_ref[0]
    v_ext[0:CTX_LEN, 0:LANES] = vc_ref[0]
    v_ext[CTX_LEN:N_KEYS, 0:LANES] = v_ref[0]
    v_ext[:, LANES:2 * LANES] = jnp.ones((N_KEYS, LANES), BF16)
    first_map = lax.broadcasted_iota(jnp.int32, (1, LANES), 1) < HEAD_DIM

    def scores(u):
        q = q_ref[0, u * ATTN_SQ:(u + 1) * ATTN_SQ, :]
        zero = jnp.zeros_like(q)
        for mp, qm in enumerate((jnp.where(first_map, q, zero), jnp.where(first_map, zero, q))):
            s = _dot_nt(qm, k_all[...])
            s_ref[u % 2, mp] = s
            m_ref[u % 2, mp] = jnp.max(s, axis=-1, keepdims=True)

    def values(u):
        o = []
        for mp in range(2):
            p = jnp.exp2(s_ref[u % 2, mp] - m_ref[u % 2, mp])
            o.append(_dot(p.astype(BF16), v_ext[...]))
        a = o[0][:, :LANES] / o[0][:, LANES:] - o[1][:, :LANES] * (lam / o[1][:, LANES:])
        ms = jnp.mean(a * a, axis=-1, keepdims=True)
        o_ref[u * ATTN_SQ:(u + 1) * ATTN_SQ, :] = (
            (a * lax.rsqrt(ms + EPS) * g_ref[...]) * (1.0 - LAM_INIT)).astype(BF16)

    n_units = ATTN_TQ // ATTN_SQ
    scores(0)
    for u in range(n_units):
        if u + 1 < n_units:
            scores(u + 1)
        values(u)


def _attn(lam_params, parts, ctx_kv, subln_g):
    nq = SEQ // ATTN_TQ
    return pl.pallas_call(
        _attn_kernel,
        out_shape=jax.ShapeDtypeStruct((BATCH * SEQ, D_MODEL), BF16),
        grid=(BATCH, N_HEADS, nq),
        in_specs=[
            pl.BlockSpec((4, HEAD_DIM), lambda b, h, i: (0, 0)),
            pl.BlockSpec((1, ATTN_TQ, LANES), lambda b, h, i: (1, b * nq + i, h)),
            pl.BlockSpec((1, SEQ, LANES), lambda b, h, i: (2, b, h)),
            pl.BlockSpec((1, SEQ, LANES), lambda b, h, i: (3, b, h)),
            pl.BlockSpec((1, CTX_LEN, LANES), lambda b, h, i: (0, b, h)),
            pl.BlockSpec((1, CTX_LEN, LANES), lambda b, h, i: (1, b, h)),
            pl.BlockSpec((1, V_DIM), lambda b, h, i: (0, 0)),
        ],
        out_specs=pl.BlockSpec((ATTN_TQ, LANES), lambda b, h, i: (b * nq + i, h)),
        scratch_shapes=[
            pltpu.VMEM((N_KEYS, LANES), BF16),
            pltpu.VMEM((N_KEYS, 2 * LANES), BF16),
            pltpu.VMEM((2, 2, ATTN_SQ, N_KEYS), F32),
            pltpu.VMEM((2, 2, ATTN_SQ, 1), F32),
        ],
        compiler_params=_params(("arbitrary", "arbitrary", "arbitrary")),
        name="attn",
    )(lam_params, parts, parts, parts, ctx_kv, ctx_kv, subln_g)


def _merge_kernel(yb_ref, s6_ref, s7_ref, o_ref, x_ref, mod_ref, g2_ref, wc_ref, wa_ref, wo_ref,
                  wrh_ref, wrl_ref, x1_ref, h2_ref, lg_ref):
    chunks = [slice(r * ROW_CHUNK, (r + 1) * ROW_CHUNK) for r in range(MERGE_TM // ROW_CHUNK)]
    for rows in chunks:
        y_conv = _dot(yb_ref[0, rows, :], wc_ref[...])
        y_attn = _dot(o_ref[rows, :], wa_ref[...])
        merged = s6_ref[0, rows, :].astype(F32) * y_conv + s7_ref[0, rows, :].astype(F32) * y_attn
        mix = _dot(merged.astype(BF16), wo_ref[...])
        x1_ref[rows, :] = x_ref[rows, :] + mod_ref[0, 2:3, :] * mix
    for rows in chunks:
        h2 = _norm_modulate(x1_ref[rows, :], g2_ref[...], mod_ref[0, 3:4, :], mod_ref[0, 4:5, :])
        h_hi = h2.astype(BF16)
        h2_ref[rows, :] = pltpu.pack_elementwise([h2[:, :D_MODEL // 2], h2[:, D_MODEL // 2:]], packed_dtype=BF16)
        h_lo = (h2 - h_hi.astype(F32)).astype(BF16)
        lg_ref[0, :, rows] = (_dot_nt(wrh_ref[...], h_hi)
                              + (_dot_nt(wrh_ref[...], h_lo) + _dot_nt(wrl_ref[...], h_hi)))


def _merge(parts, attn_o, x2, mod3, norm2_g, w_conv_bf, w_attn_bf, w_out_bf, wr_hi, wr_lo):
    per_b = SEQ // MERGE_TM
    sq = pl.BlockSpec((D_MODEL, D_MODEL), lambda i: (0, 0), pipeline_mode=pl.Buffered(1))
    wr = pl.BlockSpec((N_EXPERTS, D_MODEL), lambda i: (0, 0), pipeline_mode=pl.Buffered(1))
    tile = pl.BlockSpec((MERGE_TM, D_MODEL), lambda i: (i, 0))
    return pl.pallas_call(
        _merge_kernel,
        out_shape=(
            jax.ShapeDtypeStruct((BATCH * SEQ, D_MODEL), F32),
            jax.ShapeDtypeStruct((BATCH * SEQ, D_MODEL // 2), jnp.uint32),
            jax.ShapeDtypeStruct((BATCH, N_EXPERTS, SEQ), F32),
        ),
        grid=(BATCH * per_b,),
        in_specs=[
            pl.BlockSpec((1, MERGE_TM, D_MODEL), lambda i: (0, i, 0)),
            pl.BlockSpec((1, MERGE_TM, D_MODEL), lambda i: (4, i, 0)),
            pl.BlockSpec((1, MERGE_TM, D_MODEL), lambda i: (5, i, 0)),
            tile, tile,
            pl.BlockSpec((1, 6, D_MODEL), lambda i: (i // per_b, 0, 0)),
            pl.BlockSpec((1, D_MODEL), lambda i: (0, 0)),
            sq, sq, sq, wr, wr,
        ],
        out_specs=(
            tile,
            pl.BlockSpec((MERGE_TM, D_MODEL // 2), lambda i: (i, 0)),
            pl.BlockSpec((1, N_EXPERTS, MERGE_TM), lambda i: (i // per_b, 0, i % per_b)),
        ),
        compiler_params=_params(("arbitrary",)),
        name="merge",
    )(parts, parts, parts, attn_o, x2, mod3, norm2_g, w_conv_bf, w_attn_bf, w_out_bf, wr_hi, wr_lo)


def _route_kernel(lg_ref, rank_e_ref, rank_t_ref, gate_t_ref, bounds_ref):
    lg = lg_ref[...]
    ex = jnp.exp(lg - jnp.max(lg, axis=1, keepdims=True))
    aff = (ex / jnp.sum(ex, axis=1, keepdims=True)).reshape(BATCH * N_EXPERTS, SEQ)
    bits = lax.bitcast_convert_type(aff, jnp.int32)

    def count(mask):
        return jnp.sum(jnp.where(mask, 1.0, 0.0), axis=1, keepdims=True)

    def search(i, t):
        cand = t + lax.shift_left(jnp.int32(1), 30 - i)
        return jnp.where(count(bits >= cand) >= CAPACITY, cand, t)

    thr = lax.fori_loop(0, 31, search, jnp.zeros((BATCH * N_EXPERTS, 1), jnp.int32))
    need = CAPACITY - count(bits > thr)
    before = (lax.broadcasted_iota(jnp.int32, (LANES, LANES), 0)
              < lax.broadcasted_iota(jnp.int32, (LANES, LANES), 1)).astype(BF16)
    n_tied = jnp.zeros((BATCH * N_EXPERTS, 1), F32)
    n_sel = jnp.zeros((BATCH * N_EXPERTS, 1), F32)
    per_tile = COMBINE_TQ // LANES
    firsts, ends = [], []
    for c in range(SEQ // LANES):
        if c % per_tile == 0:
            firsts.append(n_sel)
        cols = slice(c * LANES, (c + 1) * LANES)
        bits_c = bits[:, cols]
        tied_c = jnp.where(bits_c == thr, 1.0, 0.0)
        tied_before = _dot(tied_c.astype(BF16), before) + n_tied
        sel_c = jnp.where(bits_c > thr, 1.0, jnp.where(tied_before < need, tied_c, 0.0))
        rank = _dot(sel_c.astype(BF16), before) + n_sel
        rank_c = jnp.where(sel_c > 0.0, rank, -1.0)
        gate_c = jnp.where(sel_c > 0.0, aff[:, cols], 0.0)
        n_tied = n_tied + jnp.sum(tied_c, axis=1, keepdims=True)
        n_sel = n_sel + jnp.sum(sel_c, axis=1, keepdims=True)
        rank_e_ref[:, cols] = rank_c.astype(jnp.int32)
        rank_ct = rank_c.T
        gate_ct = gate_c.T
        for b in range(BATCH):
            ecols = slice(b * N_EXPERTS, (b + 1) * N_EXPERTS)
            rank_t_ref[b, cols, :] = rank_ct[:, ecols]
            gate_t_ref[b, cols, :] = gate_ct[:, ecols]
        if c % per_tile == per_tile - 1:
            ends.append(n_sel)
    bounds_ref[...] = jnp.concatenate(firsts + ends, axis=1).astype(jnp.int32)


def _route(logits_t):
    return pl.pallas_call(
        _route_kernel,
        out_shape=(
            jax.ShapeDtypeStruct((BATCH * N_EXPERTS, SEQ), jnp.int32),
            jax.ShapeDtypeStruct((BATCH, SEQ, N_EXPERTS), F32),
            jax.ShapeDtypeStruct((BATCH, SEQ, N_EXPERTS), F32),
            jax.ShapeDtypeStruct((BATCH * N_EXPERTS, 2 * (SEQ // COMBINE_TQ)), jnp.int32),
        ),
        compiler_params=pltpu.CompilerParams(vmem_limit_bytes=VMEM_LIMIT),
        name="route",
    )(logits_t)


def _dispatch(rank_e, h_packed, first_expert):
    per_worker = BATCH * GROUP_EXPERTS // (SC_CORES * SC_SUBCORES)
    half = CAPACITY // 2
    mesh = plsc.VectorSubcoreMesh(core_axis_name="c", subcore_axis_name="s")

    @functools.partial(
        pl.kernel, mesh=mesh,
        out_type=jax.ShapeDtypeStruct((GROUP_EXPERTS * BATCH * CAPACITY, D_MODEL // 2), h_packed.dtype),
        scratch_types=[
            pltpu.VMEM((SEQ,), jnp.int32),
            pltpu.VMEM((2, half), jnp.int32),
            pltpu.VMEM((half, D_MODEL // 2), h_packed.dtype),
            pltpu.SemaphoreType.DMA,
        ],
        compiler_params=pltpu.CompilerParams(needs_layout_passes=False),
        name="dispatch",
    )
    def k(table_hbm, rank_hbm, out_hbm, rank_v, idx_v, rows_v, sem):
        worker = lax.axis_index("s") * SC_CORES + lax.axis_index("c")
        expert = worker // (BATCH // per_worker)

        @pl.loop(0, per_worker)
        def _(p):
            sample = (worker % (BATCH // per_worker)) * per_worker + p
            pltpu.sync_copy(rank_hbm.at[sample * N_EXPERTS + first_expert + expert], rank_v)

            @pl.loop(0, SEQ // SC_LANES)
            def _(i):
                rank = rank_v[pl.ds(i * SC_LANES, SC_LANES)]
                row = lax.iota(jnp.int32, SC_LANES) + (i * SC_LANES + sample * SEQ)
                plsc.store_scatter(idx_v, [lax.shift_right_arithmetic(rank, 7), rank & (half - 1)], row,
                                   mask=rank >= 0)

            for c in range(2):
                pltpu.async_copy(table_hbm.at[idx_v.at[c]], rows_v, sem).wait()
                first = (expert * BATCH + sample) * CAPACITY + c * half
                pltpu.sync_copy(rows_v, out_hbm.at[pl.ds(first, half)])

    return k(h_packed, rank_e)


def _expert_kernel(xs_ref, wg_ref, wu_ref, wd_ref, y_ref, acc_ref, wg_bf, wu_bf, wd_bf):
    t = pl.program_id(0)
    n_f = EXPERT_HIDDEN // FFN_TF
    f = (t + n_f - 1) % n_f

    n_chunks = BATCH * CAPACITY // ROW_CHUNK

    def stage(slot, part):
        up = slice(part * D_MODEL // n_chunks, (part + 1) * D_MODEL // n_chunks)
        down = slice(part * FFN_TF // n_chunks, (part + 1) * FFN_TF // n_chunks)
        wg_bf[slot, up, :] = wg_ref[0, up, :].astype(BF16)
        wu_bf[slot, up, :] = wu_ref[0, up, :].astype(BF16)
        wd_bf[slot, down, :] = wd_ref[0, down, :].astype(BF16)

    def step(cur):
        for r in range(n_chunks):
            stage(1 - cur, r)
            rows = slice(r * ROW_CHUNK, (r + 1) * ROW_CHUNK)
            packed = xs_ref[0, rows, :]
            half = D_MODEL // 2
            x_lo = pltpu.unpack_elementwise(packed, index=0, packed_dtype=BF16, unpacked_dtype=F32).astype(BF16)
            x_hi = pltpu.unpack_elementwise(packed, index=1, packed_dtype=BF16, unpacked_dtype=F32).astype(BF16)
            a = _dot(x_lo, wg_bf[cur, :half, :]) + _dot(x_hi, wg_bf[cur, half:, :])
            u = _dot(x_lo, wu_bf[cur, :half, :]) + _dot(x_hi, wu_bf[cur, half:, :])
            hidden = (a / (1.0 + jnp.exp(-a))) * u
            total = jnp.where(f == 0, 0.0, acc_ref[rows, :]) + _dot(hidden.astype(BF16), wd_bf[cur])
            acc_ref[rows, :] = total
            y_ref[0, rows, :] = total.astype(BF16)

    @pl.when(t == 0)
    def _():
        for part in range(n_chunks):
            stage(0, part)

    @pl.when(t % 2 == 1)
    def _():
        step(0)

    @pl.when((t % 2 == 0) & (t > 0))
    def _():
        step(1)


def _experts(xs, w_gate, w_up, w_down, first_expert):
    rows = BATCH * CAPACITY
    n_f = EXPERT_HIDDEN // FFN_TF
    last = GROUP_EXPERTS * n_f - 1

    def staged(t):
        return jnp.minimum(t, last)

    def computed(t):
        return jnp.maximum(t - 1, 0) // n_f

    return pl.pallas_call(
        _expert_kernel,
        out_shape=jax.ShapeDtypeStruct((GROUP_EXPERTS, rows, D_MODEL), BF16),
        grid=(GROUP_EXPERTS * n_f + 1,),
        in_specs=[
            pl.BlockSpec((1, rows, D_MODEL // 2), lambda t: (computed(t), 0, 0)),
            pl.BlockSpec((1, D_MODEL, FFN_TF), lambda t: (first_expert + staged(t) // n_f, 0, staged(t) % n_f)),
            pl.BlockSpec((1, D_MODEL, FFN_TF), lambda t: (first_expert + staged(t) // n_f, 0, staged(t) % n_f)),
            pl.BlockSpec((1, FFN_TF, D_MODEL), lambda t: (first_expert + staged(t) // n_f, staged(t) % n_f, 0)),
        ],
        out_specs=pl.BlockSpec((1, rows, D_MODEL), lambda t: (computed(t), 0, 0)),
        scratch_shapes=[
            pltpu.VMEM((rows, D_MODEL), F32),
            pltpu.VMEM((2, D_MODEL, FFN_TF), BF16),
            pltpu.VMEM((2, D_MODEL, FFN_TF), BF16),
            pltpu.VMEM((2, FFN_TF, D_MODEL), BF16),
        ],
        compiler_params=_params(("arbitrary",)),
        name="experts",
    )(xs, w_gate, w_up, w_down)


def _combine_kernel(bounds_ref, rank_ref, gate_ref, *refs):
    y_refs, (x1_ref, mod_ref, fg_ref, o_ref) = refs[:N_GROUPS], refs[N_GROUPS:]
    b, t = pl.program_id(0), pl.program_id(1)
    tiles = SEQ // COMBINE_TQ
    window = CAPACITY // 2

    def y_rows(e):
        return y_refs[e // GROUP_EXPERTS].at[e % GROUP_EXPERTS]

    def scatter(e, first, n_slots):
        slot = lax.broadcasted_iota(jnp.int32, (1, n_slots), 1).astype(F32)
        rank = rank_ref[0, :, e:e + 1] - first
        return jnp.where(rank == slot, gate_ref[0, :, e:e + 1], 0.0).astype(BF16)

    def finish(acc):
        x2 = x1_ref[...] + mod_ref[0, 5:6, :] * acc
        ms = jnp.mean(x2 * x2, axis=-1, keepdims=True)
        o_ref[...] = x2 * lax.rsqrt(ms + EPS) * fg_ref[...]

    starts, fits = [], None
    for e in range(N_EXPERTS):
        pair = (b * N_EXPERTS + e) * (2 * tiles)
        first, end = bounds_ref[pair + t], bounds_ref[pair + tiles + t]
        start = jnp.minimum(lax.shift_left(lax.shift_right_logical(first, 4), 4), CAPACITY - window)
        starts.append(start)
        ok = end <= start + window
        fits = ok if fits is None else fits & ok

    @pl.when(fits)
    def _():
        acc = jnp.zeros((COMBINE_TQ, D_MODEL), F32)
        for e in range(0, N_EXPERTS, 2):
            first = [pl.multiple_of(starts[e + i], 16) for i in range(2)]
            onehot = jnp.concatenate([scatter(e + i, first[i].astype(F32), window) for i in range(2)], axis=1)
            rows = jnp.concatenate([y_rows(e + i)[pl.ds(first[i], window), :] for i in range(2)], axis=0)
            acc = acc + _dot(onehot, rows)
        finish(acc)

    @pl.when(jnp.logical_not(fits))
    def _():
        acc = jnp.zeros((COMBINE_TQ, D_MODEL), F32)
        for e in range(N_EXPERTS):
            acc = acc + _dot(scatter(e, 0.0, CAPACITY), y_rows(e)[...])
        finish(acc)


def _combine(bounds, rank_t, gate_t, ys, x1, mod3, final_g):
    per_b = SEQ // COMBINE_TQ
    return pl.pallas_call(
        _combine_kernel,
        out_shape=jax.ShapeDtypeStruct((BATCH * SEQ, D_MODEL), F32),
        grid_spec=pltpu.PrefetchScalarGridSpec(
            num_scalar_prefetch=1,
            grid=(BATCH, per_b),
            in_specs=[
                pl.BlockSpec((1, COMBINE_TQ, N_EXPERTS), lambda b, t, _: (b, t, 0)),
                pl.BlockSpec((1, COMBINE_TQ, N_EXPERTS), lambda b, t, _: (b, t, 0)),
                *[pl.BlockSpec((GROUP_EXPERTS, CAPACITY, D_MODEL), lambda b, t, _: (0, b, 0)) for _ in ys],
                pl.BlockSpec((COMBINE_TQ, D_MODEL), lambda b, t, _: (b * per_b + t, 0)),
                pl.BlockSpec((1, 6, D_MODEL), lambda b, t, _: (b, 0, 0)),
                pl.BlockSpec((1, D_MODEL), lambda b, t, _: (0, 0)),
            ],
            out_specs=pl.BlockSpec((COMBINE_TQ, D_MODEL), lambda b, t, _: (b * per_b + t, 0)),
        ),
        compiler_params=_params(("arbitrary", "arbitrary")),
        name="combine",
    )(bounds.reshape(-1), rank_t, gate_t, *ys, x1, mod3, final_g)


def _rope_tables():
    t = jnp.arange(SEQ)
    pos = jnp.stack([t // GRID_W, t % GRID_W], axis=1).astype(F32)
    per_axis = HEAD_DIM // 2
    inv = ROPE_THETA ** (-jnp.arange(0, per_axis, 2, dtype=F32) / per_axis)
    ang = pos[:, :, None] * inv
    cos, sin = jnp.cos(ang), jnp.sin(ang)
    zero = jnp.zeros_like(sin)

    def lanes(first_half, second_half):
        one_map = jnp.stack([first_half, second_half], axis=2).reshape(SEQ, HEAD_DIM)
        return jnp.concatenate([one_map, one_map], axis=1)

    return lanes(cos, cos), lanes(-sin, zero), lanes(zero, sin)


def kernel(x, c, ctx, c_ctx, norm1_g, norm2_g, w_ada, b_ada, w_in, conv_w, w_out_conv, lambda_q1,
           lambda_k1, lambda_q2, lambda_k2, subln_g, w_o_attn, w_out, w_router, w_gate_e, w_up_e,
           w_down_e, final_g):
    cond = jnp.concatenate([c, c_ctx[None, :], jnp.zeros((COND_ROWS - BATCH - 1, D_MODEL), F32)], axis=0)
    mod3 = _ada(cond, w_ada[0], b_ada).reshape(COND_ROWS, 6, D_MODEL)

    cos, sin_hi, sin_lo = _rope_tables()
    q_scale = math.log2(math.e) * HEAD_DIM ** -0.5
    parts = _inproj(x, mod3, norm1_g, w_in[0], conv_w[0], cos, sin_hi, sin_lo, q_scale)
    ctx_kv = _ctxproj(ctx.reshape(BATCH * CTX_LEN, D_MODEL), mod3, norm1_g, w_in[0])

    lam_params = jnp.concatenate([lambda_q1, lambda_k1, lambda_q2, lambda_k2], axis=0)
    attn_o = _attn(lam_params, parts, ctx_kv, subln_g)

    wr_t = w_router[0].T
    wr_hi = wr_t.astype(BF16)
    wr_lo = (wr_t - wr_hi.astype(F32)).astype(BF16)
    x1, h2, logits_t = _merge(parts, attn_o, x.reshape(BATCH * SEQ, D_MODEL), mod3, norm2_g,
                              w_out_conv[0].astype(BF16), w_o_attn[0].astype(BF16), w_out[0].astype(BF16),
                              wr_hi, wr_lo)

    rank_e, rank_t, gate_t, bounds = _route(logits_t)
    ys = []
    for first in range(0, N_EXPERTS, GROUP_EXPERTS):
        xs = _dispatch(rank_e, h2, first).reshape(GROUP_EXPERTS, BATCH * CAPACITY, D_MODEL // 2)
        ys.append(_experts(xs, w_gate_e[0], w_up_e[0], w_down_e[0], first))
    out = _combine(bounds, rank_t, gate_t, ys, x1, mod3, final_g[None, :])
    return out.reshape(BATCH, SEQ, D_MODEL)
```

```python
import functools
import math

import jax
import jax.numpy as jnp
from jax import lax
from jax.experimental import pallas as pl
from jax.experimental.pallas import tpu as pltpu
from jax.experimental.pallas import tpu_sc as plsc

D_MODEL = 1024
BATCH = 8
SEQ = 2048
GRID_W = 64
CTX_LEN = 256
N_HEADS = 8
HEAD_DIM = 64
V_DIM = 2 * HEAD_DIM
N_EXPERTS = 16
EXPERT_HIDDEN = 2048
CAPACITY = 2 * SEQ // N_EXPERTS
ROPE_THETA = 10000.0
EPS = 1e-6
LAM_INIT = 0.8 - 0.6 * math.exp(-0.3 * 0)
N_PARTS = 8
N_KEYS = CTX_LEN + SEQ
N_GROUPS = 2
GROUP_EXPERTS = N_EXPERTS // N_GROUPS

LANES = 128
SC_CORES = 2
SC_SUBCORES = 16
SC_LANES = 16
F32 = jnp.float32
BF16 = jnp.bfloat16

ROW_CHUNK = 512
COND_ROWS = 16
ADA_TN = 512
ATTN_TQ = 2048
ATTN_SQ = 256
MERGE_TM = 1024
FFN_TF = 512
COMBINE_TQ = 512
COMBINE_SUB = 256
COMBINE_WINDOW = 64
VMEM_LIMIT = 56 * 1024 * 1024


def _dot(a, b):
    return jnp.dot(a, b, preferred_element_type=F32)


def _dot_nt(a, b):
    return lax.dot_general(a, b, (((1,), (1,)), ((), ())), preferred_element_type=F32)


def _params(sem, vmem=VMEM_LIMIT):
    return pltpu.CompilerParams(dimension_semantics=sem, vmem_limit_bytes=vmem)


def _ada_kernel(cond_ref, w_ref, b_ref, o_ref):
    c = cond_ref[...]
    s = c / (1.0 + jnp.exp(-c))
    o_ref[...] = _dot(s.astype(BF16), w_ref[...].astype(BF16)) + b_ref[...]


def _ada(cond, w_ada, b_ada):
    n = w_ada.shape[1]
    return pl.pallas_call(
        _ada_kernel,
        out_shape=jax.ShapeDtypeStruct((COND_ROWS, n), F32),
        grid=(n // ADA_TN,),
        in_specs=[
            pl.BlockSpec((COND_ROWS, D_MODEL), lambda i: (0, 0)),
            pl.BlockSpec((D_MODEL, ADA_TN), lambda i: (0, i)),
            pl.BlockSpec((1, ADA_TN), lambda i: (0, i)),
        ],
        out_specs=pl.BlockSpec((COND_ROWS, ADA_TN), lambda i: (0, i)),
        compiler_params=_params(("arbitrary",)),
        name="ada",
    )(cond, w_ada, b_ada)


def _norm_modulate(x, g, shift, scale):
    ms = jnp.mean(x * x, axis=-1, keepdims=True)
    return (x * lax.rsqrt(ms + EPS) * g) * (1.0 + scale) + shift


def _rope(a, cos, sin_hi, sin_lo):
    return a * cos + pltpu.roll(a, LANES - 16, 1) * sin_hi + pltpu.roll(a, 16, 1) * sin_lo


def _inproj_kernel(x_ref, mod_ref, g_ref, w_ref, cw_ref, cos_ref, shi_ref, slo_ref, o_ref,
                   hx_ref, z_ref, w_bf, *, q_scale):
    t = pl.program_id(0)
    j = (t + N_PARTS - 1) % N_PARTS
    chunks = [slice(r * ROW_CHUNK, (r + 1) * ROW_CHUNK) for r in range(SEQ // ROW_CHUNK)]

    def stage(slot, part):
        rows = slice(part * D_MODEL // len(chunks), (part + 1) * D_MODEL // len(chunks))
        w_bf[slot, rows, :] = w_ref[rows, :].astype(BF16)

    def proj(part, r):
        stage(1 - part % 2, r)
        return _dot(hx_ref[chunks[r], :], w_bf[part % 2])

    def branch(part):
        return pl.when((t > 0) & (j == part))

    @pl.when(t == 0)
    def _():
        for r in range(len(chunks)):
            stage(0, r)

    @branch(0)
    def _():
        shift = mod_ref[0, 0:1, :]
        scale = mod_ref[0, 1:2, :]
        for rows in chunks:
            hx_ref[rows, :] = _norm_modulate(x_ref[0, rows, :], g_ref[...], shift, scale).astype(BF16)
        z_ref[0:8, :] = jnp.zeros((8, D_MODEL), F32)
        z_ref[SEQ + 8:SEQ + 16, :] = jnp.zeros((8, D_MODEL), F32)
        for r, rows in enumerate(chunks):
            z_ref[pl.ds(rows.start + 8, ROW_CHUNK), :] = proj(0, r)

    @branch(1)
    def _():
        for r, rows in enumerate(chunks):
            zr = pl.ds(rows.start + 8, ROW_CHUNK)
            z_ref[zr, :] = z_ref[zr, :] * proj(1, r)

    @branch(2)
    def _():
        first_row = lax.broadcasted_iota(jnp.int32, (8, 1), 0) == 0
        last_row = lax.broadcasted_iota(jnp.int32, (8, 1), 0) == 7
        for r, rows in enumerate(chunks):
            base = rows.start + 8
            zc = z_ref[pl.ds(base, ROW_CHUNK), :]
            before = pltpu.roll(zc * cw_ref[0:1, :], 1, 0)
            after = pltpu.roll(zc * cw_ref[2:3, :], ROW_CHUNK - 1, 0)
            edge_b = jnp.where(first_row, z_ref[pl.ds(base - 8, 8), :][7:8, :] * cw_ref[0:1, :], before[0:8, :])
            edge_a = jnp.where(last_row, z_ref[pl.ds(base + ROW_CHUNK, 8), :][0:1, :] * cw_ref[2:3, :],
                               after[ROW_CHUNK - 8:, :])
            before = jnp.concatenate([edge_b, before[8:, :]], axis=0)
            after = jnp.concatenate([after[:ROW_CHUNK - 8, :], edge_a], axis=0)
            y = before + zc * cw_ref[1:2, :] + after
            o_ref[0, rows, :] = (proj(2, r) * y).astype(BF16)

    def rope_part(part, scale):
        for r, rows in enumerate(chunks):
            acc = proj(part, r)
            cos, shi, slo = cos_ref[rows, :], shi_ref[rows, :], slo_ref[rows, :]
            for h in range(D_MODEL // LANES):
                cols = slice(h * LANES, (h + 1) * LANES)
                roped = _rope(acc[:, cols], cos, shi, slo)
                if scale is not None:
                    roped = roped * scale
                o_ref[0, rows, cols] = roped.astype(BF16)

    @branch(3)
    def _():
        rope_part(3, q_scale)

    @branch(4)
    def _():
        rope_part(4, None)

    @branch(5)
    def _():
        for r, rows in enumerate(chunks):
            o_ref[0, rows, :] = proj(5, r).astype(BF16)

    for gate_part in (6, 7):
        @branch(gate_part)
        def _(gate_part=gate_part):
            for r, rows in enumerate(chunks):
                o_ref[0, rows, :] = (1.0 / (1.0 + jnp.exp(-proj(gate_part, r)))).astype(BF16)


def _inproj(x, mod3, norm_g, w_in, conv_w, cos, sin_hi, sin_lo, q_scale):
    last = BATCH * N_PARTS - 1

    def part(t):
        return jnp.maximum(t - 1, 0) % N_PARTS

    def sample(t):
        return jnp.maximum(t - 1, 0) // N_PARTS

    def w_map(t):
        nxt = jnp.minimum(t, last) % N_PARTS
        return (0, jnp.where(nxt == 1, 2, jnp.where(nxt == 2, 1, nxt)))

    tab = pl.BlockSpec((SEQ, LANES), lambda t: (0, 0))
    return pl.pallas_call(
        functools.partial(_inproj_kernel, q_scale=q_scale),
        out_shape=jax.ShapeDtypeStruct((N_PARTS - 2, BATCH * SEQ, D_MODEL), BF16),
        grid=(BATCH * N_PARTS + 1,),
        in_specs=[
            pl.BlockSpec((1, SEQ, D_MODEL), lambda t: (sample(t), 0, 0)),
            pl.BlockSpec((1, 6, D_MODEL), lambda t: (sample(t), 0, 0)),
            pl.BlockSpec((1, D_MODEL), lambda t: (0, 0)),
            pl.BlockSpec((D_MODEL, D_MODEL), w_map),
            pl.BlockSpec((3, D_MODEL), lambda t: (0, 0)),
            tab, tab, tab,
        ],
        out_specs=pl.BlockSpec((1, SEQ, D_MODEL), lambda t: (jnp.maximum(part(t) - 2, 0), sample(t), 0)),
        scratch_shapes=[
            pltpu.VMEM((SEQ, D_MODEL), BF16),
            pltpu.VMEM((SEQ + 16, D_MODEL), F32),
            pltpu.VMEM((2, D_MODEL, D_MODEL), BF16),
        ],
        compiler_params=_params(("arbitrary",)),
        name="inproj",
    )(x, mod3, norm_g, w_in, conv_w, cos, sin_hi, sin_lo)


def _ctxproj_kernel(c_ref, mod_ref, g_ref, w_ref, o_ref, hc_ref):
    j = pl.program_id(0)
    chunks = [slice(r * ROW_CHUNK, (r + 1) * ROW_CHUNK) for r in range(BATCH * CTX_LEN // ROW_CHUNK)]

    @pl.when(j == 0)
    def _():
        shift = mod_ref[0, 0:1, :]
        scale = mod_ref[0, 1:2, :]
        for rows in chunks:
            hc_ref[rows, :] = _norm_modulate(c_ref[rows, :], g_ref[...], shift, scale).astype(BF16)

    w = w_ref[...].astype(BF16)
    for rows in chunks:
        o_ref[0, rows, :] = _dot(hc_ref[rows, :], w).astype(BF16)


def _ctxproj(ctx2, mod3, norm_g, w_in):
    rows = BATCH * CTX_LEN
    return pl.pallas_call(
        _ctxproj_kernel,
        out_shape=jax.ShapeDtypeStruct((2, rows, D_MODEL), BF16),
        grid=(2,),
        in_specs=[
            pl.BlockSpec((rows, D_MODEL), lambda j: (0, 0)),
            pl.BlockSpec((1, 6, D_MODEL), lambda j: (BATCH, 0, 0)),
            pl.BlockSpec((1, D_MODEL), lambda j: (0, 0)),
            pl.BlockSpec((D_MODEL, D_MODEL), lambda j: (0, 4 + j)),
        ],
        out_specs=pl.BlockSpec((1, rows, D_MODEL), lambda j: (j, 0, 0)),
        scratch_shapes=[pltpu.VMEM((rows, D_MODEL), BF16)],
        compiler_params=_params(("arbitrary",)),
        name="ctxproj",
    )(ctx2, mod3, norm_g, w_in)


def _attn_kernel(lp_ref, q_ref, k_ref, v_ref, kc_ref, vc_ref, g_ref, o_ref, k_all, v_ext, s_ref, m_ref):
    lp = lp_ref[...]
    lam = (jnp.exp(jnp.sum(lp[0:1] * lp[1:2], axis=-1, keepdims=True))
           - jnp.exp(jnp.sum(lp[2:3] * lp[3:4], axis=-1, keepdims=True)) + LAM_INIT)
    k_all[0:CTX_LEN, :] = kc_ref[0]
    k_all[CTX_LEN:N_KEYS, :] = k_ref[0]
    v_ext[0:CTX_LEN, 0:LANES] = vc_ref[0]
    v_ext[CTX_LEN:N_KEYS, 0:LANES] = v_ref[0]
    v_ext[:, LANES:2 * LANES] = jnp.ones((N_KEYS, LANES), BF16)
    first_map = lax.broadcasted_iota(jnp.int32, (1, LANES), 1) < HEAD_DIM

    def scores(u):
        q = q_ref[0, u * ATTN_SQ:(u + 1) * ATTN_SQ, :]
        zero = jnp.zeros_like(q)
        for mp, qm in enumerate((jnp.where(first_map, q, zero), jnp.where(first_map, zero, q))):
            s = _dot_nt(qm, k_all[...])
            s_ref[u % 2, mp] = s
            m_ref[u % 2, mp] = jnp.max(s, axis=-1, keepdims=True)

    def values(u):
        o = []
        for mp in range(2):
            p = jnp.exp2(s_ref[u % 2, mp] - m_ref[u % 2, mp])
            o.append(_dot(p.astype(BF16), v_ext[...]))
        a = o[0][:, :LANES] / o[0][:, LANES:] - o[1][:, :LANES] * (lam / o[1][:, LANES:])
        ms = jnp.mean(a * a, axis=-1, keepdims=True)
        o_ref[u * ATTN_SQ:(u + 1) * ATTN_SQ, :] = (
            (a * lax.rsqrt(ms + EPS) * g_ref[...]) * (1.0 - LAM_INIT)).astype(BF16)

    n_units = ATTN_TQ // ATTN_SQ
    scores(0)
    for u in range(n_units):
        if u + 1 < n_units:
            scores(u + 1)
        values(u)


def _attn(lam_params, parts, ctx_kv, subln_g):
    nq = SEQ // ATTN_TQ
    return pl.pallas_call(
        _attn_kernel,
        out_shape=jax.ShapeDtypeStruct((BATCH * SEQ, D_MODEL), BF16),
        grid=(BATCH, N_HEADS, nq),
        in_specs=[
            pl.BlockSpec((4, HEAD_DIM), lambda b, h, i: (0, 0)),
            pl.BlockSpec((1, ATTN_TQ, LANES), lambda b, h, i: (1, b * nq + i, h)),
            pl.BlockSpec((1, SEQ, LANES), lambda b, h, i: (2, b, h)),
            pl.BlockSpec((1, SEQ, LANES), lambda b, h, i: (3, b, h)),
            pl.BlockSpec((1, CTX_LEN, LANES), lambda b, h, i: (0, b, h)),
            pl.BlockSpec((1, CTX_LEN, LANES), lambda b, h, i: (1, b, h)),
            pl.BlockSpec((1, V_DIM), lambda b, h, i: (0, 0)),
        ],
        out_specs=pl.BlockSpec((ATTN_TQ, LANES), lambda b, h, i: (b * nq + i, h)),
        scratch_shapes=[
            pltpu.VMEM((N_KEYS, LANES), BF16),
            pltpu.VMEM((N_KEYS, 2 * LANES), BF16),
            pltpu.VMEM((2, 2, ATTN_SQ, N_KEYS), F32),
            pltpu.VMEM((2, 2, ATTN_SQ, 1), F32),
        ],
        compiler_params=_params(("arbitrary", "arbitrary", "arbitrary")),
        name="attn",
    )(lam_params, parts, parts, parts, ctx_kv, ctx_kv, subln_g)


def _merge_kernel(yb_ref, s6_ref, s7_ref, o_ref, x_ref, mod_ref, g2_ref, wc_ref, wa_ref, wo_ref,
                  wrh_ref, wrl_ref, x1_ref, h2_ref, lg_ref):
    chunks = [slice(r * ROW_CHUNK, (r + 1) * ROW_CHUNK) for r in range(MERGE_TM // ROW_CHUNK)]
    for rows in chunks:
        y_conv = _dot(yb_ref[0, rows, :], wc_ref[...])
        y_attn = _dot(o_ref[rows, :], wa_ref[...])
        merged = s6_ref[0, rows, :].astype(F32) * y_conv + s7_ref[0, rows, :].astype(F32) * y_attn
        mix = _dot(merged.astype(BF16), wo_ref[...])
        x1_ref[rows, :] = x_ref[rows, :] + mod_ref[0, 2:3, :] * mix
    for rows in chunks:
        h2 = _norm_modulate(x1_ref[rows, :], g2_ref[...], mod_ref[0, 3:4, :], mod_ref[0, 4:5, :])
        h_hi = h2.astype(BF16)
        h2_ref[rows, :] = pltpu.pack_elementwise([h2[:, :D_MODEL // 2], h2[:, D_MODEL // 2:]], packed_dtype=BF16)
        h_lo = (h2 - h_hi.astype(F32)).astype(BF16)
        lg_ref[0, :, rows] = (_dot_nt(wrh_ref[...], h_hi)
                              + (_dot_nt(wrh_ref[...], h_lo) + _dot_nt(wrl_ref[...], h_hi)))


def _merge(parts, attn_o, x2, mod3, norm2_g, w_conv_bf, w_attn_bf, w_out_bf, wr_hi, wr_lo):
    per_b = SEQ // MERGE_TM
    sq = pl.BlockSpec((D_MODEL, D_MODEL), lambda i: (0, 0), pipeline_mode=pl.Buffered(1))
    wr = pl.BlockSpec((N_EXPERTS, D_MODEL), lambda i: (0, 0), pipeline_mode=pl.Buffered(1))
    tile = pl.BlockSpec((MERGE_TM, D_MODEL), lambda i: (i, 0))
    return pl.pallas_call(
        _merge_kernel,
        out_shape=(
            jax.ShapeDtypeStruct((BATCH * SEQ, D_MODEL), F32),
            jax.ShapeDtypeStruct((BATCH * SEQ, D_MODEL // 2), jnp.uint32),
            jax.ShapeDtypeStruct((BATCH, N_EXPERTS, SEQ), F32),
        ),
        grid=(BATCH * per_b,),
        in_specs=[
            pl.BlockSpec((1, MERGE_TM, D_MODEL), lambda i: (0, i, 0)),
            pl.BlockSpec((1, MERGE_TM, D_MODEL), lambda i: (4, i, 0)),
            pl.BlockSpec((1, MERGE_TM, D_MODEL), lambda i: (5, i, 0)),
            tile, tile,
            pl.BlockSpec((1, 6, D_MODEL), lambda i: (i // per_b, 0, 0)),
            pl.BlockSpec((1, D_MODEL), lambda i: (0, 0)),
            sq, sq, sq, wr, wr,
        ],
        out_specs=(
            tile,
            pl.BlockSpec((MERGE_TM, D_MODEL // 2), lambda i: (i, 0)),
            pl.BlockSpec((1, N_EXPERTS, MERGE_TM), lambda i: (i // per_b, 0, i % per_b)),
        ),
        compiler_params=_params(("arbitrary",)),
        name="merge",
    )(parts, parts, parts, attn_o, x2, mod3, norm2_g, w_conv_bf, w_attn_bf, w_out_bf, wr_hi, wr_lo)


def _route_kernel(lg_ref, rank_ref, gate_ref, bounds_ref):
    lg = lg_ref[...]
    ex = jnp.exp(lg - jnp.max(lg, axis=1, keepdims=True))
    aff = (ex / jnp.sum(ex, axis=1, keepdims=True)).reshape(BATCH * N_EXPERTS, SEQ)
    bits = lax.bitcast_convert_type(aff, jnp.int32)

    def count(mask):
        return jnp.sum(jnp.where(mask, 1.0, 0.0), axis=1, keepdims=True)

    def search(i, t):
        cand = t + lax.shift_left(jnp.int32(1), 30 - i)
        return jnp.where(count(bits >= cand) >= CAPACITY, cand, t)

    thr = lax.fori_loop(0, 31, search, jnp.zeros((BATCH * N_EXPERTS, 1), jnp.int32))
    need = CAPACITY - count(bits > thr)
    before = (lax.broadcasted_iota(jnp.int32, (LANES, LANES), 0)
              < lax.broadcasted_iota(jnp.int32, (LANES, LANES), 1)).astype(BF16)
    n_tied = jnp.zeros((BATCH * N_EXPERTS, 1), F32)
    n_sel = jnp.zeros((BATCH * N_EXPERTS, 1), F32)
    per_tile = COMBINE_SUB // LANES
    firsts, ends = [], []
    for c in range(SEQ // LANES):
        if c % per_tile == 0:
            firsts.append(n_sel)
        cols = slice(c * LANES, (c + 1) * LANES)
        bits_c = bits[:, cols]
        tied_c = jnp.where(bits_c == thr, 1.0, 0.0)
        tied_before = _dot(tied_c.astype(BF16), before) + n_tied
        sel_c = jnp.where(bits_c > thr, 1.0, jnp.where(tied_before < need, tied_c, 0.0))
        rank = _dot(sel_c.astype(BF16), before) + n_sel
        rank_c = jnp.where(sel_c > 0.0, rank, -1.0)
        gate_c = jnp.where(sel_c > 0.0, aff[:, cols], 0.0)
        n_tied = n_tied + jnp.sum(tied_c, axis=1, keepdims=True)
        n_sel = n_sel + jnp.sum(sel_c, axis=1, keepdims=True)
        rank_ref[:, cols] = rank_c.astype(jnp.int32)
        gate_ref[:, cols] = gate_c
        if c % per_tile == per_tile - 1:
            ends.append(n_sel)
    bounds_ref[...] = jnp.concatenate(firsts + ends, axis=1).astype(jnp.int32)


def _route(logits_t):
    return pl.pallas_call(
        _route_kernel,
        out_shape=(
            jax.ShapeDtypeStruct((BATCH * N_EXPERTS, SEQ), jnp.int32),
            jax.ShapeDtypeStruct((BATCH * N_EXPERTS, SEQ), F32),
            jax.ShapeDtypeStruct((BATCH * N_EXPERTS, 2 * (SEQ // COMBINE_SUB)), jnp.int32),
        ),
        compiler_params=pltpu.CompilerParams(vmem_limit_bytes=VMEM_LIMIT),
        name="route",
    )(logits_t)


def _dispatch(rank_e, h_packed, first_expert):
    per_worker = BATCH * GROUP_EXPERTS // (SC_CORES * SC_SUBCORES)
    half = CAPACITY // 2
    mesh = plsc.VectorSubcoreMesh(core_axis_name="c", subcore_axis_name="s")

    @functools.partial(
        pl.kernel, mesh=mesh,
        out_type=jax.ShapeDtypeStruct((GROUP_EXPERTS * BATCH * CAPACITY, D_MODEL // 2), h_packed.dtype),
        scratch_types=[
            pltpu.VMEM((SEQ,), jnp.int32),
            pltpu.VMEM((2, half), jnp.int32),
            pltpu.VMEM((half, D_MODEL // 2), h_packed.dtype),
            pltpu.SemaphoreType.DMA,
        ],
        compiler_params=pltpu.CompilerParams(needs_layout_passes=False),
        name="dispatch",
    )
    def k(table_hbm, rank_hbm, out_hbm, rank_v, idx_v, rows_v, sem):
        worker = lax.axis_index("s") * SC_CORES + lax.axis_index("c")
        expert = worker // (BATCH // per_worker)

        @pl.loop(0, per_worker)
        def _(p):
            sample = (worker % (BATCH // per_worker)) * per_worker + p
            pltpu.sync_copy(rank_hbm.at[sample * N_EXPERTS + first_expert + expert], rank_v)

            @pl.loop(0, SEQ // SC_LANES)
            def _(i):
                rank = rank_v[pl.ds(i * SC_LANES, SC_LANES)]
                row = lax.iota(jnp.int32, SC_LANES) + (i * SC_LANES + sample * SEQ)
                plsc.store_scatter(idx_v, [lax.shift_right_arithmetic(rank, 7), rank & (half - 1)], row,
                                   mask=rank >= 0)

            for c in range(2):
                pltpu.async_copy(table_hbm.at[idx_v.at[c]], rows_v, sem).wait()
                first = (expert * BATCH + sample) * CAPACITY + c * half
                pltpu.sync_copy(rows_v, out_hbm.at[pl.ds(first, half)])

    return k(h_packed, rank_e)


def _expert_kernel(xs_ref, wg_ref, wu_ref, wd_ref, y_ref, acc_ref, wg_bf, wu_bf, wd_bf):
    t = pl.program_id(0)
    n_f = EXPERT_HIDDEN // FFN_TF
    f = (t + n_f - 1) % n_f

    n_chunks = BATCH * CAPACITY // ROW_CHUNK

    def stage(slot, part):
        up = slice(part * D_MODEL // n_chunks, (part + 1) * D_MODEL // n_chunks)
        down = slice(part * FFN_TF // n_chunks, (part + 1) * FFN_TF // n_chunks)
        wg_bf[slot, up, :] = wg_ref[0, up, :].astype(BF16)
        wu_bf[slot, up, :] = wu_ref[0, up, :].astype(BF16)
        wd_bf[slot, down, :] = wd_ref[0, down, :].astype(BF16)

    def step(cur):
        for r in range(n_chunks):
            stage(1 - cur, r)
            rows = slice(r * ROW_CHUNK, (r + 1) * ROW_CHUNK)
            packed = xs_ref[0, rows, :]
            half = D_MODEL // 2
            x_lo = pltpu.unpack_elementwise(packed, index=0, packed_dtype=BF16, unpacked_dtype=F32).astype(BF16)
            x_hi = pltpu.unpack_elementwise(packed, index=1, packed_dtype=BF16, unpacked_dtype=F32).astype(BF16)
            a = _dot(x_lo, wg_bf[cur, :half, :]) + _dot(x_hi, wg_bf[cur, half:, :])
            u = _dot(x_lo, wu_bf[cur, :half, :]) + _dot(x_hi, wu_bf[cur, half:, :])
            hidden = (a / (1.0 + jnp.exp(-a))) * u
            total = jnp.where(f == 0, 0.0, acc_ref[rows, :]) + _dot(hidden.astype(BF16), wd_bf[cur])
            acc_ref[rows, :] = total
            y_ref[0, rows, :] = total.astype(BF16)

    @pl.when(t == 0)
    def _():
        for part in range(n_chunks):
            stage(0, part)

    @pl.when(t % 2 == 1)
    def _():
        step(0)

    @pl.when((t % 2 == 0) & (t > 0))
    def _():
        step(1)


def _experts(xs, w_gate, w_up, w_down, first_expert):
    rows = BATCH * CAPACITY
    n_f = EXPERT_HIDDEN // FFN_TF
    last = GROUP_EXPERTS * n_f - 1

    def staged(t):
        return jnp.minimum(t, last)

    def computed(t):
        return jnp.maximum(t - 1, 0) // n_f

    return pl.pallas_call(
        _expert_kernel,
        out_shape=jax.ShapeDtypeStruct((GROUP_EXPERTS, rows, D_MODEL), BF16),
        grid=(GROUP_EXPERTS * n_f + 1,),
        in_specs=[
            pl.BlockSpec((1, rows, D_MODEL // 2), lambda t: (computed(t), 0, 0)),
            pl.BlockSpec((1, D_MODEL, FFN_TF), lambda t: (first_expert + staged(t) // n_f, 0, staged(t) % n_f)),
            pl.BlockSpec((1, D_MODEL, FFN_TF), lambda t: (first_expert + staged(t) // n_f, 0, staged(t) % n_f)),
            pl.BlockSpec((1, FFN_TF, D_MODEL), lambda t: (first_expert + staged(t) // n_f, staged(t) % n_f, 0)),
        ],
        out_specs=pl.BlockSpec((1, rows, D_MODEL), lambda t: (computed(t), 0, 0)),
        scratch_shapes=[
            pltpu.VMEM((rows, D_MODEL), F32),
            pltpu.VMEM((2, D_MODEL, FFN_TF), BF16),
            pltpu.VMEM((2, D_MODEL, FFN_TF), BF16),
            pltpu.VMEM((2, FFN_TF, D_MODEL), BF16),
        ],
        compiler_params=_params(("arbitrary",)),
        name="experts",
    )(xs, w_gate, w_up, w_down)


def _combine_kernel(bounds_ref, rank_ref, gate_ref, *refs):
    y_refs, (x1_ref, mod_ref, fg_ref, o_ref) = refs[:N_GROUPS], refs[N_GROUPS:]
    b, t = pl.program_id(0), pl.program_id(1)
    subs = COMBINE_TQ // COMBINE_SUB
    n_sub = SEQ // COMBINE_SUB
    per_dot = CAPACITY // COMBINE_WINDOW

    def y_rows(e):
        return y_refs[e // GROUP_EXPERTS].at[e % GROUP_EXPERTS]

    def onehot_t(e, cols, first, n_slots):
        slot = lax.broadcasted_iota(jnp.int32, (n_slots, cols.stop - cols.start), 0) + first
        return jnp.where(rank_ref[e:e + 1, cols] == slot, gate_ref[e:e + 1, cols], 0.0).astype(BF16)

    def dot_t(a, b_):
        return lax.dot_general(a, b_, (((0,), (0,)), ((), ())), preferred_element_type=F32)

    def finish(rows, acc):
        x2 = x1_ref[rows, :] + mod_ref[0, 5:6, :] * acc
        ms = jnp.mean(x2 * x2, axis=-1, keepdims=True)
        o_ref[rows, :] = x2 * lax.rsqrt(ms + EPS) * fg_ref[...]

    starts, fits = {}, None
    for s in range(subs):
        for e in range(N_EXPERTS):
            pair = (b * N_EXPERTS + e) * (2 * n_sub) + t * subs + s
            first, end = bounds_ref[pair], bounds_ref[pair + n_sub]
            start = jnp.minimum(lax.shift_left(lax.shift_right_logical(first, 4), 4), CAPACITY - COMBINE_WINDOW)
            starts[s, e] = start
            ok = end <= start + COMBINE_WINDOW
            fits = ok if fits is None else fits & ok

    @pl.when(fits)
    def _():
        for s in range(subs):
            rows = slice(s * COMBINE_SUB, (s + 1) * COMBINE_SUB)
            acc = jnp.zeros((COMBINE_SUB, D_MODEL), F32)
            for e0 in range(0, N_EXPERTS, per_dot):
                first = [pl.multiple_of(starts[s, e0 + i], 16) for i in range(per_dot)]
                onehots = [onehot_t(e0 + i, rows, first[i], COMBINE_WINDOW) for i in range(per_dot)]
                y_win = [y_rows(e0 + i)[pl.ds(first[i], COMBINE_WINDOW), :] for i in range(per_dot)]
                acc = acc + dot_t(jnp.concatenate(onehots, axis=0), jnp.concatenate(y_win, axis=0))
            finish(rows, acc)

    @pl.when(jnp.logical_not(fits))
    def _():
        acc = jnp.zeros((COMBINE_TQ, D_MODEL), F32)
        for e in range(N_EXPERTS):
            acc = acc + dot_t(onehot_t(e, slice(0, COMBINE_TQ), 0, CAPACITY), y_rows(e)[...])
        finish(slice(0, COMBINE_TQ), acc)


def _combine(bounds, rank_e, gate_e, ys, x1, mod3, final_g):
    per_b = SEQ // COMBINE_TQ
    return pl.pallas_call(
        _combine_kernel,
        out_shape=jax.ShapeDtypeStruct((BATCH * SEQ, D_MODEL), F32),
        grid_spec=pltpu.PrefetchScalarGridSpec(
            num_scalar_prefetch=1,
            grid=(BATCH, per_b),
            in_specs=[
                pl.BlockSpec((N_EXPERTS, COMBINE_TQ), lambda b, t, _: (b, t)),
                pl.BlockSpec((N_EXPERTS, COMBINE_TQ), lambda b, t, _: (b, t)),
                *[pl.BlockSpec((GROUP_EXPERTS, CAPACITY, D_MODEL), lambda b, t, _: (0, b, 0)) for _ in ys],
                pl.BlockSpec((COMBINE_TQ, D_MODEL), lambda b, t, _: (b * per_b + t, 0)),
                pl.BlockSpec((1, 6, D_MODEL), lambda b, t, _: (b, 0, 0)),
                pl.BlockSpec((1, D_MODEL), lambda b, t, _: (0, 0)),
            ],
            out_specs=pl.BlockSpec((COMBINE_TQ, D_MODEL), lambda b, t, _: (b * per_b + t, 0)),
        ),
        compiler_params=_params(("arbitrary", "arbitrary")),
        name="combine",
    )(bounds.reshape(-1), rank_e, gate_e, *ys, x1, mod3, final_g)


def _rope_tables():
    t = jnp.arange(SEQ)
    pos = jnp.stack([t // GRID_W, t % GRID_W], axis=1).astype(F32)
    per_axis = HEAD_DIM // 2
    inv = ROPE_THETA ** (-jnp.arange(0, per_axis, 2, dtype=F32) / per_axis)
    ang = pos[:, :, None] * inv
    cos, sin = jnp.cos(ang), jnp.sin(ang)
    zero = jnp.zeros_like(sin)

    def lanes(first_half, second_half):
        one_map = jnp.stack([first_half, second_half], axis=2).reshape(SEQ, HEAD_DIM)
        return jnp.concatenate([one_map, one_map], axis=1)

    return lanes(cos, cos), lanes(-sin, zero), lanes(zero, sin)


def kernel(x, c, ctx, c_ctx, norm1_g, norm2_g, w_ada, b_ada, w_in, conv_w, w_out_conv, lambda_q1,
           lambda_k1, lambda_q2, lambda_k2, subln_g, w_o_attn, w_out, w_router, w_gate_e, w_up_e,
           w_down_e, final_g):
    cond = jnp.concatenate([c, c_ctx[None, :], jnp.zeros((COND_ROWS - BATCH - 1, D_MODEL), F32)], axis=0)
    mod3 = _ada(cond, w_ada[0], b_ada).reshape(COND_ROWS, 6, D_MODEL)

    cos, sin_hi, sin_lo = _rope_tables()
    q_scale = math.log2(math.e) * HEAD_DIM ** -0.5
    parts = _inproj(x, mod3, norm1_g, w_in[0], conv_w[0], cos, sin_hi, sin_lo, q_scale)
    ctx_kv = _ctxproj(ctx.reshape(BATCH * CTX_LEN, D_MODEL), mod3, norm1_g, w_in[0])

    lam_params = jnp.concatenate([lambda_q1, lambda_k1, lambda_q2, lambda_k2], axis=0)
    attn_o = _attn(lam_params, parts, ctx_kv, subln_g)

    wr_t = w_router[0].T
    wr_hi = wr_t.astype(BF16)
    wr_lo = (wr_t - wr_hi.astype(F32)).astype(BF16)
    x1, h2, logits_t = _merge(parts, attn_o, x.reshape(BATCH * SEQ, D_MODEL), mod3, norm2_g,
                              w_out_conv[0].astype(BF16), w_o_attn[0].astype(BF16), w_out[0].astype(BF16),
                              wr_hi, wr_lo)

    rank_e, gate_e, bounds = _route(logits_t)
    ys = []
    for first in range(0, N_EXPERTS, GROUP_EXPERTS):
        xs = _dispatch(rank_e, h2, first).reshape(GROUP_EXPERTS, BATCH * CAPACITY, D_MODEL // 2)
        ys.append(_experts(xs, w_gate_e[0], w_up_e[0], w_down_e[0], first))
    out = _combine(bounds, rank_e, gate_e, ys, x1, mod3, final_g[None, :])
    return out.reshape(BATCH, SEQ, D_MODEL)
```

```python
import functools
import math

import jax
import jax.numpy as jnp
import numpy as np
from jax import lax
from jax.experimental import pallas as pl
from jax.experimental.pallas import tpu as pltpu
from jax.experimental.pallas import tpu_sc as plsc

D_MODEL = 1024
BATCH = 8
SEQ = 2048
GRID_W = 64
CTX_LEN = 256
N_HEADS = 8
HEAD_DIM = 64
V_DIM = 2 * HEAD_DIM
N_EXPERTS = 16
EXPERT_HIDDEN = 2048
CAPACITY = 2 * SEQ // N_EXPERTS
ROPE_THETA = 10000.0
EPS = 1e-6
LAM_INIT = 0.8 - 0.6 * math.exp(-0.3 * 0)
N_PARTS = 8
N_KEYS = CTX_LEN + SEQ
N_GROUPS = 2
GROUP_EXPERTS = N_EXPERTS // N_GROUPS

LANES = 128
SC_CORES = 2
SC_SUBCORES = 16
SC_LANES = 16
F32 = jnp.float32
BF16 = jnp.bfloat16

ROW_CHUNK = 512
COND_ROWS = 16
ADA_TN = 512
ATTN_TQ = 2048
ATTN_SQ = 256
MERGE_TM = 1024
FFN_TF = 512
COMBINE_TQ = 1024
COMBINE_SUB = 256
COMBINE_WINDOW = 64
VMEM_LIMIT = 56 * 1024 * 1024


def _dot(a, b):
    return jnp.dot(a, b, preferred_element_type=F32)


def _dot_nt(a, b):
    return lax.dot_general(a, b, (((1,), (1,)), ((), ())), preferred_element_type=F32)


def _params(sem, vmem=VMEM_LIMIT):
    return pltpu.CompilerParams(dimension_semantics=sem, vmem_limit_bytes=vmem)


def _ada_kernel(cond_ref, w_ref, b_ref, o_ref):
    c = cond_ref[...]
    s = c / (1.0 + jnp.exp(-c))
    o_ref[...] = _dot(s.astype(BF16), w_ref[...].astype(BF16)) + b_ref[...]


def _ada(cond, w_ada, b_ada):
    n = w_ada.shape[1]
    return pl.pallas_call(
        _ada_kernel,
        out_shape=jax.ShapeDtypeStruct((COND_ROWS, n), F32),
        grid=(n // ADA_TN,),
        in_specs=[
            pl.BlockSpec((COND_ROWS, D_MODEL), lambda i: (0, 0)),
            pl.BlockSpec((D_MODEL, ADA_TN), lambda i: (0, i)),
            pl.BlockSpec((1, ADA_TN), lambda i: (0, i)),
        ],
        out_specs=pl.BlockSpec((COND_ROWS, ADA_TN), lambda i: (0, i)),
        compiler_params=_params(("arbitrary",)),
        name="ada",
    )(cond, w_ada, b_ada)


def _norm_modulate(x, g, shift, scale):
    ms = jnp.mean(x * x, axis=-1, keepdims=True)
    return (x * lax.rsqrt(ms + EPS) * g) * (1.0 + scale) + shift


def _rope(a, cos, sin_hi, sin_lo):
    return a * cos + pltpu.roll(a, LANES - 16, 1) * sin_hi + pltpu.roll(a, 16, 1) * sin_lo


def _inproj_kernel(x_ref, mod_ref, g_ref, w_ref, cw_ref, cos_ref, shi_ref, slo_ref, o_ref,
                   hx_ref, z_ref, w_bf, *, q_scale):
    t = pl.program_id(0)
    j = (t + N_PARTS - 1) % N_PARTS
    chunks = [slice(r * ROW_CHUNK, (r + 1) * ROW_CHUNK) for r in range(SEQ // ROW_CHUNK)]

    def stage(slot, part):
        rows = slice(part * D_MODEL // len(chunks), (part + 1) * D_MODEL // len(chunks))
        w_bf[slot, rows, :] = w_ref[rows, :].astype(BF16)

    def proj(part, r):
        stage(1 - part % 2, r)
        return _dot(hx_ref[chunks[r], :], w_bf[part % 2])

    def branch(part):
        return pl.when((t > 0) & (j == part))

    @pl.when(t == 0)
    def _():
        for r in range(len(chunks)):
            stage(0, r)

    @branch(0)
    def _():
        shift = mod_ref[0, 0:1, :]
        scale = mod_ref[0, 1:2, :]
        for rows in chunks:
            hx_ref[rows, :] = _norm_modulate(x_ref[0, rows, :], g_ref[...], shift, scale).astype(BF16)
        z_ref[0:8, :] = jnp.zeros((8, D_MODEL), F32)
        z_ref[SEQ + 8:SEQ + 16, :] = jnp.zeros((8, D_MODEL), F32)
        for r, rows in enumerate(chunks):
            z_ref[pl.ds(rows.start + 8, ROW_CHUNK), :] = proj(0, r)

    @branch(1)
    def _():
        for r, rows in enumerate(chunks):
            zr = pl.ds(rows.start + 8, ROW_CHUNK)
            z_ref[zr, :] = z_ref[zr, :] * proj(1, r)

    @branch(2)
    def _():
        first_row = lax.broadcasted_iota(jnp.int32, (8, 1), 0) == 0
        last_row = lax.broadcasted_iota(jnp.int32, (8, 1), 0) == 7
        for r, rows in enumerate(chunks):
            base = rows.start + 8
            zc = z_ref[pl.ds(base, ROW_CHUNK), :]
            before = pltpu.roll(zc * cw_ref[0:1, :], 1, 0)
            after = pltpu.roll(zc * cw_ref[2:3, :], ROW_CHUNK - 1, 0)
            edge_b = jnp.where(first_row, z_ref[pl.ds(base - 8, 8), :][7:8, :] * cw_ref[0:1, :], before[0:8, :])
            edge_a = jnp.where(last_row, z_ref[pl.ds(base + ROW_CHUNK, 8), :][0:1, :] * cw_ref[2:3, :],
                               after[ROW_CHUNK - 8:, :])
            before = jnp.concatenate([edge_b, before[8:, :]], axis=0)
            after = jnp.concatenate([after[:ROW_CHUNK - 8, :], edge_a], axis=0)
            y = before + zc * cw_ref[1:2, :] + after
            o_ref[0, rows, :] = (proj(2, r) * y).astype(BF16)

    def rope_part(part, scale):
        for r, rows in enumerate(chunks):
            acc = proj(part, r)
            cos, shi, slo = cos_ref[rows, :], shi_ref[rows, :], slo_ref[rows, :]
            for h in range(D_MODEL // LANES):
                cols = slice(h * LANES, (h + 1) * LANES)
                roped = _rope(acc[:, cols], cos, shi, slo)
                if scale is not None:
                    roped = roped * scale
                o_ref[0, rows, cols] = roped.astype(BF16)

    @branch(3)
    def _():
        rope_part(3, q_scale)

    @branch(4)
    def _():
        rope_part(4, None)

    @branch(5)
    def _():
        for r, rows in enumerate(chunks):
            o_ref[0, rows, :] = proj(5, r).astype(BF16)

    for gate_part in (6, 7):
        @branch(gate_part)
        def _(gate_part=gate_part):
            for r, rows in enumerate(chunks):
                o_ref[0, rows, :] = (1.0 / (1.0 + jnp.exp(-proj(gate_part, r)))).astype(BF16)


def _inproj(x, mod3, norm_g, w_in, conv_w, cos, sin_hi, sin_lo, q_scale):
    last = BATCH * N_PARTS - 1

    def part(t):
        return jnp.maximum(t - 1, 0) % N_PARTS

    def sample(t):
        return jnp.maximum(t - 1, 0) // N_PARTS

    def w_map(t):
        nxt = jnp.minimum(t, last) % N_PARTS
        return (0, jnp.where(nxt == 1, 2, jnp.where(nxt == 2, 1, nxt)))

    tab = pl.BlockSpec((SEQ, LANES), lambda t: (0, 0))
    return pl.pallas_call(
        functools.partial(_inproj_kernel, q_scale=q_scale),
        out_shape=jax.ShapeDtypeStruct((N_PARTS - 2, BATCH * SEQ, D_MODEL), BF16),
        grid=(BATCH * N_PARTS + 1,),
        in_specs=[
            pl.BlockSpec((1, SEQ, D_MODEL), lambda t: (sample(t), 0, 0)),
            pl.BlockSpec((1, 6, D_MODEL), lambda t: (sample(t), 0, 0)),
            pl.BlockSpec((1, D_MODEL), lambda t: (0, 0)),
            pl.BlockSpec((D_MODEL, D_MODEL), w_map),
            pl.BlockSpec((3, D_MODEL), lambda t: (0, 0)),
            tab, tab, tab,
        ],
        out_specs=pl.BlockSpec((1, SEQ, D_MODEL), lambda t: (jnp.maximum(part(t) - 2, 0), sample(t), 0)),
        scratch_shapes=[
            pltpu.VMEM((SEQ, D_MODEL), BF16),
            pltpu.VMEM((SEQ + 16, D_MODEL), F32),
            pltpu.VMEM((2, D_MODEL, D_MODEL), BF16),
        ],
        compiler_params=_params(("arbitrary",)),
        name="inproj",
    )(x, mod3, norm_g, w_in, conv_w, cos, sin_hi, sin_lo)


def _ctxproj_kernel(c_ref, mod_ref, g_ref, w_ref, o_ref, hc_ref):
    j = pl.program_id(0)
    chunks = [slice(r * ROW_CHUNK, (r + 1) * ROW_CHUNK) for r in range(BATCH * CTX_LEN // ROW_CHUNK)]

    @pl.when(j == 0)
    def _():
        shift = mod_ref[0, 0:1, :]
        scale = mod_ref[0, 1:2, :]
        for rows in chunks:
            hc_ref[rows, :] = _norm_modulate(c_ref[rows, :], g_ref[...], shift, scale).astype(BF16)

    w = w_ref[...].astype(BF16)
    for rows in chunks:
        o_ref[0, rows, :] = _dot(hc_ref[rows, :], w).astype(BF16)


def _ctxproj(ctx2, mod3, norm_g, w_in):
    rows = BATCH * CTX_LEN
    return pl.pallas_call(
        _ctxproj_kernel,
        out_shape=jax.ShapeDtypeStruct((2, rows, D_MODEL), BF16),
        grid=(2,),
        in_specs=[
            pl.BlockSpec((rows, D_MODEL), lambda j: (0, 0)),
            pl.BlockSpec((1, 6, D_MODEL), lambda j: (BATCH, 0, 0)),
            pl.BlockSpec((1, D_MODEL), lambda j: (0, 0)),
            pl.BlockSpec((D_MODEL, D_MODEL), lambda j: (0, 4 + j)),
        ],
        out_specs=pl.BlockSpec((1, rows, D_MODEL), lambda j: (j, 0, 0)),
        scratch_shapes=[pltpu.VMEM((rows, D_MODEL), BF16)],
        compiler_params=_params(("arbitrary",)),
        name="ctxproj",
    )(ctx2, mod3, norm_g, w_in)


def _attn_kernel(lp_ref, q_ref, k_ref, v_ref, kc_ref, vc_ref, g_ref, o_ref, k_all, v_ext, s_ref, m_ref):
    lp = lp_ref[...]
    lam = (jnp.exp(jnp.sum(lp[0:1] * lp[1:2], axis=-1, keepdims=True))
           - jnp.exp(jnp.sum(lp[2:3] * lp[3:4], axis=-1, keepdims=True)) + LAM_INIT)
    k_all[0:CTX_LEN, :] = kc_ref[0]
    k_all[CTX_LEN:N_KEYS, :] = k_ref[0]
    v_ext[0:CTX_LEN, 0:LANES] = vc_ref[0]
    v_ext[CTX_LEN:N_KEYS, 0:LANES] = v_ref[0]
    v_ext[:, LANES:2 * LANES] = jnp.ones((N_KEYS, LANES), BF16)
    first_map = lax.broadcasted_iota(jnp.int32, (1, LANES), 1) < HEAD_DIM

    def scores(u):
        q = q_ref[0, u * ATTN_SQ:(u + 1) * ATTN_SQ, :]
        zero = jnp.zeros_like(q)
        for mp, qm in enumerate((jnp.where(first_map, q, zero), jnp.where(first_map, zero, q))):
            s = _dot_nt(qm, k_all[...])
            s_ref[u % 2, mp] = s
            m_ref[u % 2, mp] = jnp.max(s, axis=-1, keepdims=True)

    def values(u):
        o = []
        for mp in range(2):
            p = jnp.exp2(s_ref[u % 2, mp] - m_ref[u % 2, mp])
            o.append(_dot(p.astype(BF16), v_ext[...]))
        a = o[0][:, :LANES] / o[0][:, LANES:] - o[1][:, :LANES] * (lam / o[1][:, LANES:])
        ms = jnp.mean(a * a, axis=-1, keepdims=True)
        o_ref[u * ATTN_SQ:(u + 1) * ATTN_SQ, :] = (
            (a * lax.rsqrt(ms + EPS) * g_ref[...]) * (1.0 - LAM_INIT)).astype(BF16)

    n_units = ATTN_TQ // ATTN_SQ
    scores(0)
    for u in range(n_units):
        if u + 1 < n_units:
            scores(u + 1)
        values(u)


def _attn(lam_params, parts, ctx_kv, subln_g):
    nq = SEQ // ATTN_TQ
    return pl.pallas_call(
        _attn_kernel,
        out_shape=jax.ShapeDtypeStruct((BATCH * SEQ, D_MODEL), BF16),
        grid=(BATCH, N_HEADS, nq),
        in_specs=[
            pl.BlockSpec((4, HEAD_DIM), lambda b, h, i: (0, 0)),
            pl.BlockSpec((1, ATTN_TQ, LANES), lambda b, h, i: (1, b * nq + i, h)),
            pl.BlockSpec((1, SEQ, LANES), lambda b, h, i: (2, b, h)),
            pl.BlockSpec((1, SEQ, LANES), lambda b, h, i: (3, b, h)),
            pl.BlockSpec((1, CTX_LEN, LANES), lambda b, h, i: (0, b, h)),
            pl.BlockSpec((1, CTX_LEN, LANES), lambda b, h, i: (1, b, h)),
            pl.BlockSpec((1, V_DIM), lambda b, h, i: (0, 0)),
        ],
        out_specs=pl.BlockSpec((ATTN_TQ, LANES), lambda b, h, i: (b * nq + i, h)),
        scratch_shapes=[
            pltpu.VMEM((N_KEYS, LANES), BF16),
            pltpu.VMEM((N_KEYS, 2 * LANES), BF16),
            pltpu.VMEM((2, 2, ATTN_SQ, N_KEYS), F32),
            pltpu.VMEM((2, 2, ATTN_SQ, 1), F32),
        ],
        compiler_params=_params(("arbitrary", "arbitrary", "arbitrary")),
        name="attn",
    )(lam_params, parts, parts, parts, ctx_kv, ctx_kv, subln_g)


def _merge_kernel(yb_ref, s6_ref, s7_ref, o_ref, x_ref, mod_ref, g2_ref, wc_ref, wa_ref, wo_ref,
                  wr_ref, x1_ref, h2_ref, lg_ref):
    chunks = [slice(r * ROW_CHUNK, (r + 1) * ROW_CHUNK) for r in range(MERGE_TM // ROW_CHUNK)]
    for rows in chunks:
        y_conv = _dot(yb_ref[0, rows, :], wc_ref[...])
        y_attn = _dot(o_ref[rows, :], wa_ref[...])
        merged = s6_ref[0, rows, :].astype(F32) * y_conv + s7_ref[0, rows, :].astype(F32) * y_attn
        mix = _dot(merged.astype(BF16), wo_ref[...])
        x1_ref[rows, :] = x_ref[rows, :] + mod_ref[0, 2:3, :] * mix
    for rows in chunks:
        h2 = _norm_modulate(x1_ref[rows, :], g2_ref[...], mod_ref[0, 3:4, :], mod_ref[0, 4:5, :])
        h_hi = h2.astype(BF16)
        h2_ref[rows, :] = pltpu.pack_elementwise([h2[:, :D_MODEL // 2], h2[:, D_MODEL // 2:]], packed_dtype=BF16)
        h_lo = (h2 - h_hi.astype(F32)).astype(BF16)
        both = _dot_nt(wr_ref[...], h_hi)
        lg_ref[0, :, rows] = both[:N_EXPERTS] + (_dot_nt(wr_ref[0:N_EXPERTS, :], h_lo) + both[N_EXPERTS:])


def _merge(parts, attn_o, x2, mod3, norm2_g, w_conv_bf, w_attn_bf, w_out_bf, wr_hi_lo):
    per_b = SEQ // MERGE_TM
    sq = pl.BlockSpec((D_MODEL, D_MODEL), lambda i: (0, 0), pipeline_mode=pl.Buffered(1))
    wr = pl.BlockSpec((2 * N_EXPERTS, D_MODEL), lambda i: (0, 0), pipeline_mode=pl.Buffered(1))
    tile = pl.BlockSpec((MERGE_TM, D_MODEL), lambda i: (i, 0))
    return pl.pallas_call(
        _merge_kernel,
        out_shape=(
            jax.ShapeDtypeStruct((BATCH * SEQ, D_MODEL), F32),
            jax.ShapeDtypeStruct((BATCH * SEQ, D_MODEL // 2), jnp.uint32),
            jax.ShapeDtypeStruct((BATCH, N_EXPERTS, SEQ), F32),
        ),
        grid=(BATCH * per_b,),
        in_specs=[
            pl.BlockSpec((1, MERGE_TM, D_MODEL), lambda i: (0, i, 0)),
            pl.BlockSpec((1, MERGE_TM, D_MODEL), lambda i: (4, i, 0)),
            pl.BlockSpec((1, MERGE_TM, D_MODEL), lambda i: (5, i, 0)),
            tile, tile,
            pl.BlockSpec((1, 6, D_MODEL), lambda i: (i // per_b, 0, 0)),
            pl.BlockSpec((1, D_MODEL), lambda i: (0, 0)),
            sq, sq, sq, wr,
        ],
        out_specs=(
            tile,
            pl.BlockSpec((MERGE_TM, D_MODEL // 2), lambda i: (i, 0)),
            pl.BlockSpec((1, N_EXPERTS, MERGE_TM), lambda i: (i // per_b, 0, i % per_b)),
        ),
        compiler_params=_params(("arbitrary",)),
        name="merge",
    )(parts, parts, parts, attn_o, x2, mod3, norm2_g, w_conv_bf, w_attn_bf, w_out_bf, wr_hi_lo)


def _route_kernel(lg_ref, rank_ref, gate_ref, bounds_ref):
    lg = lg_ref[...]
    ex = jnp.exp(lg - jnp.max(lg, axis=1, keepdims=True))
    aff = (ex / jnp.sum(ex, axis=1, keepdims=True)).reshape(BATCH * N_EXPERTS, SEQ)
    bits = lax.bitcast_convert_type(aff, jnp.int32)

    def count(mask):
        return jnp.sum(jnp.where(mask, 1.0, 0.0), axis=1, keepdims=True)

    def search(i, t):
        cand = t + lax.shift_left(jnp.int32(1), 30 - i)
        return jnp.where(count(bits >= cand) >= CAPACITY, cand, t)

    thr = lax.fori_loop(0, 31, search, jnp.zeros((BATCH * N_EXPERTS, 1), jnp.int32))
    need = CAPACITY - count(bits > thr)
    before = (lax.broadcasted_iota(jnp.int32, (LANES, LANES), 0)
              < lax.broadcasted_iota(jnp.int32, (LANES, LANES), 1)).astype(BF16)
    n_tied = jnp.zeros((BATCH * N_EXPERTS, 1), F32)
    n_sel = jnp.zeros((BATCH * N_EXPERTS, 1), F32)
    per_tile = COMBINE_SUB // LANES
    firsts, ends = [], []
    for c in range(SEQ // LANES):
        if c % per_tile == 0:
            firsts.append(n_sel)
        cols = slice(c * LANES, (c + 1) * LANES)
        bits_c = bits[:, cols]
        tied_c = jnp.where(bits_c == thr, 1.0, 0.0)
        tied_before = _dot(tied_c.astype(BF16), before) + n_tied
        sel_c = jnp.where(bits_c > thr, 1.0, jnp.where(tied_before < need, tied_c, 0.0))
        rank = _dot(sel_c.astype(BF16), before) + n_sel
        rank_c = jnp.where(sel_c > 0.0, rank, -1.0)
        gate_c = jnp.where(sel_c > 0.0, aff[:, cols], 0.0)
        n_tied = n_tied + jnp.sum(tied_c, axis=1, keepdims=True)
        n_sel = n_sel + jnp.sum(sel_c, axis=1, keepdims=True)
        rank_ref[:, cols] = rank_c.astype(jnp.int32)
        gate_ref[:, cols] = gate_c
        if c % per_tile == per_tile - 1:
            ends.append(n_sel)
    bounds_ref[...] = jnp.concatenate(firsts + ends, axis=1).astype(jnp.int32)


def _route(logits_t):
    return pl.pallas_call(
        _route_kernel,
        out_shape=(
            jax.ShapeDtypeStruct((BATCH * N_EXPERTS, SEQ), jnp.int32),
            jax.ShapeDtypeStruct((BATCH * N_EXPERTS, SEQ), F32),
            jax.ShapeDtypeStruct((BATCH * N_EXPERTS, 2 * (SEQ // COMBINE_SUB)), jnp.int32),
        ),
        compiler_params=pltpu.CompilerParams(vmem_limit_bytes=VMEM_LIMIT),
        name="route",
    )(logits_t)


def _dispatch(rank_e, h_packed, first_expert):
    per_worker = BATCH * GROUP_EXPERTS // (SC_CORES * SC_SUBCORES)
    half = CAPACITY // 2
    mesh = plsc.VectorSubcoreMesh(core_axis_name="c", subcore_axis_name="s")

    @functools.partial(
        pl.kernel, mesh=mesh,
        out_type=jax.ShapeDtypeStruct((GROUP_EXPERTS * BATCH * CAPACITY, D_MODEL // 2), h_packed.dtype),
        scratch_types=[
            pltpu.VMEM((SEQ,), jnp.int32),
            pltpu.VMEM((2, half), jnp.int32),
            pltpu.VMEM((half, D_MODEL // 2), h_packed.dtype),
            pltpu.SemaphoreType.DMA,
        ],
        compiler_params=pltpu.CompilerParams(needs_layout_passes=False),
        name="dispatch",
    )
    def k(table_hbm, rank_hbm, out_hbm, rank_v, idx_v, rows_v, sem):
        worker = lax.axis_index("s") * SC_CORES + lax.axis_index("c")
        expert = worker // (BATCH // per_worker)

        @pl.loop(0, per_worker)
        def _(p):
            sample = (worker % (BATCH // per_worker)) * per_worker + p
            pltpu.sync_copy(rank_hbm.at[sample * N_EXPERTS + first_expert + expert], rank_v)

            @pl.loop(0, SEQ // SC_LANES)
            def _(i):
                rank = rank_v[pl.ds(i * SC_LANES, SC_LANES)]
                row = lax.iota(jnp.int32, SC_LANES) + (i * SC_LANES + sample * SEQ)
                plsc.store_scatter(idx_v, [lax.shift_right_arithmetic(rank, 7), rank & (half - 1)], row,
                                   mask=rank >= 0)

            for c in range(2):
                pltpu.async_copy(table_hbm.at[idx_v.at[c]], rows_v, sem).wait()
                first = (expert * BATCH + sample) * CAPACITY + c * half
                pltpu.sync_copy(rows_v, out_hbm.at[pl.ds(first, half)])

    return k(h_packed, rank_e)


def _expert_kernel(xs_ref, wg_ref, wu_ref, wd_ref, y_ref, acc_ref, wg_bf, wu_bf, wd_bf):
    t = pl.program_id(0)
    n_f = EXPERT_HIDDEN // FFN_TF
    f = (t + n_f - 1) % n_f

    n_chunks = BATCH * CAPACITY // ROW_CHUNK

    def stage(slot, part):
        up = slice(part * D_MODEL // n_chunks, (part + 1) * D_MODEL // n_chunks)
        down = slice(part * FFN_TF // n_chunks, (part + 1) * FFN_TF // n_chunks)
        wg_bf[slot, up, :] = wg_ref[0, up, :].astype(BF16)
        wu_bf[slot, up, :] = wu_ref[0, up, :].astype(BF16)
        wd_bf[slot, down, :] = wd_ref[0, down, :].astype(BF16)

    def step(cur):
        for r in range(n_chunks):
            stage(1 - cur, r)
            rows = slice(r * ROW_CHUNK, (r + 1) * ROW_CHUNK)
            packed = xs_ref[0, rows, :]
            half = D_MODEL // 2
            x_lo = pltpu.unpack_elementwise(packed, index=0, packed_dtype=BF16, unpacked_dtype=F32).astype(BF16)
            x_hi = pltpu.unpack_elementwise(packed, index=1, packed_dtype=BF16, unpacked_dtype=F32).astype(BF16)
            a = _dot(x_lo, wg_bf[cur, :half, :]) + _dot(x_hi, wg_bf[cur, half:, :])
            u = _dot(x_lo, wu_bf[cur, :half, :]) + _dot(x_hi, wu_bf[cur, half:, :])
            hidden = (a / (1.0 + jnp.exp(-a))) * u
            total = jnp.where(f == 0, 0.0, acc_ref[rows, :]) + _dot(hidden.astype(BF16), wd_bf[cur])
            acc_ref[rows, :] = total
            y_ref[0, rows, :] = total.astype(BF16)

    @pl.when(t == 0)
    def _():
        for part in range(n_chunks):
            stage(0, part)

    @pl.when(t % 2 == 1)
    def _():
        step(0)

    @pl.when((t % 2 == 0) & (t > 0))
    def _():
        step(1)


def _experts(xs, w_gate, w_up, w_down, first_expert):
    rows = BATCH * CAPACITY
    n_f = EXPERT_HIDDEN // FFN_TF
    last = GROUP_EXPERTS * n_f - 1

    def staged(t):
        return jnp.minimum(t, last)

    def computed(t):
        return jnp.maximum(t - 1, 0) // n_f

    return pl.pallas_call(
        _expert_kernel,
        out_shape=jax.ShapeDtypeStruct((GROUP_EXPERTS, rows, D_MODEL), BF16),
        grid=(GROUP_EXPERTS * n_f + 1,),
        in_specs=[
            pl.BlockSpec((1, rows, D_MODEL // 2), lambda t: (computed(t), 0, 0)),
            pl.BlockSpec((1, D_MODEL, FFN_TF), lambda t: (first_expert + staged(t) // n_f, 0, staged(t) % n_f)),
            pl.BlockSpec((1, D_MODEL, FFN_TF), lambda t: (first_expert + staged(t) // n_f, 0, staged(t) % n_f)),
            pl.BlockSpec((1, FFN_TF, D_MODEL), lambda t: (first_expert + staged(t) // n_f, staged(t) % n_f, 0)),
        ],
        out_specs=pl.BlockSpec((1, rows, D_MODEL), lambda t: (computed(t), 0, 0)),
        scratch_shapes=[
            pltpu.VMEM((rows, D_MODEL), F32),
            pltpu.VMEM((2, D_MODEL, FFN_TF), BF16),
            pltpu.VMEM((2, D_MODEL, FFN_TF), BF16),
            pltpu.VMEM((2, FFN_TF, D_MODEL), BF16),
        ],
        compiler_params=_params(("arbitrary",)),
        name="experts",
    )(xs, w_gate, w_up, w_down)


def _combine_kernel(bounds_ref, rank_ref, gate_ref, *refs):
    y_refs, (x1_ref, mod_ref, fg_ref, o_ref) = refs[:N_GROUPS], refs[N_GROUPS:]
    b, t = pl.program_id(0), pl.program_id(1)
    subs = COMBINE_TQ // COMBINE_SUB
    n_sub = SEQ // COMBINE_SUB
    per_dot = CAPACITY // COMBINE_WINDOW

    def y_rows(e):
        return y_refs[e // GROUP_EXPERTS].at[e % GROUP_EXPERTS]

    def onehot_t(e, cols, first, n_slots):
        slot = lax.broadcasted_iota(jnp.int32, (n_slots, cols.stop - cols.start), 0) + first
        return jnp.where(rank_ref[e:e + 1, cols] == slot, gate_ref[e:e + 1, cols], 0.0).astype(BF16)

    def dot_t(a, b_):
        return lax.dot_general(a, b_, (((0,), (0,)), ((), ())), preferred_element_type=F32)

    def finish(rows, acc):
        x2 = x1_ref[rows, :] + mod_ref[0, 5:6, :] * acc
        ms = jnp.mean(x2 * x2, axis=-1, keepdims=True)
        o_ref[rows, :] = x2 * lax.rsqrt(ms + EPS) * fg_ref[...]

    starts, fits = {}, None
    for s in range(subs):
        for e in range(N_EXPERTS):
            pair = (b * N_EXPERTS + e) * (2 * n_sub) + t * subs + s
            first, end = bounds_ref[pair], bounds_ref[pair + n_sub]
            start = jnp.minimum(lax.shift_left(lax.shift_right_logical(first, 4), 4), CAPACITY - COMBINE_WINDOW)
            starts[s, e] = start
            ok = end <= start + COMBINE_WINDOW
            fits = ok if fits is None else fits & ok

    @pl.when(fits)
    def _():
        for s in range(subs):
            rows = slice(s * COMBINE_SUB, (s + 1) * COMBINE_SUB)
            acc = jnp.zeros((COMBINE_SUB, D_MODEL), F32)
            for e0 in range(0, N_EXPERTS, per_dot):
                first = [pl.multiple_of(starts[s, e0 + i], 16) for i in range(per_dot)]
                onehots = [onehot_t(e0 + i, rows, first[i], COMBINE_WINDOW) for i in range(per_dot)]
                y_win = [y_rows(e0 + i)[pl.ds(first[i], COMBINE_WINDOW), :] for i in range(per_dot)]
                acc = acc + dot_t(jnp.concatenate(onehots, axis=0), jnp.concatenate(y_win, axis=0))
            finish(rows, acc)

    @pl.when(jnp.logical_not(fits))
    def _():
        acc = jnp.zeros((COMBINE_TQ, D_MODEL), F32)
        for e in range(N_EXPERTS):
            acc = acc + dot_t(onehot_t(e, slice(0, COMBINE_TQ), 0, CAPACITY), y_rows(e)[...])
        finish(slice(0, COMBINE_TQ), acc)


def _combine(bounds, rank_e, gate_e, ys, x1, mod3, final_g):
    per_b = SEQ // COMBINE_TQ
    return pl.pallas_call(
        _combine_kernel,
        out_shape=jax.ShapeDtypeStruct((BATCH * SEQ, D_MODEL), F32),
        grid_spec=pltpu.PrefetchScalarGridSpec(
            num_scalar_prefetch=1,
            grid=(BATCH, per_b),
            in_specs=[
                pl.BlockSpec((N_EXPERTS, COMBINE_TQ), lambda b, t, _: (b, t)),
                pl.BlockSpec((N_EXPERTS, COMBINE_TQ), lambda b, t, _: (b, t)),
                *[pl.BlockSpec((GROUP_EXPERTS, CAPACITY, D_MODEL), lambda b, t, _: (0, b, 0)) for _ in ys],
                pl.BlockSpec((COMBINE_TQ, D_MODEL), lambda b, t, _: (b * per_b + t, 0)),
                pl.BlockSpec((1, 6, D_MODEL), lambda b, t, _: (b, 0, 0)),
                pl.BlockSpec((1, D_MODEL), lambda b, t, _: (0, 0)),
            ],
            out_specs=pl.BlockSpec((COMBINE_TQ, D_MODEL), lambda b, t, _: (b * per_b + t, 0)),
        ),
        compiler_params=_params(("arbitrary", "arbitrary")),
        name="combine",
    )(bounds.reshape(-1), rank_e, gate_e, *ys, x1, mod3, final_g)


def _rope_tables():
    t = np.arange(SEQ)
    pos = np.stack([t // GRID_W, t % GRID_W], axis=1).astype(np.float32)
    per_axis = HEAD_DIM // 2
    inv = np.float32(ROPE_THETA) ** (-np.arange(0, per_axis, 2, dtype=np.float32) / np.float32(per_axis))
    ang = (pos[:, :, None] * inv).astype(np.float32)
    cos, sin = np.cos(ang.astype(np.float64)), np.sin(ang.astype(np.float64))
    zero = np.zeros_like(sin)

    def lanes(first_half, second_half):
        one_map = np.stack([first_half, second_half], axis=2).reshape(SEQ, HEAD_DIM)
        return jnp.asarray(np.concatenate([one_map, one_map], axis=1), F32)

    return lanes(cos, cos), lanes(-sin, zero), lanes(zero, sin)


def kernel(x, c, ctx, c_ctx, norm1_g, norm2_g, w_ada, b_ada, w_in, conv_w, w_out_conv, lambda_q1,
           lambda_k1, lambda_q2, lambda_k2, subln_g, w_o_attn, w_out, w_router, w_gate_e, w_up_e,
           w_down_e, final_g):
    cond = jnp.concatenate([c, c_ctx[None, :], jnp.zeros((COND_ROWS - BATCH - 1, D_MODEL), F32)], axis=0)
    mod3 = _ada(cond, w_ada[0], b_ada).reshape(COND_ROWS, 6, D_MODEL)

    cos, sin_hi, sin_lo = _rope_tables()
    q_scale = math.log2(math.e) * HEAD_DIM ** -0.5
    parts = _inproj(x, mod3, norm1_g, w_in[0], conv_w[0], cos, sin_hi, sin_lo, q_scale)
    ctx_kv = _ctxproj(ctx.reshape(BATCH * CTX_LEN, D_MODEL), mod3, norm1_g, w_in[0])

    lam_params = jnp.concatenate([lambda_q1, lambda_k1, lambda_q2, lambda_k2], axis=0)
    attn_o = _attn(lam_params, parts, ctx_kv, subln_g)

    wr_t = w_router[0].T
    wr_hi = wr_t.astype(BF16)
    wr_lo = (wr_t - wr_hi.astype(F32)).astype(BF16)
    x1, h2, logits_t = _merge(parts, attn_o, x.reshape(BATCH * SEQ, D_MODEL), mod3, norm2_g,
                              w_out_conv[0].astype(BF16), w_o_attn[0].astype(BF16), w_out[0].astype(BF16),
                              jnp.concatenate([wr_hi, wr_lo], axis=0))

    rank_e, gate_e, bounds = _route(logits_t)
    ys = []
    for first in range(0, N_EXPERTS, GROUP_EXPERTS):
        xs = _dispatch(rank_e, h2, first).reshape(GROUP_EXPERTS, BATCH * CAPACITY, D_MODEL // 2)
        ys.append(_experts(xs, w_gate_e[0], w_up_e[0], w_down_e[0], first))
    out = _combine(bounds, rank_e, gate_e, ys, x1, mod3, final_g[None, :])
    return out.reshape(BATCH, SEQ, D_MODEL)
```

```python
import functools
import math

import jax
import jax.numpy as jnp
import numpy as np
from jax import lax
from jax.experimental import pallas as pl
from jax.experimental.pallas import tpu as pltpu
from jax.experimental.pallas import tpu_sc as plsc

D_MODEL = 1024
BATCH = 8
SEQ = 2048
GRID_W = 64
CTX_LEN = 256
N_HEADS = 8
HEAD_DIM = 64
V_DIM = 2 * HEAD_DIM
N_EXPERTS = 16
EXPERT_HIDDEN = 2048
CAPACITY = 2 * SEQ // N_EXPERTS
ROPE_THETA = 10000.0
EPS = 1e-6
LAM_INIT = 0.8 - 0.6 * math.exp(-0.3 * 0)
N_PARTS = 8
N_KEYS = CTX_LEN + SEQ
EXPERT_GROUPS = (4, 12)

LANES = 128
SC_CORES = 2
SC_SUBCORES = 16
SC_LANES = 16
F32 = jnp.float32
BF16 = jnp.bfloat16

ROW_CHUNK = 512
COND_ROWS = 16
ADA_TN = 1024
ATTN_TQ = 2048
ATTN_SQ = 256
MERGE_TM = 1024
FFN_TF = 512
COMBINE_TQ = 1024
COMBINE_SUB = 256
COMBINE_WINDOW = 64
VMEM_LIMIT = 56 * 1024 * 1024


def _dot(a, b):
    return jnp.dot(a, b, preferred_element_type=F32)


def _dot_nt(a, b):
    return lax.dot_general(a, b, (((1,), (1,)), ((), ())), preferred_element_type=F32)


def _params(sem, vmem=VMEM_LIMIT):
    return pltpu.CompilerParams(dimension_semantics=sem, vmem_limit_bytes=vmem)


def _ada_kernel(cond_ref, w_ref, b_ref, o_ref):
    c = cond_ref[...]
    s = c / (1.0 + jnp.exp(-c))
    o_ref[...] = _dot(s.astype(BF16), w_ref[...].astype(BF16)) + b_ref[...]


def _ada(cond, w_ada, b_ada):
    n = w_ada.shape[1]
    return pl.pallas_call(
        _ada_kernel,
        out_shape=jax.ShapeDtypeStruct((COND_ROWS, n), F32),
        grid=(n // ADA_TN,),
        in_specs=[
            pl.BlockSpec((COND_ROWS, D_MODEL), lambda i: (0, 0)),
            pl.BlockSpec((D_MODEL, ADA_TN), lambda i: (0, i)),
            pl.BlockSpec((1, ADA_TN), lambda i: (0, i)),
        ],
        out_specs=pl.BlockSpec((COND_ROWS, ADA_TN), lambda i: (0, i)),
        compiler_params=_params(("arbitrary",)),
        name="ada",
    )(cond, w_ada, b_ada)


def _norm_modulate(x, g, shift, scale):
    ms = jnp.mean(x * x, axis=-1, keepdims=True)
    return (x * lax.rsqrt(ms + EPS) * g) * (1.0 + scale) + shift


def _rope(a, cos, sin_hi, sin_lo):
    return a * cos + pltpu.roll(a, LANES - 16, 1) * sin_hi + pltpu.roll(a, 16, 1) * sin_lo


def _inproj_kernel(x_ref, mod_ref, g_ref, w_ref, cw_ref, cos_ref, shi_ref, slo_ref, o_ref,
                   hx_ref, z_ref, w_bf, *, q_scale):
    t = pl.program_id(0)
    j = (t + N_PARTS - 1) % N_PARTS
    chunks = [slice(r * ROW_CHUNK, (r + 1) * ROW_CHUNK) for r in range(SEQ // ROW_CHUNK)]

    def stage(slot, part):
        rows = slice(part * D_MODEL // len(chunks), (part + 1) * D_MODEL // len(chunks))
        w_bf[slot, rows, :] = w_ref[rows, :].astype(BF16)

    def proj(part, r):
        stage(1 - part % 2, r)
        return _dot(hx_ref[chunks[r], :], w_bf[part % 2])

    def branch(part):
        return pl.when((t > 0) & (j == part))

    @pl.when(t == 0)
    def _():
        for r in range(len(chunks)):
            stage(0, r)

    @branch(0)
    def _():
        shift = mod_ref[0, 0:1, :]
        scale = mod_ref[0, 1:2, :]
        for rows in chunks:
            hx_ref[rows, :] = _norm_modulate(x_ref[0, rows, :], g_ref[...], shift, scale).astype(BF16)
        z_ref[0:8, :] = jnp.zeros((8, D_MODEL), F32)
        z_ref[SEQ + 8:SEQ + 16, :] = jnp.zeros((8, D_MODEL), F32)
        for r, rows in enumerate(chunks):
            z_ref[pl.ds(rows.start + 8, ROW_CHUNK), :] = proj(0, r)

    @branch(1)
    def _():
        for r, rows in enumerate(chunks):
            zr = pl.ds(rows.start + 8, ROW_CHUNK)
            z_ref[zr, :] = z_ref[zr, :] * proj(1, r)

    @branch(2)
    def _():
        first_row = lax.broadcasted_iota(jnp.int32, (8, 1), 0) == 0
        last_row = lax.broadcasted_iota(jnp.int32, (8, 1), 0) == 7
        for r, rows in enumerate(chunks):
            base = rows.start + 8
            zc = z_ref[pl.ds(base, ROW_CHUNK), :]
            before = pltpu.roll(zc * cw_ref[0:1, :], 1, 0)
            after = pltpu.roll(zc * cw_ref[2:3, :], ROW_CHUNK - 1, 0)
            edge_b = jnp.where(first_row, z_ref[pl.ds(base - 8, 8), :][7:8, :] * cw_ref[0:1, :], before[0:8, :])
            edge_a = jnp.where(last_row, z_ref[pl.ds(base + ROW_CHUNK, 8), :][0:1, :] * cw_ref[2:3, :],
                               after[ROW_CHUNK - 8:, :])
            before = jnp.concatenate([edge_b, before[8:, :]], axis=0)
            after = jnp.concatenate([after[:ROW_CHUNK - 8, :], edge_a], axis=0)
            y = before + zc * cw_ref[1:2, :] + after
            o_ref[0, rows, :] = (proj(2, r) * y).astype(BF16)

    def rope_part(part, scale):
        for r, rows in enumerate(chunks):
            acc = proj(part, r)
            cos, shi, slo = cos_ref[rows, :], shi_ref[rows, :], slo_ref[rows, :]
            for h in range(D_MODEL // LANES):
                cols = slice(h * LANES, (h + 1) * LANES)
                roped = _rope(acc[:, cols], cos, shi, slo)
                if scale is not None:
                    roped = roped * scale
                o_ref[0, rows, cols] = roped.astype(BF16)

    @branch(3)
    def _():
        rope_part(3, q_scale)

    @branch(4)
    def _():
        rope_part(4, None)

    @branch(5)
    def _():
        for r, rows in enumerate(chunks):
            o_ref[0, rows, :] = proj(5, r).astype(BF16)

    for gate_part in (6, 7):
        @branch(gate_part)
        def _(gate_part=gate_part):
            for r, rows in enumerate(chunks):
                o_ref[0, rows, :] = (1.0 / (1.0 + jnp.exp(-proj(gate_part, r)))).astype(BF16)


def _inproj(x, mod3, norm_g, w_in, conv_w, cos, sin_hi, sin_lo, q_scale):
    last = BATCH * N_PARTS - 1

    def part(t):
        return jnp.maximum(t - 1, 0) % N_PARTS

    def sample(t):
        return jnp.maximum(t - 1, 0) // N_PARTS

    def w_map(t):
        nxt = jnp.minimum(t, last) % N_PARTS
        return (0, jnp.where(nxt == 1, 2, jnp.where(nxt == 2, 1, nxt)))

    tab = pl.BlockSpec((SEQ, LANES), lambda t: (0, 0))
    return pl.pallas_call(
        functools.partial(_inproj_kernel, q_scale=q_scale),
        out_shape=jax.ShapeDtypeStruct((N_PARTS - 2, BATCH * SEQ, D_MODEL), BF16),
        grid=(BATCH * N_PARTS + 1,),
        in_specs=[
            pl.BlockSpec((1, SEQ, D_MODEL), lambda t: (sample(t), 0, 0)),
            pl.BlockSpec((1, 6, D_MODEL), lambda t: (sample(t), 0, 0)),
            pl.BlockSpec((1, D_MODEL), lambda t: (0, 0)),
            pl.BlockSpec((D_MODEL, D_MODEL), w_map),
            pl.BlockSpec((3, D_MODEL), lambda t: (0, 0)),
            tab, tab, tab,
        ],
        out_specs=pl.BlockSpec((1, SEQ, D_MODEL), lambda t: (jnp.maximum(part(t) - 2, 0), sample(t), 0)),
        scratch_shapes=[
            pltpu.VMEM((SEQ, D_MODEL), BF16),
            pltpu.VMEM((SEQ + 16, D_MODEL), F32),
            pltpu.VMEM((2, D_MODEL, D_MODEL), BF16),
        ],
        compiler_params=_params(("arbitrary",)),
        name="inproj",
    )(x, mod3, norm_g, w_in, conv_w, cos, sin_hi, sin_lo)


def _ctxproj_kernel(c_ref, mod_ref, g_ref, w_ref, o_ref, hc_ref):
    j = pl.program_id(0)
    chunks = [slice(r * ROW_CHUNK, (r + 1) * ROW_CHUNK) for r in range(BATCH * CTX_LEN // ROW_CHUNK)]

    @pl.when(j == 0)
    def _():
        shift = mod_ref[0, 0:1, :]
        scale = mod_ref[0, 1:2, :]
        for rows in chunks:
            hc_ref[rows, :] = _norm_modulate(c_ref[rows, :], g_ref[...], shift, scale).astype(BF16)

    w = w_ref[...].astype(BF16)
    for rows in chunks:
        o_ref[0, rows, :] = _dot(hc_ref[rows, :], w).astype(BF16)


def _ctxproj(ctx2, mod3, norm_g, w_in):
    rows = BATCH * CTX_LEN
    return pl.pallas_call(
        _ctxproj_kernel,
        out_shape=jax.ShapeDtypeStruct((2, rows, D_MODEL), BF16),
        grid=(2,),
        in_specs=[
            pl.BlockSpec((rows, D_MODEL), lambda j: (0, 0)),
            pl.BlockSpec((1, 6, D_MODEL), lambda j: (BATCH, 0, 0)),
            pl.BlockSpec((1, D_MODEL), lambda j: (0, 0)),
            pl.BlockSpec((D_MODEL, D_MODEL), lambda j: (0, 4 + j)),
        ],
        out_specs=pl.BlockSpec((1, rows, D_MODEL), lambda j: (j, 0, 0)),
        scratch_shapes=[pltpu.VMEM((rows, D_MODEL), BF16)],
        compiler_params=_params(("arbitrary",)),
        name="ctxproj",
    )(ctx2, mod3, norm_g, w_in)


def _attn_kernel(lp_ref, q_ref, k_ref, v_ref, kc_ref, vc_ref, g_ref, o_ref, k_all, v_ext, s_ref, m_ref):
    lp = lp_ref[...]
    lam = (jnp.exp(jnp.sum(lp[0:1] * lp[1:2], axis=-1, keepdims=True))
           - jnp.exp(jnp.sum(lp[2:3] * lp[3:4], axis=-1, keepdims=True)) + LAM_INIT)
    k_all[:, 0:CTX_LEN] = kc_ref[0].T
    k_all[:, CTX_LEN:N_KEYS] = k_ref[0].T
    v_ext[0:CTX_LEN, 0:LANES] = vc_ref[0]
    v_ext[CTX_LEN:N_KEYS, 0:LANES] = v_ref[0]
    v_ext[:, LANES:2 * LANES] = jnp.ones((N_KEYS, LANES), BF16)
    first_map = lax.broadcasted_iota(jnp.int32, (1, LANES), 1) < HEAD_DIM

    def scores(u):
        q = q_ref[0, u * ATTN_SQ:(u + 1) * ATTN_SQ, :]
        zero = jnp.zeros_like(q)
        for mp, qm in enumerate((jnp.where(first_map, q, zero), jnp.where(first_map, zero, q))):
            s = _dot(qm, k_all[...])
            s_ref[u % 2, mp] = s
            m_ref[u % 2, mp] = jnp.max(s, axis=-1, keepdims=True)

    def values(u):
        o = []
        for mp in range(2):
            p = jnp.exp2(s_ref[u % 2, mp] - m_ref[u % 2, mp])
            o.append(_dot(p.astype(BF16), v_ext[...]))
        a = o[0][:, :LANES] / o[0][:, LANES:] - o[1][:, :LANES] * (lam / o[1][:, LANES:])
        ms = jnp.mean(a * a, axis=-1, keepdims=True)
        o_ref[u * ATTN_SQ:(u + 1) * ATTN_SQ, :] = (
            (a * lax.rsqrt(ms + EPS) * g_ref[...]) * (1.0 - LAM_INIT)).astype(BF16)

    n_units = ATTN_TQ // ATTN_SQ
    scores(0)
    for u in range(n_units):
        if u + 1 < n_units:
            scores(u + 1)
        values(u)


def _attn(lam_params, parts, ctx_kv, subln_g):
    nq = SEQ // ATTN_TQ
    return pl.pallas_call(
        _attn_kernel,
        out_shape=jax.ShapeDtypeStruct((BATCH * SEQ, D_MODEL), BF16),
        grid=(BATCH, N_HEADS, nq),
        in_specs=[
            pl.BlockSpec((4, HEAD_DIM), lambda b, h, i: (0, 0)),
            pl.BlockSpec((1, ATTN_TQ, LANES), lambda b, h, i: (1, b * nq + i, h)),
            pl.BlockSpec((1, SEQ, LANES), lambda b, h, i: (2, b, h)),
            pl.BlockSpec((1, SEQ, LANES), lambda b, h, i: (3, b, h)),
            pl.BlockSpec((1, CTX_LEN, LANES), lambda b, h, i: (0, b, h)),
            pl.BlockSpec((1, CTX_LEN, LANES), lambda b, h, i: (1, b, h)),
            pl.BlockSpec((1, V_DIM), lambda b, h, i: (0, 0)),
        ],
        out_specs=pl.BlockSpec((ATTN_TQ, LANES), lambda b, h, i: (b * nq + i, h)),
        scratch_shapes=[
            pltpu.VMEM((LANES, N_KEYS), BF16),
            pltpu.VMEM((N_KEYS, 2 * LANES), BF16),
            pltpu.VMEM((2, 2, ATTN_SQ, N_KEYS), F32),
            pltpu.VMEM((2, 2, ATTN_SQ, 1), F32),
        ],
        compiler_params=_params(("arbitrary", "arbitrary", "arbitrary")),
        name="attn",
    )(lam_params, parts, parts, parts, ctx_kv, ctx_kv, subln_g)


def _merge_kernel(yb_ref, s6_ref, s7_ref, o_ref, x_ref, mod_ref, g2_ref, wc_ref, wa_ref, wo_ref,
                  wr_ref, x1_ref, h2_ref, lg_ref):
    chunks = [slice(r * ROW_CHUNK, (r + 1) * ROW_CHUNK) for r in range(MERGE_TM // ROW_CHUNK)]
    for rows in chunks:
        y_conv = _dot(yb_ref[0, rows, :], wc_ref[...])
        y_attn = _dot(o_ref[rows, :], wa_ref[...])
        merged = s6_ref[0, rows, :].astype(F32) * y_conv + s7_ref[0, rows, :].astype(F32) * y_attn
        mix = _dot(merged.astype(BF16), wo_ref[...])
        x1_ref[rows, :] = x_ref[rows, :] + mod_ref[0, 2:3, :] * mix
    for rows in chunks:
        h2 = _norm_modulate(x1_ref[rows, :], g2_ref[...], mod_ref[0, 3:4, :], mod_ref[0, 4:5, :])
        h_hi = h2.astype(BF16)
        h2_ref[rows, :] = pltpu.pack_elementwise([h2[:, :D_MODEL // 2], h2[:, D_MODEL // 2:]], packed_dtype=BF16)
        h_lo = (h2 - h_hi.astype(F32)).astype(BF16)
        both = _dot_nt(wr_ref[...], h_hi)
        lg_ref[0, :, rows] = both[:N_EXPERTS] + (_dot_nt(wr_ref[0:N_EXPERTS, :], h_lo) + both[N_EXPERTS:])


def _merge(parts, attn_o, x2, mod3, norm2_g, w_conv_bf, w_attn_bf, w_out_bf, wr_hi_lo):
    per_b = SEQ // MERGE_TM
    sq = pl.BlockSpec((D_MODEL, D_MODEL), lambda i: (0, 0), pipeline_mode=pl.Buffered(1))
    wr = pl.BlockSpec((2 * N_EXPERTS, D_MODEL), lambda i: (0, 0), pipeline_mode=pl.Buffered(1))
    tile = pl.BlockSpec((MERGE_TM, D_MODEL), lambda i: (i, 0))
    return pl.pallas_call(
        _merge_kernel,
        out_shape=(
            jax.ShapeDtypeStruct((BATCH * SEQ, D_MODEL), F32),
            jax.ShapeDtypeStruct((BATCH * SEQ, D_MODEL // 2), jnp.uint32),
            jax.ShapeDtypeStruct((BATCH, N_EXPERTS, SEQ), F32),
        ),
        grid=(BATCH * per_b,),
        in_specs=[
            pl.BlockSpec((1, MERGE_TM, D_MODEL), lambda i: (0, i, 0)),
            pl.BlockSpec((1, MERGE_TM, D_MODEL), lambda i: (4, i, 0)),
            pl.BlockSpec((1, MERGE_TM, D_MODEL), lambda i: (5, i, 0)),
            tile, tile,
            pl.BlockSpec((1, 6, D_MODEL), lambda i: (i // per_b, 0, 0)),
            pl.BlockSpec((1, D_MODEL), lambda i: (0, 0)),
            sq, sq, sq, wr,
        ],
        out_specs=(
            tile,
            pl.BlockSpec((MERGE_TM, D_MODEL // 2), lambda i: (i, 0)),
            pl.BlockSpec((1, N_EXPERTS, MERGE_TM), lambda i: (i // per_b, 0, i % per_b)),
        ),
        compiler_params=_params(("arbitrary",)),
        name="merge",
    )(parts, parts, parts, attn_o, x2, mod3, norm2_g, w_conv_bf, w_attn_bf, w_out_bf, wr_hi_lo)


def _route_kernel(lg_ref, rank_ref, gate_ref, bounds_ref):
    lg = lg_ref[...]
    ex = jnp.exp(lg - jnp.max(lg, axis=1, keepdims=True))
    aff = (ex / jnp.sum(ex, axis=1, keepdims=True)).reshape(BATCH * N_EXPERTS, SEQ)
    bits = lax.bitcast_convert_type(aff, jnp.int32)

    def count(mask):
        return jnp.sum(jnp.where(mask, 1.0, 0.0), axis=1, keepdims=True)

    def search(i, t):
        cand = t + lax.shift_left(jnp.int32(1), 30 - i)
        return jnp.where(count(bits >= cand) >= CAPACITY, cand, t)

    thr = lax.fori_loop(0, 31, search, jnp.zeros((BATCH * N_EXPERTS, 1), jnp.int32))
    need = CAPACITY - count(bits > thr)
    before = (lax.broadcasted_iota(jnp.int32, (LANES, LANES), 0)
              < lax.broadcasted_iota(jnp.int32, (LANES, LANES), 1)).astype(BF16)
    n_tied = jnp.zeros((BATCH * N_EXPERTS, 1), F32)
    n_sel = jnp.zeros((BATCH * N_EXPERTS, 1), F32)
    per_tile = COMBINE_SUB // LANES
    firsts, ends = [], []
    for c in range(SEQ // LANES):
        if c % per_tile == 0:
            firsts.append(n_sel)
        cols = slice(c * LANES, (c + 1) * LANES)
        bits_c = bits[:, cols]
        tied_c = jnp.where(bits_c == thr, 1.0, 0.0)
        tied_before = _dot(tied_c.astype(BF16), before) + n_tied
        sel_c = jnp.where(bits_c > thr, 1.0, jnp.where(tied_before < need, tied_c, 0.0))
        rank = _dot(sel_c.astype(BF16), before) + n_sel
        rank_c = jnp.where(sel_c > 0.0, rank, -1.0)
        gate_c = jnp.where(sel_c > 0.0, aff[:, cols], 0.0)
        n_tied = n_tied + jnp.sum(tied_c, axis=1, keepdims=True)
        n_sel = n_sel + jnp.sum(sel_c, axis=1, keepdims=True)
        rank_ref[:, cols] = rank_c.astype(jnp.int32)
        gate_ref[:, cols] = gate_c
        if c % per_tile == per_tile - 1:
            ends.append(n_sel)
    bounds_ref[...] = jnp.concatenate(firsts + ends, axis=1).astype(jnp.int32)


def _route(logits_t):
    return pl.pallas_call(
        _route_kernel,
        out_shape=(
            jax.ShapeDtypeStruct((BATCH * N_EXPERTS, SEQ), jnp.int32),
            jax.ShapeDtypeStruct((BATCH * N_EXPERTS, SEQ), F32),
            jax.ShapeDtypeStruct((BATCH * N_EXPERTS, 2 * (SEQ // COMBINE_SUB)), jnp.int32),
        ),
        compiler_params=pltpu.CompilerParams(vmem_limit_bytes=VMEM_LIMIT),
        name="route",
    )(logits_t)


def _dispatch(rank_e, h_packed, first_expert, n_experts):
    per_worker = n_experts * BATCH // (SC_CORES * SC_SUBCORES)
    assert per_worker * SC_CORES * SC_SUBCORES == n_experts * BATCH
    half = CAPACITY // 2
    mesh = plsc.VectorSubcoreMesh(core_axis_name="c", subcore_axis_name="s")

    @functools.partial(
        pl.kernel, mesh=mesh,
        out_type=jax.ShapeDtypeStruct((n_experts * BATCH * CAPACITY, D_MODEL // 2), h_packed.dtype),
        scratch_types=[
            pltpu.VMEM((SEQ,), jnp.int32),
            pltpu.VMEM((2, half), jnp.int32),
            pltpu.VMEM((half, D_MODEL // 2), h_packed.dtype),
            pltpu.SemaphoreType.DMA,
        ],
        compiler_params=pltpu.CompilerParams(needs_layout_passes=False),
        name="dispatch",
    )
    def k(table_hbm, rank_hbm, out_hbm, rank_v, idx_v, rows_v, sem):
        worker = lax.axis_index("s") * SC_CORES + lax.axis_index("c")

        @pl.loop(0, per_worker)
        def _(p):
            pair = worker * per_worker + p
            expert = lax.shift_right_logical(pair, BATCH.bit_length() - 1)
            sample = pair & (BATCH - 1)
            pltpu.sync_copy(rank_hbm.at[sample * N_EXPERTS + first_expert + expert], rank_v)

            @pl.loop(0, SEQ // SC_LANES)
            def _(i):
                rank = rank_v[pl.ds(i * SC_LANES, SC_LANES)]
                row = lax.iota(jnp.int32, SC_LANES) + (i * SC_LANES + sample * SEQ)
                plsc.store_scatter(idx_v, [lax.shift_right_arithmetic(rank, 7), rank & (half - 1)], row,
                                   mask=rank >= 0)

            for c in range(2):
                pltpu.async_copy(table_hbm.at[idx_v.at[c]], rows_v, sem).wait()
                pltpu.sync_copy(rows_v, out_hbm.at[pl.ds(pair * CAPACITY + c * half, half)])

    return k(h_packed, rank_e)


def _expert_kernel(xs_ref, wg_ref, wu_ref, wd_ref, y_ref, acc_ref, wg_bf, wu_bf, wd_bf):
    t = pl.program_id(0)
    n_f = EXPERT_HIDDEN // FFN_TF
    f = (t + n_f - 1) % n_f

    n_chunks = BATCH * CAPACITY // ROW_CHUNK

    def stage(slot, part):
        up = slice(part * D_MODEL // n_chunks, (part + 1) * D_MODEL // n_chunks)
        down = slice(part * FFN_TF // n_chunks, (part + 1) * FFN_TF // n_chunks)
        wg_bf[slot, up, :] = wg_ref[0, up, :].astype(BF16)
        wu_bf[slot, up, :] = wu_ref[0, up, :].astype(BF16)
        wd_bf[slot, down, :] = wd_ref[0, down, :].astype(BF16)

    def step(cur):
        for r in range(n_chunks):
            stage(1 - cur, r)
            rows = slice(r * ROW_CHUNK, (r + 1) * ROW_CHUNK)
            packed = xs_ref[0, rows, :]
            half = D_MODEL // 2
            x_lo = pltpu.unpack_elementwise(packed, index=0, packed_dtype=BF16, unpacked_dtype=F32).astype(BF16)
            x_hi = pltpu.unpack_elementwise(packed, index=1, packed_dtype=BF16, unpacked_dtype=F32).astype(BF16)
            a = _dot(x_lo, wg_bf[cur, :half, :]) + _dot(x_hi, wg_bf[cur, half:, :])
            u = _dot(x_lo, wu_bf[cur, :half, :]) + _dot(x_hi, wu_bf[cur, half:, :])
            hidden = (a / (1.0 + jnp.exp(-a))) * u
            total = jnp.where(f == 0, 0.0, acc_ref[rows, :]) + _dot(hidden.astype(BF16), wd_bf[cur])
            acc_ref[rows, :] = total
            y_ref[0, rows, :] = total.astype(BF16)

    @pl.when(t == 0)
    def _():
        for part in range(n_chunks):
            stage(0, part)

    @pl.when(t % 2 == 1)
    def _():
        step(0)

    @pl.when((t % 2 == 0) & (t > 0))
    def _():
        step(1)


def _experts(xs, w_gate, w_up, w_down, first_expert):
    rows = BATCH * CAPACITY
    n_f = EXPERT_HIDDEN // FFN_TF
    n_experts = xs.shape[0]
    last = n_experts * n_f - 1

    def staged(t):
        return jnp.minimum(t, last)

    def computed(t):
        return jnp.maximum(t - 1, 0) // n_f

    return pl.pallas_call(
        _expert_kernel,
        out_shape=jax.ShapeDtypeStruct((n_experts, rows, D_MODEL), BF16),
        grid=(n_experts * n_f + 1,),
        in_specs=[
            pl.BlockSpec((1, rows, D_MODEL // 2), lambda t: (computed(t), 0, 0)),
            pl.BlockSpec((1, D_MODEL, FFN_TF), lambda t: (first_expert + staged(t) // n_f, 0, staged(t) % n_f)),
            pl.BlockSpec((1, D_MODEL, FFN_TF), lambda t: (first_expert + staged(t) // n_f, 0, staged(t) % n_f)),
            pl.BlockSpec((1, FFN_TF, D_MODEL), lambda t: (first_expert + staged(t) // n_f, staged(t) % n_f, 0)),
        ],
        out_specs=pl.BlockSpec((1, rows, D_MODEL), lambda t: (computed(t), 0, 0)),
        scratch_shapes=[
            pltpu.VMEM((rows, D_MODEL), F32),
            pltpu.VMEM((2, D_MODEL, FFN_TF), BF16),
            pltpu.VMEM((2, D_MODEL, FFN_TF), BF16),
            pltpu.VMEM((2, FFN_TF, D_MODEL), BF16),
        ],
        compiler_params=_params(("arbitrary",)),
        name="experts",
    )(xs, w_gate, w_up, w_down)


def _combine_kernel(bounds_ref, rank_ref, gate_ref, *refs):
    n_groups = len(EXPERT_GROUPS)
    y_refs, (x1_ref, mod_ref, fg_ref, o_ref) = refs[:n_groups], refs[n_groups:]
    group_of = [(g, e) for g, n in enumerate(EXPERT_GROUPS) for e in range(n)]
    b, t = pl.program_id(0), pl.program_id(1)
    subs = COMBINE_TQ // COMBINE_SUB
    n_sub = SEQ // COMBINE_SUB
    per_dot = CAPACITY // COMBINE_WINDOW

    def y_rows(e):
        return y_refs[group_of[e][0]].at[group_of[e][1]]

    def onehot_t(e, cols, first, n_slots):
        slot = lax.broadcasted_iota(jnp.int32, (n_slots, cols.stop - cols.start), 0) + first
        return jnp.where(rank_ref[e:e + 1, cols] == slot, gate_ref[e:e + 1, cols], 0.0).astype(BF16)

    def dot_t(a, b_):
        return lax.dot_general(a, b_, (((0,), (0,)), ((), ())), preferred_element_type=F32)

    def finish(rows, acc):
        x2 = x1_ref[rows, :] + mod_ref[0, 5:6, :] * acc
        ms = jnp.mean(x2 * x2, axis=-1, keepdims=True)
        o_ref[rows, :] = x2 * lax.rsqrt(ms + EPS) * fg_ref[...]

    starts, fits = {}, None
    for s in range(subs):
        for e in range(N_EXPERTS):
            pair = (b * N_EXPERTS + e) * (2 * n_sub) + t * subs + s
            first, end = bounds_ref[pair], bounds_ref[pair + n_sub]
            start = jnp.minimum(lax.shift_left(lax.shift_right_logical(first, 4), 4), CAPACITY - COMBINE_WINDOW)
            starts[s, e] = start
            ok = end <= start + COMBINE_WINDOW
            fits = ok if fits is None else fits & ok

    @pl.when(fits)
    def _():
        for s in range(subs):
            rows = slice(s * COMBINE_SUB, (s + 1) * COMBINE_SUB)
            acc = jnp.zeros((COMBINE_SUB, D_MODEL), F32)
            for e0 in range(0, N_EXPERTS, per_dot):
                first = [pl.multiple_of(starts[s, e0 + i], 16) for i in range(per_dot)]
                onehots = [onehot_t(e0 + i, rows, first[i], COMBINE_WINDOW) for i in range(per_dot)]
                y_win = [y_rows(e0 + i)[pl.ds(first[i], COMBINE_WINDOW), :] for i in range(per_dot)]
                acc = acc + dot_t(jnp.concatenate(onehots, axis=0), jnp.concatenate(y_win, axis=0))
            finish(rows, acc)

    @pl.when(jnp.logical_not(fits))
    def _():
        acc = jnp.zeros((COMBINE_TQ, D_MODEL), F32)
        for e in range(N_EXPERTS):
            acc = acc + dot_t(onehot_t(e, slice(0, COMBINE_TQ), 0, CAPACITY), y_rows(e)[...])
        finish(slice(0, COMBINE_TQ), acc)


def _combine(bounds, rank_e, gate_e, ys, x1, mod3, final_g):
    per_b = SEQ // COMBINE_TQ
    return pl.pallas_call(
        _combine_kernel,
        out_shape=jax.ShapeDtypeStruct((BATCH * SEQ, D_MODEL), F32),
        grid_spec=pltpu.PrefetchScalarGridSpec(
            num_scalar_prefetch=1,
            grid=(BATCH, per_b),
            in_specs=[
                pl.BlockSpec((N_EXPERTS, COMBINE_TQ), lambda b, t, _: (b, t)),
                pl.BlockSpec((N_EXPERTS, COMBINE_TQ), lambda b, t, _: (b, t)),
                *[pl.BlockSpec((y.shape[0], CAPACITY, D_MODEL), lambda b, t, _: (0, b, 0)) for y in ys],
                pl.BlockSpec((COMBINE_TQ, D_MODEL), lambda b, t, _: (b * per_b + t, 0)),
                pl.BlockSpec((1, 6, D_MODEL), lambda b, t, _: (b, 0, 0)),
                pl.BlockSpec((1, D_MODEL), lambda b, t, _: (0, 0)),
            ],
            out_specs=pl.BlockSpec((COMBINE_TQ, D_MODEL), lambda b, t, _: (b * per_b + t, 0)),
        ),
        compiler_params=_params(("arbitrary", "arbitrary")),
        name="combine",
    )(bounds.reshape(-1), rank_e, gate_e, *ys, x1, mod3, final_g)


def _rope_tables():
    t = np.arange(SEQ)
    pos = np.stack([t // GRID_W, t % GRID_W], axis=1).astype(np.float32)
    per_axis = HEAD_DIM // 2
    inv = np.float32(ROPE_THETA) ** (-np.arange(0, per_axis, 2, dtype=np.float32) / np.float32(per_axis))
    ang = (pos[:, :, None] * inv).astype(np.float32)
    cos, sin = np.cos(ang.astype(np.float64)), np.sin(ang.astype(np.float64))
    zero = np.zeros_like(sin)

    def lanes(first_half, second_half):
        one_map = np.stack([first_half, second_half], axis=2).reshape(SEQ, HEAD_DIM)
        return jnp.asarray(np.concatenate([one_map, one_map], axis=1), F32)

    return lanes(cos, cos), lanes(-sin, zero), lanes(zero, sin)


def kernel(x, c, ctx, c_ctx, norm1_g, norm2_g, w_ada, b_ada, w_in, conv_w, w_out_conv, lambda_q1,
           lambda_k1, lambda_q2, lambda_k2, subln_g, w_o_attn, w_out, w_router, w_gate_e, w_up_e,
           w_down_e, final_g):
    cond = jnp.concatenate([c, c_ctx[None, :], jnp.zeros((COND_ROWS - BATCH - 1, D_MODEL), F32)], axis=0)
    mod3 = _ada(cond, w_ada[0], b_ada).reshape(COND_ROWS, 6, D_MODEL)

    cos, sin_hi, sin_lo = _rope_tables()
    q_scale = math.log2(math.e) * HEAD_DIM ** -0.5
    parts = _inproj(x, mod3, norm1_g, w_in[0], conv_w[0], cos, sin_hi, sin_lo, q_scale)
    ctx_kv = _ctxproj(ctx.reshape(BATCH * CTX_LEN, D_MODEL), mod3, norm1_g, w_in[0])

    lam_params = jnp.concatenate([lambda_q1, lambda_k1, lambda_q2, lambda_k2], axis=0)
    attn_o = _attn(lam_params, parts, ctx_kv, subln_g)

    wr_t = w_router[0].T
    wr_hi = wr_t.astype(BF16)
    wr_lo = (wr_t - wr_hi.astype(F32)).astype(BF16)
    x1, h2, logits_t = _merge(parts, attn_o, x.reshape(BATCH * SEQ, D_MODEL), mod3, norm2_g,
                              w_out_conv[0].astype(BF16), w_o_attn[0].astype(BF16), w_out[0].astype(BF16),
                              jnp.concatenate([wr_hi, wr_lo], axis=0))

    rank_e, gate_e, bounds = _route(logits_t)
    ys, first = [], 0
    for n in EXPERT_GROUPS:
        xs = _dispatch(rank_e, h2, first, n).reshape(n, BATCH * CAPACITY, D_MODEL // 2)
        ys.append(_experts(xs, w_gate_e[0], w_up_e[0], w_down_e[0], first))
        first += n
    out = _combine(bounds, rank_e, gate_e, ys, x1, mod3, final_g[None, :])
    return out.reshape(BATCH, SEQ, D_MODEL)
```

```python
import functools
import math

import jax
import jax.numpy as jnp
import numpy as np
from jax import lax
from jax.experimental import pallas as pl
from jax.experimental.pallas import tpu as pltpu
from jax.experimental.pallas import tpu_sc as plsc

D_MODEL = 1024
BATCH = 8
SEQ = 2048
GRID_W = 64
CTX_LEN = 256
N_HEADS = 8
HEAD_DIM = 64
V_DIM = 2 * HEAD_DIM
N_EXPERTS = 16
EXPERT_HIDDEN = 2048
CAPACITY = 2 * SEQ // N_EXPERTS
ROPE_THETA = 10000.0
EPS = 1e-6
LAM_INIT = 0.8 - 0.6 * math.exp(-0.3 * 0)
N_PARTS = 8
N_KEYS = CTX_LEN + SEQ
EXPERT_GROUPS = (4, 12)

LANES = 128
SC_CORES = 2
SC_SUBCORES = 16
SC_LANES = 16
F32 = jnp.float32
BF16 = jnp.bfloat16

ROW_CHUNK = 512
COND_ROWS = 16
ADA_TN = 1024
ATTN_TQ = 2048
ATTN_SQ = 256
MERGE_TM = 1024
FFN_TF = 512
COMBINE_TQ = 1024
COMBINE_SUB = 256
COMBINE_WINDOW = 64
VMEM_LIMIT = 56 * 1024 * 1024


def _dot(a, b):
    return jnp.dot(a, b, preferred_element_type=F32)


def _dot_nt(a, b):
    return lax.dot_general(a, b, (((1,), (1,)), ((), ())), preferred_element_type=F32)


def _params(sem, vmem=VMEM_LIMIT):
    return pltpu.CompilerParams(dimension_semantics=sem, vmem_limit_bytes=vmem)


def _ada_kernel(cond_ref, w_ref, b_ref, o_ref):
    c = cond_ref[...]
    s = c / (1.0 + jnp.exp(-c))
    o_ref[...] = _dot(s.astype(BF16), w_ref[...].astype(BF16)) + b_ref[...]


def _ada(cond, w_ada, b_ada):
    n = w_ada.shape[1]
    return pl.pallas_call(
        _ada_kernel,
        out_shape=jax.ShapeDtypeStruct((COND_ROWS, n), F32),
        grid=(n // ADA_TN,),
        in_specs=[
            pl.BlockSpec((COND_ROWS, D_MODEL), lambda i: (0, 0)),
            pl.BlockSpec((D_MODEL, ADA_TN), lambda i: (0, i)),
            pl.BlockSpec((1, ADA_TN), lambda i: (0, i)),
        ],
        out_specs=pl.BlockSpec((COND_ROWS, ADA_TN), lambda i: (0, i)),
        compiler_params=_params(("arbitrary",)),
        name="ada",
    )(cond, w_ada, b_ada)


def _norm_modulate(x, g, shift, scale):
    ms = jnp.mean(x * x, axis=-1, keepdims=True)
    return (x * lax.rsqrt(ms + EPS) * g) * (1.0 + scale) + shift


def _rope(a, cos, sin_hi, sin_lo):
    return a * cos + pltpu.roll(a, LANES - 16, 1) * sin_hi + pltpu.roll(a, 16, 1) * sin_lo


def _inproj_kernel(x_ref, mod_ref, g_ref, w_ref, cw_ref, cos_ref, shi_ref, slo_ref, o_ref,
                   hx_ref, z_ref, w_bf, *, q_scale):
    t = pl.program_id(0)
    j = (t + N_PARTS - 1) % N_PARTS
    chunks = [slice(r * ROW_CHUNK, (r + 1) * ROW_CHUNK) for r in range(SEQ // ROW_CHUNK)]

    def stage(slot, part):
        rows = slice(part * D_MODEL // len(chunks), (part + 1) * D_MODEL // len(chunks))
        w_bf[slot, rows, :] = w_ref[rows, :].astype(BF16)

    def proj(part, r):
        stage(1 - part % 2, r)
        return _dot(hx_ref[chunks[r], :], w_bf[part % 2])

    def branch(part):
        return pl.when((t > 0) & (j == part))

    @pl.when(t == 0)
    def _():
        for r in range(len(chunks)):
            stage(0, r)

    @branch(0)
    def _():
        shift = mod_ref[0, 0:1, :]
        scale = mod_ref[0, 1:2, :]
        for rows in chunks:
            hx_ref[rows, :] = _norm_modulate(x_ref[0, rows, :], g_ref[...], shift, scale).astype(BF16)
        z_ref[0:8, :] = jnp.zeros((8, D_MODEL), F32)
        z_ref[SEQ + 8:SEQ + 16, :] = jnp.zeros((8, D_MODEL), F32)
        for r, rows in enumerate(chunks):
            z_ref[pl.ds(rows.start + 8, ROW_CHUNK), :] = proj(0, r)

    @branch(1)
    def _():
        for r, rows in enumerate(chunks):
            zr = pl.ds(rows.start + 8, ROW_CHUNK)
            z_ref[zr, :] = z_ref[zr, :] * proj(1, r)

    @branch(2)
    def _():
        first_row = lax.broadcasted_iota(jnp.int32, (8, 1), 0) == 0
        last_row = lax.broadcasted_iota(jnp.int32, (8, 1), 0) == 7
        for r, rows in enumerate(chunks):
            base = rows.start + 8
            zc = z_ref[pl.ds(base, ROW_CHUNK), :]
            before = pltpu.roll(zc * cw_ref[0:1, :], 1, 0)
            after = pltpu.roll(zc * cw_ref[2:3, :], ROW_CHUNK - 1, 0)
            edge_b = jnp.where(first_row, z_ref[pl.ds(base - 8, 8), :][7:8, :] * cw_ref[0:1, :], before[0:8, :])
            edge_a = jnp.where(last_row, z_ref[pl.ds(base + ROW_CHUNK, 8), :][0:1, :] * cw_ref[2:3, :],
                               after[ROW_CHUNK - 8:, :])
            before = jnp.concatenate([edge_b, before[8:, :]], axis=0)
            after = jnp.concatenate([after[:ROW_CHUNK - 8, :], edge_a], axis=0)
            y = before + zc * cw_ref[1:2, :] + after
            o_ref[0, rows, :] = (proj(2, r) * y).astype(BF16)

    def rope_part(part, scale):
        for r, rows in enumerate(chunks):
            acc = proj(part, r)
            cos, shi, slo = cos_ref[rows, :], shi_ref[rows, :], slo_ref[rows, :]
            for h in range(D_MODEL // LANES):
                cols = slice(h * LANES, (h + 1) * LANES)
                roped = _rope(acc[:, cols], cos, shi, slo)
                if scale is not None:
                    roped = roped * scale
                o_ref[0, rows, cols] = roped.astype(BF16)

    @branch(3)
    def _():
        rope_part(3, q_scale)

    @branch(4)
    def _():
        rope_part(4, None)

    for plain_part in (5, 6, 7):
        @branch(plain_part)
        def _(plain_part=plain_part):
            for r, rows in enumerate(chunks):
                o_ref[0, rows, :] = proj(plain_part, r).astype(BF16)


def _inproj(x, mod3, norm_g, w_in, conv_w, cos, sin_hi, sin_lo, q_scale):
    last = BATCH * N_PARTS - 1

    def part(t):
        return jnp.maximum(t - 1, 0) % N_PARTS

    def sample(t):
        return jnp.maximum(t - 1, 0) // N_PARTS

    def w_map(t):
        nxt = jnp.minimum(t, last) % N_PARTS
        return (0, jnp.where(nxt == 1, 2, jnp.where(nxt == 2, 1, nxt)))

    tab = pl.BlockSpec((SEQ, LANES), lambda t: (0, 0))
    return pl.pallas_call(
        functools.partial(_inproj_kernel, q_scale=q_scale),
        out_shape=jax.ShapeDtypeStruct((N_PARTS - 2, BATCH * SEQ, D_MODEL), BF16),
        grid=(BATCH * N_PARTS + 1,),
        in_specs=[
            pl.BlockSpec((1, SEQ, D_MODEL), lambda t: (sample(t), 0, 0)),
            pl.BlockSpec((1, 6, D_MODEL), lambda t: (sample(t), 0, 0)),
            pl.BlockSpec((1, D_MODEL), lambda t: (0, 0)),
            pl.BlockSpec((D_MODEL, D_MODEL), w_map),
            pl.BlockSpec((3, D_MODEL), lambda t: (0, 0)),
            tab, tab, tab,
        ],
        out_specs=pl.BlockSpec((1, SEQ, D_MODEL), lambda t: (jnp.maximum(part(t) - 2, 0), sample(t), 0)),
        scratch_shapes=[
            pltpu.VMEM((SEQ, D_MODEL), BF16),
            pltpu.VMEM((SEQ + 16, D_MODEL), F32),
            pltpu.VMEM((2, D_MODEL, D_MODEL), BF16),
        ],
        compiler_params=_params(("arbitrary",)),
        name="inproj",
    )(x, mod3, norm_g, w_in, conv_w, cos, sin_hi, sin_lo)


def _ctxproj_kernel(c_ref, mod_ref, g_ref, w_ref, o_ref, hc_ref):
    j = pl.program_id(0)
    chunks = [slice(r * ROW_CHUNK, (r + 1) * ROW_CHUNK) for r in range(BATCH * CTX_LEN // ROW_CHUNK)]

    @pl.when(j == 0)
    def _():
        shift = mod_ref[0, 0:1, :]
        scale = mod_ref[0, 1:2, :]
        for rows in chunks:
            hc_ref[rows, :] = _norm_modulate(c_ref[rows, :], g_ref[...], shift, scale).astype(BF16)

    w = w_ref[...].astype(BF16)
    for rows in chunks:
        o_ref[0, rows, :] = _dot(hc_ref[rows, :], w).astype(BF16)


def _ctxproj(ctx2, mod3, norm_g, w_in):
    rows = BATCH * CTX_LEN
    return pl.pallas_call(
        _ctxproj_kernel,
        out_shape=jax.ShapeDtypeStruct((2, rows, D_MODEL), BF16),
        grid=(2,),
        in_specs=[
            pl.BlockSpec((rows, D_MODEL), lambda j: (0, 0)),
            pl.BlockSpec((1, 6, D_MODEL), lambda j: (BATCH, 0, 0)),
            pl.BlockSpec((1, D_MODEL), lambda j: (0, 0)),
            pl.BlockSpec((D_MODEL, D_MODEL), lambda j: (0, 4 + j)),
        ],
        out_specs=pl.BlockSpec((1, rows, D_MODEL), lambda j: (j, 0, 0)),
        scratch_shapes=[pltpu.VMEM((rows, D_MODEL), BF16)],
        compiler_params=_params(("arbitrary",)),
        name="ctxproj",
    )(ctx2, mod3, norm_g, w_in)


def _attn_kernel(lp_ref, q_ref, k_ref, v_ref, kc_ref, vc_ref, g_ref, o_ref, k_all, v_ext, s_ref, m_ref):
    lp = lp_ref[...]
    lam = (jnp.exp(jnp.sum(lp[0:1] * lp[1:2], axis=-1, keepdims=True))
           - jnp.exp(jnp.sum(lp[2:3] * lp[3:4], axis=-1, keepdims=True)) + LAM_INIT)
    k_all[0:CTX_LEN, :] = kc_ref[0]
    k_all[CTX_LEN:N_KEYS, :] = k_ref[0]
    v_ext[0:CTX_LEN, 0:LANES] = vc_ref[0]
    v_ext[CTX_LEN:N_KEYS, 0:LANES] = v_ref[0]
    v_ext[:, LANES:2 * LANES] = jnp.ones((N_KEYS, LANES), BF16)
    first_map = lax.broadcasted_iota(jnp.int32, (1, LANES), 1) < HEAD_DIM

    def scores(u):
        q = q_ref[0, u * ATTN_SQ:(u + 1) * ATTN_SQ, :]
        zero = jnp.zeros_like(q)
        for mp, qm in enumerate((jnp.where(first_map, q, zero), jnp.where(first_map, zero, q))):
            s = _dot_nt(qm, k_all[...])
            s_ref[u % 2, mp] = s
            m_ref[u % 2, mp] = jnp.max(s, axis=-1, keepdims=True)

    def values(u):
        o = []
        for mp in range(2):
            p = jnp.exp2(s_ref[u % 2, mp] - m_ref[u % 2, mp])
            o.append(_dot(p.astype(BF16), v_ext[...]))
        a = o[0][:, :LANES] / o[0][:, LANES:] - o[1][:, :LANES] * (lam / o[1][:, LANES:])
        ms = jnp.mean(a * a, axis=-1, keepdims=True)
        o_ref[u * ATTN_SQ:(u + 1) * ATTN_SQ, :] = (
            (a * lax.rsqrt(ms + EPS) * g_ref[...]) * (1.0 - LAM_INIT)).astype(BF16)

    n_units = ATTN_TQ // ATTN_SQ
    scores(0)
    for u in range(n_units):
        if u + 1 < n_units:
            scores(u + 1)
        values(u)


def _attn(lam_params, parts, ctx_kv, subln_g):
    nq = SEQ // ATTN_TQ
    return pl.pallas_call(
        _attn_kernel,
        out_shape=jax.ShapeDtypeStruct((BATCH * SEQ, D_MODEL), BF16),
        grid=(BATCH, N_HEADS, nq),
        in_specs=[
            pl.BlockSpec((4, HEAD_DIM), lambda b, h, i: (0, 0)),
            pl.BlockSpec((1, ATTN_TQ, LANES), lambda b, h, i: (1, b * nq + i, h)),
            pl.BlockSpec((1, SEQ, LANES), lambda b, h, i: (2, b, h)),
            pl.BlockSpec((1, SEQ, LANES), lambda b, h, i: (3, b, h)),
            pl.BlockSpec((1, CTX_LEN, LANES), lambda b, h, i: (0, b, h)),
            pl.BlockSpec((1, CTX_LEN, LANES), lambda b, h, i: (1, b, h)),
            pl.BlockSpec((1, V_DIM), lambda b, h, i: (0, 0)),
        ],
        out_specs=pl.BlockSpec((ATTN_TQ, LANES), lambda b, h, i: (b * nq + i, h)),
        scratch_shapes=[
            pltpu.VMEM((N_KEYS, LANES), BF16),
            pltpu.VMEM((N_KEYS, 2 * LANES), BF16),
            pltpu.VMEM((2, 2, ATTN_SQ, N_KEYS), F32),
            pltpu.VMEM((2, 2, ATTN_SQ, 1), F32),
        ],
        compiler_params=_params(("arbitrary", "arbitrary", "arbitrary")),
        name="attn",
    )(lam_params, parts, parts, parts, ctx_kv, ctx_kv, subln_g)


def _merge_kernel(yb_ref, s6_ref, s7_ref, o_ref, x_ref, mod_ref, g2_ref, wc_ref, wa_ref, wo_ref,
                  wr_ref, x1_ref, h2_ref, lg_ref):
    chunks = [slice(r * ROW_CHUNK, (r + 1) * ROW_CHUNK) for r in range(MERGE_TM // ROW_CHUNK)]
    for rows in chunks:
        y_conv = _dot(yb_ref[0, rows, :], wc_ref[...])
        y_attn = _dot(o_ref[rows, :], wa_ref[...])
        gate_conv = 1.0 / (1.0 + jnp.exp(-s6_ref[0, rows, :].astype(F32)))
        gate_attn = 1.0 / (1.0 + jnp.exp(-s7_ref[0, rows, :].astype(F32)))
        merged = gate_conv * y_conv + gate_attn * y_attn
        mix = _dot(merged.astype(BF16), wo_ref[...])
        x1_ref[rows, :] = x_ref[rows, :] + mod_ref[0, 2:3, :] * mix
    for rows in chunks:
        h2 = _norm_modulate(x1_ref[rows, :], g2_ref[...], mod_ref[0, 3:4, :], mod_ref[0, 4:5, :])
        h_hi = h2.astype(BF16)
        h2_ref[rows, :] = pltpu.pack_elementwise([h2[:, :D_MODEL // 2], h2[:, D_MODEL // 2:]], packed_dtype=BF16)
        h_lo = (h2 - h_hi.astype(F32)).astype(BF16)
        both = _dot_nt(wr_ref[...], h_hi)
        lg_ref[0, :, rows] = both[:N_EXPERTS] + (_dot_nt(wr_ref[0:N_EXPERTS, :], h_lo) + both[N_EXPERTS:])


def _merge(parts, attn_o, x2, mod3, norm2_g, w_conv_bf, w_attn_bf, w_out_bf, wr_hi_lo):
    per_b = SEQ // MERGE_TM
    sq = pl.BlockSpec((D_MODEL, D_MODEL), lambda i: (0, 0), pipeline_mode=pl.Buffered(1))
    wr = pl.BlockSpec((2 * N_EXPERTS, D_MODEL), lambda i: (0, 0), pipeline_mode=pl.Buffered(1))
    tile = pl.BlockSpec((MERGE_TM, D_MODEL), lambda i: (i, 0))
    return pl.pallas_call(
        _merge_kernel,
        out_shape=(
            jax.ShapeDtypeStruct((BATCH * SEQ, D_MODEL), F32),
            jax.ShapeDtypeStruct((BATCH * SEQ, D_MODEL // 2), jnp.uint32),
            jax.ShapeDtypeStruct((BATCH, N_EXPERTS, SEQ), F32),
        ),
        grid=(BATCH * per_b,),
        in_specs=[
            pl.BlockSpec((1, MERGE_TM, D_MODEL), lambda i: (0, i, 0)),
            pl.BlockSpec((1, MERGE_TM, D_MODEL), lambda i: (4, i, 0)),
            pl.BlockSpec((1, MERGE_TM, D_MODEL), lambda i: (5, i, 0)),
            tile, tile,
            pl.BlockSpec((1, 6, D_MODEL), lambda i: (i // per_b, 0, 0)),
            pl.BlockSpec((1, D_MODEL), lambda i: (0, 0)),
            sq, sq, sq, wr,
        ],
        out_specs=(
            tile,
            pl.BlockSpec((MERGE_TM, D_MODEL // 2), lambda i: (i, 0)),
            pl.BlockSpec((1, N_EXPERTS, MERGE_TM), lambda i: (i // per_b, 0, i % per_b)),
        ),
        compiler_params=_params(("arbitrary",)),
        name="merge",
    )(parts, parts, parts, attn_o, x2, mod3, norm2_g, w_conv_bf, w_attn_bf, w_out_bf, wr_hi_lo)


def _route_kernel(lg_ref, rank_ref, gate_ref, bounds_ref):
    lg = lg_ref[...]
    ex = jnp.exp(lg - jnp.max(lg, axis=1, keepdims=True))
    aff = (ex / jnp.sum(ex, axis=1, keepdims=True)).reshape(BATCH * N_EXPERTS, SEQ)
    bits = lax.bitcast_convert_type(aff, jnp.int32)

    def count(mask):
        return jnp.sum(jnp.where(mask, 1.0, 0.0), axis=1, keepdims=True)

    def search(i, t):
        cand = t + lax.shift_left(jnp.int32(1), 30 - i)
        return jnp.where(count(bits >= cand) >= CAPACITY, cand, t)

    thr = lax.fori_loop(0, 31, search, jnp.zeros((BATCH * N_EXPERTS, 1), jnp.int32))
    need = CAPACITY - count(bits > thr)
    before = (lax.broadcasted_iota(jnp.int32, (LANES, LANES), 0)
              < lax.broadcasted_iota(jnp.int32, (LANES, LANES), 1)).astype(BF16)
    n_tied = jnp.zeros((BATCH * N_EXPERTS, 1), F32)
    n_sel = jnp.zeros((BATCH * N_EXPERTS, 1), F32)
    per_tile = COMBINE_SUB // LANES
    firsts, ends = [], []
    for c in range(SEQ // LANES):
        if c % per_tile == 0:
            firsts.append(n_sel)
        cols = slice(c * LANES, (c + 1) * LANES)
        bits_c = bits[:, cols]
        tied_c = jnp.where(bits_c == thr, 1.0, 0.0)
        tied_before = _dot(tied_c.astype(BF16), before) + n_tied
        sel_c = jnp.where(bits_c > thr, 1.0, jnp.where(tied_before < need, tied_c, 0.0))
        rank = _dot(sel_c.astype(BF16), before) + n_sel
        rank_c = jnp.where(sel_c > 0.0, rank, -1.0)
        gate_c = jnp.where(sel_c > 0.0, aff[:, cols], 0.0)
        n_tied = n_tied + jnp.sum(tied_c, axis=1, keepdims=True)
        n_sel = n_sel + jnp.sum(sel_c, axis=1, keepdims=True)
        rank_ref[:, cols] = rank_c.astype(jnp.int32)
        gate_ref[:, cols] = gate_c
        if c % per_tile == per_tile - 1:
            ends.append(n_sel)
    bounds_ref[...] = jnp.concatenate(firsts + ends, axis=1).astype(jnp.int32)


def _route(logits_t):
    return pl.pallas_call(
        _route_kernel,
        out_shape=(
            jax.ShapeDtypeStruct((BATCH * N_EXPERTS, SEQ), jnp.int32),
            jax.ShapeDtypeStruct((BATCH * N_EXPERTS, SEQ), F32),
            jax.ShapeDtypeStruct((BATCH * N_EXPERTS, 2 * (SEQ // COMBINE_SUB)), jnp.int32),
        ),
        compiler_params=pltpu.CompilerParams(vmem_limit_bytes=VMEM_LIMIT),
        name="route",
    )(logits_t)


def _dispatch(rank_e, h_packed, first_expert, n_experts):
    per_worker = n_experts * BATCH // (SC_CORES * SC_SUBCORES)
    assert per_worker * SC_CORES * SC_SUBCORES == n_experts * BATCH
    half = CAPACITY // 2
    mesh = plsc.VectorSubcoreMesh(core_axis_name="c", subcore_axis_name="s")

    @functools.partial(
        pl.kernel, mesh=mesh,
        out_type=jax.ShapeDtypeStruct((n_experts * BATCH * CAPACITY, D_MODEL // 2), h_packed.dtype),
        scratch_types=[
            pltpu.VMEM((SEQ,), jnp.int32),
            pltpu.VMEM((2, half), jnp.int32),
            pltpu.VMEM((half, D_MODEL // 2), h_packed.dtype),
            pltpu.SemaphoreType.DMA,
        ],
        compiler_params=pltpu.CompilerParams(needs_layout_passes=False),
        name="dispatch",
    )
    def k(table_hbm, rank_hbm, out_hbm, rank_v, idx_v, rows_v, sem):
        worker = lax.axis_index("s") * SC_CORES + lax.axis_index("c")

        @pl.loop(0, per_worker)
        def _(p):
            pair = worker * per_worker + p
            expert = lax.shift_right_logical(pair, BATCH.bit_length() - 1)
            sample = pair & (BATCH - 1)
            pltpu.sync_copy(rank_hbm.at[sample * N_EXPERTS + first_expert + expert], rank_v)

            @pl.loop(0, SEQ // SC_LANES)
            def _(i):
                rank = rank_v[pl.ds(i * SC_LANES, SC_LANES)]
                row = lax.iota(jnp.int32, SC_LANES) + (i * SC_LANES + sample * SEQ)
                plsc.store_scatter(idx_v, [lax.shift_right_arithmetic(rank, 7), rank & (half - 1)], row,
                                   mask=rank >= 0)

            for c in range(2):
                pltpu.async_copy(table_hbm.at[idx_v.at[c]], rows_v, sem).wait()
                pltpu.sync_copy(rows_v, out_hbm.at[pl.ds(pair * CAPACITY + c * half, half)])

    return k(h_packed, rank_e)


def _expert_kernel(xs_ref, wg_ref, wu_ref, wd_ref, y_ref, acc_ref, wg_bf, wu_bf, wd_bf):
    t = pl.program_id(0)
    n_f = EXPERT_HIDDEN // FFN_TF
    f = (t + n_f - 1) % n_f

    n_chunks = BATCH * CAPACITY // ROW_CHUNK

    def stage(slot, part):
        up = slice(part * D_MODEL // n_chunks, (part + 1) * D_MODEL // n_chunks)
        down = slice(part * FFN_TF // n_chunks, (part + 1) * FFN_TF // n_chunks)
        wg_bf[slot, up, :] = wg_ref[0, up, :].astype(BF16)
        wu_bf[slot, up, :] = wu_ref[0, up, :].astype(BF16)
        wd_bf[slot, down, :] = wd_ref[0, down, :].astype(BF16)

    def step(cur):
        for r in range(n_chunks):
            stage(1 - cur, r)
            rows = slice(r * ROW_CHUNK, (r + 1) * ROW_CHUNK)
            packed = xs_ref[0, rows, :]
            half = D_MODEL // 2
            x_lo = pltpu.unpack_elementwise(packed, index=0, packed_dtype=BF16, unpacked_dtype=F32).astype(BF16)
            x_hi = pltpu.unpack_elementwise(packed, index=1, packed_dtype=BF16, unpacked_dtype=F32).astype(BF16)
            a = _dot(x_lo, wg_bf[cur, :half, :]) + _dot(x_hi, wg_bf[cur, half:, :])
            u = _dot(x_lo, wu_bf[cur, :half, :]) + _dot(x_hi, wu_bf[cur, half:, :])
            hidden = (a / (1.0 + jnp.exp(-a))) * u
            total = jnp.where(f == 0, 0.0, acc_ref[rows, :]) + _dot(hidden.astype(BF16), wd_bf[cur])
            acc_ref[rows, :] = total
            y_ref[0, rows, :] = total.astype(BF16)

    @pl.when(t == 0)
    def _():
        for part in range(n_chunks):
            stage(0, part)

    @pl.when(t % 2 == 1)
    def _():
        step(0)

    @pl.when((t % 2 == 0) & (t > 0))
    def _():
        step(1)


def _experts(xs, w_gate, w_up, w_down, first_expert):
    rows = BATCH * CAPACITY
    n_f = EXPERT_HIDDEN // FFN_TF
    n_experts = xs.shape[0]
    last = n_experts * n_f - 1

    def staged(t):
        return jnp.minimum(t, last)

    def computed(t):
        return jnp.maximum(t - 1, 0) // n_f

    return pl.pallas_call(
        _expert_kernel,
        out_shape=jax.ShapeDtypeStruct((n_experts, rows, D_MODEL), BF16),
        grid=(n_experts * n_f + 1,),
        in_specs=[
            pl.BlockSpec((1, rows, D_MODEL // 2), lambda t: (computed(t), 0, 0)),
            pl.BlockSpec((1, D_MODEL, FFN_TF), lambda t: (first_expert + staged(t) // n_f, 0, staged(t) % n_f)),
            pl.BlockSpec((1, D_MODEL, FFN_TF), lambda t: (first_expert + staged(t) // n_f, 0, staged(t) % n_f)),
            pl.BlockSpec((1, FFN_TF, D_MODEL), lambda t: (first_expert + staged(t) // n_f, staged(t) % n_f, 0)),
        ],
        out_specs=pl.BlockSpec((1, rows, D_MODEL), lambda t: (computed(t), 0, 0)),
        scratch_shapes=[
            pltpu.VMEM((rows, D_MODEL), F32),
            pltpu.VMEM((2, D_MODEL, FFN_TF), BF16),
            pltpu.VMEM((2, D_MODEL, FFN_TF), BF16),
            pltpu.VMEM((2, FFN_TF, D_MODEL), BF16),
        ],
        compiler_params=_params(("arbitrary",)),
        name="experts",
    )(xs, w_gate, w_up, w_down)


def _combine_kernel(bounds_ref, rank_ref, gate_ref, *refs):
    n_groups = len(EXPERT_GROUPS)
    y_refs, (x1_ref, mod_ref, fg_ref, o_ref) = refs[:n_groups], refs[n_groups:]
    group_of = [(g, e) for g, n in enumerate(EXPERT_GROUPS) for e in range(n)]
    b, t = pl.program_id(0), pl.program_id(1)
    subs = COMBINE_TQ // COMBINE_SUB
    n_sub = SEQ // COMBINE_SUB
    per_dot = CAPACITY // COMBINE_WINDOW

    def y_rows(e):
        return y_refs[group_of[e][0]].at[group_of[e][1]]

    def onehot_t(e, cols, first, n_slots):
        slot = lax.broadcasted_iota(jnp.int32, (n_slots, cols.stop - cols.start), 0) + first
        return jnp.where(rank_ref[e:e + 1, cols] == slot, gate_ref[e:e + 1, cols], 0.0).astype(BF16)

    def dot_t(a, b_):
        return lax.dot_general(a, b_, (((0,), (0,)), ((), ())), preferred_element_type=F32)

    def finish(rows, acc):
        x2 = x1_ref[rows, :] + mod_ref[0, 5:6, :] * acc
        ms = jnp.mean(x2 * x2, axis=-1, keepdims=True)
        o_ref[rows, :] = x2 * lax.rsqrt(ms + EPS) * fg_ref[...]

    starts, fits = {}, None
    for s in range(subs):
        for e in range(N_EXPERTS):
            pair = (b * N_EXPERTS + e) * (2 * n_sub) + t * subs + s
            first, end = bounds_ref[pair], bounds_ref[pair + n_sub]
            start = jnp.minimum(lax.shift_left(lax.shift_right_logical(first, 4), 4), CAPACITY - COMBINE_WINDOW)
            starts[s, e] = start
            ok = end <= start + COMBINE_WINDOW
            fits = ok if fits is None else fits & ok

    @pl.when(fits)
    def _():
        for s in range(subs):
            rows = slice(s * COMBINE_SUB, (s + 1) * COMBINE_SUB)
            acc = jnp.zeros((COMBINE_SUB, D_MODEL), F32)
            for e0 in range(0, N_EXPERTS, per_dot):
                first = [pl.multiple_of(starts[s, e0 + i], 16) for i in range(per_dot)]
                onehots = [onehot_t(e0 + i, rows, first[i], COMBINE_WINDOW) for i in range(per_dot)]
                y_win = [y_rows(e0 + i)[pl.ds(first[i], COMBINE_WINDOW), :] for i in range(per_dot)]
                acc = acc + dot_t(jnp.concatenate(onehots, axis=0), jnp.concatenate(y_win, axis=0))
            finish(rows, acc)

    @pl.when(jnp.logical_not(fits))
    def _():
        acc = jnp.zeros((COMBINE_TQ, D_MODEL), F32)
        for e in range(N_EXPERTS):
            acc = acc + dot_t(onehot_t(e, slice(0, COMBINE_TQ), 0, CAPACITY), y_rows(e)[...])
        finish(slice(0, COMBINE_TQ), acc)


def _combine(bounds, rank_e, gate_e, ys, x1, mod3, final_g):
    per_b = SEQ // COMBINE_TQ
    return pl.pallas_call(
        _combine_kernel,
        out_shape=jax.ShapeDtypeStruct((BATCH * SEQ, D_MODEL), F32),
        grid_spec=pltpu.PrefetchScalarGridSpec(
            num_scalar_prefetch=1,
            grid=(BATCH, per_b),
            in_specs=[
                pl.BlockSpec((N_EXPERTS, COMBINE_TQ), lambda b, t, _: (b, t)),
                pl.BlockSpec((N_EXPERTS, COMBINE_TQ), lambda b, t, _: (b, t)),
                *[pl.BlockSpec((y.shape[0], CAPACITY, D_MODEL), lambda b, t, _: (0, b, 0)) for y in ys],
                pl.BlockSpec((COMBINE_TQ, D_MODEL), lambda b, t, _: (b * per_b + t, 0)),
                pl.BlockSpec((1, 6, D_MODEL), lambda b, t, _: (b, 0, 0)),
                pl.BlockSpec((1, D_MODEL), lambda b, t, _: (0, 0)),
            ],
            out_specs=pl.BlockSpec((COMBINE_TQ, D_MODEL), lambda b, t, _: (b * per_b + t, 0)),
        ),
        compiler_params=_params(("arbitrary", "arbitrary")),
        name="combine",
    )(bounds.reshape(-1), rank_e, gate_e, *ys, x1, mod3, final_g)


def _rope_tables():
    t = np.arange(SEQ)
    pos = np.stack([t // GRID_W, t % GRID_W], axis=1).astype(np.float32)
    per_axis = HEAD_DIM // 2
    inv = np.float32(ROPE_THETA) ** (-np.arange(0, per_axis, 2, dtype=np.float32) / np.float32(per_axis))
    ang = (pos[:, :, None] * inv).astype(np.float32)
    cos, sin = np.cos(ang.astype(np.float64)), np.sin(ang.astype(np.float64))
    zero = np.zeros_like(sin)

    def lanes(first_half, second_half):
        one_map = np.stack([first_half, second_half], axis=2).reshape(SEQ, HEAD_DIM)
        return jnp.asarray(np.concatenate([one_map, one_map], axis=1), F32)

    return lanes(cos, cos), lanes(-sin, zero), lanes(zero, sin)


def kernel(x, c, ctx, c_ctx, norm1_g, norm2_g, w_ada, b_ada, w_in, conv_w, w_out_conv, lambda_q1,
           lambda_k1, lambda_q2, lambda_k2, subln_g, w_o_attn, w_out, w_router, w_gate_e, w_up_e,
           w_down_e, final_g):
    cond = jnp.concatenate([c, c_ctx[None, :], jnp.zeros((COND_ROWS - BATCH - 1, D_MODEL), F32)], axis=0)
    mod3 = _ada(cond, w_ada[0], b_ada).reshape(COND_ROWS, 6, D_MODEL)

    cos, sin_hi, sin_lo = _rope_tables()
    q_scale = math.log2(math.e) * HEAD_DIM ** -0.5
    parts = _inproj(x, mod3, norm1_g, w_in[0], conv_w[0], cos, sin_hi, sin_lo, q_scale)
    ctx_kv = _ctxproj(ctx.reshape(BATCH * CTX_LEN, D_MODEL), mod3, norm1_g, w_in[0])

    lam_params = jnp.concatenate([lambda_q1, lambda_k1, lambda_q2, lambda_k2], axis=0)
    attn_o = _attn(lam_params, parts, ctx_kv, subln_g)

    wr_t = w_router[0].T
    wr_hi = wr_t.astype(BF16)
    wr_lo = (wr_t - wr_hi.astype(F32)).astype(BF16)
    x1, h2, logits_t = _merge(parts, attn_o, x.reshape(BATCH * SEQ, D_MODEL), mod3, norm2_g,
                              w_out_conv[0].astype(BF16), w_o_attn[0].astype(BF16), w_out[0].astype(BF16),
                              jnp.concatenate([wr_hi, wr_lo], axis=0))

    rank_e, gate_e, bounds = _route(logits_t)
    ys, first = [], 0
    for n in EXPERT_GROUPS:
        xs = _dispatch(rank_e, h2, first, n).reshape(n, BATCH * CAPACITY, D_MODEL // 2)
        ys.append(_experts(xs, w_gate_e[0], w_up_e[0], w_down_e[0], first))
        first += n
    out = _combine(bounds, rank_e, gate_e, ys, x1, mod3, final_g[None, :])
    return out.reshape(BATCH, SEQ, D_MODEL)
```

```python
import functools
import math

import jax
import jax.numpy as jnp
import numpy as np
from jax import lax
from jax.experimental import pallas as pl
from jax.experimental.pallas import tpu as pltpu
from jax.experimental.pallas import tpu_sc as plsc

D_MODEL = 1024
BATCH = 8
SEQ = 2048
GRID_W = 64
CTX_LEN = 256
N_HEADS = 8
HEAD_DIM = 64
V_DIM = 2 * HEAD_DIM
N_EXPERTS = 16
EXPERT_HIDDEN = 2048
CAPACITY = 2 * SEQ // N_EXPERTS
ROPE_THETA = 10000.0
EPS = 1e-6
LAM_INIT = 0.8 - 0.6 * math.exp(-0.3 * 0)
N_PARTS = 8
N_KEYS = CTX_LEN + SEQ
EXPERT_GROUPS = (4, 12)

LANES = 128
SC_CORES = 2
SC_SUBCORES = 16
SC_LANES = 16
F32 = jnp.float32
BF16 = jnp.bfloat16

ROW_CHUNK = 512
COND_ROWS = 16
W_RING = 3
ADA_TN = 1024
ATTN_TQ = 2048
ATTN_SQ = 256
MERGE_TM = 1024
FFN_TF = 512
COMBINE_TQ = 1024
COMBINE_SUB = 256
COMBINE_WINDOW = 64
VMEM_LIMIT = 56 * 1024 * 1024
INPROJ_VMEM_LIMIT = 58 * 1024 * 1024


def _dot(a, b):
    return jnp.dot(a, b, preferred_element_type=F32)


def _dot_nt(a, b):
    return lax.dot_general(a, b, (((1,), (1,)), ((), ())), preferred_element_type=F32)


def _params(sem, vmem=VMEM_LIMIT):
    return pltpu.CompilerParams(dimension_semantics=sem, vmem_limit_bytes=vmem)


def _ada_kernel(cond_ref, w_ref, b_ref, o_ref):
    c = cond_ref[...]
    s = c / (1.0 + jnp.exp(-c))
    o_ref[...] = _dot(s.astype(BF16), w_ref[...].astype(BF16)) + b_ref[...]


def _ada(cond, w_ada, b_ada):
    n = w_ada.shape[1]
    return pl.pallas_call(
        _ada_kernel,
        out_shape=jax.ShapeDtypeStruct((COND_ROWS, n), F32),
        grid=(n // ADA_TN,),
        in_specs=[
            pl.BlockSpec((COND_ROWS, D_MODEL), lambda i: (0, 0)),
            pl.BlockSpec((D_MODEL, ADA_TN), lambda i: (0, i)),
            pl.BlockSpec((1, ADA_TN), lambda i: (0, i)),
        ],
        out_specs=pl.BlockSpec((COND_ROWS, ADA_TN), lambda i: (0, i)),
        compiler_params=_params(("arbitrary",)),
        name="ada",
    )(cond, w_ada, b_ada)


def _norm_modulate(x, g, shift, scale):
    ms = jnp.mean(x * x, axis=-1, keepdims=True)
    return (x * lax.rsqrt(ms + EPS) * g) * (1.0 + scale) + shift


def _rope(a, cos, sin_hi, sin_lo):
    return a * cos + pltpu.roll(a, LANES - 16, 1) * sin_hi + pltpu.roll(a, 16, 1) * sin_lo


def _inproj_weight_block(s):
    nxt = jnp.minimum(s, BATCH * N_PARTS - 1) % N_PARTS
    return jnp.where(nxt == 1, 2, jnp.where(nxt == 2, 1, nxt))


def _inproj_kernel(x_ref, mod_ref, g_ref, w_hbm, cw_ref, cos_ref, shi_ref, slo_ref, o_ref,
                   hx_ref, z_ref, w_bf, w_f32, w_sem, *, q_scale):
    t = pl.program_id(0)
    j = (t + N_PARTS - 1) % N_PARTS
    chunks = [slice(r * ROW_CHUNK, (r + 1) * ROW_CHUNK) for r in range(SEQ // ROW_CHUNK)]
    ring_slot = t % W_RING

    def weight_copy(s):
        cols = pl.ds(pl.multiple_of(_inproj_weight_block(s) * D_MODEL, D_MODEL), D_MODEL)
        return pltpu.make_async_copy(w_hbm.at[:, cols], w_f32.at[s % W_RING], w_sem.at[s % W_RING])

    @pl.when(t == 0)
    def _():
        for s in range(W_RING - 1):
            weight_copy(s).start()

    @pl.when(t + W_RING - 1 <= BATCH * N_PARTS)
    def _():
        weight_copy(t + W_RING - 1).start()

    weight_copy(t).wait()

    def stage(slot, part):
        rows = slice(part * D_MODEL // len(chunks), (part + 1) * D_MODEL // len(chunks))
        w_bf[slot, rows, :] = w_f32[ring_slot, rows, :].astype(BF16)

    def proj(part, r):
        stage(1 - part % 2, r)
        return _dot(hx_ref[chunks[r], :], w_bf[part % 2])

    def branch(part):
        return pl.when((t > 0) & (j == part))

    @pl.when(t == 0)
    def _():
        for r in range(len(chunks)):
            stage(0, r)

    @branch(0)
    def _():
        shift = mod_ref[0, 0:1, :]
        scale = mod_ref[0, 1:2, :]
        for rows in chunks:
            hx_ref[rows, :] = _norm_modulate(x_ref[0, rows, :], g_ref[...], shift, scale).astype(BF16)
        z_ref[0:8, :] = jnp.zeros((8, D_MODEL), F32)
        z_ref[SEQ + 8:SEQ + 16, :] = jnp.zeros((8, D_MODEL), F32)
        for r, rows in enumerate(chunks):
            z_ref[pl.ds(rows.start + 8, ROW_CHUNK), :] = proj(0, r)

    @branch(1)
    def _():
        for r, rows in enumerate(chunks):
            zr = pl.ds(rows.start + 8, ROW_CHUNK)
            z_ref[zr, :] = z_ref[zr, :] * proj(1, r)

    @branch(2)
    def _():
        first_row = lax.broadcasted_iota(jnp.int32, (8, 1), 0) == 0
        last_row = lax.broadcasted_iota(jnp.int32, (8, 1), 0) == 7
        for r, rows in enumerate(chunks):
            base = rows.start + 8
            zc = z_ref[pl.ds(base, ROW_CHUNK), :]
            before = pltpu.roll(zc * cw_ref[0:1, :], 1, 0)
            after = pltpu.roll(zc * cw_ref[2:3, :], ROW_CHUNK - 1, 0)
            edge_b = jnp.where(first_row, z_ref[pl.ds(base - 8, 8), :][7:8, :] * cw_ref[0:1, :], before[0:8, :])
            edge_a = jnp.where(last_row, z_ref[pl.ds(base + ROW_CHUNK, 8), :][0:1, :] * cw_ref[2:3, :],
                               after[ROW_CHUNK - 8:, :])
            before = jnp.concatenate([edge_b, before[8:, :]], axis=0)
            after = jnp.concatenate([after[:ROW_CHUNK - 8, :], edge_a], axis=0)
            y = before + zc * cw_ref[1:2, :] + after
            o_ref[0, rows, :] = (proj(2, r) * y).astype(BF16)

    def rope_part(part, scale):
        for r, rows in enumerate(chunks):
            acc = proj(part, r)
            cos, shi, slo = cos_ref[rows, :], shi_ref[rows, :], slo_ref[rows, :]
            for h in range(D_MODEL // LANES):
                cols = slice(h * LANES, (h + 1) * LANES)
                roped = _rope(acc[:, cols], cos, shi, slo)
                if scale is not None:
                    roped = roped * scale
                o_ref[0, rows, cols] = roped.astype(BF16)

    @branch(3)
    def _():
        rope_part(3, q_scale)

    @branch(4)
    def _():
        rope_part(4, None)

    for plain_part in (5, 6, 7):
        @branch(plain_part)
        def _(plain_part=plain_part):
            for r, rows in enumerate(chunks):
                o_ref[0, rows, :] = proj(plain_part, r).astype(BF16)


def _inproj(x, mod3, norm_g, w_in, conv_w, cos, sin_hi, sin_lo, q_scale):
    def part(t):
        return jnp.maximum(t - 1, 0) % N_PARTS

    def sample(t):
        return jnp.maximum(t - 1, 0) // N_PARTS

    tab = pl.BlockSpec((SEQ, LANES), lambda t: (0, 0), pipeline_mode=pl.Buffered(1))
    return pl.pallas_call(
        functools.partial(_inproj_kernel, q_scale=q_scale),
        out_shape=jax.ShapeDtypeStruct((N_PARTS - 2, BATCH * SEQ, D_MODEL), BF16),
        grid=(BATCH * N_PARTS + 1,),
        in_specs=[
            pl.BlockSpec((1, SEQ, D_MODEL), lambda t: (sample(t), 0, 0)),
            pl.BlockSpec((1, 6, D_MODEL), lambda t: (sample(t), 0, 0)),
            pl.BlockSpec((1, D_MODEL), lambda t: (0, 0)),
            pl.BlockSpec(memory_space=pl.ANY),
            pl.BlockSpec((3, D_MODEL), lambda t: (0, 0)),
            tab, tab, tab,
        ],
        out_specs=pl.BlockSpec((1, SEQ, D_MODEL), lambda t: (jnp.maximum(part(t) - 2, 0), sample(t), 0)),
        scratch_shapes=[
            pltpu.VMEM((SEQ, D_MODEL), BF16),
            pltpu.VMEM((SEQ + 16, D_MODEL), F32),
            pltpu.VMEM((2, D_MODEL, D_MODEL), BF16),
            pltpu.VMEM((W_RING, D_MODEL, D_MODEL), F32),
            pltpu.SemaphoreType.DMA((W_RING,)),
        ],
        compiler_params=_params(("arbitrary",), vmem=INPROJ_VMEM_LIMIT),
        name="inproj",
    )(x, mod3, norm_g, w_in, conv_w, cos, sin_hi, sin_lo)


def _ctxproj_kernel(c_ref, mod_ref, g_ref, w_ref, o_ref, hc_ref):
    j = pl.program_id(0)
    chunks = [slice(r * ROW_CHUNK, (r + 1) * ROW_CHUNK) for r in range(BATCH * CTX_LEN // ROW_CHUNK)]

    @pl.when(j == 0)
    def _():
        shift = mod_ref[0, 0:1, :]
        scale = mod_ref[0, 1:2, :]
        for rows in chunks:
            hc_ref[rows, :] = _norm_modulate(c_ref[rows, :], g_ref[...], shift, scale).astype(BF16)

    w = w_ref[...].astype(BF16)
    for rows in chunks:
        o_ref[0, rows, :] = _dot(hc_ref[rows, :], w).astype(BF16)


def _ctxproj(ctx2, mod3, norm_g, w_in):
    rows = BATCH * CTX_LEN
    return pl.pallas_call(
        _ctxproj_kernel,
        out_shape=jax.ShapeDtypeStruct((2, rows, D_MODEL), BF16),
        grid=(2,),
        in_specs=[
            pl.BlockSpec((rows, D_MODEL), lambda j: (0, 0)),
            pl.BlockSpec((1, 6, D_MODEL), lambda j: (BATCH, 0, 0)),
            pl.BlockSpec((1, D_MODEL), lambda j: (0, 0)),
            pl.BlockSpec((D_MODEL, D_MODEL), lambda j: (0, 4 + j)),
        ],
        out_specs=pl.BlockSpec((1, rows, D_MODEL), lambda j: (j, 0, 0)),
        scratch_shapes=[pltpu.VMEM((rows, D_MODEL), BF16)],
        compiler_params=_params(("arbitrary",)),
        name="ctxproj",
    )(ctx2, mod3, norm_g, w_in)


def _attn_kernel(lp_ref, q_ref, k_ref, v_ref, kc_ref, vc_ref, g_ref, o_ref, k_all, v_ext, s_ref, m_ref):
    lp = lp_ref[...]
    lam = (jnp.exp(jnp.sum(lp[0:1] * lp[1:2], axis=-1, keepdims=True))
           - jnp.exp(jnp.sum(lp[2:3] * lp[3:4], axis=-1, keepdims=True)) + LAM_INIT)
    k_all[0:CTX_LEN, :] = kc_ref[0]
    k_all[CTX_LEN:N_KEYS, :] = k_ref[0]
    v_ext[0:CTX_LEN, 0:LANES] = vc_ref[0]
    v_ext[CTX_LEN:N_KEYS, 0:LANES] = v_ref[0]
    v_ext[:, LANES:2 * LANES] = jnp.ones((N_KEYS, LANES), BF16)
    first_map = lax.broadcasted_iota(jnp.int32, (1, LANES), 1) < HEAD_DIM

    def scores(u):
        q = q_ref[0, u * ATTN_SQ:(u + 1) * ATTN_SQ, :]
        zero = jnp.zeros_like(q)
        for mp, qm in enumerate((jnp.where(first_map, q, zero), jnp.where(first_map, zero, q))):
            s = _dot_nt(qm, k_all[...])
            s_ref[u % 2, mp] = s
            m_ref[u % 2, mp] = jnp.max(s, axis=-1, keepdims=True)

    def values(u):
        o = []
        for mp in range(2):
            p = jnp.exp2(s_ref[u % 2, mp] - m_ref[u % 2, mp])
            o.append(_dot(p.astype(BF16), v_ext[...]))
        a = o[0][:, :LANES] / o[0][:, LANES:] - o[1][:, :LANES] * (lam / o[1][:, LANES:])
        ms = jnp.mean(a * a, axis=-1, keepdims=True)
        o_ref[u * ATTN_SQ:(u + 1) * ATTN_SQ, :] = (
            (a * lax.rsqrt(ms + EPS) * g_ref[...]) * (1.0 - LAM_INIT)).astype(BF16)

    n_units = ATTN_TQ // ATTN_SQ
    scores(0)
    for u in range(n_units):
        if u + 1 < n_units:
            scores(u + 1)
        values(u)


def _attn(lam_params, parts, ctx_kv, subln_g):
    nq = SEQ // ATTN_TQ
    return pl.pallas_call(
        _attn_kernel,
        out_shape=jax.ShapeDtypeStruct((BATCH * SEQ, D_MODEL), BF16),
        grid=(BATCH, N_HEADS, nq),
        in_specs=[
            pl.BlockSpec((4, HEAD_DIM), lambda b, h, i: (0, 0)),
            pl.BlockSpec((1, ATTN_TQ, LANES), lambda b, h, i: (1, b * nq + i, h)),
            pl.BlockSpec((1, SEQ, LANES), lambda b, h, i: (2, b, h)),
            pl.BlockSpec((1, SEQ, LANES), lambda b, h, i: (3, b, h)),
            pl.BlockSpec((1, CTX_LEN, LANES), lambda b, h, i: (0, b, h)),
            pl.BlockSpec((1, CTX_LEN, LANES), lambda b, h, i: (1, b, h)),
            pl.BlockSpec((1, V_DIM), lambda b, h, i: (0, 0)),
        ],
        out_specs=pl.BlockSpec((ATTN_TQ, LANES), lambda b, h, i: (b * nq + i, h)),
        scratch_shapes=[
            pltpu.VMEM((N_KEYS, LANES), BF16),
            pltpu.VMEM((N_KEYS, 2 * LANES), BF16),
            pltpu.VMEM((2, 2, ATTN_SQ, N_KEYS), F32),
            pltpu.VMEM((2, 2, ATTN_SQ, 1), F32),
        ],
        compiler_params=_params(("arbitrary", "arbitrary", "arbitrary")),
        name="attn",
    )(lam_params, parts, parts, parts, ctx_kv, ctx_kv, subln_g)


def _merge_kernel(yb_ref, s6_ref, s7_ref, o_ref, x_ref, mod_ref, g2_ref, wc_ref, wa_ref, wo_ref,
                  wr_ref, x1_ref, h2_ref, lg_ref):
    chunks = [slice(r * ROW_CHUNK, (r + 1) * ROW_CHUNK) for r in range(MERGE_TM // ROW_CHUNK)]
    for rows in chunks:
        y_conv = _dot(yb_ref[0, rows, :], wc_ref[...])
        y_attn = _dot(o_ref[rows, :], wa_ref[...])
        gate_conv = 1.0 / (1.0 + jnp.exp(-s6_ref[0, rows, :].astype(F32)))
        gate_attn = 1.0 / (1.0 + jnp.exp(-s7_ref[0, rows, :].astype(F32)))
        merged = gate_conv * y_conv + gate_attn * y_attn
        mix = _dot(merged.astype(BF16), wo_ref[...])
        x1_ref[rows, :] = x_ref[rows, :] + mod_ref[0, 2:3, :] * mix
    for rows in chunks:
        h2 = _norm_modulate(x1_ref[rows, :], g2_ref[...], mod_ref[0, 3:4, :], mod_ref[0, 4:5, :])
        h_hi = h2.astype(BF16)
        h2_ref[rows, :] = pltpu.pack_elementwise([h2[:, :D_MODEL // 2], h2[:, D_MODEL // 2:]], packed_dtype=BF16)
        h_lo = (h2 - h_hi.astype(F32)).astype(BF16)
        both = _dot_nt(wr_ref[...], h_hi)
        lg_ref[0, :, rows] = both[:N_EXPERTS] + (_dot_nt(wr_ref[0:N_EXPERTS, :], h_lo) + both[N_EXPERTS:])


def _merge(parts, attn_o, x2, mod3, norm2_g, w_conv_bf, w_attn_bf, w_out_bf, wr_hi_lo):
    per_b = SEQ // MERGE_TM
    sq = pl.BlockSpec((D_MODEL, D_MODEL), lambda i: (0, 0), pipeline_mode=pl.Buffered(1))
    wr = pl.BlockSpec((2 * N_EXPERTS, D_MODEL), lambda i: (0, 0), pipeline_mode=pl.Buffered(1))
    tile = pl.BlockSpec((MERGE_TM, D_MODEL), lambda i: (i, 0))
    return pl.pallas_call(
        _merge_kernel,
        out_shape=(
            jax.ShapeDtypeStruct((BATCH * SEQ, D_MODEL), F32),
            jax.ShapeDtypeStruct((BATCH * SEQ, D_MODEL // 2), jnp.uint32),
            jax.ShapeDtypeStruct((BATCH, N_EXPERTS, SEQ), F32),
        ),
        grid=(BATCH * per_b,),
        in_specs=[
            pl.BlockSpec((1, MERGE_TM, D_MODEL), lambda i: (0, i, 0)),
            pl.BlockSpec((1, MERGE_TM, D_MODEL), lambda i: (4, i, 0)),
            pl.BlockSpec((1, MERGE_TM, D_MODEL), lambda i: (5, i, 0)),
            tile, tile,
            pl.BlockSpec((1, 6, D_MODEL), lambda i: (i // per_b, 0, 0)),
            pl.BlockSpec((1, D_MODEL), lambda i: (0, 0)),
            sq, sq, sq, wr,
        ],
        out_specs=(
            tile,
            pl.BlockSpec((MERGE_TM, D_MODEL // 2), lambda i: (i, 0)),
            pl.BlockSpec((1, N_EXPERTS, MERGE_TM), lambda i: (i // per_b, 0, i % per_b)),
        ),
        compiler_params=_params(("arbitrary",)),
        name="merge",
    )(parts, parts, parts, attn_o, x2, mod3, norm2_g, w_conv_bf, w_attn_bf, w_out_bf, wr_hi_lo)


def _route_kernel(lg_ref, rank_ref, gate_ref, bounds_ref):
    lg = lg_ref[...]
    ex = jnp.exp(lg - jnp.max(lg, axis=1, keepdims=True))
    aff = (ex / jnp.sum(ex, axis=1, keepdims=True)).reshape(BATCH * N_EXPERTS, SEQ)
    bits = lax.bitcast_convert_type(aff, jnp.int32)

    def count(mask):
        return jnp.sum(jnp.where(mask, 1.0, 0.0), axis=1, keepdims=True)

    def search(i, t):
        cand = t + lax.shift_left(jnp.int32(1), 30 - i)
        return jnp.where(count(bits >= cand) >= CAPACITY, cand, t)

    thr = lax.fori_loop(0, 31, search, jnp.zeros((BATCH * N_EXPERTS, 1), jnp.int32))
    need = CAPACITY - count(bits > thr)
    before = (lax.broadcasted_iota(jnp.int32, (LANES, LANES), 0)
              < lax.broadcasted_iota(jnp.int32, (LANES, LANES), 1)).astype(BF16)
    n_tied = jnp.zeros((BATCH * N_EXPERTS, 1), F32)
    n_sel = jnp.zeros((BATCH * N_EXPERTS, 1), F32)
    per_tile = COMBINE_SUB // LANES
    firsts, ends = [], []
    for c in range(SEQ // LANES):
        if c % per_tile == 0:
            firsts.append(n_sel)
        cols = slice(c * LANES, (c + 1) * LANES)
        bits_c = bits[:, cols]
        tied_c = jnp.where(bits_c == thr, 1.0, 0.0)
        tied_before = _dot(tied_c.astype(BF16), before) + n_tied
        sel_c = jnp.where(bits_c > thr, 1.0, jnp.where(tied_before < need, tied_c, 0.0))
        rank = _dot(sel_c.astype(BF16), before) + n_sel
        rank_c = jnp.where(sel_c > 0.0, rank, -1.0)
        gate_c = jnp.where(sel_c > 0.0, aff[:, cols], 0.0)
        n_tied = n_tied + jnp.sum(tied_c, axis=1, keepdims=True)
        n_sel = n_sel + jnp.sum(sel_c, axis=1, keepdims=True)
        rank_ref[:, cols] = rank_c.astype(jnp.int32)
        gate_ref[:, cols] = gate_c
        if c % per_tile == per_tile - 1:
            ends.append(n_sel)
    bounds_ref[...] = jnp.concatenate(firsts + ends, axis=1).astype(jnp.int32)


def _route(logits_t):
    return pl.pallas_call(
        _route_kernel,
        out_shape=(
            jax.ShapeDtypeStruct((BATCH * N_EXPERTS, SEQ), jnp.int32),
            jax.ShapeDtypeStruct((BATCH * N_EXPERTS, SEQ), F32),
            jax.ShapeDtypeStruct((BATCH * N_EXPERTS, 2 * (SEQ // COMBINE_SUB)), jnp.int32),
        ),
        compiler_params=pltpu.CompilerParams(vmem_limit_bytes=VMEM_LIMIT),
        name="route",
    )(logits_t)


def _dispatch(rank_e, h_packed, first_expert, n_experts):
    per_worker = n_experts * BATCH // (SC_CORES * SC_SUBCORES)
    assert per_worker * SC_CORES * SC_SUBCORES == n_experts * BATCH
    half = CAPACITY // 2
    mesh = plsc.VectorSubcoreMesh(core_axis_name="c", subcore_axis_name="s")

    @functools.partial(
        pl.kernel, mesh=mesh,
        out_type=jax.ShapeDtypeStruct((n_experts * BATCH * CAPACITY, D_MODEL // 2), h_packed.dtype),
        scratch_types=[
            pltpu.VMEM((SEQ,), jnp.int32),
            pltpu.VMEM((2, half), jnp.int32),
            pltpu.VMEM((half, D_MODEL // 2), h_packed.dtype),
            pltpu.SemaphoreType.DMA,
        ],
        compiler_params=pltpu.CompilerParams(needs_layout_passes=False),
        name="dispatch",
    )
    def k(table_hbm, rank_hbm, out_hbm, rank_v, idx_v, rows_v, sem):
        worker = lax.axis_index("s") * SC_CORES + lax.axis_index("c")

        @pl.loop(0, per_worker)
        def _(p):
            pair = worker * per_worker + p
            expert = lax.shift_right_logical(pair, BATCH.bit_length() - 1)
            sample = pair & (BATCH - 1)
            pltpu.sync_copy(rank_hbm.at[sample * N_EXPERTS + first_expert + expert], rank_v)

            @pl.loop(0, SEQ // SC_LANES)
            def _(i):
                rank = rank_v[pl.ds(i * SC_LANES, SC_LANES)]
                row = lax.iota(jnp.int32, SC_LANES) + (i * SC_LANES + sample * SEQ)
                plsc.store_scatter(idx_v, [lax.shift_right_arithmetic(rank, 7), rank & (half - 1)], row,
                                   mask=rank >= 0)

            for c in range(2):
                pltpu.async_copy(table_hbm.at[idx_v.at[c]], rows_v, sem).wait()
                pltpu.sync_copy(rows_v, out_hbm.at[pl.ds(pair * CAPACITY + c * half, half)])

    return k(h_packed, rank_e)


def _expert_kernel(xs_ref, wg_ref, wu_ref, wd_ref, y_ref, acc_ref, wg_bf, wu_bf, wd_bf):
    t = pl.program_id(0)
    n_f = EXPERT_HIDDEN // FFN_TF
    f = (t + n_f - 1) % n_f

    n_chunks = BATCH * CAPACITY // ROW_CHUNK

    def stage(slot, part):
        up = slice(part * D_MODEL // n_chunks, (part + 1) * D_MODEL // n_chunks)
        down = slice(part * FFN_TF // n_chunks, (part + 1) * FFN_TF // n_chunks)
        wg_bf[slot, up, :] = wg_ref[0, up, :].astype(BF16)
        wu_bf[slot, up, :] = wu_ref[0, up, :].astype(BF16)
        wd_bf[slot, down, :] = wd_ref[0, down, :].astype(BF16)

    def step(cur):
        for r in range(n_chunks):
            stage(1 - cur, r)
            rows = slice(r * ROW_CHUNK, (r + 1) * ROW_CHUNK)
            packed = xs_ref[0, rows, :]
            half = D_MODEL // 2
            x_lo = pltpu.unpack_elementwise(packed, index=0, packed_dtype=BF16, unpacked_dtype=F32).astype(BF16)
            x_hi = pltpu.unpack_elementwise(packed, index=1, packed_dtype=BF16, unpacked_dtype=F32).astype(BF16)
            a = _dot(x_lo, wg_bf[cur, :half, :]) + _dot(x_hi, wg_bf[cur, half:, :])
            u = _dot(x_lo, wu_bf[cur, :half, :]) + _dot(x_hi, wu_bf[cur, half:, :])
            hidden = (a / (1.0 + jnp.exp(-a))) * u
            total = jnp.where(f == 0, 0.0, acc_ref[rows, :]) + _dot(hidden.astype(BF16), wd_bf[cur])
            acc_ref[rows, :] = total
            y_ref[0, rows, :] = total.astype(BF16)

    @pl.when(t == 0)
    def _():
        for part in range(n_chunks):
            stage(0, part)

    @pl.when(t % 2 == 1)
    def _():
        step(0)

    @pl.when((t % 2 == 0) & (t > 0))
    def _():
        step(1)


def _experts(xs, w_gate, w_up, w_down, first_expert):
    rows = BATCH * CAPACITY
    n_f = EXPERT_HIDDEN // FFN_TF
    n_experts = xs.shape[0]
    last = n_experts * n_f - 1

    def staged(t):
        return jnp.minimum(t, last)

    def computed(t):
        return jnp.maximum(t - 1, 0) // n_f

    return pl.pallas_call(
        _expert_kernel,
        out_shape=jax.ShapeDtypeStruct((n_experts, rows, D_MODEL), BF16),
        grid=(n_experts * n_f + 1,),
        in_specs=[
            pl.BlockSpec((1, rows, D_MODEL // 2), lambda t: (computed(t), 0, 0)),
            pl.BlockSpec((1, D_MODEL, FFN_TF), lambda t: (first_expert + staged(t) // n_f, 0, staged(t) % n_f)),
            pl.BlockSpec((1, D_MODEL, FFN_TF), lambda t: (first_expert + staged(t) // n_f, 0, staged(t) % n_f)),
            pl.BlockSpec((1, FFN_TF, D_MODEL), lambda t: (first_expert + staged(t) // n_f, staged(t) % n_f, 0)),
        ],
        out_specs=pl.BlockSpec((1, rows, D_MODEL), lambda t: (computed(t), 0, 0)),
        scratch_shapes=[
            pltpu.VMEM((rows, D_MODEL), F32),
            pltpu.VMEM((2, D_MODEL, FFN_TF), BF16),
            pltpu.VMEM((2, D_MODEL, FFN_TF), BF16),
            pltpu.VMEM((2, FFN_TF, D_MODEL), BF16),
        ],
        compiler_params=_params(("arbitrary",)),
        name="experts",
    )(xs, w_gate, w_up, w_down)


def _combine_kernel(bounds_ref, rank_ref, gate_ref, *refs):
    n_groups = len(EXPERT_GROUPS)
    y_refs, (x1_ref, mod_ref, fg_ref, o_ref) = refs[:n_groups], refs[n_groups:]
    group_of = [(g, e) for g, n in enumerate(EXPERT_GROUPS) for e in range(n)]
    b, t = pl.program_id(0), pl.program_id(1)
    subs = COMBINE_TQ // COMBINE_SUB
    n_sub = SEQ // COMBINE_SUB
    per_dot = CAPACITY // COMBINE_WINDOW

    def y_rows(e):
        return y_refs[group_of[e][0]].at[group_of[e][1]]

    def onehot_t(e, cols, first, n_slots):
        slot = lax.broadcasted_iota(jnp.int32, (n_slots, cols.stop - cols.start), 0) + first
        return jnp.where(rank_ref[e:e + 1, cols] == slot, gate_ref[e:e + 1, cols], 0.0).astype(BF16)

    def dot_t(a, b_):
        return lax.dot_general(a, b_, (((0,), (0,)), ((), ())), preferred_element_type=F32)

    def finish(rows, acc):
        x2 = x1_ref[rows, :] + mod_ref[0, 5:6, :] * acc
        ms = jnp.mean(x2 * x2, axis=-1, keepdims=True)
        o_ref[rows, :] = x2 * lax.rsqrt(ms + EPS) * fg_ref[...]

    starts, fits = {}, None
    for s in range(subs):
        for e in range(N_EXPERTS):
            pair = (b * N_EXPERTS + e) * (2 * n_sub) + t * subs + s
            first, end = bounds_ref[pair], bounds_ref[pair + n_sub]
            start = jnp.minimum(lax.shift_left(lax.shift_right_logical(first, 4), 4), CAPACITY - COMBINE_WINDOW)
            starts[s, e] = start
            ok = end <= start + COMBINE_WINDOW
            fits = ok if fits is None else fits & ok

    @pl.when(fits)
    def _():
        for s in range(subs):
            rows = slice(s * COMBINE_SUB, (s + 1) * COMBINE_SUB)
            acc = jnp.zeros((COMBINE_SUB, D_MODEL), F32)
            for e0 in range(0, N_EXPERTS, per_dot):
                first = [pl.multiple_of(starts[s, e0 + i], 16) for i in range(per_dot)]
                onehots = [onehot_t(e0 + i, rows, first[i], COMBINE_WINDOW) for i in range(per_dot)]
                y_win = [y_rows(e0 + i)[pl.ds(first[i], COMBINE_WINDOW), :] for i in range(per_dot)]
                acc = acc + dot_t(jnp.concatenate(onehots, axis=0), jnp.concatenate(y_win, axis=0))
            finish(rows, acc)

    @pl.when(jnp.logical_not(fits))
    def _():
        acc = jnp.zeros((COMBINE_TQ, D_MODEL), F32)
        for e in range(N_EXPERTS):
            acc = acc + dot_t(onehot_t(e, slice(0, COMBINE_TQ), 0, CAPACITY), y_rows(e)[...])
        finish(slice(0, COMBINE_TQ), acc)


def _combine(bounds, rank_e, gate_e, ys, x1, mod3, final_g):
    per_b = SEQ // COMBINE_TQ
    return pl.pallas_call(
        _combine_kernel,
        out_shape=jax.ShapeDtypeStruct((BATCH * SEQ, D_MODEL), F32),
        grid_spec=pltpu.PrefetchScalarGridSpec(
            num_scalar_prefetch=1,
            grid=(BATCH, per_b),
            in_specs=[
                pl.BlockSpec((N_EXPERTS, COMBINE_TQ), lambda b, t, _: (b, t)),
                pl.BlockSpec((N_EXPERTS, COMBINE_TQ), lambda b, t, _: (b, t)),
                *[pl.BlockSpec((y.shape[0], CAPACITY, D_MODEL), lambda b, t, _: (0, b, 0)) for y in ys],
                pl.BlockSpec((COMBINE_TQ, D_MODEL), lambda b, t, _: (b * per_b + t, 0)),
                pl.BlockSpec((1, 6, D_MODEL), lambda b, t, _: (b, 0, 0)),
                pl.BlockSpec((1, D_MODEL), lambda b, t, _: (0, 0)),
            ],
            out_specs=pl.BlockSpec((COMBINE_TQ, D_MODEL), lambda b, t, _: (b * per_b + t, 0)),
        ),
        compiler_params=_params(("arbitrary", "arbitrary")),
        name="combine",
    )(bounds.reshape(-1), rank_e, gate_e, *ys, x1, mod3, final_g)


def _rope_tables():
    t = np.arange(SEQ)
    pos = np.stack([t // GRID_W, t % GRID_W], axis=1).astype(np.float32)
    per_axis = HEAD_DIM // 2
    inv = np.float32(ROPE_THETA) ** (-np.arange(0, per_axis, 2, dtype=np.float32) / np.float32(per_axis))
    ang = (pos[:, :, None] * inv).astype(np.float32)
    cos, sin = np.cos(ang.astype(np.float64)), np.sin(ang.astype(np.float64))
    zero = np.zeros_like(sin)

    def lanes(first_half, second_half):
        one_map = np.stack([first_half, second_half], axis=2).reshape(SEQ, HEAD_DIM)
        return jnp.asarray(np.concatenate([one_map, one_map], axis=1), F32)

    return lanes(cos, cos), lanes(-sin, zero), lanes(zero, sin)


def kernel(x, c, ctx, c_ctx, norm1_g, norm2_g, w_ada, b_ada, w_in, conv_w, w_out_conv, lambda_q1,
           lambda_k1, lambda_q2, lambda_k2, subln_g, w_o_attn, w_out, w_router, w_gate_e, w_up_e,
           w_down_e, final_g):
    cond = jnp.concatenate([c, c_ctx[None, :], jnp.zeros((COND_ROWS - BATCH - 1, D_MODEL), F32)], axis=0)
    mod3 = _ada(cond, w_ada[0], b_ada).reshape(COND_ROWS, 6, D_MODEL)

    cos, sin_hi, sin_lo = _rope_tables()
    q_scale = math.log2(math.e) * HEAD_DIM ** -0.5
    parts = _inproj(x, mod3, norm1_g, w_in[0], conv_w[0], cos, sin_hi, sin_lo, q_scale)
    ctx_kv = _ctxproj(ctx.reshape(BATCH * CTX_LEN, D_MODEL), mod3, norm1_g, w_in[0])

    lam_params = jnp.concatenate([lambda_q1, lambda_k1, lambda_q2, lambda_k2], axis=0)
    attn_o = _attn(lam_params, parts, ctx_kv, subln_g)

    wr_t = w_router[0].T
    wr_hi = wr_t.astype(BF16)
    wr_lo = (wr_t - wr_hi.astype(F32)).astype(BF16)
    x1, h2, logits_t = _merge(parts, attn_o, x.reshape(BATCH * SEQ, D_MODEL), mod3, norm2_g,
                              w_out_conv[0].astype(BF16), w_o_attn[0].astype(BF16), w_out[0].astype(BF16),
                              jnp.concatenate([wr_hi, wr_lo], axis=0))

    rank_e, gate_e, bounds = _route(logits_t)
    ys, first = [], 0
    for n in EXPERT_GROUPS:
        xs = _dispatch(rank_e, h2, first, n).reshape(n, BATCH * CAPACITY, D_MODEL // 2)
        ys.append(_experts(xs, w_gate_e[0], w_up_e[0], w_down_e[0], first))
        first += n
    out = _combine(bounds, rank_e, gate_e, ys, x1, mod3, final_g[None, :])
    return out.reshape(BATCH, SEQ, D_MODEL)
```

```python
import functools
import math

import jax
import jax.numpy as jnp
import numpy as np
from jax import lax
from jax.experimental import pallas as pl
from jax.experimental.pallas import tpu as pltpu
from jax.experimental.pallas import tpu_sc as plsc

D_MODEL = 1024
BATCH = 8
SEQ = 2048
GRID_W = 64
CTX_LEN = 256
N_HEADS = 8
HEAD_DIM = 64
V_DIM = 2 * HEAD_DIM
N_EXPERTS = 16
EXPERT_HIDDEN = 2048
CAPACITY = 2 * SEQ // N_EXPERTS
ROPE_THETA = 10000.0
EPS = 1e-6
LAM_INIT = 0.8 - 0.6 * math.exp(-0.3 * 0)
N_PARTS = 8
N_KEYS = CTX_LEN + SEQ
EXPERT_GROUPS = (4, 12)

LANES = 128
SC_CORES = 2
SC_SUBCORES = 16
SC_LANES = 16
F32 = jnp.float32
BF16 = jnp.bfloat16

ROW_CHUNK = 512
COND_ROWS = 16
W_RING = 3
ADA_TN = 1024
ATTN_TQ = 2048
ATTN_SQ = 256
MERGE_TM = 1024
FFN_TF = 512
COMBINE_TQ = 1024
COMBINE_SUB = 256
COMBINE_WINDOW = 64
VMEM_LIMIT = 56 * 1024 * 1024


def _dot(a, b):
    return jnp.dot(a, b, preferred_element_type=F32)


def _dot_nt(a, b):
    return lax.dot_general(a, b, (((1,), (1,)), ((), ())), preferred_element_type=F32)


def _params(sem, vmem=VMEM_LIMIT):
    return pltpu.CompilerParams(dimension_semantics=sem, vmem_limit_bytes=vmem)


def _ada_kernel(cond_ref, w_ref, b_ref, o_ref):
    c = cond_ref[...]
    s = c / (1.0 + jnp.exp(-c))
    o_ref[...] = _dot(s.astype(BF16), w_ref[...].astype(BF16)) + b_ref[...]


def _ada(cond, w_ada, b_ada):
    n = w_ada.shape[1]
    return pl.pallas_call(
        _ada_kernel,
        out_shape=jax.ShapeDtypeStruct((COND_ROWS, n), F32),
        grid=(n // ADA_TN,),
        in_specs=[
            pl.BlockSpec((COND_ROWS, D_MODEL), lambda i: (0, 0)),
            pl.BlockSpec((D_MODEL, ADA_TN), lambda i: (0, i)),
            pl.BlockSpec((1, ADA_TN), lambda i: (0, i)),
        ],
        out_specs=pl.BlockSpec((COND_ROWS, ADA_TN), lambda i: (0, i)),
        compiler_params=_params(("arbitrary",)),
        name="ada",
    )(cond, w_ada, b_ada)


def _norm_modulate(x, g, shift, scale):
    ms = jnp.mean(x * x, axis=-1, keepdims=True)
    return (x * lax.rsqrt(ms + EPS) * g) * (1.0 + scale) + shift


def _rope(a, cos, sin_hi, sin_lo):
    return a * cos + pltpu.roll(a, LANES - 16, 1) * sin_hi + pltpu.roll(a, 16, 1) * sin_lo


def _inproj_weight_block(s):
    nxt = jnp.minimum(s, BATCH * N_PARTS - 1) % N_PARTS
    return jnp.where(nxt == 1, 2, jnp.where(nxt == 2, 1, nxt))


def _inproj_kernel(x_hbm, mod_ref, g_ref, w_hbm, cw_ref, cos_ref, shi_ref, slo_ref, o_ref,
                   hx_ref, z_ref, w_bf, w_f32, w_sem, x_ref, x_sem, *, q_scale):
    t = pl.program_id(0)
    j = (t + N_PARTS - 1) % N_PARTS
    chunks = [slice(r * ROW_CHUNK, (r + 1) * ROW_CHUNK) for r in range(SEQ // ROW_CHUNK)]
    ring_slot = t % W_RING

    def weight_copy(s):
        cols = pl.ds(pl.multiple_of(_inproj_weight_block(s) * D_MODEL, D_MODEL), D_MODEL)
        return pltpu.make_async_copy(w_hbm.at[:, cols], w_f32.at[s % W_RING], w_sem.at[s % W_RING])

    @pl.when(t == 0)
    def _():
        for s in range(W_RING - 1):
            weight_copy(s).start()

    @pl.when(t + W_RING - 1 <= BATCH * N_PARTS)
    def _():
        weight_copy(t + W_RING - 1).start()

    weight_copy(t).wait()

    sample = jnp.maximum(t - 1, 0) // N_PARTS

    def x_copy(s):
        return pltpu.make_async_copy(x_hbm.at[s], x_ref, x_sem)

    @pl.when(t == 0)
    def _():
        x_copy(0).start()

    @pl.when((t > 0) & (j == 1) & (sample + 1 < BATCH))
    def _():
        x_copy(sample + 1).start()

    def stage(slot, part):
        rows = slice(part * D_MODEL // len(chunks), (part + 1) * D_MODEL // len(chunks))
        w_bf[slot, rows, :] = w_f32[ring_slot, rows, :].astype(BF16)

    def proj(part, r):
        stage(1 - part % 2, r)
        return _dot(hx_ref[chunks[r], :], w_bf[part % 2])

    def branch(part):
        return pl.when((t > 0) & (j == part))

    @pl.when(t == 0)
    def _():
        for r in range(len(chunks)):
            stage(0, r)

    @branch(0)
    def _():
        shift = mod_ref[0, 0:1, :]
        scale = mod_ref[0, 1:2, :]
        x_copy(sample).wait()
        for rows in chunks:
            hx_ref[rows, :] = _norm_modulate(x_ref[rows, :], g_ref[...], shift, scale).astype(BF16)
        z_ref[0:8, :] = jnp.zeros((8, D_MODEL), F32)
        z_ref[SEQ + 8:SEQ + 16, :] = jnp.zeros((8, D_MODEL), F32)
        for r, rows in enumerate(chunks):
            z_ref[pl.ds(rows.start + 8, ROW_CHUNK), :] = proj(0, r)

    @branch(1)
    def _():
        for r, rows in enumerate(chunks):
            zr = pl.ds(rows.start + 8, ROW_CHUNK)
            z_ref[zr, :] = z_ref[zr, :] * proj(1, r)

    @branch(2)
    def _():
        first_row = lax.broadcasted_iota(jnp.int32, (8, 1), 0) == 0
        last_row = lax.broadcasted_iota(jnp.int32, (8, 1), 0) == 7
        for r, rows in enumerate(chunks):
            base = rows.start + 8
            zc = z_ref[pl.ds(base, ROW_CHUNK), :]
            before = pltpu.roll(zc * cw_ref[0:1, :], 1, 0)
            after = pltpu.roll(zc * cw_ref[2:3, :], ROW_CHUNK - 1, 0)
            edge_b = jnp.where(first_row, z_ref[pl.ds(base - 8, 8), :][7:8, :] * cw_ref[0:1, :], before[0:8, :])
            edge_a = jnp.where(last_row, z_ref[pl.ds(base + ROW_CHUNK, 8), :][0:1, :] * cw_ref[2:3, :],
                               after[ROW_CHUNK - 8:, :])
            before = jnp.concatenate([edge_b, before[8:, :]], axis=0)
            after = jnp.concatenate([after[:ROW_CHUNK - 8, :], edge_a], axis=0)
            y = before + zc * cw_ref[1:2, :] + after
            o_ref[0, rows, :] = (proj(2, r) * y).astype(BF16)

    def rope_part(part, scale):
        for r, rows in enumerate(chunks):
            acc = proj(part, r)
            cos, shi, slo = cos_ref[rows, :], shi_ref[rows, :], slo_ref[rows, :]
            for h in range(D_MODEL // LANES):
                cols = slice(h * LANES, (h + 1) * LANES)
                roped = _rope(acc[:, cols], cos, shi, slo)
                if scale is not None:
                    roped = roped * scale
                o_ref[0, rows, cols] = roped.astype(BF16)

    @branch(3)
    def _():
        rope_part(3, q_scale)

    @branch(4)
    def _():
        rope_part(4, None)

    for plain_part in (5, 6, 7):
        @branch(plain_part)
        def _(plain_part=plain_part):
            for r, rows in enumerate(chunks):
                o_ref[0, rows, :] = proj(plain_part, r).astype(BF16)


def _inproj(x, mod3, norm_g, w_in, conv_w, cos, sin_hi, sin_lo, q_scale):
    def part(t):
        return jnp.maximum(t - 1, 0) % N_PARTS

    def sample(t):
        return jnp.maximum(t - 1, 0) // N_PARTS

    tab = pl.BlockSpec((SEQ, LANES), lambda t: (0, 0), pipeline_mode=pl.Buffered(1))
    return pl.pallas_call(
        functools.partial(_inproj_kernel, q_scale=q_scale),
        out_shape=jax.ShapeDtypeStruct((N_PARTS - 2, BATCH * SEQ, D_MODEL), BF16),
        grid=(BATCH * N_PARTS + 1,),
        in_specs=[
            pl.BlockSpec(memory_space=pl.ANY),
            pl.BlockSpec((1, 6, D_MODEL), lambda t: (sample(t), 0, 0)),
            pl.BlockSpec((1, D_MODEL), lambda t: (0, 0)),
            pl.BlockSpec(memory_space=pl.ANY),
            pl.BlockSpec((3, D_MODEL), lambda t: (0, 0)),
            tab, tab, tab,
        ],
        out_specs=pl.BlockSpec((1, SEQ, D_MODEL), lambda t: (jnp.maximum(part(t) - 2, 0), sample(t), 0)),
        scratch_shapes=[
            pltpu.VMEM((SEQ, D_MODEL), BF16),
            pltpu.VMEM((SEQ + 16, D_MODEL), F32),
            pltpu.VMEM((2, D_MODEL, D_MODEL), BF16),
            pltpu.VMEM((W_RING, D_MODEL, D_MODEL), F32),
            pltpu.SemaphoreType.DMA((W_RING,)),
            pltpu.VMEM((SEQ, D_MODEL), F32),
            pltpu.SemaphoreType.DMA,
        ],
        compiler_params=_params(("arbitrary",)),
        name="inproj",
    )(x, mod3, norm_g, w_in, conv_w, cos, sin_hi, sin_lo)


def _ctxproj_kernel(c_ref, mod_ref, g_ref, w_ref, o_ref, hc_ref):
    j = pl.program_id(0)
    chunks = [slice(r * ROW_CHUNK, (r + 1) * ROW_CHUNK) for r in range(BATCH * CTX_LEN // ROW_CHUNK)]

    @pl.when(j == 0)
    def _():
        shift = mod_ref[0, 0:1, :]
        scale = mod_ref[0, 1:2, :]
        for rows in chunks:
            hc_ref[rows, :] = _norm_modulate(c_ref[rows, :], g_ref[...], shift, scale).astype(BF16)

    w = w_ref[...].astype(BF16)
    for rows in chunks:
        o_ref[0, rows, :] = _dot(hc_ref[rows, :], w).astype(BF16)


def _ctxproj(ctx2, mod3, norm_g, w_in):
    rows = BATCH * CTX_LEN
    return pl.pallas_call(
        _ctxproj_kernel,
        out_shape=jax.ShapeDtypeStruct((2, rows, D_MODEL), BF16),
        grid=(2,),
        in_specs=[
            pl.BlockSpec((rows, D_MODEL), lambda j: (0, 0)),
            pl.BlockSpec((1, 6, D_MODEL), lambda j: (BATCH, 0, 0)),
            pl.BlockSpec((1, D_MODEL), lambda j: (0, 0)),
            pl.BlockSpec((D_MODEL, D_MODEL), lambda j: (0, 4 + j)),
        ],
        out_specs=pl.BlockSpec((1, rows, D_MODEL), lambda j: (j, 0, 0)),
        scratch_shapes=[pltpu.VMEM((rows, D_MODEL), BF16)],
        compiler_params=_params(("arbitrary",)),
        name="ctxproj",
    )(ctx2, mod3, norm_g, w_in)


def _attn_kernel(lp_ref, q_ref, k_ref, v_ref, kc_ref, vc_ref, g_ref, o_ref, k_all, v_ext, s_ref, m_ref):
    lp = lp_ref[...]
    lam = (jnp.exp(jnp.sum(lp[0:1] * lp[1:2], axis=-1, keepdims=True))
           - jnp.exp(jnp.sum(lp[2:3] * lp[3:4], axis=-1, keepdims=True)) + LAM_INIT)
    k_all[0:CTX_LEN, :] = kc_ref[0]
    k_all[CTX_LEN:N_KEYS, :] = k_ref[0]
    v_ext[0:CTX_LEN, 0:LANES] = vc_ref[0]
    v_ext[CTX_LEN:N_KEYS, 0:LANES] = v_ref[0]
    v_ext[:, LANES:2 * LANES] = jnp.ones((N_KEYS, LANES), BF16)
    first_map = lax.broadcasted_iota(jnp.int32, (1, LANES), 1) < HEAD_DIM

    def scores(u):
        q = q_ref[0, u * ATTN_SQ:(u + 1) * ATTN_SQ, :]
        zero = jnp.zeros_like(q)
        for mp, qm in enumerate((jnp.where(first_map, q, zero), jnp.where(first_map, zero, q))):
            s = _dot_nt(qm, k_all[...])
            s_ref[u % 2, mp] = s
            m_ref[u % 2, mp] = jnp.max(s, axis=-1, keepdims=True)

    def values(u):
        o = []
        for mp in range(2):
            p = jnp.exp2(s_ref[u % 2, mp] - m_ref[u % 2, mp])
            o.append(_dot(p.astype(BF16), v_ext[...]))
        a = o[0][:, :LANES] / o[0][:, LANES:] - o[1][:, :LANES] * (lam / o[1][:, LANES:])
        ms = jnp.mean(a * a, axis=-1, keepdims=True)
        o_ref[u * ATTN_SQ:(u + 1) * ATTN_SQ, :] = (
            (a * lax.rsqrt(ms + EPS) * g_ref[...]) * (1.0 - LAM_INIT)).astype(BF16)

    n_units = ATTN_TQ // ATTN_SQ
    scores(0)
    for u in range(n_units):
        if u + 1 < n_units:
            scores(u + 1)
        values(u)


def _attn(lam_params, parts, ctx_kv, subln_g):
    nq = SEQ // ATTN_TQ
    return pl.pallas_call(
        _attn_kernel,
        out_shape=jax.ShapeDtypeStruct((BATCH * SEQ, D_MODEL), BF16),
        grid=(BATCH, N_HEADS, nq),
        in_specs=[
            pl.BlockSpec((4, HEAD_DIM), lambda b, h, i: (0, 0)),
            pl.BlockSpec((1, ATTN_TQ, LANES), lambda b, h, i: (1, b * nq + i, h)),
            pl.BlockSpec((1, SEQ, LANES), lambda b, h, i: (2, b, h)),
            pl.BlockSpec((1, SEQ, LANES), lambda b, h, i: (3, b, h)),
            pl.BlockSpec((1, CTX_LEN, LANES), lambda b, h, i: (0, b, h)),
            pl.BlockSpec((1, CTX_LEN, LANES), lambda b, h, i: (1, b, h)),
            pl.BlockSpec((1, V_DIM), lambda b, h, i: (0, 0)),
        ],
        out_specs=pl.BlockSpec((ATTN_TQ, LANES), lambda b, h, i: (b * nq + i, h)),
        scratch_shapes=[
            pltpu.VMEM((N_KEYS, LANES), BF16),
            pltpu.VMEM((N_KEYS, 2 * LANES), BF16),
            pltpu.VMEM((2, 2, ATTN_SQ, N_KEYS), F32),
            pltpu.VMEM((2, 2, ATTN_SQ, 1), F32),
        ],
        compiler_params=_params(("arbitrary", "arbitrary", "arbitrary")),
        name="attn",
    )(lam_params, parts, parts, parts, ctx_kv, ctx_kv, subln_g)


def _merge_kernel(yb_ref, s6_ref, s7_ref, o_ref, x_ref, mod_ref, g2_ref, wc_ref, wa_ref, wo_ref,
                  wr_ref, x1_ref, h2_ref, lg_ref):
    chunks = [slice(r * ROW_CHUNK, (r + 1) * ROW_CHUNK) for r in range(MERGE_TM // ROW_CHUNK)]
    for rows in chunks:
        y_conv = _dot(yb_ref[0, rows, :], wc_ref[...])
        y_attn = _dot(o_ref[rows, :], wa_ref[...])
        gate_conv = 1.0 / (1.0 + jnp.exp(-s6_ref[0, rows, :].astype(F32)))
        gate_attn = 1.0 / (1.0 + jnp.exp(-s7_ref[0, rows, :].astype(F32)))
        merged = gate_conv * y_conv + gate_attn * y_attn
        mix = _dot(merged.astype(BF16), wo_ref[...])
        x1_ref[rows, :] = x_ref[rows, :] + mod_ref[0, 2:3, :] * mix
    for rows in chunks:
        h2 = _norm_modulate(x1_ref[rows, :], g2_ref[...], mod_ref[0, 3:4, :], mod_ref[0, 4:5, :])
        h_hi = h2.astype(BF16)
        h2_ref[rows, :] = pltpu.pack_elementwise([h2[:, :D_MODEL // 2], h2[:, D_MODEL // 2:]], packed_dtype=BF16)
        h_lo = (h2 - h_hi.astype(F32)).astype(BF16)
        both = _dot_nt(wr_ref[...], h_hi)
        lg_ref[0, :, rows] = both[:N_EXPERTS] + (_dot_nt(wr_ref[0:N_EXPERTS, :], h_lo) + both[N_EXPERTS:])


def _merge(parts, attn_o, x2, mod3, norm2_g, w_conv_bf, w_attn_bf, w_out_bf, wr_hi_lo):
    per_b = SEQ // MERGE_TM
    sq = pl.BlockSpec((D_MODEL, D_MODEL), lambda i: (0, 0), pipeline_mode=pl.Buffered(1))
    wr = pl.BlockSpec((2 * N_EXPERTS, D_MODEL), lambda i: (0, 0), pipeline_mode=pl.Buffered(1))
    tile = pl.BlockSpec((MERGE_TM, D_MODEL), lambda i: (i, 0))
    return pl.pallas_call(
        _merge_kernel,
        out_shape=(
            jax.ShapeDtypeStruct((BATCH * SEQ, D_MODEL), F32),
            jax.ShapeDtypeStruct((BATCH * SEQ, D_MODEL // 2), jnp.uint32),
            jax.ShapeDtypeStruct((BATCH, N_EXPERTS, SEQ), F32),
        ),
        grid=(BATCH * per_b,),
        in_specs=[
            pl.BlockSpec((1, MERGE_TM, D_MODEL), lambda i: (0, i, 0)),
            pl.BlockSpec((1, MERGE_TM, D_MODEL), lambda i: (4, i, 0)),
            pl.BlockSpec((1, MERGE_TM, D_MODEL), lambda i: (5, i, 0)),
            tile, tile,
            pl.BlockSpec((1, 6, D_MODEL), lambda i: (i // per_b, 0, 0)),
            pl.BlockSpec((1, D_MODEL), lambda i: (0, 0)),
            sq, sq, sq, wr,
        ],
        out_specs=(
            tile,
            pl.BlockSpec((MERGE_TM, D_MODEL // 2), lambda i: (i, 0)),
            pl.BlockSpec((1, N_EXPERTS, MERGE_TM), lambda i: (i // per_b, 0, i % per_b)),
        ),
        compiler_params=_params(("arbitrary",)),
        name="merge",
    )(parts, parts, parts, attn_o, x2, mod3, norm2_g, w_conv_bf, w_attn_bf, w_out_bf, wr_hi_lo)


def _route_kernel(lg_ref, rank_ref, gate_ref, bounds_ref):
    lg = lg_ref[...]
    ex = jnp.exp(lg - jnp.max(lg, axis=1, keepdims=True))
    aff = (ex / jnp.sum(ex, axis=1, keepdims=True)).reshape(BATCH * N_EXPERTS, SEQ)
    bits = lax.bitcast_convert_type(aff, jnp.int32)

    def count(mask):
        return jnp.sum(jnp.where(mask, 1.0, 0.0), axis=1, keepdims=True)

    def search(i, t):
        cand = t + lax.shift_left(jnp.int32(1), 30 - i)
        return jnp.where(count(bits >= cand) >= CAPACITY, cand, t)

    thr = lax.fori_loop(0, 31, search, jnp.zeros((BATCH * N_EXPERTS, 1), jnp.int32))
    need = CAPACITY - count(bits > thr)
    before = (lax.broadcasted_iota(jnp.int32, (LANES, LANES), 0)
              < lax.broadcasted_iota(jnp.int32, (LANES, LANES), 1)).astype(BF16)
    n_tied = jnp.zeros((BATCH * N_EXPERTS, 1), F32)
    n_sel = jnp.zeros((BATCH * N_EXPERTS, 1), F32)
    per_tile = COMBINE_SUB // LANES
    firsts, ends = [], []
    for c in range(SEQ // LANES):
        if c % per_tile == 0:
            firsts.append(n_sel)
        cols = slice(c * LANES, (c + 1) * LANES)
        bits_c = bits[:, cols]
        tied_c = jnp.where(bits_c == thr, 1.0, 0.0)
        tied_before = _dot(tied_c.astype(BF16), before) + n_tied
        sel_c = jnp.where(bits_c > thr, 1.0, jnp.where(tied_before < need, tied_c, 0.0))
        rank = _dot(sel_c.astype(BF16), before) + n_sel
        rank_c = jnp.where(sel_c > 0.0, rank, -1.0)
        gate_c = jnp.where(sel_c > 0.0, aff[:, cols], 0.0)
        n_tied = n_tied + jnp.sum(tied_c, axis=1, keepdims=True)
        n_sel = n_sel + jnp.sum(sel_c, axis=1, keepdims=True)
        rank_ref[:, cols] = rank_c.astype(jnp.int32)
        gate_ref[:, cols] = gate_c
        if c % per_tile == per_tile - 1:
            ends.append(n_sel)
    bounds_ref[...] = jnp.concatenate(firsts + ends, axis=1).astype(jnp.int32)


def _route(logits_t):
    return pl.pallas_call(
        _route_kernel,
        out_shape=(
            jax.ShapeDtypeStruct((BATCH * N_EXPERTS, SEQ), jnp.int32),
            jax.ShapeDtypeStruct((BATCH * N_EXPERTS, SEQ), F32),
            jax.ShapeDtypeStruct((BATCH * N_EXPERTS, 2 * (SEQ // COMBINE_SUB)), jnp.int32),
        ),
        compiler_params=pltpu.CompilerParams(vmem_limit_bytes=VMEM_LIMIT),
        name="route",
    )(logits_t)


def _dispatch(rank_e, h_packed, first_expert, n_experts):
    per_worker = n_experts * BATCH // (SC_CORES * SC_SUBCORES)
    assert per_worker * SC_CORES * SC_SUBCORES == n_experts * BATCH
    half = CAPACITY // 2
    mesh = plsc.VectorSubcoreMesh(core_axis_name="c", subcore_axis_name="s")

    @functools.partial(
        pl.kernel, mesh=mesh,
        out_type=jax.ShapeDtypeStruct((n_experts * BATCH * CAPACITY, D_MODEL // 2), h_packed.dtype),
        scratch_types=[
            pltpu.VMEM((SEQ,), jnp.int32),
            pltpu.VMEM((2, half), jnp.int32),
            pltpu.VMEM((half, D_MODEL // 2), h_packed.dtype),
            pltpu.SemaphoreType.DMA,
        ],
        compiler_params=pltpu.CompilerParams(needs_layout_passes=False),
        name="dispatch",
    )
    def k(table_hbm, rank_hbm, out_hbm, rank_v, idx_v, rows_v, sem):
        worker = lax.axis_index("s") * SC_CORES + lax.axis_index("c")

        @pl.loop(0, per_worker)
        def _(p):
            pair = worker * per_worker + p
            expert = lax.shift_right_logical(pair, BATCH.bit_length() - 1)
            sample = pair & (BATCH - 1)
            pltpu.sync_copy(rank_hbm.at[sample * N_EXPERTS + first_expert + expert], rank_v)

            @pl.loop(0, SEQ // SC_LANES)
            def _(i):
                rank = rank_v[pl.ds(i * SC_LANES, SC_LANES)]
                row = lax.iota(jnp.int32, SC_LANES) + (i * SC_LANES + sample * SEQ)
                plsc.store_scatter(idx_v, [lax.shift_right_arithmetic(rank, 7), rank & (half - 1)], row,
                                   mask=rank >= 0)

            for c in range(2):
                pltpu.async_copy(table_hbm.at[idx_v.at[c]], rows_v, sem).wait()
                pltpu.sync_copy(rows_v, out_hbm.at[pl.ds(pair * CAPACITY + c * half, half)])

    return k(h_packed, rank_e)


def _expert_kernel(xs_ref, wg_ref, wu_ref, wd_ref, y_ref, acc_ref, wg_bf, wu_bf, wd_bf):
    t = pl.program_id(0)
    n_f = EXPERT_HIDDEN // FFN_TF
    f = (t + n_f - 1) % n_f

    n_chunks = BATCH * CAPACITY // ROW_CHUNK

    def stage(slot, part):
        up = slice(part * D_MODEL // n_chunks, (part + 1) * D_MODEL // n_chunks)
        down = slice(part * FFN_TF // n_chunks, (part + 1) * FFN_TF // n_chunks)
        wg_bf[slot, up, :] = wg_ref[0, up, :].astype(BF16)
        wu_bf[slot, up, :] = wu_ref[0, up, :].astype(BF16)
        wd_bf[slot, down, :] = wd_ref[0, down, :].astype(BF16)

    def step(cur):
        for r in range(n_chunks):
            stage(1 - cur, r)
            rows = slice(r * ROW_CHUNK, (r + 1) * ROW_CHUNK)
            packed = xs_ref[0, rows, :]
            half = D_MODEL // 2
            x_lo = pltpu.unpack_elementwise(packed, index=0, packed_dtype=BF16, unpacked_dtype=F32).astype(BF16)
            x_hi = pltpu.unpack_elementwise(packed, index=1, packed_dtype=BF16, unpacked_dtype=F32).astype(BF16)
            a = _dot(x_lo, wg_bf[cur, :half, :]) + _dot(x_hi, wg_bf[cur, half:, :])
            u = _dot(x_lo, wu_bf[cur, :half, :]) + _dot(x_hi, wu_bf[cur, half:, :])
            hidden = (a / (1.0 + jnp.exp(-a))) * u
            total = jnp.where(f == 0, 0.0, acc_ref[rows, :]) + _dot(hidden.astype(BF16), wd_bf[cur])
            acc_ref[rows, :] = total
            y_ref[0, rows, :] = total.astype(BF16)

    @pl.when(t == 0)
    def _():
        for part in range(n_chunks):
            stage(0, part)

    @pl.when(t % 2 == 1)
    def _():
        step(0)

    @pl.when((t % 2 == 0) & (t > 0))
    def _():
        step(1)


def _experts(xs, w_gate, w_up, w_down, first_expert):
    rows = BATCH * CAPACITY
    n_f = EXPERT_HIDDEN // FFN_TF
    n_experts = xs.shape[0]
    last = n_experts * n_f - 1

    def staged(t):
        return jnp.minimum(t, last)

    def computed(t):
        return jnp.maximum(t - 1, 0) // n_f

    return pl.pallas_call(
        _expert_kernel,
        out_shape=jax.ShapeDtypeStruct((n_experts, rows, D_MODEL), BF16),
        grid=(n_experts * n_f + 1,),
        in_specs=[
            pl.BlockSpec((1, rows, D_MODEL // 2), lambda t: (computed(t), 0, 0)),
            pl.BlockSpec((1, D_MODEL, FFN_TF), lambda t: (first_expert + staged(t) // n_f, 0, staged(t) % n_f)),
            pl.BlockSpec((1, D_MODEL, FFN_TF), lambda t: (first_expert + staged(t) // n_f, 0, staged(t) % n_f)),
            pl.BlockSpec((1, FFN_TF, D_MODEL), lambda t: (first_expert + staged(t) // n_f, staged(t) % n_f, 0)),
        ],
        out_specs=pl.BlockSpec((1, rows, D_MODEL), lambda t: (computed(t), 0, 0)),
        scratch_shapes=[
            pltpu.VMEM((rows, D_MODEL), F32),
            pltpu.VMEM((2, D_MODEL, FFN_TF), BF16),
            pltpu.VMEM((2, D_MODEL, FFN_TF), BF16),
            pltpu.VMEM((2, FFN_TF, D_MODEL), BF16),
        ],
        compiler_params=_params(("arbitrary",)),
        name="experts",
    )(xs, w_gate, w_up, w_down)


def _combine_kernel(bounds_ref, rank_ref, gate_ref, *refs):
    n_groups = len(EXPERT_GROUPS)
    y_refs, (x1_ref, mod_ref, fg_ref, o_ref) = refs[:n_groups], refs[n_groups:]
    group_of = [(g, e) for g, n in enumerate(EXPERT_GROUPS) for e in range(n)]
    b, t = pl.program_id(0), pl.program_id(1)
    subs = COMBINE_TQ // COMBINE_SUB
    n_sub = SEQ // COMBINE_SUB
    per_dot = CAPACITY // COMBINE_WINDOW

    def y_rows(e):
        return y_refs[group_of[e][0]].at[group_of[e][1]]

    def onehot_t(e, cols, first, n_slots):
        slot = lax.broadcasted_iota(jnp.int32, (n_slots, cols.stop - cols.start), 0) + first
        return jnp.where(rank_ref[e:e + 1, cols] == slot, gate_ref[e:e + 1, cols], 0.0).astype(BF16)

    def dot_t(a, b_):
        return lax.dot_general(a, b_, (((0,), (0,)), ((), ())), preferred_element_type=F32)

    def finish(rows, acc):
        x2 = x1_ref[rows, :] + mod_ref[0, 5:6, :] * acc
        ms = jnp.mean(x2 * x2, axis=-1, keepdims=True)
        o_ref[rows, :] = x2 * lax.rsqrt(ms + EPS) * fg_ref[...]

    starts, fits = {}, None
    for s in range(subs):
        for e in range(N_EXPERTS):
            pair = (b * N_EXPERTS + e) * (2 * n_sub) + t * subs + s
            first, end = bounds_ref[pair], bounds_ref[pair + n_sub]
            start = jnp.minimum(lax.shift_left(lax.shift_right_logical(first, 4), 4), CAPACITY - COMBINE_WINDOW)
            starts[s, e] = start
            ok = end <= start + COMBINE_WINDOW
            fits = ok if fits is None else fits & ok

    @pl.when(fits)
    def _():
        for s in range(subs):
            rows = slice(s * COMBINE_SUB, (s + 1) * COMBINE_SUB)
            acc = jnp.zeros((COMBINE_SUB, D_MODEL), F32)
            for e0 in range(0, N_EXPERTS, per_dot):
                first = [pl.multiple_of(starts[s, e0 + i], 16) for i in range(per_dot)]
                onehots = [onehot_t(e0 + i, rows, first[i], COMBINE_WINDOW) for i in range(per_dot)]
                y_win = [y_rows(e0 + i)[pl.ds(first[i], COMBINE_WINDOW), :] for i in range(per_dot)]
                acc = acc + dot_t(jnp.concatenate(onehots, axis=0), jnp.concatenate(y_win, axis=0))
            finish(rows, acc)

    @pl.when(jnp.logical_not(fits))
    def _():
        acc = jnp.zeros((COMBINE_TQ, D_MODEL), F32)
        for e in range(N_EXPERTS):
            acc = acc + dot_t(onehot_t(e, slice(0, COMBINE_TQ), 0, CAPACITY), y_rows(e)[...])
        finish(slice(0, COMBINE_TQ), acc)


def _combine(bounds, rank_e, gate_e, ys, x1, mod3, final_g):
    per_b = SEQ // COMBINE_TQ
    return pl.pallas_call(
        _combine_kernel,
        out_shape=jax.ShapeDtypeStruct((BATCH * SEQ, D_MODEL), F32),
        grid_spec=pltpu.PrefetchScalarGridSpec(
            num_scalar_prefetch=1,
            grid=(BATCH, per_b),
            in_specs=[
                pl.BlockSpec((N_EXPERTS, COMBINE_TQ), lambda b, t, _: (b, t)),
                pl.BlockSpec((N_EXPERTS, COMBINE_TQ), lambda b, t, _: (b, t)),
                *[pl.BlockSpec((y.shape[0], CAPACITY, D_MODEL), lambda b, t, _: (0, b, 0)) for y in ys],
                pl.BlockSpec((COMBINE_TQ, D_MODEL), lambda b, t, _: (b * per_b + t, 0)),
                pl.BlockSpec((1, 6, D_MODEL), lambda b, t, _: (b, 0, 0)),
                pl.BlockSpec((1, D_MODEL), lambda b, t, _: (0, 0)),
            ],
            out_specs=pl.BlockSpec((COMBINE_TQ, D_MODEL), lambda b, t, _: (b * per_b + t, 0)),
        ),
        compiler_params=_params(("arbitrary", "arbitrary")),
        name="combine",
    )(bounds.reshape(-1), rank_e, gate_e, *ys, x1, mod3, final_g)


def _rope_tables():
    t = np.arange(SEQ)
    pos = np.stack([t // GRID_W, t % GRID_W], axis=1).astype(np.float32)
    per_axis = HEAD_DIM // 2
    inv = np.float32(ROPE_THETA) ** (-np.arange(0, per_axis, 2, dtype=np.float32) / np.float32(per_axis))
    ang = (pos[:, :, None] * inv).astype(np.float32)
    cos, sin = np.cos(ang.astype(np.float64)), np.sin(ang.astype(np.float64))
    zero = np.zeros_like(sin)

    def lanes(first_half, second_half):
        one_map = np.stack([first_half, second_half], axis=2).reshape(SEQ, HEAD_DIM)
        return jnp.asarray(np.concatenate([one_map, one_map], axis=1), F32)

    return lanes(cos, cos), lanes(-sin, zero), lanes(zero, sin)


def kernel(x, c, ctx, c_ctx, norm1_g, norm2_g, w_ada, b_ada, w_in, conv_w, w_out_conv, lambda_q1,
           lambda_k1, lambda_q2, lambda_k2, subln_g, w_o_attn, w_out, w_router, w_gate_e, w_up_e,
           w_down_e, final_g):
    cond = jnp.concatenate([c, c_ctx[None, :], jnp.zeros((COND_ROWS - BATCH - 1, D_MODEL), F32)], axis=0)
    mod3 = _ada(cond, w_ada[0], b_ada).reshape(COND_ROWS, 6, D_MODEL)

    cos, sin_hi, sin_lo = _rope_tables()
    q_scale = math.log2(math.e) * HEAD_DIM ** -0.5
    parts = _inproj(x, mod3, norm1_g, w_in[0], conv_w[0], cos, sin_hi, sin_lo, q_scale)
    ctx_kv = _ctxproj(ctx.reshape(BATCH * CTX_LEN, D_MODEL), mod3, norm1_g, w_in[0])

    lam_params = jnp.concatenate([lambda_q1, lambda_k1, lambda_q2, lambda_k2], axis=0)
    attn_o = _attn(lam_params, parts, ctx_kv, subln_g)

    wr_t = w_router[0].T
    wr_hi = wr_t.astype(BF16)
    wr_lo = (wr_t - wr_hi.astype(F32)).astype(BF16)
    x1, h2, logits_t = _merge(parts, attn_o, x.reshape(BATCH * SEQ, D_MODEL), mod3, norm2_g,
                              w_out_conv[0].astype(BF16), w_o_attn[0].astype(BF16), w_out[0].astype(BF16),
                              jnp.concatenate([wr_hi, wr_lo], axis=0))

    rank_e, gate_e, bounds = _route(logits_t)
    ys, first = [], 0
    for n in EXPERT_GROUPS:
        xs = _dispatch(rank_e, h2, first, n).reshape(n, BATCH * CAPACITY, D_MODEL // 2)
        ys.append(_experts(xs, w_gate_e[0], w_up_e[0], w_down_e[0], first))
        first += n
    out = _combine(bounds, rank_e, gate_e, ys, x1, mod3, final_g[None, :])
    return out.reshape(BATCH, SEQ, D_MODEL)
```

```python
import functools
import math

import jax
import jax.numpy as jnp
import numpy as np
from jax import lax
from jax.experimental import pallas as pl
from jax.experimental.pallas import tpu as pltpu
from jax.experimental.pallas import tpu_sc as plsc

D_MODEL = 1024
BATCH = 8
SEQ = 2048
GRID_W = 64
CTX_LEN = 256
N_HEADS = 8
HEAD_DIM = 64
V_DIM = 2 * HEAD_DIM
N_EXPERTS = 16
EXPERT_HIDDEN = 2048
CAPACITY = 2 * SEQ // N_EXPERTS
ROPE_THETA = 10000.0
EPS = 1e-6
LAM_INIT = 0.8 - 0.6 * math.exp(-0.3 * 0)
N_PARTS = 8
N_KEYS = CTX_LEN + SEQ
EXPERT_GROUPS = (4, 12)

LANES = 128
SC_CORES = 2
SC_SUBCORES = 16
SC_LANES = 16
F32 = jnp.float32
BF16 = jnp.bfloat16

ROW_CHUNK = 512
COND_ROWS = 16
W_RING = 3
ADA_TN = 1024
ATTN_TQ = 2048
ATTN_SQ = 256
MERGE_TM = 1024
FFN_TF = 512
COMBINE_TQ = 1024
COMBINE_SUB = 256
COMBINE_WINDOW = 64
VMEM_LIMIT = 56 * 1024 * 1024
INPROJ_VMEM_LIMIT = 58 * 1024 * 1024


def _dot(a, b):
    return jnp.dot(a, b, preferred_element_type=F32)


def _dot_nt(a, b):
    return lax.dot_general(a, b, (((1,), (1,)), ((), ())), preferred_element_type=F32)


def _params(sem, vmem=VMEM_LIMIT):
    return pltpu.CompilerParams(dimension_semantics=sem, vmem_limit_bytes=vmem)


def _ada_kernel(cond_ref, w_ref, b_ref, o_ref):
    c = cond_ref[...]
    s = c / (1.0 + jnp.exp(-c))
    o_ref[...] = _dot(s.astype(BF16), w_ref[...].astype(BF16)) + b_ref[...]


def _ada(cond, w_ada, b_ada):
    n = w_ada.shape[1]
    return pl.pallas_call(
        _ada_kernel,
        out_shape=jax.ShapeDtypeStruct((COND_ROWS, n), F32),
        grid=(n // ADA_TN,),
        in_specs=[
            pl.BlockSpec((COND_ROWS, D_MODEL), lambda i: (0, 0)),
            pl.BlockSpec((D_MODEL, ADA_TN), lambda i: (0, i)),
            pl.BlockSpec((1, ADA_TN), lambda i: (0, i)),
        ],
        out_specs=pl.BlockSpec((COND_ROWS, ADA_TN), lambda i: (0, i)),
        compiler_params=_params(("arbitrary",)),
        name="ada",
    )(cond, w_ada, b_ada)


def _norm_modulate(x, g, shift, scale):
    ms = jnp.mean(x * x, axis=-1, keepdims=True)
    return (x * lax.rsqrt(ms + EPS) * g) * (1.0 + scale) + shift


def _rope(a, cos, sin_hi, sin_lo):
    return a * cos + pltpu.roll(a, LANES - 16, 1) * sin_hi + pltpu.roll(a, 16, 1) * sin_lo


def _inproj_weight_block(s):
    nxt = jnp.minimum(s, BATCH * N_PARTS - 1) % N_PARTS
    return jnp.where(nxt == 1, 2, jnp.where(nxt == 2, 1, nxt))


def _inproj_kernel(x_ref, mod_ref, g_ref, w_hbm, cw_ref, cos_ref, shi_ref, slo_ref, o_ref,
                   hx_ref, z_ref, w_bf, w_f32, w_sem, *, q_scale):
    t = pl.program_id(0)
    j = (t + N_PARTS - 1) % N_PARTS
    chunks = [slice(r * ROW_CHUNK, (r + 1) * ROW_CHUNK) for r in range(SEQ // ROW_CHUNK)]
    ring_slot = t % W_RING

    def weight_copy(s):
        cols = pl.ds(pl.multiple_of(_inproj_weight_block(s) * D_MODEL, D_MODEL), D_MODEL)
        return pltpu.make_async_copy(w_hbm.at[:, cols], w_f32.at[s % W_RING], w_sem.at[s % W_RING])

    @pl.when(t == 0)
    def _():
        for s in range(W_RING - 1):
            weight_copy(s).start()

    @pl.when(t + W_RING - 1 <= BATCH * N_PARTS)
    def _():
        weight_copy(t + W_RING - 1).start()

    weight_copy(t).wait()

    def stage(slot, part):
        rows = slice(part * D_MODEL // len(chunks), (part + 1) * D_MODEL // len(chunks))
        w_bf[slot, rows, :] = w_f32[ring_slot, rows, :].astype(BF16)

    def proj(part, r):
        stage(1 - part % 2, r)
        return _dot(hx_ref[chunks[r], :], w_bf[part % 2])

    def branch(part):
        return pl.when((t > 0) & (j == part))

    @pl.when(t == 0)
    def _():
        for r in range(len(chunks)):
            stage(0, r)

    @branch(0)
    def _():
        shift = mod_ref[0, 0:1, :]
        scale = mod_ref[0, 1:2, :]
        for rows in chunks:
            hx_ref[rows, :] = _norm_modulate(x_ref[0, rows, :], g_ref[...], shift, scale).astype(BF16)
        z_ref[0:8, :] = jnp.zeros((8, D_MODEL), F32)
        z_ref[SEQ + 8:SEQ + 16, :] = jnp.zeros((8, D_MODEL), F32)
        for r, rows in enumerate(chunks):
            z_ref[pl.ds(rows.start + 8, ROW_CHUNK), :] = proj(0, r)

    @branch(1)
    def _():
        for r, rows in enumerate(chunks):
            zr = pl.ds(rows.start + 8, ROW_CHUNK)
            z_ref[zr, :] = z_ref[zr, :] * proj(1, r)

    @branch(2)
    def _():
        first_row = lax.broadcasted_iota(jnp.int32, (8, 1), 0) == 0
        last_row = lax.broadcasted_iota(jnp.int32, (8, 1), 0) == 7
        for r, rows in enumerate(chunks):
            base = rows.start + 8
            zc = z_ref[pl.ds(base, ROW_CHUNK), :]
            before = pltpu.roll(zc * cw_ref[0:1, :], 1, 0)
            after = pltpu.roll(zc * cw_ref[2:3, :], ROW_CHUNK - 1, 0)
            edge_b = jnp.where(first_row, z_ref[pl.ds(base - 8, 8), :][7:8, :] * cw_ref[0:1, :], before[0:8, :])
            edge_a = jnp.where(last_row, z_ref[pl.ds(base + ROW_CHUNK, 8), :][0:1, :] * cw_ref[2:3, :],
                               after[ROW_CHUNK - 8:, :])
            before = jnp.concatenate([edge_b, before[8:, :]], axis=0)
            after = jnp.concatenate([after[:ROW_CHUNK - 8, :], edge_a], axis=0)
            y = before + zc * cw_ref[1:2, :] + after
            o_ref[0, rows, :] = (proj(2, r) * y).astype(BF16)

    def rope_part(part, scale):
        for r, rows in enumerate(chunks):
            acc = proj(part, r)
            cos, shi, slo = cos_ref[rows, :], shi_ref[rows, :], slo_ref[rows, :]
            for h in range(D_MODEL // LANES):
                cols = slice(h * LANES, (h + 1) * LANES)
                roped = _rope(acc[:, cols], cos, shi, slo)
                if scale is not None:
                    roped = roped * scale
                o_ref[0, rows, cols] = roped.astype(BF16)

    @branch(3)
    def _():
        rope_part(3, q_scale)

    @branch(4)
    def _():
        rope_part(4, None)

    for plain_part in (5, 6, 7):
        @branch(plain_part)
        def _(plain_part=plain_part):
            for r, rows in enumerate(chunks):
                o_ref[0, rows, :] = proj(plain_part, r).astype(BF16)


def _inproj(x, mod3, norm_g, w_in, conv_w, cos, sin_hi, sin_lo, q_scale):
    def part(t):
        return jnp.maximum(t - 1, 0) % N_PARTS

    def sample(t):
        return jnp.maximum(t - 1, 0) // N_PARTS

    tab = pl.BlockSpec((SEQ, LANES), lambda t: (0, 0), pipeline_mode=pl.Buffered(1))
    return pl.pallas_call(
        functools.partial(_inproj_kernel, q_scale=q_scale),
        out_shape=jax.ShapeDtypeStruct((N_PARTS - 2, BATCH * SEQ, D_MODEL), BF16),
        grid=(BATCH * N_PARTS + 1,),
        in_specs=[
            pl.BlockSpec((1, SEQ, D_MODEL), lambda t: (sample(t), 0, 0)),
            pl.BlockSpec((1, 6, D_MODEL), lambda t: (sample(t), 0, 0)),
            pl.BlockSpec((1, D_MODEL), lambda t: (0, 0)),
            pl.BlockSpec(memory_space=pl.ANY),
            pl.BlockSpec((3, D_MODEL), lambda t: (0, 0)),
            tab, tab, tab,
        ],
        out_specs=pl.BlockSpec((1, SEQ, D_MODEL), lambda t: (jnp.maximum(part(t) - 2, 0), sample(t), 0)),
        scratch_shapes=[
            pltpu.VMEM((SEQ, D_MODEL), BF16),
            pltpu.VMEM((SEQ + 16, D_MODEL), F32),
            pltpu.VMEM((2, D_MODEL, D_MODEL), BF16),
            pltpu.VMEM((W_RING, D_MODEL, D_MODEL), F32),
            pltpu.SemaphoreType.DMA((W_RING,)),
        ],
        compiler_params=_params(("arbitrary",), vmem=INPROJ_VMEM_LIMIT),
        name="inproj",
    )(x, mod3, norm_g, w_in, conv_w, cos, sin_hi, sin_lo)


def _ctxproj_kernel(c_ref, mod_ref, g_ref, w_ref, o_ref, hc_ref):
    j = pl.program_id(0)
    chunks = [slice(r * ROW_CHUNK, (r + 1) * ROW_CHUNK) for r in range(BATCH * CTX_LEN // ROW_CHUNK)]

    @pl.when(j == 0)
    def _():
        shift = mod_ref[0, 0:1, :]
        scale = mod_ref[0, 1:2, :]
        for rows in chunks:
            hc_ref[rows, :] = _norm_modulate(c_ref[rows, :], g_ref[...], shift, scale).astype(BF16)

    w = w_ref[...].astype(BF16)
    for rows in chunks:
        o_ref[0, rows, :] = _dot(hc_ref[rows, :], w).astype(BF16)


def _ctxproj(ctx2, mod3, norm_g, w_in):
    rows = BATCH * CTX_LEN
    return pl.pallas_call(
        _ctxproj_kernel,
        out_shape=jax.ShapeDtypeStruct((2, rows, D_MODEL), BF16),
        grid=(2,),
        in_specs=[
            pl.BlockSpec((rows, D_MODEL), lambda j: (0, 0)),
            pl.BlockSpec((1, 6, D_MODEL), lambda j: (BATCH, 0, 0)),
            pl.BlockSpec((1, D_MODEL), lambda j: (0, 0)),
            pl.BlockSpec((D_MODEL, D_MODEL), lambda j: (0, 4 + j)),
        ],
        out_specs=pl.BlockSpec((1, rows, D_MODEL), lambda j: (j, 0, 0)),
        scratch_shapes=[pltpu.VMEM((rows, D_MODEL), BF16)],
        compiler_params=_params(("arbitrary",)),
        name="ctxproj",
    )(ctx2, mod3, norm_g, w_in)


def _attn_kernel(lp_ref, q_ref, k_ref, v_ref, kc_ref, vc_ref, g_ref, o_ref, k_all, v_ext, s_ref, m_ref):
    lp = lp_ref[...]
    lam = (jnp.exp(jnp.sum(lp[0:1] * lp[1:2], axis=-1, keepdims=True))
           - jnp.exp(jnp.sum(lp[2:3] * lp[3:4], axis=-1, keepdims=True)) + LAM_INIT)
    k_all[0:CTX_LEN, :] = kc_ref[0]
    k_all[CTX_LEN:N_KEYS, :] = k_ref[0]
    v_ext[0:CTX_LEN, 0:LANES] = vc_ref[0]
    v_ext[CTX_LEN:N_KEYS, 0:LANES] = v_ref[0]
    v_ext[:, LANES:2 * LANES] = jnp.ones((N_KEYS, LANES), BF16)
    first_map = lax.broadcasted_iota(jnp.int32, (1, LANES), 1) < HEAD_DIM

    def scores(u):
        q = q_ref[0, u * ATTN_SQ:(u + 1) * ATTN_SQ, :]
        zero = jnp.zeros_like(q)
        for mp, qm in enumerate((jnp.where(first_map, q, zero), jnp.where(first_map, zero, q))):
            s = _dot_nt(qm, k_all[...])
            s_ref[u % 2, mp] = s
            m_ref[u % 2, mp] = jnp.max(s, axis=-1, keepdims=True)

    def values(u):
        o = []
        for mp in range(2):
            p = jnp.exp2(s_ref[u % 2, mp] - m_ref[u % 2, mp])
            o.append(_dot(p.astype(BF16), v_ext[...]))
        a = o[0][:, :LANES] / o[0][:, LANES:] - o[1][:, :LANES] * (lam / o[1][:, LANES:])
        ms = jnp.mean(a * a, axis=-1, keepdims=True)
        o_ref[u * ATTN_SQ:(u + 1) * ATTN_SQ, :] = (
            (a * lax.rsqrt(ms + EPS) * g_ref[...]) * (1.0 - LAM_INIT)).astype(BF16)

    n_units = ATTN_TQ // ATTN_SQ
    scores(0)
    for u in range(n_units):
        if u + 1 < n_units:
            scores(u + 1)
        values(u)


def _attn(lam_params, parts, ctx_kv, subln_g):
    nq = SEQ // ATTN_TQ
    return pl.pallas_call(
        _attn_kernel,
        out_shape=jax.ShapeDtypeStruct((BATCH * SEQ, D_MODEL), BF16),
        grid=(BATCH, N_HEADS, nq),
        in_specs=[
            pl.BlockSpec((4, HEAD_DIM), lambda b, h, i: (0, 0)),
            pl.BlockSpec((1, ATTN_TQ, LANES), lambda b, h, i: (1, b * nq + i, h)),
            pl.BlockSpec((1, SEQ, LANES), lambda b, h, i: (2, b, h)),
            pl.BlockSpec((1, SEQ, LANES), lambda b, h, i: (3, b, h)),
            pl.BlockSpec((1, CTX_LEN, LANES), lambda b, h, i: (0, b, h)),
            pl.BlockSpec((1, CTX_LEN, LANES), lambda b, h, i: (1, b, h)),
            pl.BlockSpec((1, V_DIM), lambda b, h, i: (0, 0)),
        ],
        out_specs=pl.BlockSpec((ATTN_TQ, LANES), lambda b, h, i: (b * nq + i, h)),
        scratch_shapes=[
            pltpu.VMEM((N_KEYS, LANES), BF16),
            pltpu.VMEM((N_KEYS, 2 * LANES), BF16),
            pltpu.VMEM((2, 2, ATTN_SQ, N_KEYS), F32),
            pltpu.VMEM((2, 2, ATTN_SQ, 1), F32),
        ],
        compiler_params=_params(("arbitrary", "arbitrary", "arbitrary")),
        name="attn",
    )(lam_params, parts, parts, parts, ctx_kv, ctx_kv, subln_g)


def _merge_kernel(yb_ref, s6_ref, s7_ref, o_ref, x_ref, mod_ref, g2_ref, wc_ref, wa_ref, wo_ref,
                  wr_ref, x1_ref, h2_ref, lg_ref):
    chunks = [slice(r * ROW_CHUNK, (r + 1) * ROW_CHUNK) for r in range(MERGE_TM // ROW_CHUNK)]
    for rows in chunks:
        y_conv = _dot(yb_ref[0, rows, :], wc_ref[...])
        y_attn = _dot(o_ref[rows, :], wa_ref[...])
        gate_conv = 1.0 / (1.0 + jnp.exp(-s6_ref[0, rows, :].astype(F32)))
        gate_attn = 1.0 / (1.0 + jnp.exp(-s7_ref[0, rows, :].astype(F32)))
        merged = gate_conv * y_conv + gate_attn * y_attn
        mix = _dot(merged.astype(BF16), wo_ref[...])
        x1_ref[rows, :] = x_ref[rows, :] + mod_ref[0, 2:3, :] * mix
    for rows in chunks:
        h2 = _norm_modulate(x1_ref[rows, :], g2_ref[...], mod_ref[0, 3:4, :], mod_ref[0, 4:5, :])
        h_hi = h2.astype(BF16)
        h2_ref[rows, :] = pltpu.pack_elementwise([h2[:, :D_MODEL // 2], h2[:, D_MODEL // 2:]], packed_dtype=BF16)
        h_lo = (h2 - h_hi.astype(F32)).astype(BF16)
        both = _dot_nt(wr_ref[...], h_hi)
        lg_ref[0, :, rows] = both[:N_EXPERTS] + (_dot_nt(wr_ref[0:N_EXPERTS, :], h_lo) + both[N_EXPERTS:])


def _merge(parts, attn_o, x2, mod3, norm2_g, w_conv_bf, w_attn_bf, w_out_bf, wr_hi_lo):
    per_b = SEQ // MERGE_TM
    sq = pl.BlockSpec((D_MODEL, D_MODEL), lambda i: (0, 0), pipeline_mode=pl.Buffered(1))
    wr = pl.BlockSpec((2 * N_EXPERTS, D_MODEL), lambda i: (0, 0), pipeline_mode=pl.Buffered(1))
    tile = pl.BlockSpec((MERGE_TM, D_MODEL), lambda i: (i, 0))
    return pl.pallas_call(
        _merge_kernel,
        out_shape=(
            jax.ShapeDtypeStruct((BATCH * SEQ, D_MODEL), F32),
            jax.ShapeDtypeStruct((BATCH * SEQ, D_MODEL // 2), jnp.uint32),
            jax.ShapeDtypeStruct((BATCH, N_EXPERTS, SEQ), F32),
        ),
        grid=(BATCH * per_b,),
        in_specs=[
            pl.BlockSpec((1, MERGE_TM, D_MODEL), lambda i: (0, i, 0)),
            pl.BlockSpec((1, MERGE_TM, D_MODEL), lambda i: (4, i, 0)),
            pl.BlockSpec((1, MERGE_TM, D_MODEL), lambda i: (5, i, 0)),
            tile, tile,
            pl.BlockSpec((1, 6, D_MODEL), lambda i: (i // per_b, 0, 0)),
            pl.BlockSpec((1, D_MODEL), lambda i: (0, 0)),
            sq, sq, sq, wr,
        ],
        out_specs=(
            tile,
            pl.BlockSpec((MERGE_TM, D_MODEL // 2), lambda i: (i, 0)),
            pl.BlockSpec((1, N_EXPERTS, MERGE_TM), lambda i: (i // per_b, 0, i % per_b)),
        ),
        compiler_params=_params(("arbitrary",)),
        name="merge",
    )(parts, parts, parts, attn_o, x2, mod3, norm2_g, w_conv_bf, w_attn_bf, w_out_bf, wr_hi_lo)


def _route_kernel(lg_ref, rank_ref, gate_ref, bounds_ref):
    lg = lg_ref[...]
    ex = jnp.exp(lg - jnp.max(lg, axis=1, keepdims=True))
    aff = (ex / jnp.sum(ex, axis=1, keepdims=True)).reshape(BATCH * N_EXPERTS, SEQ)
    bits = lax.bitcast_convert_type(aff, jnp.int32)

    def count(mask):
        return jnp.sum(jnp.where(mask, 1.0, 0.0), axis=1, keepdims=True)

    def search(i, t):
        cand = t + lax.shift_left(jnp.int32(1), 30 - i)
        return jnp.where(count(bits >= cand) >= CAPACITY, cand, t)

    thr = lax.fori_loop(0, 31, search, jnp.zeros((BATCH * N_EXPERTS, 1), jnp.int32))
    need = CAPACITY - count(bits > thr)
    before = (lax.broadcasted_iota(jnp.int32, (LANES, LANES), 0)
              < lax.broadcasted_iota(jnp.int32, (LANES, LANES), 1)).astype(BF16)
    n_tied = jnp.zeros((BATCH * N_EXPERTS, 1), F32)
    n_sel = jnp.zeros((BATCH * N_EXPERTS, 1), F32)
    per_tile = COMBINE_SUB // LANES
    firsts, ends = [], []
    for c in range(SEQ // LANES):
        if c % per_tile == 0:
            firsts.append(n_sel)
        cols = slice(c * LANES, (c + 1) * LANES)
        bits_c = bits[:, cols]
        tied_c = jnp.where(bits_c == thr, 1.0, 0.0)
        tied_before = _dot(tied_c.astype(BF16), before) + n_tied
        sel_c = jnp.where(bits_c > thr, 1.0, jnp.where(tied_before < need, tied_c, 0.0))
        rank = _dot(sel_c.astype(BF16), before) + n_sel
        rank_c = jnp.where(sel_c > 0.0, rank, -1.0)
        gate_c = jnp.where(sel_c > 0.0, aff[:, cols], 0.0)
        n_tied = n_tied + jnp.sum(tied_c, axis=1, keepdims=True)
        n_sel = n_sel + jnp.sum(sel_c, axis=1, keepdims=True)
        rank_ref[:, cols] = rank_c.astype(jnp.int32)
        gate_ref[:, cols] = gate_c
        if c % per_tile == per_tile - 1:
            ends.append(n_sel)
    bounds_ref[...] = jnp.concatenate(firsts + ends, axis=1).astype(jnp.int32)


def _route(logits_t):
    return pl.pallas_call(
        _route_kernel,
        out_shape=(
            jax.ShapeDtypeStruct((BATCH * N_EXPERTS, SEQ), jnp.int32),
            jax.ShapeDtypeStruct((BATCH * N_EXPERTS, SEQ), F32),
            jax.ShapeDtypeStruct((BATCH * N_EXPERTS, 2 * (SEQ // COMBINE_SUB)), jnp.int32),
        ),
        compiler_params=pltpu.CompilerParams(vmem_limit_bytes=VMEM_LIMIT),
        name="route",
    )(logits_t)


def _dispatch(rank_e, h_packed, first_expert, n_experts):
    per_worker = n_experts * BATCH // (SC_CORES * SC_SUBCORES)
    assert per_worker * SC_CORES * SC_SUBCORES == n_experts * BATCH
    half = CAPACITY // 2
    mesh = plsc.VectorSubcoreMesh(core_axis_name="c", subcore_axis_name="s")

    @functools.partial(
        pl.kernel, mesh=mesh,
        out_type=jax.ShapeDtypeStruct((n_experts * BATCH * CAPACITY, D_MODEL // 2), h_packed.dtype),
        scratch_types=[
            pltpu.VMEM((SEQ,), jnp.int32),
            pltpu.VMEM((2, half), jnp.int32),
            pltpu.VMEM((half, D_MODEL // 2), h_packed.dtype),
            pltpu.SemaphoreType.DMA,
        ],
        compiler_params=pltpu.CompilerParams(needs_layout_passes=False),
        name="dispatch",
    )
    def k(table_hbm, rank_hbm, out_hbm, rank_v, idx_v, rows_v, sem):
        worker = lax.axis_index("s") * SC_CORES + lax.axis_index("c")

        @pl.loop(0, per_worker)
        def _(p):
            pair = worker * per_worker + p
            expert = lax.shift_right_logical(pair, BATCH.bit_length() - 1)
            sample = pair & (BATCH - 1)
            pltpu.sync_copy(rank_hbm.at[sample * N_EXPERTS + first_expert + expert], rank_v)

            @pl.loop(0, SEQ // SC_LANES)
            def _(i):
                rank = rank_v[pl.ds(i * SC_LANES, SC_LANES)]
                row = lax.iota(jnp.int32, SC_LANES) + (i * SC_LANES + sample * SEQ)
                plsc.store_scatter(idx_v, [lax.shift_right_arithmetic(rank, 7), rank & (half - 1)], row,
                                   mask=rank >= 0)

            for c in range(2):
                pltpu.async_copy(table_hbm.at[idx_v.at[c]], rows_v, sem).wait()
                pltpu.sync_copy(rows_v, out_hbm.at[pl.ds(pair * CAPACITY + c * half, half)])

    return k(h_packed, rank_e)


def _expert_kernel(xs_ref, wg_ref, wu_ref, wd_ref, y_ref, acc_ref, wg_bf, wu_bf, wd_bf):
    t = pl.program_id(0)
    n_f = EXPERT_HIDDEN // FFN_TF
    f = (t + n_f - 1) % n_f

    n_chunks = BATCH * CAPACITY // ROW_CHUNK

    def stage(slot, part):
        up = slice(part * D_MODEL // n_chunks, (part + 1) * D_MODEL // n_chunks)
        down = slice(part * FFN_TF // n_chunks, (part + 1) * FFN_TF // n_chunks)
        wg_bf[slot, up, :] = wg_ref[0, up, :].astype(BF16)
        wu_bf[slot, up, :] = wu_ref[0, up, :].astype(BF16)
        wd_bf[slot, down, :] = wd_ref[0, down, :].astype(BF16)

    def step(cur):
        for r in range(n_chunks):
            stage(1 - cur, r)
            rows = slice(r * ROW_CHUNK, (r + 1) * ROW_CHUNK)
            packed = xs_ref[0, rows, :]
            half = D_MODEL // 2
            x_lo = pltpu.unpack_elementwise(packed, index=0, packed_dtype=BF16, unpacked_dtype=F32).astype(BF16)
            x_hi = pltpu.unpack_elementwise(packed, index=1, packed_dtype=BF16, unpacked_dtype=F32).astype(BF16)
            a = _dot(x_lo, wg_bf[cur, :half, :]) + _dot(x_hi, wg_bf[cur, half:, :])
            u = _dot(x_lo, wu_bf[cur, :half, :]) + _dot(x_hi, wu_bf[cur, half:, :])
            hidden = (a / (1.0 + jnp.exp(-a))) * u
            total = jnp.where(f == 0, 0.0, acc_ref[rows, :]) + _dot(hidden.astype(BF16), wd_bf[cur])
            acc_ref[rows, :] = total
            y_ref[0, rows, :] = total.astype(BF16)

    @pl.when(t == 0)
    def _():
        for part in range(n_chunks):
            stage(0, part)

    @pl.when(t % 2 == 1)
    def _():
        step(0)

    @pl.when((t % 2 == 0) & (t > 0))
    def _():
        step(1)


def _experts(xs, w_gate, w_up, w_down, first_expert):
    rows = BATCH * CAPACITY
    n_f = EXPERT_HIDDEN // FFN_TF
    n_experts = xs.shape[0]
    last = n_experts * n_f - 1

    def staged(t):
        return jnp.minimum(t, last)

    def computed(t):
        return jnp.maximum(t - 1, 0) // n_f

    return pl.pallas_call(
        _expert_kernel,
        out_shape=jax.ShapeDtypeStruct((n_experts, rows, D_MODEL), BF16),
        grid=(n_experts * n_f + 1,),
        in_specs=[
            pl.BlockSpec((1, rows, D_MODEL // 2), lambda t: (computed(t), 0, 0)),
            pl.BlockSpec((1, D_MODEL, FFN_TF), lambda t: (first_expert + staged(t) // n_f, 0, staged(t) % n_f)),
            pl.BlockSpec((1, D_MODEL, FFN_TF), lambda t: (first_expert + staged(t) // n_f, 0, staged(t) % n_f)),
            pl.BlockSpec((1, FFN_TF, D_MODEL), lambda t: (first_expert + staged(t) // n_f, staged(t) % n_f, 0)),
        ],
        out_specs=pl.BlockSpec((1, rows, D_MODEL), lambda t: (computed(t), 0, 0)),
        scratch_shapes=[
            pltpu.VMEM((rows, D_MODEL), F32),
            pltpu.VMEM((2, D_MODEL, FFN_TF), BF16),
            pltpu.VMEM((2, D_MODEL, FFN_TF), BF16),
            pltpu.VMEM((2, FFN_TF, D_MODEL), BF16),
        ],
        compiler_params=_params(("arbitrary",)),
        name="experts",
    )(xs, w_gate, w_up, w_down)


def _combine_kernel(bounds_ref, rank_ref, gate_ref, *refs):
    n_groups = len(EXPERT_GROUPS)
    y_hbm, (x1_ref, mod_ref, fg_ref, o_ref), scratch = refs[:n_groups], refs[n_groups:n_groups + 4], refs[n_groups + 4:]
    y_bufs, y_sems = scratch[:n_groups], scratch[n_groups:]
    group_of = [(g, e) for g, n in enumerate(EXPERT_GROUPS) for e in range(n)]
    b, t = pl.program_id(0), pl.program_id(1)
    subs = COMBINE_TQ // COMBINE_SUB
    n_sub = SEQ // COMBINE_SUB
    per_dot = CAPACITY // COMBINE_WINDOW

    def y_copies(s):
        rows = pl.ds(pl.multiple_of(s * CAPACITY, CAPACITY), CAPACITY)
        return [pltpu.make_async_copy(y_hbm[g].at[:, rows, :], y_bufs[g].at[s % 2], y_sems[g].at[s % 2])
                for g in range(n_groups)]

    @pl.when((b == 0) & (t == 0))
    def _():
        for copy in y_copies(0):
            copy.start()

    @pl.when((t == 0) & (b + 1 < BATCH))
    def _():
        for copy in y_copies(b + 1):
            copy.start()

    @pl.when(t == 0)
    def _():
        for copy in y_copies(b):
            copy.wait()

    def y_rows(e):
        return y_bufs[group_of[e][0]].at[b % 2, group_of[e][1]]

    def onehot_t(e, cols, first, n_slots):
        slot = lax.broadcasted_iota(jnp.int32, (n_slots, cols.stop - cols.start), 0) + first
        return jnp.where(rank_ref[e:e + 1, cols] == slot, gate_ref[e:e + 1, cols], 0.0).astype(BF16)

    def dot_t(a, b_):
        return lax.dot_general(a, b_, (((0,), (0,)), ((), ())), preferred_element_type=F32)

    def finish(rows, acc):
        x2 = x1_ref[rows, :] + mod_ref[0, 5:6, :] * acc
        ms = jnp.mean(x2 * x2, axis=-1, keepdims=True)
        o_ref[rows, :] = x2 * lax.rsqrt(ms + EPS) * fg_ref[...]

    starts, fits = {}, None
    for s in range(subs):
        for e in range(N_EXPERTS):
            pair = (b * N_EXPERTS + e) * (2 * n_sub) + t * subs + s
            first, end = bounds_ref[pair], bounds_ref[pair + n_sub]
            start = jnp.minimum(lax.shift_left(lax.shift_right_logical(first, 4), 4), CAPACITY - COMBINE_WINDOW)
            starts[s, e] = start
            ok = end <= start + COMBINE_WINDOW
            fits = ok if fits is None else fits & ok

    @pl.when(fits)
    def _():
        for s in range(subs):
            rows = slice(s * COMBINE_SUB, (s + 1) * COMBINE_SUB)
            acc = jnp.zeros((COMBINE_SUB, D_MODEL), F32)
            for e0 in range(0, N_EXPERTS, per_dot):
                first = [pl.multiple_of(starts[s, e0 + i], 16) for i in range(per_dot)]
                onehots = [onehot_t(e0 + i, rows, first[i], COMBINE_WINDOW) for i in range(per_dot)]
                y_win = [y_rows(e0 + i)[pl.ds(first[i], COMBINE_WINDOW), :] for i in range(per_dot)]
                acc = acc + dot_t(jnp.concatenate(onehots, axis=0), jnp.concatenate(y_win, axis=0))
            finish(rows, acc)

    @pl.when(jnp.logical_not(fits))
    def _():
        acc = jnp.zeros((COMBINE_TQ, D_MODEL), F32)
        for e in range(N_EXPERTS):
            acc = acc + dot_t(onehot_t(e, slice(0, COMBINE_TQ), 0, CAPACITY), y_rows(e)[...])
        finish(slice(0, COMBINE_TQ), acc)


def _combine(bounds, rank_e, gate_e, ys, x1, mod3, final_g):
    per_b = SEQ // COMBINE_TQ
    return pl.pallas_call(
        _combine_kernel,
        out_shape=jax.ShapeDtypeStruct((BATCH * SEQ, D_MODEL), F32),
        grid_spec=pltpu.PrefetchScalarGridSpec(
            num_scalar_prefetch=1,
            grid=(BATCH, per_b),
            in_specs=[
                pl.BlockSpec((N_EXPERTS, COMBINE_TQ), lambda b, t, _: (b, t)),
                pl.BlockSpec((N_EXPERTS, COMBINE_TQ), lambda b, t, _: (b, t)),
                *[pl.BlockSpec(memory_space=pl.ANY) for _ in ys],
                pl.BlockSpec((COMBINE_TQ, D_MODEL), lambda b, t, _: (b * per_b + t, 0)),
                pl.BlockSpec((1, 6, D_MODEL), lambda b, t, _: (b, 0, 0)),
                pl.BlockSpec((1, D_MODEL), lambda b, t, _: (0, 0)),
            ],
            out_specs=pl.BlockSpec((COMBINE_TQ, D_MODEL), lambda b, t, _: (b * per_b + t, 0)),
            scratch_shapes=[
                *[pltpu.VMEM((2, y.shape[0], CAPACITY, D_MODEL), BF16) for y in ys],
                *[pltpu.SemaphoreType.DMA((2,)) for _ in ys],
            ],
        ),
        compiler_params=_params(("arbitrary", "arbitrary")),
        name="combine",
    )(bounds.reshape(-1), rank_e, gate_e, *ys, x1, mod3, final_g)


def _rope_tables():
    t = np.arange(SEQ)
    pos = np.stack([t // GRID_W, t % GRID_W], axis=1).astype(np.float32)
    per_axis = HEAD_DIM // 2
    inv = np.float32(ROPE_THETA) ** (-np.arange(0, per_axis, 2, dtype=np.float32) / np.float32(per_axis))
    ang = (pos[:, :, None] * inv).astype(np.float32)
    cos, sin = np.cos(ang.astype(np.float64)), np.sin(ang.astype(np.float64))
    zero = np.zeros_like(sin)

    def lanes(first_half, second_half):
        one_map = np.stack([first_half, second_half], axis=2).reshape(SEQ, HEAD_DIM)
        return jnp.asarray(np.concatenate([one_map, one_map], axis=1), F32)

    return lanes(cos, cos), lanes(-sin, zero), lanes(zero, sin)


def kernel(x, c, ctx, c_ctx, norm1_g, norm2_g, w_ada, b_ada, w_in, conv_w, w_out_conv, lambda_q1,
           lambda_k1, lambda_q2, lambda_k2, subln_g, w_o_attn, w_out, w_router, w_gate_e, w_up_e,
           w_down_e, final_g):
    cond = jnp.concatenate([c, c_ctx[None, :], jnp.zeros((COND_ROWS - BATCH - 1, D_MODEL), F32)], axis=0)
    mod3 = _ada(cond, w_ada[0], b_ada).reshape(COND_ROWS, 6, D_MODEL)

    cos, sin_hi, sin_lo = _rope_tables()
    q_scale = math.log2(math.e) * HEAD_DIM ** -0.5
    parts = _inproj(x, mod3, norm1_g, w_in[0], conv_w[0], cos, sin_hi, sin_lo, q_scale)
    ctx_kv = _ctxproj(ctx.reshape(BATCH * CTX_LEN, D_MODEL), mod3, norm1_g, w_in[0])

    lam_params = jnp.concatenate([lambda_q1, lambda_k1, lambda_q2, lambda_k2], axis=0)
    attn_o = _attn(lam_params, parts, ctx_kv, subln_g)

    wr_t = w_router[0].T
    wr_hi = wr_t.astype(BF16)
    wr_lo = (wr_t - wr_hi.astype(F32)).astype(BF16)
    x1, h2, logits_t = _merge(parts, attn_o, x.reshape(BATCH * SEQ, D_MODEL), mod3, norm2_g,
                              w_out_conv[0].astype(BF16), w_o_attn[0].astype(BF16), w_out[0].astype(BF16),
                              jnp.concatenate([wr_hi, wr_lo], axis=0))

    rank_e, gate_e, bounds = _route(logits_t)
    ys, first = [], 0
    for n in EXPERT_GROUPS:
        xs = _dispatch(rank_e, h2, first, n).reshape(n, BATCH * CAPACITY, D_MODEL // 2)
        ys.append(_experts(xs, w_gate_e[0], w_up_e[0], w_down_e[0], first))
        first += n
    out = _combine(bounds, rank_e, gate_e, ys, x1, mod3, final_g[None, :])
    return out.reshape(BATCH, SEQ, D_MODEL)
```

```python
import functools
import math

import jax
import jax.numpy as jnp
import numpy as np
from jax import lax
from jax.experimental import pallas as pl
from jax.experimental.pallas import tpu as pltpu
from jax.experimental.pallas import tpu_sc as plsc

D_MODEL = 1024
BATCH = 8
SEQ = 2048
GRID_W = 64
CTX_LEN = 256
N_HEADS = 8
HEAD_DIM = 64
V_DIM = 2 * HEAD_DIM
N_EXPERTS = 16
EXPERT_HIDDEN = 2048
CAPACITY = 2 * SEQ // N_EXPERTS
ROPE_THETA = 10000.0
EPS = 1e-6
LAM_INIT = 0.8 - 0.6 * math.exp(-0.3 * 0)
N_PARTS = 8
N_KEYS = CTX_LEN + SEQ
EXPERT_GROUPS = (4, 12)

LANES = 128
SC_CORES = 2
SC_SUBCORES = 16
SC_LANES = 16
F32 = jnp.float32
BF16 = jnp.bfloat16

ROW_CHUNK = 512
COND_ROWS = 16
W_RING = 3
ADA_TN = 1024
ATTN_TQ = 2048
ATTN_SQ = 256
MERGE_TM = 1024
MERGE_X_RING = 3
FFN_TF = 512
COMBINE_TQ = 1024
COMBINE_SUB = 256
COMBINE_WINDOW = 64
VMEM_LIMIT = 56 * 1024 * 1024
INPROJ_VMEM_LIMIT = 58 * 1024 * 1024


def _dot(a, b):
    return jnp.dot(a, b, preferred_element_type=F32)


def _dot_nt(a, b):
    return lax.dot_general(a, b, (((1,), (1,)), ((), ())), preferred_element_type=F32)


def _params(sem, vmem=VMEM_LIMIT):
    return pltpu.CompilerParams(dimension_semantics=sem, vmem_limit_bytes=vmem)


def _ada_kernel(cond_ref, w_ref, b_ref, o_ref):
    c = cond_ref[...]
    s = c / (1.0 + jnp.exp(-c))
    o_ref[...] = _dot(s.astype(BF16), w_ref[...].astype(BF16)) + b_ref[...]


def _ada(cond, w_ada, b_ada):
    n = w_ada.shape[1]
    return pl.pallas_call(
        _ada_kernel,
        out_shape=jax.ShapeDtypeStruct((COND_ROWS, n), F32),
        grid=(n // ADA_TN,),
        in_specs=[
            pl.BlockSpec((COND_ROWS, D_MODEL), lambda i: (0, 0)),
            pl.BlockSpec((D_MODEL, ADA_TN), lambda i: (0, i)),
            pl.BlockSpec((1, ADA_TN), lambda i: (0, i)),
        ],
        out_specs=pl.BlockSpec((COND_ROWS, ADA_TN), lambda i: (0, i)),
        compiler_params=_params(("arbitrary",)),
        name="ada",
    )(cond, w_ada, b_ada)


def _norm_modulate(x, g, shift, scale):
    ms = jnp.mean(x * x, axis=-1, keepdims=True)
    return (x * lax.rsqrt(ms + EPS) * g) * (1.0 + scale) + shift


def _rope(a, cos, sin_hi, sin_lo):
    return a * cos + pltpu.roll(a, LANES - 16, 1) * sin_hi + pltpu.roll(a, 16, 1) * sin_lo


def _inproj_weight_block(s):
    nxt = jnp.minimum(s, BATCH * N_PARTS - 1) % N_PARTS
    return jnp.where(nxt == 1, 2, jnp.where(nxt == 2, 1, nxt))


def _inproj_kernel(x_ref, mod_ref, g_ref, w_hbm, cw_ref, cos_ref, shi_ref, slo_ref, o_ref,
                   hx_ref, z_ref, w_bf, w_f32, w_sem, *, q_scale):
    t = pl.program_id(0)
    j = (t + N_PARTS - 1) % N_PARTS
    chunks = [slice(r * ROW_CHUNK, (r + 1) * ROW_CHUNK) for r in range(SEQ // ROW_CHUNK)]
    ring_slot = t % W_RING

    def weight_copy(s):
        cols = pl.ds(pl.multiple_of(_inproj_weight_block(s) * D_MODEL, D_MODEL), D_MODEL)
        return pltpu.make_async_copy(w_hbm.at[:, cols], w_f32.at[s % W_RING], w_sem.at[s % W_RING])

    @pl.when(t == 0)
    def _():
        for s in range(W_RING - 1):
            weight_copy(s).start()

    @pl.when(t + W_RING - 1 <= BATCH * N_PARTS)
    def _():
        weight_copy(t + W_RING - 1).start()

    weight_copy(t).wait()

    def stage(slot, part):
        rows = slice(part * D_MODEL // len(chunks), (part + 1) * D_MODEL // len(chunks))
        w_bf[slot, rows, :] = w_f32[ring_slot, rows, :].astype(BF16)

    def proj(part, r):
        stage(1 - part % 2, r)
        return _dot(hx_ref[chunks[r], :], w_bf[part % 2])

    def branch(part):
        return pl.when((t > 0) & (j == part))

    @pl.when(t == 0)
    def _():
        for r in range(len(chunks)):
            stage(0, r)

    @branch(0)
    def _():
        shift = mod_ref[0, 0:1, :]
        scale = mod_ref[0, 1:2, :]
        for rows in chunks:
            hx_ref[rows, :] = _norm_modulate(x_ref[0, rows, :], g_ref[...], shift, scale).astype(BF16)
        z_ref[0:8, :] = jnp.zeros((8, D_MODEL), F32)
        z_ref[SEQ + 8:SEQ + 16, :] = jnp.zeros((8, D_MODEL), F32)
        for r, rows in enumerate(chunks):
            z_ref[pl.ds(rows.start + 8, ROW_CHUNK), :] = proj(0, r)

    @branch(1)
    def _():
        for r, rows in enumerate(chunks):
            zr = pl.ds(rows.start + 8, ROW_CHUNK)
            z_ref[zr, :] = z_ref[zr, :] * proj(1, r)

    @branch(2)
    def _():
        first_row = lax.broadcasted_iota(jnp.int32, (8, 1), 0) == 0
        last_row = lax.broadcasted_iota(jnp.int32, (8, 1), 0) == 7
        for r, rows in enumerate(chunks):
            base = rows.start + 8
            zc = z_ref[pl.ds(base, ROW_CHUNK), :]
            before = pltpu.roll(zc * cw_ref[0:1, :], 1, 0)
            after = pltpu.roll(zc * cw_ref[2:3, :], ROW_CHUNK - 1, 0)
            edge_b = jnp.where(first_row, z_ref[pl.ds(base - 8, 8), :][7:8, :] * cw_ref[0:1, :], before[0:8, :])
            edge_a = jnp.where(last_row, z_ref[pl.ds(base + ROW_CHUNK, 8), :][0:1, :] * cw_ref[2:3, :],
                               after[ROW_CHUNK - 8:, :])
            before = jnp.concatenate([edge_b, before[8:, :]], axis=0)
            after = jnp.concatenate([after[:ROW_CHUNK - 8, :], edge_a], axis=0)
            y = before + zc * cw_ref[1:2, :] + after
            o_ref[0, rows, :] = (proj(2, r) * y).astype(BF16)

    def rope_part(part, scale):
        for r, rows in enumerate(chunks):
            acc = proj(part, r)
            cos, shi, slo = cos_ref[rows, :], shi_ref[rows, :], slo_ref[rows, :]
            for h in range(D_MODEL // LANES):
                cols = slice(h * LANES, (h + 1) * LANES)
                roped = _rope(acc[:, cols], cos, shi, slo)
                if scale is not None:
                    roped = roped * scale
                o_ref[0, rows, cols] = roped.astype(BF16)

    @branch(3)
    def _():
        rope_part(3, q_scale)

    @branch(4)
    def _():
        rope_part(4, None)

    for plain_part in (5, 6, 7):
        @branch(plain_part)
        def _(plain_part=plain_part):
            for r, rows in enumerate(chunks):
                o_ref[0, rows, :] = proj(plain_part, r).astype(BF16)


def _inproj(x, mod3, norm_g, w_in, conv_w, cos, sin_hi, sin_lo, q_scale):
    def part(t):
        return jnp.maximum(t - 1, 0) % N_PARTS

    def sample(t):
        return jnp.maximum(t - 1, 0) // N_PARTS

    tab = pl.BlockSpec((SEQ, LANES), lambda t: (0, 0), pipeline_mode=pl.Buffered(1))
    return pl.pallas_call(
        functools.partial(_inproj_kernel, q_scale=q_scale),
        out_shape=jax.ShapeDtypeStruct((N_PARTS - 2, BATCH * SEQ, D_MODEL), BF16),
        grid=(BATCH * N_PARTS + 1,),
        in_specs=[
            pl.BlockSpec((1, SEQ, D_MODEL), lambda t: (sample(t), 0, 0)),
            pl.BlockSpec((1, 6, D_MODEL), lambda t: (sample(t), 0, 0)),
            pl.BlockSpec((1, D_MODEL), lambda t: (0, 0)),
            pl.BlockSpec(memory_space=pl.ANY),
            pl.BlockSpec((3, D_MODEL), lambda t: (0, 0)),
            tab, tab, tab,
        ],
        out_specs=pl.BlockSpec((1, SEQ, D_MODEL), lambda t: (jnp.maximum(part(t) - 2, 0), sample(t), 0)),
        scratch_shapes=[
            pltpu.VMEM((SEQ, D_MODEL), BF16),
            pltpu.VMEM((SEQ + 16, D_MODEL), F32),
            pltpu.VMEM((2, D_MODEL, D_MODEL), BF16),
            pltpu.VMEM((W_RING, D_MODEL, D_MODEL), F32),
            pltpu.SemaphoreType.DMA((W_RING,)),
        ],
        compiler_params=_params(("arbitrary",), vmem=INPROJ_VMEM_LIMIT),
        name="inproj",
    )(x, mod3, norm_g, w_in, conv_w, cos, sin_hi, sin_lo)


def _ctxproj_kernel(c_ref, mod_ref, g_ref, w_ref, o_ref, hc_ref):
    j = pl.program_id(0)
    chunks = [slice(r * ROW_CHUNK, (r + 1) * ROW_CHUNK) for r in range(BATCH * CTX_LEN // ROW_CHUNK)]

    @pl.when(j == 0)
    def _():
        shift = mod_ref[0, 0:1, :]
        scale = mod_ref[0, 1:2, :]
        for rows in chunks:
            hc_ref[rows, :] = _norm_modulate(c_ref[rows, :], g_ref[...], shift, scale).astype(BF16)

    w = w_ref[...].astype(BF16)
    for rows in chunks:
        o_ref[0, rows, :] = _dot(hc_ref[rows, :], w).astype(BF16)


def _ctxproj(ctx2, mod3, norm_g, w_in):
    rows = BATCH * CTX_LEN
    return pl.pallas_call(
        _ctxproj_kernel,
        out_shape=jax.ShapeDtypeStruct((2, rows, D_MODEL), BF16),
        grid=(2,),
        in_specs=[
            pl.BlockSpec((rows, D_MODEL), lambda j: (0, 0)),
            pl.BlockSpec((1, 6, D_MODEL), lambda j: (BATCH, 0, 0)),
            pl.BlockSpec((1, D_MODEL), lambda j: (0, 0)),
            pl.BlockSpec((D_MODEL, D_MODEL), lambda j: (0, 4 + j)),
        ],
        out_specs=pl.BlockSpec((1, rows, D_MODEL), lambda j: (j, 0, 0)),
        scratch_shapes=[pltpu.VMEM((rows, D_MODEL), BF16)],
        compiler_params=_params(("arbitrary",)),
        name="ctxproj",
    )(ctx2, mod3, norm_g, w_in)


def _attn_kernel(lp_ref, q_ref, k_ref, v_ref, kc_ref, vc_ref, g_ref, o_ref, k_all, v_ext, s_ref, m_ref):
    lp = lp_ref[...]
    lam = (jnp.exp(jnp.sum(lp[0:1] * lp[1:2], axis=-1, keepdims=True))
           - jnp.exp(jnp.sum(lp[2:3] * lp[3:4], axis=-1, keepdims=True)) + LAM_INIT)
    k_all[0:CTX_LEN, :] = kc_ref[0]
    k_all[CTX_LEN:N_KEYS, :] = k_ref[0]
    v_ext[0:CTX_LEN, 0:LANES] = vc_ref[0]
    v_ext[CTX_LEN:N_KEYS, 0:LANES] = v_ref[0]
    v_ext[:, LANES:2 * LANES] = jnp.ones((N_KEYS, LANES), BF16)
    first_map = lax.broadcasted_iota(jnp.int32, (1, LANES), 1) < HEAD_DIM

    def scores(u):
        q = q_ref[0, u * ATTN_SQ:(u + 1) * ATTN_SQ, :]
        zero = jnp.zeros_like(q)
        for mp, qm in enumerate((jnp.where(first_map, q, zero), jnp.where(first_map, zero, q))):
            s = _dot_nt(qm, k_all[...])
            s_ref[u % 2, mp] = s
            m_ref[u % 2, mp] = jnp.max(s, axis=-1, keepdims=True)

    def values(u):
        o = []
        for mp in range(2):
            p = jnp.exp2(s_ref[u % 2, mp] - m_ref[u % 2, mp])
            o.append(_dot(p.astype(BF16), v_ext[...]))
        a = o[0][:, :LANES] / o[0][:, LANES:] - o[1][:, :LANES] * (lam / o[1][:, LANES:])
        ms = jnp.mean(a * a, axis=-1, keepdims=True)
        o_ref[u * ATTN_SQ:(u + 1) * ATTN_SQ, :] = (
            (a * lax.rsqrt(ms + EPS) * g_ref[...]) * (1.0 - LAM_INIT)).astype(BF16)

    n_units = ATTN_TQ // ATTN_SQ
    scores(0)
    for u in range(n_units):
        if u + 1 < n_units:
            scores(u + 1)
        values(u)


def _attn(lam_params, parts, ctx_kv, subln_g):
    nq = SEQ // ATTN_TQ
    return pl.pallas_call(
        _attn_kernel,
        out_shape=jax.ShapeDtypeStruct((BATCH * SEQ, D_MODEL), BF16),
        grid=(BATCH, N_HEADS, nq),
        in_specs=[
            pl.BlockSpec((4, HEAD_DIM), lambda b, h, i: (0, 0)),
            pl.BlockSpec((1, ATTN_TQ, LANES), lambda b, h, i: (1, b * nq + i, h)),
            pl.BlockSpec((1, SEQ, LANES), lambda b, h, i: (2, b, h)),
            pl.BlockSpec((1, SEQ, LANES), lambda b, h, i: (3, b, h)),
            pl.BlockSpec((1, CTX_LEN, LANES), lambda b, h, i: (0, b, h)),
            pl.BlockSpec((1, CTX_LEN, LANES), lambda b, h, i: (1, b, h)),
            pl.BlockSpec((1, V_DIM), lambda b, h, i: (0, 0)),
        ],
        out_specs=pl.BlockSpec((ATTN_TQ, LANES), lambda b, h, i: (b * nq + i, h)),
        scratch_shapes=[
            pltpu.VMEM((N_KEYS, LANES), BF16),
            pltpu.VMEM((N_KEYS, 2 * LANES), BF16),
            pltpu.VMEM((2, 2, ATTN_SQ, N_KEYS), F32),
            pltpu.VMEM((2, 2, ATTN_SQ, 1), F32),
        ],
        compiler_params=_params(("arbitrary", "arbitrary", "arbitrary")),
        name="attn",
    )(lam_params, parts, parts, parts, ctx_kv, ctx_kv, subln_g)


def _merge_kernel(yb_ref, s6_ref, s7_ref, o_ref, x_hbm, mod_ref, g2_ref, wc_ref, wa_ref, wo_ref,
                  wr_ref, x1_ref, h2_ref, lg_ref, x_ring, x_sem):
    i = pl.program_id(0)
    chunks = [slice(r * ROW_CHUNK, (r + 1) * ROW_CHUNK) for r in range(MERGE_TM // ROW_CHUNK)]

    def x_copy(s):
        rows = pl.ds(pl.multiple_of(s * MERGE_TM, MERGE_TM), MERGE_TM)
        return pltpu.make_async_copy(x_hbm.at[rows, :], x_ring.at[s % MERGE_X_RING], x_sem.at[s % MERGE_X_RING])

    @pl.when(i == 0)
    def _():
        for s in range(MERGE_X_RING - 1):
            x_copy(s).start()

    @pl.when(i + MERGE_X_RING - 1 < BATCH * SEQ // MERGE_TM)
    def _():
        x_copy(i + MERGE_X_RING - 1).start()

    x_copy(i).wait()
    x_ref = x_ring.at[i % MERGE_X_RING]
    for rows in chunks:
        y_conv = _dot(yb_ref[0, rows, :], wc_ref[...])
        y_attn = _dot(o_ref[rows, :], wa_ref[...])
        gate_conv = 1.0 / (1.0 + jnp.exp(-s6_ref[0, rows, :].astype(F32)))
        gate_attn = 1.0 / (1.0 + jnp.exp(-s7_ref[0, rows, :].astype(F32)))
        merged = gate_conv * y_conv + gate_attn * y_attn
        mix = _dot(merged.astype(BF16), wo_ref[...])
        x1_ref[rows, :] = x_ref[rows, :] + mod_ref[0, 2:3, :] * mix
    for rows in chunks:
        h2 = _norm_modulate(x1_ref[rows, :], g2_ref[...], mod_ref[0, 3:4, :], mod_ref[0, 4:5, :])
        h_hi = h2.astype(BF16)
        h2_ref[rows, :] = pltpu.pack_elementwise([h2[:, :D_MODEL // 2], h2[:, D_MODEL // 2:]], packed_dtype=BF16)
        h_lo = (h2 - h_hi.astype(F32)).astype(BF16)
        both = _dot_nt(wr_ref[...], h_hi)
        lg_ref[0, :, rows] = both[:N_EXPERTS] + (_dot_nt(wr_ref[0:N_EXPERTS, :], h_lo) + both[N_EXPERTS:])


def _merge(parts, attn_o, x2, mod3, norm2_g, w_conv_bf, w_attn_bf, w_out_bf, wr_hi_lo):
    per_b = SEQ // MERGE_TM
    sq = pl.BlockSpec((D_MODEL, D_MODEL), lambda i: (0, 0), pipeline_mode=pl.Buffered(1))
    wr = pl.BlockSpec((2 * N_EXPERTS, D_MODEL), lambda i: (0, 0), pipeline_mode=pl.Buffered(1))
    tile = pl.BlockSpec((MERGE_TM, D_MODEL), lambda i: (i, 0))
    return pl.pallas_call(
        _merge_kernel,
        out_shape=(
            jax.ShapeDtypeStruct((BATCH * SEQ, D_MODEL), F32),
            jax.ShapeDtypeStruct((BATCH * SEQ, D_MODEL // 2), jnp.uint32),
            jax.ShapeDtypeStruct((BATCH, N_EXPERTS, SEQ), F32),
        ),
        grid=(BATCH * per_b,),
        in_specs=[
            pl.BlockSpec((1, MERGE_TM, D_MODEL), lambda i: (0, i, 0)),
            pl.BlockSpec((1, MERGE_TM, D_MODEL), lambda i: (4, i, 0)),
            pl.BlockSpec((1, MERGE_TM, D_MODEL), lambda i: (5, i, 0)),
            tile,
            pl.BlockSpec(memory_space=pl.ANY),
            pl.BlockSpec((1, 6, D_MODEL), lambda i: (i // per_b, 0, 0)),
            pl.BlockSpec((1, D_MODEL), lambda i: (0, 0)),
            sq, sq, sq, wr,
        ],
        out_specs=(
            tile,
            pl.BlockSpec((MERGE_TM, D_MODEL // 2), lambda i: (i, 0)),
            pl.BlockSpec((1, N_EXPERTS, MERGE_TM), lambda i: (i // per_b, 0, i % per_b)),
        ),
        scratch_shapes=[
            pltpu.VMEM((MERGE_X_RING, MERGE_TM, D_MODEL), F32),
            pltpu.SemaphoreType.DMA((MERGE_X_RING,)),
        ],
        compiler_params=_params(("arbitrary",)),
        name="merge",
    )(parts, parts, parts, attn_o, x2, mod3, norm2_g, w_conv_bf, w_attn_bf, w_out_bf, wr_hi_lo)


def _route_kernel(lg_ref, rank_ref, gate_ref, bounds_ref):
    lg = lg_ref[...]
    ex = jnp.exp(lg - jnp.max(lg, axis=1, keepdims=True))
    aff = (ex / jnp.sum(ex, axis=1, keepdims=True)).reshape(BATCH * N_EXPERTS, SEQ)
    bits = lax.bitcast_convert_type(aff, jnp.int32)

    def count(mask):
        return jnp.sum(jnp.where(mask, 1.0, 0.0), axis=1, keepdims=True)

    def search(i, t):
        cand = t + lax.shift_left(jnp.int32(1), 30 - i)
        return jnp.where(count(bits >= cand) >= CAPACITY, cand, t)

    thr = lax.fori_loop(0, 31, search, jnp.zeros((BATCH * N_EXPERTS, 1), jnp.int32))
    need = CAPACITY - count(bits > thr)
    before = (lax.broadcasted_iota(jnp.int32, (LANES, LANES), 0)
              < lax.broadcasted_iota(jnp.int32, (LANES, LANES), 1)).astype(BF16)
    n_tied = jnp.zeros((BATCH * N_EXPERTS, 1), F32)
    n_sel = jnp.zeros((BATCH * N_EXPERTS, 1), F32)
    per_tile = COMBINE_SUB // LANES
    firsts, ends = [], []
    for c in range(SEQ // LANES):
        if c % per_tile == 0:
            firsts.append(n_sel)
        cols = slice(c * LANES, (c + 1) * LANES)
        bits_c = bits[:, cols]
        tied_c = jnp.where(bits_c == thr, 1.0, 0.0)
        tied_before = _dot(tied_c.astype(BF16), before) + n_tied
        sel_c = jnp.where(bits_c > thr, 1.0, jnp.where(tied_before < need, tied_c, 0.0))
        rank = _dot(sel_c.astype(BF16), before) + n_sel
        rank_c = jnp.where(sel_c > 0.0, rank, -1.0)
        gate_c = jnp.where(sel_c > 0.0, aff[:, cols], 0.0)
        n_tied = n_tied + jnp.sum(tied_c, axis=1, keepdims=True)
        n_sel = n_sel + jnp.sum(sel_c, axis=1, keepdims=True)
        rank_ref[:, cols] = rank_c.astype(jnp.int32)
        gate_ref[:, cols] = gate_c
        if c % per_tile == per_tile - 1:
            ends.append(n_sel)
    bounds_ref[...] = jnp.concatenate(firsts + ends, axis=1).astype(jnp.int32)


def _route(logits_t):
    return pl.pallas_call(
        _route_kernel,
        out_shape=(
            jax.ShapeDtypeStruct((BATCH * N_EXPERTS, SEQ), jnp.int32),
            jax.ShapeDtypeStruct((BATCH * N_EXPERTS, SEQ), F32),
            jax.ShapeDtypeStruct((BATCH * N_EXPERTS, 2 * (SEQ // COMBINE_SUB)), jnp.int32),
        ),
        compiler_params=pltpu.CompilerParams(vmem_limit_bytes=VMEM_LIMIT),
        name="route",
    )(logits_t)


def _dispatch(rank_e, h_packed, first_expert, n_experts):
    per_worker = n_experts * BATCH // (SC_CORES * SC_SUBCORES)
    assert per_worker * SC_CORES * SC_SUBCORES == n_experts * BATCH
    half = CAPACITY // 2
    mesh = plsc.VectorSubcoreMesh(core_axis_name="c", subcore_axis_name="s")

    @functools.partial(
        pl.kernel, mesh=mesh,
        out_type=jax.ShapeDtypeStruct((n_experts * BATCH * CAPACITY, D_MODEL // 2), h_packed.dtype),
        scratch_types=[
            pltpu.VMEM((SEQ,), jnp.int32),
            pltpu.VMEM((2, half), jnp.int32),
            pltpu.VMEM((half, D_MODEL // 2), h_packed.dtype),
            pltpu.SemaphoreType.DMA,
        ],
        compiler_params=pltpu.CompilerParams(needs_layout_passes=False),
        name="dispatch",
    )
    def k(table_hbm, rank_hbm, out_hbm, rank_v, idx_v, rows_v, sem):
        worker = lax.axis_index("s") * SC_CORES + lax.axis_index("c")

        @pl.loop(0, per_worker)
        def _(p):
            pair = worker * per_worker + p
            expert = lax.shift_right_logical(pair, BATCH.bit_length() - 1)
            sample = pair & (BATCH - 1)
            pltpu.sync_copy(rank_hbm.at[sample * N_EXPERTS + first_expert + expert], rank_v)

            @pl.loop(0, SEQ // SC_LANES)
            def _(i):
                rank = rank_v[pl.ds(i * SC_LANES, SC_LANES)]
                row = lax.iota(jnp.int32, SC_LANES) + (i * SC_LANES + sample * SEQ)
                plsc.store_scatter(idx_v, [lax.shift_right_arithmetic(rank, 7), rank & (half - 1)], row,
                                   mask=rank >= 0)

            for c in range(2):
                pltpu.async_copy(table_hbm.at[idx_v.at[c]], rows_v, sem).wait()
                pltpu.sync_copy(rows_v, out_hbm.at[pl.ds(pair * CAPACITY + c * half, half)])

    return k(h_packed, rank_e)


def _expert_kernel(xs_ref, wg_ref, wu_ref, wd_ref, y_ref, acc_ref, wg_bf, wu_bf, wd_bf):
    t = pl.program_id(0)
    n_f = EXPERT_HIDDEN // FFN_TF
    f = (t + n_f - 1) % n_f

    n_chunks = BATCH * CAPACITY // ROW_CHUNK

    def stage(slot, part):
        up = slice(part * D_MODEL // n_chunks, (part + 1) * D_MODEL // n_chunks)
        down = slice(part * FFN_TF // n_chunks, (part + 1) * FFN_TF // n_chunks)
        wg_bf[slot, up, :] = wg_ref[0, up, :].astype(BF16)
        wu_bf[slot, up, :] = wu_ref[0, up, :].astype(BF16)
        wd_bf[slot, down, :] = wd_ref[0, down, :].astype(BF16)

    def step(cur):
        for r in range(n_chunks):
            stage(1 - cur, r)
            rows = slice(r * ROW_CHUNK, (r + 1) * ROW_CHUNK)
            packed = xs_ref[0, rows, :]
            half = D_MODEL // 2
            x_lo = pltpu.unpack_elementwise(packed, index=0, packed_dtype=BF16, unpacked_dtype=F32).astype(BF16)
            x_hi = pltpu.unpack_elementwise(packed, index=1, packed_dtype=BF16, unpacked_dtype=F32).astype(BF16)
            a = _dot(x_lo, wg_bf[cur, :half, :]) + _dot(x_hi, wg_bf[cur, half:, :])
            u = _dot(x_lo, wu_bf[cur, :half, :]) + _dot(x_hi, wu_bf[cur, half:, :])
            hidden = (a / (1.0 + jnp.exp(-a))) * u
            total = jnp.where(f == 0, 0.0, acc_ref[rows, :]) + _dot(hidden.astype(BF16), wd_bf[cur])
            acc_ref[rows, :] = total
            y_ref[0, rows, :] = total.astype(BF16)

    @pl.when(t == 0)
    def _():
        for part in range(n_chunks):
            stage(0, part)

    @pl.when(t % 2 == 1)
    def _():
        step(0)

    @pl.when((t % 2 == 0) & (t > 0))
    def _():
        step(1)


def _experts(xs, w_gate, w_up, w_down, first_expert):
    rows = BATCH * CAPACITY
    n_f = EXPERT_HIDDEN // FFN_TF
    n_experts = xs.shape[0]
    last = n_experts * n_f - 1

    def staged(t):
        return jnp.minimum(t, last)

    def computed(t):
        return jnp.maximum(t - 1, 0) // n_f

    return pl.pallas_call(
        _expert_kernel,
        out_shape=jax.ShapeDtypeStruct((n_experts, rows, D_MODEL), BF16),
        grid=(n_experts * n_f + 1,),
        in_specs=[
            pl.BlockSpec((1, rows, D_MODEL // 2), lambda t: (computed(t), 0, 0)),
            pl.BlockSpec((1, D_MODEL, FFN_TF), lambda t: (first_expert + staged(t) // n_f, 0, staged(t) % n_f)),
            pl.BlockSpec((1, D_MODEL, FFN_TF), lambda t: (first_expert + staged(t) // n_f, 0, staged(t) % n_f)),
            pl.BlockSpec((1, FFN_TF, D_MODEL), lambda t: (first_expert + staged(t) // n_f, staged(t) % n_f, 0)),
        ],
        out_specs=pl.BlockSpec((1, rows, D_MODEL), lambda t: (computed(t), 0, 0)),
        scratch_shapes=[
            pltpu.VMEM((rows, D_MODEL), F32),
            pltpu.VMEM((2, D_MODEL, FFN_TF), BF16),
            pltpu.VMEM((2, D_MODEL, FFN_TF), BF16),
            pltpu.VMEM((2, FFN_TF, D_MODEL), BF16),
        ],
        compiler_params=_params(("arbitrary",)),
        name="experts",
    )(xs, w_gate, w_up, w_down)


def _combine_kernel(bounds_ref, rank_ref, gate_ref, *refs):
    n_groups = len(EXPERT_GROUPS)
    y_hbm, (x1_ref, mod_ref, fg_ref, o_ref), scratch = refs[:n_groups], refs[n_groups:n_groups + 4], refs[n_groups + 4:]
    y_bufs, y_sems = scratch[:n_groups], scratch[n_groups:]
    group_of = [(g, e) for g, n in enumerate(EXPERT_GROUPS) for e in range(n)]
    b, t = pl.program_id(0), pl.program_id(1)
    subs = COMBINE_TQ // COMBINE_SUB
    n_sub = SEQ // COMBINE_SUB
    per_dot = CAPACITY // COMBINE_WINDOW

    def y_copies(s):
        rows = pl.ds(pl.multiple_of(s * CAPACITY, CAPACITY), CAPACITY)
        return [pltpu.make_async_copy(y_hbm[g].at[:, rows, :], y_bufs[g].at[s % 2], y_sems[g].at[s % 2])
                for g in range(n_groups)]

    @pl.when((b == 0) & (t == 0))
    def _():
        for copy in y_copies(0):
            copy.start()

    @pl.when((t == 0) & (b + 1 < BATCH))
    def _():
        for copy in y_copies(b + 1):
            copy.start()

    @pl.when(t == 0)
    def _():
        for copy in y_copies(b):
            copy.wait()

    def y_rows(e):
        return y_bufs[group_of[e][0]].at[b % 2, group_of[e][1]]

    def onehot_t(e, cols, first, n_slots):
        slot = lax.broadcasted_iota(jnp.int32, (n_slots, cols.stop - cols.start), 0) + first
        return jnp.where(rank_ref[e:e + 1, cols] == slot, gate_ref[e:e + 1, cols], 0.0).astype(BF16)

    def dot_t(a, b_):
        return lax.dot_general(a, b_, (((0,), (0,)), ((), ())), preferred_element_type=F32)

    def finish(rows, acc):
        x2 = x1_ref[rows, :] + mod_ref[0, 5:6, :] * acc
        ms = jnp.mean(x2 * x2, axis=-1, keepdims=True)
        o_ref[rows, :] = x2 * lax.rsqrt(ms + EPS) * fg_ref[...]

    starts, fits = {}, None
    for s in range(subs):
        for e in range(N_EXPERTS):
            pair = (b * N_EXPERTS + e) * (2 * n_sub) + t * subs + s
            first, end = bounds_ref[pair], bounds_ref[pair + n_sub]
            start = jnp.minimum(lax.shift_left(lax.shift_right_logical(first, 4), 4), CAPACITY - COMBINE_WINDOW)
            starts[s, e] = start
            ok = end <= start + COMBINE_WINDOW
            fits = ok if fits is None else fits & ok

    @pl.when(fits)
    def _():
        for s in range(subs):
            rows = slice(s * COMBINE_SUB, (s + 1) * COMBINE_SUB)
            acc = jnp.zeros((COMBINE_SUB, D_MODEL), F32)
            for e0 in range(0, N_EXPERTS, per_dot):
                first = [pl.multiple_of(starts[s, e0 + i], 16) for i in range(per_dot)]
                onehots = [onehot_t(e0 + i, rows, first[i], COMBINE_WINDOW) for i in range(per_dot)]
                y_win = [y_rows(e0 + i)[pl.ds(first[i], COMBINE_WINDOW), :] for i in range(per_dot)]
                acc = acc + dot_t(jnp.concatenate(onehots, axis=0), jnp.concatenate(y_win, axis=0))
            finish(rows, acc)

    @pl.when(jnp.logical_not(fits))
    def _():
        acc = jnp.zeros((COMBINE_TQ, D_MODEL), F32)
        for e in range(N_EXPERTS):
            acc = acc + dot_t(onehot_t(e, slice(0, COMBINE_TQ), 0, CAPACITY), y_rows(e)[...])
        finish(slice(0, COMBINE_TQ), acc)


def _combine(bounds, rank_e, gate_e, ys, x1, mod3, final_g):
    per_b = SEQ // COMBINE_TQ
    return pl.pallas_call(
        _combine_kernel,
        out_shape=jax.ShapeDtypeStruct((BATCH * SEQ, D_MODEL), F32),
        grid_spec=pltpu.PrefetchScalarGridSpec(
            num_scalar_prefetch=1,
            grid=(BATCH, per_b),
            in_specs=[
                pl.BlockSpec((N_EXPERTS, COMBINE_TQ), lambda b, t, _: (b, t)),
                pl.BlockSpec((N_EXPERTS, COMBINE_TQ), lambda b, t, _: (b, t)),
                *[pl.BlockSpec(memory_space=pl.ANY) for _ in ys],
                pl.BlockSpec((COMBINE_TQ, D_MODEL), lambda b, t, _: (b * per_b + t, 0)),
                pl.BlockSpec((1, 6, D_MODEL), lambda b, t, _: (b, 0, 0)),
                pl.BlockSpec((1, D_MODEL), lambda b, t, _: (0, 0)),
            ],
            out_specs=pl.BlockSpec((COMBINE_TQ, D_MODEL), lambda b, t, _: (b * per_b + t, 0)),
            scratch_shapes=[
                *[pltpu.VMEM((2, y.shape[0], CAPACITY, D_MODEL), BF16) for y in ys],
                *[pltpu.SemaphoreType.DMA((2,)) for _ in ys],
            ],
        ),
        compiler_params=_params(("arbitrary", "arbitrary")),
        name="combine",
    )(bounds.reshape(-1), rank_e, gate_e, *ys, x1, mod3, final_g)


def _rope_tables():
    t = np.arange(SEQ)
    pos = np.stack([t // GRID_W, t % GRID_W], axis=1).astype(np.float32)
    per_axis = HEAD_DIM // 2
    inv = np.float32(ROPE_THETA) ** (-np.arange(0, per_axis, 2, dtype=np.float32) / np.float32(per_axis))
    ang = (pos[:, :, None] * inv).astype(np.float32)
    cos, sin = np.cos(ang.astype(np.float64)), np.sin(ang.astype(np.float64))
    zero = np.zeros_like(sin)

    def lanes(first_half, second_half):
        one_map = np.stack([first_half, second_half], axis=2).reshape(SEQ, HEAD_DIM)
        return jnp.asarray(np.concatenate([one_map, one_map], axis=1), F32)

    return lanes(cos, cos), lanes(-sin, zero), lanes(zero, sin)


def kernel(x, c, ctx, c_ctx, norm1_g, norm2_g, w_ada, b_ada, w_in, conv_w, w_out_conv, lambda_q1,
           lambda_k1, lambda_q2, lambda_k2, subln_g, w_o_attn, w_out, w_router, w_gate_e, w_up_e,
           w_down_e, final_g):
    cond = jnp.concatenate([c, c_ctx[None, :], jnp.zeros((COND_ROWS - BATCH - 1, D_MODEL), F32)], axis=0)
    mod3 = _ada(cond, w_ada[0], b_ada).reshape(COND_ROWS, 6, D_MODEL)

    cos, sin_hi, sin_lo = _rope_tables()
    q_scale = math.log2(math.e) * HEAD_DIM ** -0.5
    parts = _inproj(x, mod3, norm1_g, w_in[0], conv_w[0], cos, sin_hi, sin_lo, q_scale)
    ctx_kv = _ctxproj(ctx.reshape(BATCH * CTX_LEN, D_MODEL), mod3, norm1_g, w_in[0])

    lam_params = jnp.concatenate([lambda_q1, lambda_k1, lambda_q2, lambda_k2], axis=0)
    attn_o = _attn(lam_params, parts, ctx_kv, subln_g)

    wr_t = w_router[0].T
    wr_hi = wr_t.astype(BF16)
    wr_lo = (wr_t - wr_hi.astype(F32)).astype(BF16)
    x1, h2, logits_t = _merge(parts, attn_o, x.reshape(BATCH * SEQ, D_MODEL), mod3, norm2_g,
                              w_out_conv[0].astype(BF16), w_o_attn[0].astype(BF16), w_out[0].astype(BF16),
                              jnp.concatenate([wr_hi, wr_lo], axis=0))

    rank_e, gate_e, bounds = _route(logits_t)
    ys, first = [], 0
    for n in EXPERT_GROUPS:
        xs = _dispatch(rank_e, h2, first, n).reshape(n, BATCH * CAPACITY, D_MODEL // 2)
        ys.append(_experts(xs, w_gate_e[0], w_up_e[0], w_down_e[0], first))
        first += n
    out = _combine(bounds, rank_e, gate_e, ys, x1, mod3, final_g[None, :])
    return out.reshape(BATCH, SEQ, D_MODEL)
```

```python
import functools
import math

import jax
import jax.numpy as jnp
import numpy as np
from jax import lax
from jax.experimental import pallas as pl
from jax.experimental.pallas import tpu as pltpu
from jax.experimental.pallas import tpu_sc as plsc

D_MODEL = 1024
BATCH = 8
SEQ = 2048
GRID_W = 64
CTX_LEN = 256
N_HEADS = 8
HEAD_DIM = 64
V_DIM = 2 * HEAD_DIM
N_EXPERTS = 16
EXPERT_HIDDEN = 2048
CAPACITY = 2 * SEQ // N_EXPERTS
ROPE_THETA = 10000.0
EPS = 1e-6
LAM_INIT = 0.8 - 0.6 * math.exp(-0.3 * 0)
N_PARTS = 8
N_KEYS = CTX_LEN + SEQ
EXPERT_GROUPS = (4, 12)

LANES = 128
SC_CORES = 2
SC_SUBCORES = 16
SC_LANES = 16
F32 = jnp.float32
BF16 = jnp.bfloat16

ROW_CHUNK = 512
COND_ROWS = 16
W_RING = 3
PREFETCH_DMA_PRIORITY = 1
ADA_TN = 1024
ATTN_TQ = 2048
ATTN_SQ = 256
MERGE_TM = 1024
FFN_TF = 512
COMBINE_TQ = 1024
COMBINE_SUB = 256
COMBINE_WINDOW = 64
VMEM_LIMIT = 56 * 1024 * 1024
INPROJ_VMEM_LIMIT = 58 * 1024 * 1024


def _dot(a, b):
    return jnp.dot(a, b, preferred_element_type=F32)


def _dot_nt(a, b):
    return lax.dot_general(a, b, (((1,), (1,)), ((), ())), preferred_element_type=F32)


def _params(sem, vmem=VMEM_LIMIT):
    return pltpu.CompilerParams(dimension_semantics=sem, vmem_limit_bytes=vmem)


def _ada_kernel(cond_ref, w_ref, b_ref, o_ref):
    c = cond_ref[...]
    s = c / (1.0 + jnp.exp(-c))
    o_ref[...] = _dot(s.astype(BF16), w_ref[...].astype(BF16)) + b_ref[...]


def _ada(cond, w_ada, b_ada):
    n = w_ada.shape[1]
    return pl.pallas_call(
        _ada_kernel,
        out_shape=jax.ShapeDtypeStruct((COND_ROWS, n), F32),
        grid=(n // ADA_TN,),
        in_specs=[
            pl.BlockSpec((COND_ROWS, D_MODEL), lambda i: (0, 0)),
            pl.BlockSpec((D_MODEL, ADA_TN), lambda i: (0, i)),
            pl.BlockSpec((1, ADA_TN), lambda i: (0, i)),
        ],
        out_specs=pl.BlockSpec((COND_ROWS, ADA_TN), lambda i: (0, i)),
        compiler_params=_params(("arbitrary",)),
        name="ada",
    )(cond, w_ada, b_ada)


def _norm_modulate(x, g, shift, scale):
    ms = jnp.mean(x * x, axis=-1, keepdims=True)
    return (x * lax.rsqrt(ms + EPS) * g) * (1.0 + scale) + shift


def _rope(a, cos, sin_hi, sin_lo):
    return a * cos + pltpu.roll(a, LANES - 16, 1) * sin_hi + pltpu.roll(a, 16, 1) * sin_lo


def _inproj_weight_block(s):
    nxt = jnp.minimum(s, BATCH * N_PARTS - 1) % N_PARTS
    return jnp.where(nxt == 1, 2, jnp.where(nxt == 2, 1, nxt))


def _inproj_kernel(x_ref, mod_ref, g_ref, w_hbm, cw_ref, cos_ref, shi_ref, slo_ref, o_ref,
                   hx_ref, z_ref, w_bf, w_f32, w_sem, *, q_scale):
    t = pl.program_id(0)
    j = (t + N_PARTS - 1) % N_PARTS
    chunks = [slice(r * ROW_CHUNK, (r + 1) * ROW_CHUNK) for r in range(SEQ // ROW_CHUNK)]
    ring_slot = t % W_RING

    def weight_copy(s):
        cols = pl.ds(pl.multiple_of(_inproj_weight_block(s) * D_MODEL, D_MODEL), D_MODEL)
        return pltpu.make_async_copy(w_hbm.at[:, cols], w_f32.at[s % W_RING], w_sem.at[s % W_RING])

    @pl.when(t == 0)
    def _():
        for s in range(W_RING - 1):
            weight_copy(s).start()

    @pl.when(t + W_RING - 1 <= BATCH * N_PARTS)
    def _():
        weight_copy(t + W_RING - 1).start(priority=PREFETCH_DMA_PRIORITY)

    weight_copy(t).wait()

    def stage(slot, part):
        rows = slice(part * D_MODEL // len(chunks), (part + 1) * D_MODEL // len(chunks))
        w_bf[slot, rows, :] = w_f32[ring_slot, rows, :].astype(BF16)

    def proj(part, r):
        stage(1 - part % 2, r)
        return _dot(hx_ref[chunks[r], :], w_bf[part % 2])

    def branch(part):
        return pl.when((t > 0) & (j == part))

    @pl.when(t == 0)
    def _():
        for r in range(len(chunks)):
            stage(0, r)

    @branch(0)
    def _():
        shift = mod_ref[0, 0:1, :]
        scale = mod_ref[0, 1:2, :]
        for rows in chunks:
            hx_ref[rows, :] = _norm_modulate(x_ref[0, rows, :], g_ref[...], shift, scale).astype(BF16)
        z_ref[0:8, :] = jnp.zeros((8, D_MODEL), F32)
        z_ref[SEQ + 8:SEQ + 16, :] = jnp.zeros((8, D_MODEL), F32)
        for r, rows in enumerate(chunks):
            z_ref[pl.ds(rows.start + 8, ROW_CHUNK), :] = proj(0, r)

    @branch(1)
    def _():
        for r, rows in enumerate(chunks):
            zr = pl.ds(rows.start + 8, ROW_CHUNK)
            z_ref[zr, :] = z_ref[zr, :] * proj(1, r)

    @branch(2)
    def _():
        first_row = lax.broadcasted_iota(jnp.int32, (8, 1), 0) == 0
        last_row = lax.broadcasted_iota(jnp.int32, (8, 1), 0) == 7
        for r, rows in enumerate(chunks):
            base = rows.start + 8
            zc = z_ref[pl.ds(base, ROW_CHUNK), :]
            before = pltpu.roll(zc * cw_ref[0:1, :], 1, 0)
            after = pltpu.roll(zc * cw_ref[2:3, :], ROW_CHUNK - 1, 0)
            edge_b = jnp.where(first_row, z_ref[pl.ds(base - 8, 8), :][7:8, :] * cw_ref[0:1, :], before[0:8, :])
            edge_a = jnp.where(last_row, z_ref[pl.ds(base + ROW_CHUNK, 8), :][0:1, :] * cw_ref[2:3, :],
                               after[ROW_CHUNK - 8:, :])
            before = jnp.concatenate([edge_b, before[8:, :]], axis=0)
            after = jnp.concatenate([after[:ROW_CHUNK - 8, :], edge_a], axis=0)
            y = before + zc * cw_ref[1:2, :] + after
            o_ref[0, rows, :] = (proj(2, r) * y).astype(BF16)

    def rope_part(part, scale):
        for r, rows in enumerate(chunks):
            acc = proj(part, r)
            cos, shi, slo = cos_ref[rows, :], shi_ref[rows, :], slo_ref[rows, :]
            for h in range(D_MODEL // LANES):
                cols = slice(h * LANES, (h + 1) * LANES)
                roped = _rope(acc[:, cols], cos, shi, slo)
                if scale is not None:
                    roped = roped * scale
                o_ref[0, rows, cols] = roped.astype(BF16)

    @branch(3)
    def _():
        rope_part(3, q_scale)

    @branch(4)
    def _():
        rope_part(4, None)

    for plain_part in (5, 6, 7):
        @branch(plain_part)
        def _(plain_part=plain_part):
            for r, rows in enumerate(chunks):
                o_ref[0, rows, :] = proj(plain_part, r).astype(BF16)


def _inproj(x, mod3, norm_g, w_in, conv_w, cos, sin_hi, sin_lo, q_scale):
    def part(t):
        return jnp.maximum(t - 1, 0) % N_PARTS

    def sample(t):
        return jnp.maximum(t - 1, 0) // N_PARTS

    tab = pl.BlockSpec((SEQ, LANES), lambda t: (0, 0), pipeline_mode=pl.Buffered(1))
    return pl.pallas_call(
        functools.partial(_inproj_kernel, q_scale=q_scale),
        out_shape=jax.ShapeDtypeStruct((N_PARTS - 2, BATCH * SEQ, D_MODEL), BF16),
        grid=(BATCH * N_PARTS + 1,),
        in_specs=[
            pl.BlockSpec((1, SEQ, D_MODEL), lambda t: (sample(t), 0, 0)),
            pl.BlockSpec((1, 6, D_MODEL), lambda t: (sample(t), 0, 0)),
            pl.BlockSpec((1, D_MODEL), lambda t: (0, 0)),
            pl.BlockSpec(memory_space=pl.ANY),
            pl.BlockSpec((3, D_MODEL), lambda t: (0, 0)),
            tab, tab, tab,
        ],
        out_specs=pl.BlockSpec((1, SEQ, D_MODEL), lambda t: (jnp.maximum(part(t) - 2, 0), sample(t), 0)),
        scratch_shapes=[
            pltpu.VMEM((SEQ, D_MODEL), BF16),
            pltpu.VMEM((SEQ + 16, D_MODEL), F32),
            pltpu.VMEM((2, D_MODEL, D_MODEL), BF16),
            pltpu.VMEM((W_RING, D_MODEL, D_MODEL), F32),
            pltpu.SemaphoreType.DMA((W_RING,)),
        ],
        compiler_params=_params(("arbitrary",), vmem=INPROJ_VMEM_LIMIT),
        name="inproj",
    )(x, mod3, norm_g, w_in, conv_w, cos, sin_hi, sin_lo)


def _ctxproj_kernel(c_ref, mod_ref, g_ref, w_ref, o_ref, hc_ref):
    j = pl.program_id(0)
    chunks = [slice(r * ROW_CHUNK, (r + 1) * ROW_CHUNK) for r in range(BATCH * CTX_LEN // ROW_CHUNK)]

    @pl.when(j == 0)
    def _():
        shift = mod_ref[0, 0:1, :]
        scale = mod_ref[0, 1:2, :]
        for rows in chunks:
            hc_ref[rows, :] = _norm_modulate(c_ref[rows, :], g_ref[...], shift, scale).astype(BF16)

    w = w_ref[...].astype(BF16)
    for rows in chunks:
        o_ref[0, rows, :] = _dot(hc_ref[rows, :], w).astype(BF16)


def _ctxproj(ctx2, mod3, norm_g, w_in):
    rows = BATCH * CTX_LEN
    return pl.pallas_call(
        _ctxproj_kernel,
        out_shape=jax.ShapeDtypeStruct((2, rows, D_MODEL), BF16),
        grid=(2,),
        in_specs=[
            pl.BlockSpec((rows, D_MODEL), lambda j: (0, 0)),
            pl.BlockSpec((1, 6, D_MODEL), lambda j: (BATCH, 0, 0)),
            pl.BlockSpec((1, D_MODEL), lambda j: (0, 0)),
            pl.BlockSpec((D_MODEL, D_MODEL), lambda j: (0, 4 + j)),
        ],
        out_specs=pl.BlockSpec((1, rows, D_MODEL), lambda j: (j, 0, 0)),
        scratch_shapes=[pltpu.VMEM((rows, D_MODEL), BF16)],
        compiler_params=_params(("arbitrary",)),
        name="ctxproj",
    )(ctx2, mod3, norm_g, w_in)


def _attn_kernel(lp_ref, q_ref, k_ref, v_ref, kc_ref, vc_ref, g_ref, o_ref, k_all, v_ext, s_ref, m_ref):
    lp = lp_ref[...]
    lam = (jnp.exp(jnp.sum(lp[0:1] * lp[1:2], axis=-1, keepdims=True))
           - jnp.exp(jnp.sum(lp[2:3] * lp[3:4], axis=-1, keepdims=True)) + LAM_INIT)
    k_all[0:CTX_LEN, :] = kc_ref[0]
    k_all[CTX_LEN:N_KEYS, :] = k_ref[0]
    v_ext[0:CTX_LEN, 0:LANES] = vc_ref[0]
    v_ext[CTX_LEN:N_KEYS, 0:LANES] = v_ref[0]
    v_ext[:, LANES:2 * LANES] = jnp.ones((N_KEYS, LANES), BF16)
    first_map = lax.broadcasted_iota(jnp.int32, (1, LANES), 1) < HEAD_DIM

    def scores(u):
        q = q_ref[0, u * ATTN_SQ:(u + 1) * ATTN_SQ, :]
        zero = jnp.zeros_like(q)
        for mp, qm in enumerate((jnp.where(first_map, q, zero), jnp.where(first_map, zero, q))):
            s = _dot_nt(qm, k_all[...])
            s_ref[u % 2, mp] = s
            m_ref[u % 2, mp] = jnp.max(s, axis=-1, keepdims=True)

    def values(u):
        o = []
        for mp in range(2):
            p = jnp.exp2(s_ref[u % 2, mp] - m_ref[u % 2, mp])
            o.append(_dot(p.astype(BF16), v_ext[...]))
        a = o[0][:, :LANES] / o[0][:, LANES:] - o[1][:, :LANES] * (lam / o[1][:, LANES:])
        ms = jnp.mean(a * a, axis=-1, keepdims=True)
        o_ref[u * ATTN_SQ:(u + 1) * ATTN_SQ, :] = (
            (a * lax.rsqrt(ms + EPS) * g_ref[...]) * (1.0 - LAM_INIT)).astype(BF16)

    n_units = ATTN_TQ // ATTN_SQ
    scores(0)
    for u in range(n_units):
        if u + 1 < n_units:
            scores(u + 1)
        values(u)


def _attn(lam_params, parts, ctx_kv, subln_g):
    nq = SEQ // ATTN_TQ
    return pl.pallas_call(
        _attn_kernel,
        out_shape=jax.ShapeDtypeStruct((BATCH * SEQ, D_MODEL), BF16),
        grid=(BATCH, N_HEADS, nq),
        in_specs=[
            pl.BlockSpec((4, HEAD_DIM), lambda b, h, i: (0, 0)),
            pl.BlockSpec((1, ATTN_TQ, LANES), lambda b, h, i: (1, b * nq + i, h)),
            pl.BlockSpec((1, SEQ, LANES), lambda b, h, i: (2, b, h)),
            pl.BlockSpec((1, SEQ, LANES), lambda b, h, i: (3, b, h)),
            pl.BlockSpec((1, CTX_LEN, LANES), lambda b, h, i: (0, b, h)),
            pl.BlockSpec((1, CTX_LEN, LANES), lambda b, h, i: (1, b, h)),
            pl.BlockSpec((1, V_DIM), lambda b, h, i: (0, 0)),
        ],
        out_specs=pl.BlockSpec((ATTN_TQ, LANES), lambda b, h, i: (b * nq + i, h)),
        scratch_shapes=[
            pltpu.VMEM((N_KEYS, LANES), BF16),
            pltpu.VMEM((N_KEYS, 2 * LANES), BF16),
            pltpu.VMEM((2, 2, ATTN_SQ, N_KEYS), F32),
            pltpu.VMEM((2, 2, ATTN_SQ, 1), F32),
        ],
        compiler_params=_params(("arbitrary", "arbitrary", "arbitrary")),
        name="attn",
    )(lam_params, parts, parts, parts, ctx_kv, ctx_kv, subln_g)


def _merge_kernel(yb_ref, s6_ref, s7_ref, o_ref, x_ref, mod_ref, g2_ref, wc_ref, wa_ref, wo_ref,
                  wr_ref, x1_ref, h2_ref, lg_ref):
    chunks = [slice(r * ROW_CHUNK, (r + 1) * ROW_CHUNK) for r in range(MERGE_TM // ROW_CHUNK)]
    for rows in chunks:
        y_conv = _dot(yb_ref[0, rows, :], wc_ref[...])
        y_attn = _dot(o_ref[rows, :], wa_ref[...])
        gate_conv = 1.0 / (1.0 + jnp.exp(-s6_ref[0, rows, :].astype(F32)))
        gate_attn = 1.0 / (1.0 + jnp.exp(-s7_ref[0, rows, :].astype(F32)))
        merged = gate_conv * y_conv + gate_attn * y_attn
        mix = _dot(merged.astype(BF16), wo_ref[...])
        x1_ref[rows, :] = x_ref[rows, :] + mod_ref[0, 2:3, :] * mix
    for rows in chunks:
        h2 = _norm_modulate(x1_ref[rows, :], g2_ref[...], mod_ref[0, 3:4, :], mod_ref[0, 4:5, :])
        h_hi = h2.astype(BF16)
        h2_ref[rows, :] = pltpu.pack_elementwise([h2[:, :D_MODEL // 2], h2[:, D_MODEL // 2:]], packed_dtype=BF16)
        h_lo = (h2 - h_hi.astype(F32)).astype(BF16)
        both = _dot_nt(wr_ref[...], h_hi)
        lg_ref[0, :, rows] = both[:N_EXPERTS] + (_dot_nt(wr_ref[0:N_EXPERTS, :], h_lo) + both[N_EXPERTS:])


def _merge(parts, attn_o, x2, mod3, norm2_g, w_conv_bf, w_attn_bf, w_out_bf, wr_hi_lo):
    per_b = SEQ // MERGE_TM
    sq = pl.BlockSpec((D_MODEL, D_MODEL), lambda i: (0, 0), pipeline_mode=pl.Buffered(1))
    wr = pl.BlockSpec((2 * N_EXPERTS, D_MODEL), lambda i: (0, 0), pipeline_mode=pl.Buffered(1))
    tile = pl.BlockSpec((MERGE_TM, D_MODEL), lambda i: (i, 0))
    return pl.pallas_call(
        _merge_kernel,
        out_shape=(
            jax.ShapeDtypeStruct((BATCH * SEQ, D_MODEL), F32),
            jax.ShapeDtypeStruct((BATCH * SEQ, D_MODEL // 2), jnp.uint32),
            jax.ShapeDtypeStruct((BATCH, N_EXPERTS, SEQ), F32),
        ),
        grid=(BATCH * per_b,),
        in_specs=[
            pl.BlockSpec((1, MERGE_TM, D_MODEL), lambda i: (0, i, 0)),
            pl.BlockSpec((1, MERGE_TM, D_MODEL), lambda i: (4, i, 0)),
            pl.BlockSpec((1, MERGE_TM, D_MODEL), lambda i: (5, i, 0)),
            tile, tile,
            pl.BlockSpec((1, 6, D_MODEL), lambda i: (i // per_b, 0, 0)),
            pl.BlockSpec((1, D_MODEL), lambda i: (0, 0)),
            sq, sq, sq, wr,
        ],
        out_specs=(
            tile,
            pl.BlockSpec((MERGE_TM, D_MODEL // 2), lambda i: (i, 0)),
            pl.BlockSpec((1, N_EXPERTS, MERGE_TM), lambda i: (i // per_b, 0, i % per_b)),
        ),
        compiler_params=_params(("arbitrary",)),
        name="merge",
    )(parts, parts, parts, attn_o, x2, mod3, norm2_g, w_conv_bf, w_attn_bf, w_out_bf, wr_hi_lo)


def _route_kernel(lg_ref, rank_ref, gate_ref, bounds_ref):
    lg = lg_ref[...]
    ex = jnp.exp(lg - jnp.max(lg, axis=1, keepdims=True))
    aff = (ex / jnp.sum(ex, axis=1, keepdims=True)).reshape(BATCH * N_EXPERTS, SEQ)
    bits = lax.bitcast_convert_type(aff, jnp.int32)

    def count(mask):
        return jnp.sum(jnp.where(mask, 1.0, 0.0), axis=1, keepdims=True)

    def search(i, t):
        cand = t + lax.shift_left(jnp.int32(1), 30 - i)
        return jnp.where(count(bits >= cand) >= CAPACITY, cand, t)

    thr = lax.fori_loop(0, 31, search, jnp.zeros((BATCH * N_EXPERTS, 1), jnp.int32))
    need = CAPACITY - count(bits > thr)
    before = (lax.broadcasted_iota(jnp.int32, (LANES, LANES), 0)
              < lax.broadcasted_iota(jnp.int32, (LANES, LANES), 1)).astype(BF16)
    n_tied = jnp.zeros((BATCH * N_EXPERTS, 1), F32)
    n_sel = jnp.zeros((BATCH * N_EXPERTS, 1), F32)
    per_tile = COMBINE_SUB // LANES
    firsts, ends = [], []
    for c in range(SEQ // LANES):
        if c % per_tile == 0:
            firsts.append(n_sel)
        cols = slice(c * LANES, (c + 1) * LANES)
        bits_c = bits[:, cols]
        tied_c = jnp.where(bits_c == thr, 1.0, 0.0)
        tied_before = _dot(tied_c.astype(BF16), before) + n_tied
        sel_c = jnp.where(bits_c > thr, 1.0, jnp.where(tied_before < need, tied_c, 0.0))
        rank = _dot(sel_c.astype(BF16), before) + n_sel
        rank_c = jnp.where(sel_c > 0.0, rank, -1.0)
        gate_c = jnp.where(sel_c > 0.0, aff[:, cols], 0.0)
        n_tied = n_tied + jnp.sum(tied_c, axis=1, keepdims=True)
        n_sel = n_sel + jnp.sum(sel_c, axis=1, keepdims=True)
        rank_ref[:, cols] = rank_c.astype(jnp.int32)
        gate_ref[:, cols] = gate_c
        if c % per_tile == per_tile - 1:
            ends.append(n_sel)
    bounds_ref[...] = jnp.concatenate(firsts + ends, axis=1).astype(jnp.int32)


def _route(logits_t):
    return pl.pallas_call(
        _route_kernel,
        out_shape=(
            jax.ShapeDtypeStruct((BATCH * N_EXPERTS, SEQ), jnp.int32),
            jax.ShapeDtypeStruct((BATCH * N_EXPERTS, SEQ), F32),
            jax.ShapeDtypeStruct((BATCH * N_EXPERTS, 2 * (SEQ // COMBINE_SUB)), jnp.int32),
        ),
        compiler_params=pltpu.CompilerParams(vmem_limit_bytes=VMEM_LIMIT),
        name="route",
    )(logits_t)


def _dispatch(rank_e, h_packed, first_expert, n_experts):
    per_worker = n_experts * BATCH // (SC_CORES * SC_SUBCORES)
    assert per_worker * SC_CORES * SC_SUBCORES == n_experts * BATCH
    half = CAPACITY // 2
    mesh = plsc.VectorSubcoreMesh(core_axis_name="c", subcore_axis_name="s")

    @functools.partial(
        pl.kernel, mesh=mesh,
        out_type=jax.ShapeDtypeStruct((n_experts * BATCH * CAPACITY, D_MODEL // 2), h_packed.dtype),
        scratch_types=[
            pltpu.VMEM((SEQ,), jnp.int32),
            pltpu.VMEM((2, half), jnp.int32),
            pltpu.VMEM((half, D_MODEL // 2), h_packed.dtype),
            pltpu.SemaphoreType.DMA,
        ],
        compiler_params=pltpu.CompilerParams(needs_layout_passes=False),
        name="dispatch",
    )
    def k(table_hbm, rank_hbm, out_hbm, rank_v, idx_v, rows_v, sem):
        worker = lax.axis_index("s") * SC_CORES + lax.axis_index("c")

        @pl.loop(0, per_worker)
        def _(p):
            pair = worker * per_worker + p
            expert = lax.shift_right_logical(pair, BATCH.bit_length() - 1)
            sample = pair & (BATCH - 1)
            pltpu.sync_copy(rank_hbm.at[sample * N_EXPERTS + first_expert + expert], rank_v)

            @pl.loop(0, SEQ // SC_LANES)
            def _(i):
                rank = rank_v[pl.ds(i * SC_LANES, SC_LANES)]
                row = lax.iota(jnp.int32, SC_LANES) + (i * SC_LANES + sample * SEQ)
                plsc.store_scatter(idx_v, [lax.shift_right_arithmetic(rank, 7), rank & (half - 1)], row,
                                   mask=rank >= 0)

            for c in range(2):
                pltpu.async_copy(table_hbm.at[idx_v.at[c]], rows_v, sem).wait()
                pltpu.sync_copy(rows_v, out_hbm.at[pl.ds(pair * CAPACITY + c * half, half)])

    return k(h_packed, rank_e)


def _expert_kernel(xs_ref, wg_ref, wu_ref, wd_ref, y_ref, acc_ref, wg_bf, wu_bf, wd_bf):
    t = pl.program_id(0)
    n_f = EXPERT_HIDDEN // FFN_TF
    f = (t + n_f - 1) % n_f

    n_chunks = BATCH * CAPACITY // ROW_CHUNK

    def stage(slot, part):
        up = slice(part * D_MODEL // n_chunks, (part + 1) * D_MODEL // n_chunks)
        down = slice(part * FFN_TF // n_chunks, (part + 1) * FFN_TF // n_chunks)
        wg_bf[slot, up, :] = wg_ref[0, up, :].astype(BF16)
        wu_bf[slot, up, :] = wu_ref[0, up, :].astype(BF16)
        wd_bf[slot, down, :] = wd_ref[0, down, :].astype(BF16)

    def step(cur):
        for r in range(n_chunks):
            stage(1 - cur, r)
            rows = slice(r * ROW_CHUNK, (r + 1) * ROW_CHUNK)
            packed = xs_ref[0, rows, :]
            half = D_MODEL // 2
            x_lo = pltpu.unpack_elementwise(packed, index=0, packed_dtype=BF16, unpacked_dtype=F32).astype(BF16)
            x_hi = pltpu.unpack_elementwise(packed, index=1, packed_dtype=BF16, unpacked_dtype=F32).astype(BF16)
            a = _dot(x_lo, wg_bf[cur, :half, :]) + _dot(x_hi, wg_bf[cur, half:, :])
            u = _dot(x_lo, wu_bf[cur, :half, :]) + _dot(x_hi, wu_bf[cur, half:, :])
            hidden = (a / (1.0 + jnp.exp(-a))) * u
            total = jnp.where(f == 0, 0.0, acc_ref[rows, :]) + _dot(hidden.astype(BF16), wd_bf[cur])
            acc_ref[rows, :] = total
            y_ref[0, rows, :] = total.astype(BF16)

    @pl.when(t == 0)
    def _():
        for part in range(n_chunks):
            stage(0, part)

    @pl.when(t % 2 == 1)
    def _():
        step(0)

    @pl.when((t % 2 == 0) & (t > 0))
    def _():
        step(1)


def _experts(xs, w_gate, w_up, w_down, first_expert):
    rows = BATCH * CAPACITY
    n_f = EXPERT_HIDDEN // FFN_TF
    n_experts = xs.shape[0]
    last = n_experts * n_f - 1

    def staged(t):
        return jnp.minimum(t, last)

    def computed(t):
        return jnp.maximum(t - 1, 0) // n_f

    return pl.pallas_call(
        _expert_kernel,
        out_shape=jax.ShapeDtypeStruct((n_experts, rows, D_MODEL), BF16),
        grid=(n_experts * n_f + 1,),
        in_specs=[
            pl.BlockSpec((1, rows, D_MODEL // 2), lambda t: (computed(t), 0, 0)),
            pl.BlockSpec((1, D_MODEL, FFN_TF), lambda t: (first_expert + staged(t) // n_f, 0, staged(t) % n_f)),
            pl.BlockSpec((1, D_MODEL, FFN_TF), lambda t: (first_expert + staged(t) // n_f, 0, staged(t) % n_f)),
            pl.BlockSpec((1, FFN_TF, D_MODEL), lambda t: (first_expert + staged(t) // n_f, staged(t) % n_f, 0)),
        ],
        out_specs=pl.BlockSpec((1, rows, D_MODEL), lambda t: (computed(t), 0, 0)),
        scratch_shapes=[
            pltpu.VMEM((rows, D_MODEL), F32),
            pltpu.VMEM((2, D_MODEL, FFN_TF), BF16),
            pltpu.VMEM((2, D_MODEL, FFN_TF), BF16),
            pltpu.VMEM((2, FFN_TF, D_MODEL), BF16),
        ],
        compiler_params=_params(("arbitrary",)),
        name="experts",
    )(xs, w_gate, w_up, w_down)


def _combine_kernel(bounds_ref, rank_ref, gate_ref, *refs):
    n_groups = len(EXPERT_GROUPS)
    y_hbm, (x1_ref, mod_ref, fg_ref, o_ref), scratch = refs[:n_groups], refs[n_groups:n_groups + 4], refs[n_groups + 4:]
    y_bufs, y_sems = scratch[:n_groups], scratch[n_groups:]
    group_of = [(g, e) for g, n in enumerate(EXPERT_GROUPS) for e in range(n)]
    b, t = pl.program_id(0), pl.program_id(1)
    subs = COMBINE_TQ // COMBINE_SUB
    n_sub = SEQ // COMBINE_SUB
    per_dot = CAPACITY // COMBINE_WINDOW

    def y_copies(s):
        rows = pl.ds(pl.multiple_of(s * CAPACITY, CAPACITY), CAPACITY)
        return [pltpu.make_async_copy(y_hbm[g].at[:, rows, :], y_bufs[g].at[s % 2], y_sems[g].at[s % 2])
                for g in range(n_groups)]

    @pl.when((b == 0) & (t == 0))
    def _():
        for copy in y_copies(0):
            copy.start()

    @pl.when((t == 0) & (b + 1 < BATCH))
    def _():
        for copy in y_copies(b + 1):
            copy.start(priority=PREFETCH_DMA_PRIORITY)

    @pl.when(t == 0)
    def _():
        for copy in y_copies(b):
            copy.wait()

    def y_rows(e):
        return y_bufs[group_of[e][0]].at[b % 2, group_of[e][1]]

    def onehot_t(e, cols, first, n_slots):
        slot = lax.broadcasted_iota(jnp.int32, (n_slots, cols.stop - cols.start), 0) + first
        return jnp.where(rank_ref[e:e + 1, cols] == slot, gate_ref[e:e + 1, cols], 0.0).astype(BF16)

    def dot_t(a, b_):
        return lax.dot_general(a, b_, (((0,), (0,)), ((), ())), preferred_element_type=F32)

    def finish(rows, acc):
        x2 = x1_ref[rows, :] + mod_ref[0, 5:6, :] * acc
        ms = jnp.mean(x2 * x2, axis=-1, keepdims=True)
        o_ref[rows, :] = x2 * lax.rsqrt(ms + EPS) * fg_ref[...]

    starts, fits = {}, None
    for s in range(subs):
        for e in range(N_EXPERTS):
            pair = (b * N_EXPERTS + e) * (2 * n_sub) + t * subs + s
            first, end = bounds_ref[pair], bounds_ref[pair + n_sub]
            start = jnp.minimum(lax.shift_left(lax.shift_right_logical(first, 4), 4), CAPACITY - COMBINE_WINDOW)
            starts[s, e] = start
            ok = end <= start + COMBINE_WINDOW
            fits = ok if fits is None else fits & ok

    @pl.when(fits)
    def _():
        for s in range(subs):
            rows = slice(s * COMBINE_SUB, (s + 1) * COMBINE_SUB)
            acc = jnp.zeros((COMBINE_SUB, D_MODEL), F32)
            for e0 in range(0, N_EXPERTS, per_dot):
                first = [pl.multiple_of(starts[s, e0 + i], 16) for i in range(per_dot)]
                onehots = [onehot_t(e0 + i, rows, first[i], COMBINE_WINDOW) for i in range(per_dot)]
                y_win = [y_rows(e0 + i)[pl.ds(first[i], COMBINE_WINDOW), :] for i in range(per_dot)]
                acc = acc + dot_t(jnp.concatenate(onehots, axis=0), jnp.concatenate(y_win, axis=0))
            finish(rows, acc)

    @pl.when(jnp.logical_not(fits))
    def _():
        acc = jnp.zeros((COMBINE_TQ, D_MODEL), F32)
        for e in range(N_EXPERTS):
            acc = acc + dot_t(onehot_t(e, slice(0, COMBINE_TQ), 0, CAPACITY), y_rows(e)[...])
        finish(slice(0, COMBINE_TQ), acc)


def _combine(bounds, rank_e, gate_e, ys, x1, mod3, final_g):
    per_b = SEQ // COMBINE_TQ
    return pl.pallas_call(
        _combine_kernel,
        out_shape=jax.ShapeDtypeStruct((BATCH * SEQ, D_MODEL), F32),
        grid_spec=pltpu.PrefetchScalarGridSpec(
            num_scalar_prefetch=1,
            grid=(BATCH, per_b),
            in_specs=[
                pl.BlockSpec((N_EXPERTS, COMBINE_TQ), lambda b, t, _: (b, t)),
                pl.BlockSpec((N_EXPERTS, COMBINE_TQ), lambda b, t, _: (b, t)),
                *[pl.BlockSpec(memory_space=pl.ANY) for _ in ys],
                pl.BlockSpec((COMBINE_TQ, D_MODEL), lambda b, t, _: (b * per_b + t, 0)),
                pl.BlockSpec((1, 6, D_MODEL), lambda b, t, _: (b, 0, 0)),
                pl.BlockSpec((1, D_MODEL), lambda b, t, _: (0, 0)),
            ],
            out_specs=pl.BlockSpec((COMBINE_TQ, D_MODEL), lambda b, t, _: (b * per_b + t, 0)),
            scratch_shapes=[
                *[pltpu.VMEM((2, y.shape[0], CAPACITY, D_MODEL), BF16) for y in ys],
                *[pltpu.SemaphoreType.DMA((2,)) for _ in ys],
            ],
        ),
        compiler_params=_params(("arbitrary", "arbitrary")),
        name="combine",
    )(bounds.reshape(-1), rank_e, gate_e, *ys, x1, mod3, final_g)


def _rope_tables():
    t = np.arange(SEQ)
    pos = np.stack([t // GRID_W, t % GRID_W], axis=1).astype(np.float32)
    per_axis = HEAD_DIM // 2
    inv = np.float32(ROPE_THETA) ** (-np.arange(0, per_axis, 2, dtype=np.float32) / np.float32(per_axis))
    ang = (pos[:, :, None] * inv).astype(np.float32)
    cos, sin = np.cos(ang.astype(np.float64)), np.sin(ang.astype(np.float64))
    zero = np.zeros_like(sin)

    def lanes(first_half, second_half):
        one_map = np.stack([first_half, second_half], axis=2).reshape(SEQ, HEAD_DIM)
        return jnp.asarray(np.concatenate([one_map, one_map], axis=1), F32)

    return lanes(cos, cos), lanes(-sin, zero), lanes(zero, sin)


def kernel(x, c, ctx, c_ctx, norm1_g, norm2_g, w_ada, b_ada, w_in, conv_w, w_out_conv, lambda_q1,
           lambda_k1, lambda_q2, lambda_k2, subln_g, w_o_attn, w_out, w_router, w_gate_e, w_up_e,
           w_down_e, final_g):
    cond = jnp.concatenate([c, c_ctx[None, :], jnp.zeros((COND_ROWS - BATCH - 1, D_MODEL), F32)], axis=0)
    mod3 = _ada(cond, w_ada[0], b_ada).reshape(COND_ROWS, 6, D_MODEL)

    cos, sin_hi, sin_lo = _rope_tables()
    q_scale = math.log2(math.e) * HEAD_DIM ** -0.5
    parts = _inproj(x, mod3, norm1_g, w_in[0], conv_w[0], cos, sin_hi, sin_lo, q_scale)
    ctx_kv = _ctxproj(ctx.reshape(BATCH * CTX_LEN, D_MODEL), mod3, norm1_g, w_in[0])

    lam_params = jnp.concatenate([lambda_q1, lambda_k1, lambda_q2, lambda_k2], axis=0)
    attn_o = _attn(lam_params, parts, ctx_kv, subln_g)

    wr_t = w_router[0].T
    wr_hi = wr_t.astype(BF16)
    wr_lo = (wr_t - wr_hi.astype(F32)).astype(BF16)
    x1, h2, logits_t = _merge(parts, attn_o, x.reshape(BATCH * SEQ, D_MODEL), mod3, norm2_g,
                              w_out_conv[0].astype(BF16), w_o_attn[0].astype(BF16), w_out[0].astype(BF16),
                              jnp.concatenate([wr_hi, wr_lo], axis=0))

    rank_e, gate_e, bounds = _route(logits_t)
    ys, first = [], 0
    for n in EXPERT_GROUPS:
        xs = _dispatch(rank_e, h2, first, n).reshape(n, BATCH * CAPACITY, D_MODEL // 2)
        ys.append(_experts(xs, w_gate_e[0], w_up_e[0], w_down_e[0], first))
        first += n
    out = _combine(bounds, rank_e, gate_e, ys, x1, mod3, final_g[None, :])
    return out.reshape(BATCH, SEQ, D_MODEL)
```

```python
import functools
import math

import jax
import jax.numpy as jnp
import numpy as np
from jax import lax
from jax.experimental import pallas as pl
from jax.experimental.pallas import tpu as pltpu
from jax.experimental.pallas import tpu_sc as plsc

D_MODEL = 1024
BATCH = 8
SEQ = 2048
GRID_W = 64
CTX_LEN = 256
N_HEADS = 8
HEAD_DIM = 64
V_DIM = 2 * HEAD_DIM
N_EXPERTS = 16
EXPERT_HIDDEN = 2048
CAPACITY = 2 * SEQ // N_EXPERTS
ROPE_THETA = 10000.0
EPS = 1e-6
LAM_INIT = 0.8 - 0.6 * math.exp(-0.3 * 0)
N_PARTS = 8
N_KEYS = CTX_LEN + SEQ
EXPERT_GROUPS = (4, 12)

LANES = 128
SC_CORES = 2
SC_SUBCORES = 16
SC_LANES = 16
F32 = jnp.float32
BF16 = jnp.bfloat16

ROW_CHUNK = 512
COND_ROWS = 16
W_RING = 3
ADA_TN = 1024
ATTN_TQ = 2048
ATTN_SQ = 256
MERGE_TM = 1024
FFN_TF = 512
COMBINE_TQ = 1024
COMBINE_SUB = 256
COMBINE_WINDOW = 64
VMEM_LIMIT = 56 * 1024 * 1024
INPROJ_VMEM_LIMIT = 58 * 1024 * 1024


def _dot(a, b):
    return jnp.dot(a, b, preferred_element_type=F32)


def _dot_nt(a, b):
    return lax.dot_general(a, b, (((1,), (1,)), ((), ())), preferred_element_type=F32)


def _params(sem, vmem=VMEM_LIMIT):
    return pltpu.CompilerParams(dimension_semantics=sem, vmem_limit_bytes=vmem)


def _ada_kernel(cond_ref, w_ref, b_ref, o_ref):
    c = cond_ref[...]
    s = c / (1.0 + jnp.exp(-c))
    o_ref[...] = _dot(s.astype(BF16), w_ref[...].astype(BF16)) + b_ref[...]


def _ada(cond, w_ada, b_ada):
    n = w_ada.shape[1]
    return pl.pallas_call(
        _ada_kernel,
        out_shape=jax.ShapeDtypeStruct((COND_ROWS, n), F32),
        grid=(n // ADA_TN,),
        in_specs=[
            pl.BlockSpec((COND_ROWS, D_MODEL), lambda i: (0, 0)),
            pl.BlockSpec((D_MODEL, ADA_TN), lambda i: (0, i)),
            pl.BlockSpec((1, ADA_TN), lambda i: (0, i)),
        ],
        out_specs=pl.BlockSpec((COND_ROWS, ADA_TN), lambda i: (0, i)),
        compiler_params=_params(("arbitrary",)),
        name="ada",
    )(cond, w_ada, b_ada)


def _norm_modulate(x, g, shift, scale):
    ms = jnp.mean(x * x, axis=-1, keepdims=True)
    return (x * lax.rsqrt(ms + EPS) * g) * (1.0 + scale) + shift


def _rope(a, cos, sin_hi, sin_lo):
    return a * cos + pltpu.roll(a, LANES - 16, 1) * sin_hi + pltpu.roll(a, 16, 1) * sin_lo


def _inproj_weight_block(s):
    nxt = jnp.minimum(s, BATCH * N_PARTS - 1) % N_PARTS
    return jnp.where(nxt == 1, 2, jnp.where(nxt == 2, 1, nxt))


def _inproj_kernel(x_ref, mod_ref, g_ref, w_hbm, cw_ref, cos_ref, shi_ref, slo_ref, o_ref,
                   hx_ref, z_ref, w_bf, w_f32, w_sem, *, q_scale):
    t = pl.program_id(0)
    j = (t + N_PARTS - 1) % N_PARTS
    chunks = [slice(r * ROW_CHUNK, (r + 1) * ROW_CHUNK) for r in range(SEQ // ROW_CHUNK)]
    ring_slot = t % W_RING

    def weight_copy(s):
        cols = pl.ds(pl.multiple_of(_inproj_weight_block(s) * D_MODEL, D_MODEL), D_MODEL)
        return pltpu.make_async_copy(w_hbm.at[:, cols], w_f32.at[s % W_RING], w_sem.at[s % W_RING])

    @pl.when(t == 0)
    def _():
        for s in range(W_RING - 1):
            weight_copy(s).start()

    @pl.when(t + W_RING - 1 <= BATCH * N_PARTS)
    def _():
        weight_copy(t + W_RING - 1).start()

    weight_copy(t).wait()

    def stage(slot, part):
        rows = slice(part * D_MODEL // len(chunks), (part + 1) * D_MODEL // len(chunks))
        w_bf[slot, rows, :] = w_f32[ring_slot, rows, :].astype(BF16)

    def proj(part, r):
        stage(1 - part % 2, r)
        return _dot(hx_ref[chunks[r], :], w_bf[part % 2])

    def branch(part):
        return pl.when((t > 0) & (j == part))

    @pl.when(t == 0)
    def _():
        for r in range(len(chunks)):
            stage(0, r)

    @branch(0)
    def _():
        shift = mod_ref[0, 0:1, :]
        scale = mod_ref[0, 1:2, :]
        for rows in chunks:
            hx_ref[rows, :] = _norm_modulate(x_ref[0, rows, :], g_ref[...], shift, scale).astype(BF16)
        z_ref[0:8, :] = jnp.zeros((8, D_MODEL), F32)
        z_ref[SEQ + 8:SEQ + 16, :] = jnp.zeros((8, D_MODEL), F32)
        for r, rows in enumerate(chunks):
            z_ref[pl.ds(rows.start + 8, ROW_CHUNK), :] = proj(0, r)

    @branch(1)
    def _():
        for r, rows in enumerate(chunks):
            zr = pl.ds(rows.start + 8, ROW_CHUNK)
            z_ref[zr, :] = z_ref[zr, :] * proj(1, r)

    @branch(2)
    def _():
        first_row = lax.broadcasted_iota(jnp.int32, (8, 1), 0) == 0
        last_row = lax.broadcasted_iota(jnp.int32, (8, 1), 0) == 7
        for r, rows in enumerate(chunks):
            base = rows.start + 8
            zc = z_ref[pl.ds(base, ROW_CHUNK), :]
            before = pltpu.roll(zc * cw_ref[0:1, :], 1, 0)
            after = pltpu.roll(zc * cw_ref[2:3, :], ROW_CHUNK - 1, 0)
            edge_b = jnp.where(first_row, z_ref[pl.ds(base - 8, 8), :][7:8, :] * cw_ref[0:1, :], before[0:8, :])
            edge_a = jnp.where(last_row, z_ref[pl.ds(base + ROW_CHUNK, 8), :][0:1, :] * cw_ref[2:3, :],
                               after[ROW_CHUNK - 8:, :])
            before = jnp.concatenate([edge_b, before[8:, :]], axis=0)
            after = jnp.concatenate([after[:ROW_CHUNK - 8, :], edge_a], axis=0)
            y = before + zc * cw_ref[1:2, :] + after
            o_ref[0, rows, :] = (proj(2, r) * y).astype(BF16)

    def rope_part(part, scale):
        for r, rows in enumerate(chunks):
            acc = proj(part, r)
            cos, shi, slo = cos_ref[rows, :], shi_ref[rows, :], slo_ref[rows, :]
            for h in range(D_MODEL // LANES):
                cols = slice(h * LANES, (h + 1) * LANES)
                roped = _rope(acc[:, cols], cos, shi, slo)
                if scale is not None:
                    roped = roped * scale
                o_ref[0, rows, cols] = roped.astype(BF16)

    @branch(3)
    def _():
        rope_part(3, q_scale)

    @branch(4)
    def _():
        rope_part(4, None)

    for plain_part in (5, 6, 7):
        @branch(plain_part)
        def _(plain_part=plain_part):
            for r, rows in enumerate(chunks):
                o_ref[0, rows, :] = proj(plain_part, r).astype(BF16)


def _inproj(x, mod3, norm_g, w_in, conv_w, cos, sin_hi, sin_lo, q_scale):
    def part(t):
        return jnp.maximum(t - 1, 0) % N_PARTS

    def sample(t):
        return jnp.maximum(t - 1, 0) // N_PARTS

    tab = pl.BlockSpec((SEQ, LANES), lambda t: (0, 0), pipeline_mode=pl.Buffered(1))
    return pl.pallas_call(
        functools.partial(_inproj_kernel, q_scale=q_scale),
        out_shape=jax.ShapeDtypeStruct((N_PARTS - 2, BATCH * SEQ, D_MODEL), BF16),
        grid=(BATCH * N_PARTS + 1,),
        in_specs=[
            pl.BlockSpec((1, SEQ, D_MODEL), lambda t: (sample(t), 0, 0)),
            pl.BlockSpec((1, 6, D_MODEL), lambda t: (sample(t), 0, 0)),
            pl.BlockSpec((1, D_MODEL), lambda t: (0, 0)),
            pl.BlockSpec(memory_space=pl.ANY),
            pl.BlockSpec((3, D_MODEL), lambda t: (0, 0)),
            tab, tab, tab,
        ],
        out_specs=pl.BlockSpec((1, SEQ, D_MODEL), lambda t: (jnp.maximum(part(t) - 2, 0), sample(t), 0)),
        scratch_shapes=[
            pltpu.VMEM((SEQ, D_MODEL), BF16),
            pltpu.VMEM((SEQ + 16, D_MODEL), F32),
            pltpu.VMEM((2, D_MODEL, D_MODEL), BF16),
            pltpu.VMEM((W_RING, D_MODEL, D_MODEL), F32),
            pltpu.SemaphoreType.DMA((W_RING,)),
        ],
        compiler_params=_params(("arbitrary",), vmem=INPROJ_VMEM_LIMIT),
        name="inproj",
    )(x, mod3, norm_g, w_in, conv_w, cos, sin_hi, sin_lo)


def _ctxproj_kernel(c_ref, mod_ref, g_ref, w_ref, o_ref, hc_ref):
    j = pl.program_id(0)
    chunks = [slice(r * ROW_CHUNK, (r + 1) * ROW_CHUNK) for r in range(BATCH * CTX_LEN // ROW_CHUNK)]

    @pl.when(j == 0)
    def _():
        shift = mod_ref[0, 0:1, :]
        scale = mod_ref[0, 1:2, :]
        for rows in chunks:
            hc_ref[rows, :] = _norm_modulate(c_ref[rows, :], g_ref[...], shift, scale).astype(BF16)

    w = w_ref[...].astype(BF16)
    for rows in chunks:
        o_ref[0, rows, :] = _dot(hc_ref[rows, :], w).astype(BF16)


def _ctxproj(ctx2, mod3, norm_g, w_in):
    rows = BATCH * CTX_LEN
    return pl.pallas_call(
        _ctxproj_kernel,
        out_shape=jax.ShapeDtypeStruct((2, rows, D_MODEL), BF16),
        grid=(2,),
        in_specs=[
            pl.BlockSpec((rows, D_MODEL), lambda j: (0, 0)),
            pl.BlockSpec((1, 6, D_MODEL), lambda j: (BATCH, 0, 0)),
            pl.BlockSpec((1, D_MODEL), lambda j: (0, 0)),
            pl.BlockSpec((D_MODEL, D_MODEL), lambda j: (0, 4 + j)),
        ],
        out_specs=pl.BlockSpec((1, rows, D_MODEL), lambda j: (j, 0, 0)),
        scratch_shapes=[pltpu.VMEM((rows, D_MODEL), BF16)],
        compiler_params=_params(("arbitrary",)),
        name="ctxproj",
    )(ctx2, mod3, norm_g, w_in)


def _attn_kernel(lp_ref, q_ref, k_ref, v_ref, kc_ref, vc_ref, g_ref, o_ref, k_all, v_ext, s_ref, m_ref):
    lp = lp_ref[...]
    lam = (jnp.exp(jnp.sum(lp[0:1] * lp[1:2], axis=-1, keepdims=True))
           - jnp.exp(jnp.sum(lp[2:3] * lp[3:4], axis=-1, keepdims=True)) + LAM_INIT)
    k_all[0:CTX_LEN, :] = kc_ref[0]
    k_all[CTX_LEN:N_KEYS, :] = k_ref[0]
    v_ext[0:CTX_LEN, 0:LANES] = vc_ref[0]
    v_ext[CTX_LEN:N_KEYS, 0:LANES] = v_ref[0]
    v_ext[:, LANES:2 * LANES] = jnp.ones((N_KEYS, LANES), BF16)
    first_map = lax.broadcasted_iota(jnp.int32, (1, LANES), 1) < HEAD_DIM

    def scores(u):
        q = q_ref[0, u * ATTN_SQ:(u + 1) * ATTN_SQ, :]
        zero = jnp.zeros_like(q)
        for mp, qm in enumerate((jnp.where(first_map, q, zero), jnp.where(first_map, zero, q))):
            s = _dot_nt(qm, k_all[...])
            s_ref[u % 2, mp] = s
            m_ref[u % 2, mp] = jnp.max(s, axis=-1, keepdims=True)

    def values(u):
        o = []
        for mp in range(2):
            p = jnp.exp2(s_ref[u % 2, mp] - m_ref[u % 2, mp])
            o.append(_dot(p.astype(BF16), v_ext[...]))
        a = o[0][:, :LANES] / o[0][:, LANES:] - o[1][:, :LANES] * (lam / o[1][:, LANES:])
        ms = jnp.mean(a * a, axis=-1, keepdims=True)
        o_ref[u * ATTN_SQ:(u + 1) * ATTN_SQ, :] = (
            (a * lax.rsqrt(ms + EPS) * g_ref[...]) * (1.0 - LAM_INIT)).astype(BF16)

    n_units = ATTN_TQ // ATTN_SQ
    scores(0)
    for u in range(n_units):
        if u + 1 < n_units:
            scores(u + 1)
        values(u)


def _attn(lam_params, parts, ctx_kv, subln_g):
    nq = SEQ // ATTN_TQ
    return pl.pallas_call(
        _attn_kernel,
        out_shape=jax.ShapeDtypeStruct((BATCH * SEQ, D_MODEL), BF16),
        grid=(BATCH, N_HEADS, nq),
        in_specs=[
            pl.BlockSpec((4, HEAD_DIM), lambda b, h, i: (0, 0)),
            pl.BlockSpec((1, ATTN_TQ, LANES), lambda b, h, i: (1, b * nq + i, h)),
            pl.BlockSpec((1, SEQ, LANES), lambda b, h, i: (2, b, h)),
            pl.BlockSpec((1, SEQ, LANES), lambda b, h, i: (3, b, h)),
            pl.BlockSpec((1, CTX_LEN, LANES), lambda b, h, i: (0, b, h)),
            pl.BlockSpec((1, CTX_LEN, LANES), lambda b, h, i: (1, b, h)),
            pl.BlockSpec((1, V_DIM), lambda b, h, i: (0, 0)),
        ],
        out_specs=pl.BlockSpec((ATTN_TQ, LANES), lambda b, h, i: (b * nq + i, h)),
        scratch_shapes=[
            pltpu.VMEM((N_KEYS, LANES), BF16),
            pltpu.VMEM((N_KEYS, 2 * LANES), BF16),
            pltpu.VMEM((2, 2, ATTN_SQ, N_KEYS), F32),
            pltpu.VMEM((2, 2, ATTN_SQ, 1), F32),
        ],
        compiler_params=_params(("arbitrary", "arbitrary", "arbitrary")),
        name="attn",
    )(lam_params, parts, parts, parts, ctx_kv, ctx_kv, subln_g)


def _merge_kernel(yb_ref, s6_ref, s7_ref, o_ref, x_ref, mod_ref, g2_ref, wc_ref, wa_ref, wo_ref,
                  wr_ref, x1_ref, h2_ref, lg_ref):
    chunks = [slice(r * ROW_CHUNK, (r + 1) * ROW_CHUNK) for r in range(MERGE_TM // ROW_CHUNK)]
    for rows in chunks:
        y_conv = _dot(yb_ref[0, rows, :], wc_ref[...])
        y_attn = _dot(o_ref[rows, :], wa_ref[...])
        gate_conv = 1.0 / (1.0 + jnp.exp(-s6_ref[0, rows, :].astype(F32)))
        gate_attn = 1.0 / (1.0 + jnp.exp(-s7_ref[0, rows, :].astype(F32)))
        merged = gate_conv * y_conv + gate_attn * y_attn
        mix = _dot(merged.astype(BF16), wo_ref[...])
        x1_ref[rows, :] = x_ref[rows, :] + mod_ref[0, 2:3, :] * mix
    for rows in chunks:
        h2 = _norm_modulate(x1_ref[rows, :], g2_ref[...], mod_ref[0, 3:4, :], mod_ref[0, 4:5, :])
        h_hi = h2.astype(BF16)
        h2_ref[rows, :] = pltpu.pack_elementwise([h2[:, :D_MODEL // 2], h2[:, D_MODEL // 2:]], packed_dtype=BF16)
        h_lo = (h2 - h_hi.astype(F32)).astype(BF16)
        both = _dot_nt(wr_ref[...], h_hi)
        lg_ref[0, :, rows] = both[:N_EXPERTS] + (_dot_nt(wr_ref[0:N_EXPERTS, :], h_lo) + both[N_EXPERTS:])


def _merge(parts, attn_o, x2, mod3, norm2_g, w_conv_bf, w_attn_bf, w_out_bf, wr_hi_lo):
    per_b = SEQ // MERGE_TM
    sq = pl.BlockSpec((D_MODEL, D_MODEL), lambda i: (0, 0), pipeline_mode=pl.Buffered(1))
    wr = pl.BlockSpec((2 * N_EXPERTS, D_MODEL), lambda i: (0, 0), pipeline_mode=pl.Buffered(1))
    tile = pl.BlockSpec((MERGE_TM, D_MODEL), lambda i: (i, 0))
    return pl.pallas_call(
        _merge_kernel,
        out_shape=(
            jax.ShapeDtypeStruct((BATCH * SEQ, D_MODEL), F32),
            jax.ShapeDtypeStruct((BATCH * SEQ, D_MODEL // 2), jnp.uint32),
            jax.ShapeDtypeStruct((BATCH, N_EXPERTS, SEQ), F32),
        ),
        grid=(BATCH * per_b,),
        in_specs=[
            pl.BlockSpec((1, MERGE_TM, D_MODEL), lambda i: (0, i, 0)),
            pl.BlockSpec((1, MERGE_TM, D_MODEL), lambda i: (4, i, 0)),
            pl.BlockSpec((1, MERGE_TM, D_MODEL), lambda i: (5, i, 0)),
            tile, tile,
            pl.BlockSpec((1, 6, D_MODEL), lambda i: (i // per_b, 0, 0)),
            pl.BlockSpec((1, D_MODEL), lambda i: (0, 0)),
            sq, sq, sq, wr,
        ],
        out_specs=(
            tile,
            pl.BlockSpec((MERGE_TM, D_MODEL // 2), lambda i: (i, 0)),
            pl.BlockSpec((1, N_EXPERTS, MERGE_TM), lambda i: (i // per_b, 0, i % per_b)),
        ),
        compiler_params=_params(("arbitrary",)),
        name="merge",
    )(parts, parts, parts, attn_o, x2, mod3, norm2_g, w_conv_bf, w_attn_bf, w_out_bf, wr_hi_lo)


def _route_kernel(lg_ref, rank_ref, gate_ref, bounds_ref):
    lg = lg_ref[...]
    ex = jnp.exp(lg - jnp.max(lg, axis=1, keepdims=True))
    aff = (ex / jnp.sum(ex, axis=1, keepdims=True)).reshape(BATCH * N_EXPERTS, SEQ)
    bits = lax.bitcast_convert_type(aff, jnp.int32)

    def count(mask):
        return jnp.sum(jnp.where(mask, 1.0, 0.0), axis=1, keepdims=True)

    def search(i, t):
        cand = t + lax.shift_left(jnp.int32(1), 30 - i)
        return jnp.where(count(bits >= cand) >= CAPACITY, cand, t)

    thr = lax.fori_loop(0, 31, search, jnp.zeros((BATCH * N_EXPERTS, 1), jnp.int32))
    need = CAPACITY - count(bits > thr)
    before = (lax.broadcasted_iota(jnp.int32, (LANES, LANES), 0)
              < lax.broadcasted_iota(jnp.int32, (LANES, LANES), 1)).astype(BF16)
    n_tied = jnp.zeros((BATCH * N_EXPERTS, 1), F32)
    n_sel = jnp.zeros((BATCH * N_EXPERTS, 1), F32)
    per_tile = COMBINE_SUB // LANES
    firsts, ends = [], []
    for c in range(SEQ // LANES):
        if c % per_tile == 0:
            firsts.append(n_sel)
        cols = slice(c * LANES, (c + 1) * LANES)
        bits_c = bits[:, cols]
        tied_c = jnp.where(bits_c == thr, 1.0, 0.0)
        tied_before = _dot(tied_c.astype(BF16), before) + n_tied
        sel_c = jnp.where(bits_c > thr, 1.0, jnp.where(tied_before < need, tied_c, 0.0))
        rank = _dot(sel_c.astype(BF16), before) + n_sel
        rank_c = jnp.where(sel_c > 0.0, rank, -1.0)
        gate_c = jnp.where(sel_c > 0.0, aff[:, cols], 0.0)
        n_tied = n_tied + jnp.sum(tied_c, axis=1, keepdims=True)
        n_sel = n_sel + jnp.sum(sel_c, axis=1, keepdims=True)
        rank_ref[:, cols] = rank_c.astype(jnp.int32)
        gate_ref[:, cols] = gate_c
        if c % per_tile == per_tile - 1:
            ends.append(n_sel)
    bounds_ref[...] = jnp.concatenate(firsts + ends, axis=1).astype(jnp.int32)


def _route(logits_t):
    return pl.pallas_call(
        _route_kernel,
        out_shape=(
            jax.ShapeDtypeStruct((BATCH * N_EXPERTS, SEQ), jnp.int32),
            jax.ShapeDtypeStruct((BATCH * N_EXPERTS, SEQ), F32),
            jax.ShapeDtypeStruct((BATCH * N_EXPERTS, 2 * (SEQ // COMBINE_SUB)), jnp.int32),
        ),
        compiler_params=pltpu.CompilerParams(vmem_limit_bytes=VMEM_LIMIT),
        name="route",
    )(logits_t)


def _dispatch(rank_e, h_packed, first_expert, n_experts):
    per_worker = n_experts * BATCH // (SC_CORES * SC_SUBCORES)
    assert per_worker * SC_CORES * SC_SUBCORES == n_experts * BATCH
    half = CAPACITY // 2
    mesh = plsc.VectorSubcoreMesh(core_axis_name="c", subcore_axis_name="s")

    @functools.partial(
        pl.kernel, mesh=mesh,
        out_type=jax.ShapeDtypeStruct((n_experts * BATCH * CAPACITY, D_MODEL // 2), h_packed.dtype),
        scratch_types=[
            pltpu.VMEM((SEQ,), jnp.int32),
            pltpu.VMEM((2, half), jnp.int32),
            pltpu.VMEM((half, D_MODEL // 2), h_packed.dtype),
            pltpu.SemaphoreType.DMA,
        ],
        compiler_params=pltpu.CompilerParams(needs_layout_passes=False),
        name="dispatch",
    )
    def k(table_hbm, rank_hbm, out_hbm, rank_v, idx_v, rows_v, sem):
        worker = lax.axis_index("s") * SC_CORES + lax.axis_index("c")

        @pl.loop(0, per_worker)
        def _(p):
            pair = worker * per_worker + p
            expert = lax.shift_right_logical(pair, BATCH.bit_length() - 1)
            sample = pair & (BATCH - 1)
            pltpu.sync_copy(rank_hbm.at[sample * N_EXPERTS + first_expert + expert], rank_v)

            @pl.loop(0, SEQ // SC_LANES)
            def _(i):
                rank = rank_v[pl.ds(i * SC_LANES, SC_LANES)]
                row = lax.iota(jnp.int32, SC_LANES) + (i * SC_LANES + sample * SEQ)
                plsc.store_scatter(idx_v, [lax.shift_right_arithmetic(rank, 7), rank & (half - 1)], row,
                                   mask=rank >= 0)

            for c in range(2):
                pltpu.async_copy(table_hbm.at[idx_v.at[c]], rows_v, sem).wait()
                pltpu.sync_copy(rows_v, out_hbm.at[pl.ds(pair * CAPACITY + c * half, half)])

    return k(h_packed, rank_e)


def _expert_kernel(xs_ref, wg_ref, wu_ref, wd_ref, y_ref, acc_ref, wg_bf, wu_bf, wd_bf):
    t = pl.program_id(0)
    n_f = EXPERT_HIDDEN // FFN_TF
    f = (t + n_f - 1) % n_f

    n_chunks = BATCH * CAPACITY // ROW_CHUNK

    def stage(slot, part):
        up = slice(part * D_MODEL // n_chunks, (part + 1) * D_MODEL // n_chunks)
        down = slice(part * FFN_TF // n_chunks, (part + 1) * FFN_TF // n_chunks)
        wg_bf[slot, up, :] = wg_ref[0, up, :].astype(BF16)
        wu_bf[slot, up, :] = wu_ref[0, up, :].astype(BF16)
        wd_bf[slot, down, :] = wd_ref[0, down, :].astype(BF16)

    def step(cur):
        for r in range(n_chunks):
            stage(1 - cur, r)
            rows = slice(r * ROW_CHUNK, (r + 1) * ROW_CHUNK)
            packed = xs_ref[0, rows, :]
            half = D_MODEL // 2
            x_lo = pltpu.unpack_elementwise(packed, index=0, packed_dtype=BF16, unpacked_dtype=F32).astype(BF16)
            x_hi = pltpu.unpack_elementwise(packed, index=1, packed_dtype=BF16, unpacked_dtype=F32).astype(BF16)
            a = _dot(x_lo, wg_bf[cur, :half, :]) + _dot(x_hi, wg_bf[cur, half:, :])
            u = _dot(x_lo, wu_bf[cur, :half, :]) + _dot(x_hi, wu_bf[cur, half:, :])
            hidden = (a / (1.0 + jnp.exp(-a))) * u
            total = jnp.where(f == 0, 0.0, acc_ref[rows, :]) + _dot(hidden.astype(BF16), wd_bf[cur])
            acc_ref[rows, :] = total
            y_ref[0, rows, :] = total.astype(BF16)

    @pl.when(t == 0)
    def _():
        for part in range(n_chunks):
            stage(0, part)

    @pl.when(t % 2 == 1)
    def _():
        step(0)

    @pl.when((t % 2 == 0) & (t > 0))
    def _():
        step(1)


def _experts(xs, w_gate, w_up, w_down, first_expert):
    rows = BATCH * CAPACITY
    n_f = EXPERT_HIDDEN // FFN_TF
    n_experts = xs.shape[0]
    last = n_experts * n_f - 1

    def staged(t):
        return jnp.minimum(t, last)

    def computed(t):
        return jnp.maximum(t - 1, 0) // n_f

    return pl.pallas_call(
        _expert_kernel,
        out_shape=jax.ShapeDtypeStruct((n_experts, rows, D_MODEL), BF16),
        grid=(n_experts * n_f + 1,),
        in_specs=[
            pl.BlockSpec((1, rows, D_MODEL // 2), lambda t: (computed(t), 0, 0)),
            pl.BlockSpec((1, D_MODEL, FFN_TF), lambda t: (first_expert + staged(t) // n_f, 0, staged(t) % n_f)),
            pl.BlockSpec((1, D_MODEL, FFN_TF), lambda t: (first_expert + staged(t) // n_f, 0, staged(t) % n_f)),
            pl.BlockSpec((1, FFN_TF, D_MODEL), lambda t: (first_expert + staged(t) // n_f, staged(t) % n_f, 0)),
        ],
        out_specs=pl.BlockSpec((1, rows, D_MODEL), lambda t: (computed(t), 0, 0)),
        scratch_shapes=[
            pltpu.VMEM((rows, D_MODEL), F32),
            pltpu.VMEM((2, D_MODEL, FFN_TF), BF16),
            pltpu.VMEM((2, D_MODEL, FFN_TF), BF16),
            pltpu.VMEM((2, FFN_TF, D_MODEL), BF16),
        ],
        compiler_params=_params(("arbitrary",)),
        name="experts",
    )(xs, w_gate, w_up, w_down)


def _combine_kernel(bounds_ref, rank_ref, gate_ref, *refs):
    n_groups = len(EXPERT_GROUPS)
    y_hbm, (x1_ref, mod_ref, fg_ref, o_ref), scratch = refs[:n_groups], refs[n_groups:n_groups + 4], refs[n_groups + 4:]
    y_bufs, y_sems = scratch[:n_groups], scratch[n_groups:]
    group_of = [(g, e) for g, n in enumerate(EXPERT_GROUPS) for e in range(n)]
    b, t = pl.program_id(0), pl.program_id(1)
    subs = COMBINE_TQ // COMBINE_SUB
    n_sub = SEQ // COMBINE_SUB
    per_dot = CAPACITY // COMBINE_WINDOW

    def y_copies(s):
        rows = pl.ds(pl.multiple_of(s * CAPACITY, CAPACITY), CAPACITY)
        return [pltpu.make_async_copy(y_hbm[g].at[:, rows, :], y_bufs[g].at[s % 2], y_sems[g].at[s % 2])
                for g in range(n_groups)]

    @pl.when((b == 0) & (t == 0))
    def _():
        for copy in y_copies(0):
            copy.start()

    @pl.when(t == 0)
    def _():
        for copy in y_copies(b):
            copy.wait()

    @pl.when((t == 0) & (b + 1 < BATCH))
    def _():
        for copy in y_copies(b + 1):
            copy.start()

    def y_rows(e):
        return y_bufs[group_of[e][0]].at[b % 2, group_of[e][1]]

    def onehot_t(e, cols, first, n_slots):
        slot = lax.broadcasted_iota(jnp.int32, (n_slots, cols.stop - cols.start), 0) + first
        return jnp.where(rank_ref[e:e + 1, cols] == slot, gate_ref[e:e + 1, cols], 0.0).astype(BF16)

    def dot_t(a, b_):
        return lax.dot_general(a, b_, (((0,), (0,)), ((), ())), preferred_element_type=F32)

    def finish(rows, acc):
        x2 = x1_ref[rows, :] + mod_ref[0, 5:6, :] * acc
        ms = jnp.mean(x2 * x2, axis=-1, keepdims=True)
        o_ref[rows, :] = x2 * lax.rsqrt(ms + EPS) * fg_ref[...]

    starts, fits = {}, None
    for s in range(subs):
        for e in range(N_EXPERTS):
            pair = (b * N_EXPERTS + e) * (2 * n_sub) + t * subs + s
            first, end = bounds_ref[pair], bounds_ref[pair + n_sub]
            start = jnp.minimum(lax.shift_left(lax.shift_right_logical(first, 4), 4), CAPACITY - COMBINE_WINDOW)
            starts[s, e] = start
            ok = end <= start + COMBINE_WINDOW
            fits = ok if fits is None else fits & ok

    @pl.when(fits)
    def _():
        for s in range(subs):
            rows = slice(s * COMBINE_SUB, (s + 1) * COMBINE_SUB)
            acc = jnp.zeros((COMBINE_SUB, D_MODEL), F32)
            for e0 in range(0, N_EXPERTS, per_dot):
                first = [pl.multiple_of(starts[s, e0 + i], 16) for i in range(per_dot)]
                onehots = [onehot_t(e0 + i, rows, first[i], COMBINE_WINDOW) for i in range(per_dot)]
                y_win = [y_rows(e0 + i)[pl.ds(first[i], COMBINE_WINDOW), :] for i in range(per_dot)]
                acc = acc + dot_t(jnp.concatenate(onehots, axis=0), jnp.concatenate(y_win, axis=0))
            finish(rows, acc)

    @pl.when(jnp.logical_not(fits))
    def _():
        acc = jnp.zeros((COMBINE_TQ, D_MODEL), F32)
        for e in range(N_EXPERTS):
            acc = acc + dot_t(onehot_t(e, slice(0, COMBINE_TQ), 0, CAPACITY), y_rows(e)[...])
        finish(slice(0, COMBINE_TQ), acc)


def _combine(bounds, rank_e, gate_e, ys, x1, mod3, final_g):
    per_b = SEQ // COMBINE_TQ
    return pl.pallas_call(
        _combine_kernel,
        out_shape=jax.ShapeDtypeStruct((BATCH * SEQ, D_MODEL), F32),
        grid_spec=pltpu.PrefetchScalarGridSpec(
            num_scalar_prefetch=1,
            grid=(BATCH, per_b),
            in_specs=[
                pl.BlockSpec((N_EXPERTS, COMBINE_TQ), lambda b, t, _: (b, t)),
                pl.BlockSpec((N_EXPERTS, COMBINE_TQ), lambda b, t, _: (b, t)),
                *[pl.BlockSpec(memory_space=pl.ANY) for _ in ys],
                pl.BlockSpec((COMBINE_TQ, D_MODEL), lambda b, t, _: (b * per_b + t, 0)),
                pl.BlockSpec((1, 6, D_MODEL), lambda b, t, _: (b, 0, 0)),
                pl.BlockSpec((1, D_MODEL), lambda b, t, _: (0, 0)),
            ],
            out_specs=pl.BlockSpec((COMBINE_TQ, D_MODEL), lambda b, t, _: (b * per_b + t, 0)),
            scratch_shapes=[
                *[pltpu.VMEM((2, y.shape[0], CAPACITY, D_MODEL), BF16) for y in ys],
                *[pltpu.SemaphoreType.DMA((2,)) for _ in ys],
            ],
        ),
        compiler_params=_params(("arbitrary", "arbitrary")),
        name="combine",
    )(bounds.reshape(-1), rank_e, gate_e, *ys, x1, mod3, final_g)


def _rope_tables():
    t = np.arange(SEQ)
    pos = np.stack([t // GRID_W, t % GRID_W], axis=1).astype(np.float32)
    per_axis = HEAD_DIM // 2
    inv = np.float32(ROPE_THETA) ** (-np.arange(0, per_axis, 2, dtype=np.float32) / np.float32(per_axis))
    ang = (pos[:, :, None] * inv).astype(np.float32)
    cos, sin = np.cos(ang.astype(np.float64)), np.sin(ang.astype(np.float64))
    zero = np.zeros_like(sin)

    def lanes(first_half, second_half):
        one_map = np.stack([first_half, second_half], axis=2).reshape(SEQ, HEAD_DIM)
        return jnp.asarray(np.concatenate([one_map, one_map], axis=1), F32)

    return lanes(cos, cos), lanes(-sin, zero), lanes(zero, sin)


def kernel(x, c, ctx, c_ctx, norm1_g, norm2_g, w_ada, b_ada, w_in, conv_w, w_out_conv, lambda_q1,
           lambda_k1, lambda_q2, lambda_k2, subln_g, w_o_attn, w_out, w_router, w_gate_e, w_up_e,
           w_down_e, final_g):
    cond = jnp.concatenate([c, c_ctx[None, :], jnp.zeros((COND_ROWS - BATCH - 1, D_MODEL), F32)], axis=0)
    mod3 = _ada(cond, w_ada[0], b_ada).reshape(COND_ROWS, 6, D_MODEL)

    cos, sin_hi, sin_lo = _rope_tables()
    q_scale = math.log2(math.e) * HEAD_DIM ** -0.5
    parts = _inproj(x, mod3, norm1_g, w_in[0], conv_w[0], cos, sin_hi, sin_lo, q_scale)
    ctx_kv = _ctxproj(ctx.reshape(BATCH * CTX_LEN, D_MODEL), mod3, norm1_g, w_in[0])

    lam_params = jnp.concatenate([lambda_q1, lambda_k1, lambda_q2, lambda_k2], axis=0)
    attn_o = _attn(lam_params, parts, ctx_kv, subln_g)

    wr_t = w_router[0].T
    wr_hi = wr_t.astype(BF16)
    wr_lo = (wr_t - wr_hi.astype(F32)).astype(BF16)
    x1, h2, logits_t = _merge(parts, attn_o, x.reshape(BATCH * SEQ, D_MODEL), mod3, norm2_g,
                              w_out_conv[0].astype(BF16), w_o_attn[0].astype(BF16), w_out[0].astype(BF16),
                              jnp.concatenate([wr_hi, wr_lo], axis=0))

    rank_e, gate_e, bounds = _route(logits_t)
    ys, first = [], 0
    for n in EXPERT_GROUPS:
        xs = _dispatch(rank_e, h2, first, n).reshape(n, BATCH * CAPACITY, D_MODEL // 2)
        ys.append(_experts(xs, w_gate_e[0], w_up_e[0], w_down_e[0], first))
        first += n
    out = _combine(bounds, rank_e, gate_e, ys, x1, mod3, final_g[None, :])
    return out.reshape(BATCH, SEQ, D_MODEL)
```
